```python
import math
import jax, jax.numpy as jnp
from jax import lax
import numpy as np

D_MODEL = 1024
BATCH = 8
SEQ = 2048
DEPTH = 1
DEC_BATCH = 128
DEC_SEQ = 4
PAST_LEN = 2048
PAGE_SIZE = 128

D_MIX = D_MODEL
W_A = D_MIX // 2
W_B = D_MIX - W_A
HEAD_DIM = 64
N_HEADS = W_A // HEAD_DIM
N_KV = 2
N_IDX_HEADS = 8
IDX_DIM = 64
TOPK_MAX = 256
CHUNK = 128
N_GROUPS_B = 4
GROUP_W_B = W_B // N_GROUPS_B
Q_BLOCK = 128
ROPE_THETA = 10000.0
LN_EPS = 1e-5
ALPHA = (2.0 * DEPTH) ** 0.25
BETA = (8.0 * DEPTH) ** -0.25
NEG = -1e30
SPLITS = (W_A, N_KV * HEAD_DIM, N_KV * HEAD_DIM, W_A, N_IDX_HEADS * IDX_DIM, IDX_DIM, N_IDX_HEADS, W_B, W_B, W_B)
D_IN = W_A * 2 + 2 * N_KV * HEAD_DIM + N_IDX_HEADS * IDX_DIM + IDX_DIM + N_IDX_HEADS + 3 * W_B

kernel_name = "hymba_dsa_gmlp_decode_step"


def layer_norm(x, g, b):
    xf = x.astype(jnp.float32)
    mu = jnp.mean(xf, axis=-1, keepdims=True)
    var = jnp.mean(jnp.square(xf - mu), axis=-1, keepdims=True)
    return ((xf - mu) * lax.rsqrt(var + LN_EPS) * g.astype(jnp.float32) + b.astype(jnp.float32)).astype(x.dtype)


def rope(x, pos):
    half = x.shape[-1] // 2
    inv = ROPE_THETA ** (-jnp.arange(half, dtype=jnp.float32) / half)
    ang = pos.astype(jnp.float32)[:, None] * inv[None, :]
    cos = jnp.cos(ang)[:, None, :]
    sin = jnp.sin(ang)[:, None, :]
    xf = x.astype(jnp.float32)
    x1, x2 = xf[..., :half], xf[..., half:]
    return jnp.concatenate([x1 * cos - x2 * sin, x2 * cos + x1 * sin], axis=-1).astype(x.dtype)


def project(x, w_in, pos):
    z = jnp.einsum('ntd,de->nte', x, w_in)
    cuts, acc = [], 0
    for w in SPLITS[:-1]:
        acc += w
        cuts.append(acc)
    q, k, v, ga, qi, ki, wi, u, vb, gb = jnp.split(z, cuts, axis=-1)
    n, t = x.shape[0], x.shape[1]
    q = rope(q.reshape(n, t, N_HEADS, HEAD_DIM), pos)
    k = rope(k.reshape(n, t, N_KV, HEAD_DIM), pos)
    v = v.reshape(n, t, N_KV, HEAD_DIM)
    qi = rope(qi.reshape(n, t, N_IDX_HEADS, IDX_DIM), pos)
    ki = rope(ki[:, :, None, :], pos)[:, :, 0]
    return q, k, v, ga, qi, ki, wi, u, vb, gb


def indexer_topk(qi, wi, ki, qpos, topk):
    s = jnp.einsum('nqhd,nld->nqhl', qi.astype(jnp.float32), ki.astype(jnp.float32))
    w = wi.astype(jnp.float32) * (N_IDX_HEADS * IDX_DIM) ** -0.5
    score = jnp.einsum('nqhl,nqh->nql', jax.nn.relu(s), w)
    kpos = jnp.arange(ki.shape[1], dtype=jnp.int32)
    allowed = kpos[None, :] <= qpos[:, None]
    score = jnp.where(allowed[None], score, NEG)
    _, idx = lax.top_k(score, topk)
    valid = idx <= qpos[None, :, None]
    return idx, valid


def gather_rows(rows, idx):
    return jax.vmap(lambda r, i: r[i])(rows, idx)


def sparse_attend(q, k_sel, v_sel, valid):
    n, nq = q.shape[0], q.shape[1]
    qg = q.reshape(n, nq, N_KV, N_HEADS // N_KV, HEAD_DIM).astype(jnp.float32)
    s = jnp.einsum('nqhgd,nqkhd->nqhgk', qg, k_sel.astype(jnp.float32)) * HEAD_DIM ** -0.5
    s = jnp.where(valid[:, :, None, None, :], s, NEG)
    p = jax.nn.softmax(s, axis=-1)
    o = jnp.einsum('nqhgk,nqkhd->nqhgd', p, v_sel.astype(jnp.float32))
    return o.reshape(n, nq, W_A).astype(q.dtype)


def chunk_mix(vn, w_s, b_s, t):
    mask = jnp.tril(jnp.ones((t, t), dtype=vn.dtype))
    wm = w_s[:, :t, :t] * mask[None]
    out = jnp.einsum('gts,ncsgd->nctgd', wm, vn)
    return out + jnp.transpose(b_s[:, :t])[:, :, None]


def mixer_out(x, attn, ga, u, mixed, gb, w_out, ln_g, ln_b):
    a_out = attn * jax.nn.silu(ga)
    b_out = u * mixed * jax.nn.silu(gb)
    out = jnp.einsum('nte,ed->ntd', jnp.concatenate([a_out, b_out], axis=-1), w_out)
    return layer_norm(ALPHA * x + out, ln_g, ln_b)


def layer_prompt(x, w_in, w_s, b_s, ln_v_g, ln_v_b, w_out, ln_g, ln_b):
    n = x.shape[0]
    pos = jnp.arange(SEQ, dtype=jnp.int32)
    q, k, v, ga, qi, ki, wi, u, vb, gb = project(x, w_in, pos)
    topk = min(TOPK_MAX, SEQ // 4)
    n_blocks = SEQ // Q_BLOCK

    def block(i):
        start = i * Q_BLOCK
        sl = lambda a: lax.dynamic_slice_in_dim(a, start, Q_BLOCK, axis=1)
        qpos = start + jnp.arange(Q_BLOCK, dtype=jnp.int32)
        idx, valid = indexer_topk(sl(qi), sl(wi), ki, qpos, topk)
        return sparse_attend(sl(q), gather_rows(k, idx), gather_rows(v, idx), valid)

    attn = lax.map(block, jnp.arange(n_blocks, dtype=jnp.int32))
    attn = jnp.transpose(attn, (1, 0, 2, 3)).reshape(n, SEQ, W_A)
    vn = layer_norm(vb, ln_v_g, ln_v_b)
    mixed = chunk_mix(vn.reshape(n, SEQ // CHUNK, CHUNK, N_GROUPS_B, GROUP_W_B), w_s, b_s, CHUNK).reshape(n, SEQ, W_B)
    y = mixer_out(x, attn, ga, u, mixed, gb, w_out, ln_g, ln_b)
    return y, k, v, ki, vn[:, SEQ - CHUNK:]


def layer_sample(x, cache_k, cache_v, cache_k_idx, page_table, w_in, w_s, b_s, ln_v_g, ln_v_b, w_out, ln_g, ln_b):
    n = x.shape[0]
    pos = PAST_LEN + jnp.arange(DEC_SEQ, dtype=jnp.int32)
    q, k, v, ga, qi, ki, wi, u, vb, gb = project(x, w_in, pos)
    total = PAST_LEN + DEC_SEQ
    topk = min(TOPK_MAX, total // 4)
    ki_past = cache_k_idx[page_table].reshape(n, PAST_LEN, IDX_DIM)
    ki_all = jnp.concatenate([ki_past.astype(ki.dtype), ki], axis=1)
    idx, valid = indexer_topk(qi, wi, ki_all, pos, topk)
    is_past = (idx < PAST_LEN)[..., None, None]
    pidx = jnp.minimum(idx, PAST_LEN - 1)
    phys = page_table[jnp.arange(n)[:, None, None], pidx // PAGE_SIZE]
    off = pidx % PAGE_SIZE
    nidx = jnp.clip(idx - PAST_LEN, 0, DEC_SEQ - 1)
    k_sel = jnp.where(is_past, cache_k[phys, off].astype(k.dtype), gather_rows(k, nidx))
    v_sel = jnp.where(is_past, cache_v[phys, off].astype(v.dtype), gather_rows(v, nidx))
    attn = sparse_attend(q, k_sel, v_sel, valid)
    vn = layer_norm(vb, ln_v_g, ln_v_b)
    mixed = chunk_mix(vn.reshape(n, 1, DEC_SEQ, N_GROUPS_B, GROUP_W_B), w_s, b_s, DEC_SEQ).reshape(n, DEC_SEQ, W_B)
    y = mixer_out(x, attn, ga, u, mixed, gb, w_out, ln_g, ln_b)
    return y, k, v, ki, vn


def setup_inputs(seed: int = 0) -> dict:
    key = jax.random.key(seed)
    ks = jax.random.split(key, 16)
    n_pages = PAST_LEN // PAGE_SIZE
    n_pool = (DEC_BATCH * n_pages * 5) // 4
    f32 = jnp.float32
    x_prompt = jax.random.normal(ks[0], (BATCH, SEQ, D_MODEL), f32)
    x_sample = jax.random.normal(ks[1], (DEC_BATCH, DEC_SEQ, D_MODEL), f32)
    cache_k = jax.random.normal(ks[2], (DEPTH, n_pool, PAGE_SIZE, N_KV, HEAD_DIM), f32)
    cache_v = jax.random.normal(ks[3], (DEPTH, n_pool, PAGE_SIZE, N_KV, HEAD_DIM), f32)
    cache_k_idx = jax.random.normal(ks[4], (DEPTH, n_pool, PAGE_SIZE, IDX_DIM), f32)
    perm = jax.random.permutation(ks[5], n_pool)[: DEC_BATCH * n_pages]
    page_table = perm.reshape(DEC_BATCH, n_pages).astype(jnp.int32)
    w_in = jax.random.normal(ks[6], (DEPTH, D_MODEL, D_IN), f32) * D_MODEL ** -0.5
    w_s = jax.random.normal(ks[7], (DEPTH, N_GROUPS_B, CHUNK, CHUNK), f32) * CHUNK ** -0.5
    b_s = 1.0 + 0.01 * jax.random.normal(ks[8], (DEPTH, N_GROUPS_B, CHUNK), f32)
    ln_v_g = 1.0 + 0.01 * jax.random.normal(ks[9], (DEPTH, W_B), f32)
    ln_v_b = 0.01 * jax.random.normal(ks[10], (DEPTH, W_B), f32)
    w_out = jax.random.normal(ks[11], (DEPTH, D_MIX, D_MODEL), f32) * (D_MIX ** -0.5) * BETA
    ln_g = 1.0 + 0.01 * jax.random.normal(ks[12], (DEPTH, D_MODEL), f32)
    ln_b = 0.01 * jax.random.normal(ks[13], (DEPTH, D_MODEL), f32)
    return {"x_prompt": x_prompt, "x_sample": x_sample, "cache_k": cache_k, "cache_v": cache_v,
            "cache_k_idx": cache_k_idx, "page_table": page_table, "w_in": w_in, "w_s": w_s, "b_s": b_s,
            "ln_v_g": ln_v_g, "ln_v_b": ln_v_b, "w_out": w_out, "ln_g": ln_g, "ln_b": ln_b}


def reference(x_prompt, x_sample, cache_k, cache_v, cache_k_idx, page_table, w_in, w_s, b_s, ln_v_g, ln_v_b, w_out, ln_g, ln_b):
    xp, xs = x_prompt, x_sample
    kp, vp, kip, vbp = [], [], [], []
    kss, vss, kis, vbs = [], [], [], []
    for l in range(DEPTH):
        xp, k1, v1, ki1, vb1 = layer_prompt(xp, w_in[l], w_s[l], b_s[l], ln_v_g[l], ln_v_b[l], w_out[l], ln_g[l], ln_b[l])
        xs, k2, v2, ki2, vb2 = layer_sample(xs, cache_k[l], cache_v[l], cache_k_idx[l], page_table,
                                            w_in[l], w_s[l], b_s[l], ln_v_g[l], ln_v_b[l], w_out[l], ln_g[l], ln_b[l])
        kp.append(k1); vp.append(v1); kip.append(ki1); vbp.append(vb1)
        kss.append(k2); vss.append(v2); kis.append(ki2); vbs.append(vb2)
    return (xp, xs, jnp.stack(kp), jnp.stack(vp), jnp.stack(kip), jnp.stack(vbp),
            jnp.stack(kss), jnp.stack(vss), jnp.stack(kis), jnp.stack(vbs))
```

```python
import functools

import jax
import jax.numpy as jnp
from jax import lax
from jax.experimental import pallas as pl
from jax.experimental.pallas import tpu as pltpu

F32 = jnp.float32
BF16 = jnp.bfloat16

D_MODEL = 1024
SEQ = 2048
BATCH = 8
DEC_BATCH = 128
DEC_SEQ = 4
PAST_LEN = 2048
PAGE_SIZE = 128
N_PAGES = PAST_LEN // PAGE_SIZE
W_A = 512
W_B = 512
HEAD_DIM = 64
N_HEADS = 8
N_KV = 2
N_IDX_HEADS = 8
IDX_DIM = 64
TOPK = 256
CHUNK = 128
N_GROUPS_B = 4
GROUP_W_B = 128
ROPE_THETA = 10000.0
LN_EPS = 1e-5
ALPHA = 2.0 ** 0.25
NEG = -1e30
IDX_SCALE = float((N_IDX_HEADS * IDX_DIM) ** -0.5)

LANES = 128
KEY_TILE = 128
ROW_TILE = 512
WT_ROWS = 1168
BISECT_ITERS = 24
VMEM_LIMIT = 48 * 1024 * 1024

SAMPLE_GROUP = 4
SAMPLE_KEYS = PAST_LEN + KEY_TILE
SAMPLE_Q = 8


def _proj_kernel(x_ref, wstd_ref, wt_ref, cosl_ref, sinl_ref, cost_ref, sint_ref,
                 k_ref, v_ref, ki_ref, kbf_ref, kiw_ref, qt_ref, qit_ref, vt_ref, wit_ref):
    tm = x_ref.shape[0]
    xb = x_ref[...].astype(BF16)
    zs = jnp.dot(xb, wstd_ref[...], preferred_element_type=F32)
    cosl = cosl_ref[...]
    sinl = sinl_ref[...]
    lane = lax.broadcasted_iota(jnp.int32, (tm, LANES), 1)
    first_half = (lane % HEAD_DIM) < (HEAD_DIM // 2)

    def rope_lanes(z):
        partner = jnp.where(first_half, pltpu.roll(z, 96, 1), pltpu.roll(z, 32, 1))
        return z * cosl + partner * sinl

    k = rope_lanes(zs[:, 0:128])
    v = zs[:, 128:256]
    kiw = rope_lanes(zs[:, 256:384])
    k_ref[...] = k
    v_ref[...] = v
    ki_ref[...] = kiw[:, 0:IDX_DIM]
    kbf_ref[...] = k.astype(BF16)
    kiw_ref[...] = kiw.astype(BF16)

    zt = lax.dot_general(wt_ref[...], xb, (((1,), (1,)), ((), ())),
                         preferred_element_type=F32)
    cost = cost_ref[...]
    sint = sint_ref[...]
    half = HEAD_DIM // 2

    def rope_rows(base):
        pieces = []
        for h in range(N_HEADS):
            x1 = zt[base + h * HEAD_DIM: base + h * HEAD_DIM + half]
            x2 = zt[base + h * HEAD_DIM + half: base + (h + 1) * HEAD_DIM]
            pieces.append(x1 * cost - x2 * sint)
            pieces.append(x2 * cost + x1 * sint)
        return jnp.concatenate(pieces, axis=0)

    qt_ref[...] = rope_rows(0).astype(BF16)
    qit_ref[...] = rope_rows(512).astype(BF16)
    vtb = zt[1024:1152].astype(BF16)
    for c in range(tm // KEY_TILE):
        vt_ref[c] = vtb[:, c * KEY_TILE:(c + 1) * KEY_TILE]
    wit_ref[...] = zt[1152:1160] * IDX_SCALE


def _project(x2d, wstd, wt, cosl, sinl, cost, sint, n_pos_tiles):
    rows = x2d.shape[0]
    tm = ROW_TILE
    grid = (rows // tm,)
    row_map = lambda r: (r, 0)
    col_map = lambda r: (0, r)
    const = lambda r: (0, 0)
    out_shape = (
        jax.ShapeDtypeStruct((rows, 128), F32),
        jax.ShapeDtypeStruct((rows, 128), F32),
        jax.ShapeDtypeStruct((rows, IDX_DIM), F32),
        jax.ShapeDtypeStruct((rows, 128), BF16),
        jax.ShapeDtypeStruct((rows, 128), BF16),
        jax.ShapeDtypeStruct((512, rows), BF16),
        jax.ShapeDtypeStruct((512, rows), BF16),
        jax.ShapeDtypeStruct((rows // KEY_TILE, 128, KEY_TILE), BF16),
        jax.ShapeDtypeStruct((8, rows), F32),
    )
    out_specs = (
        pl.BlockSpec((tm, 128), row_map),
        pl.BlockSpec((tm, 128), row_map),
        pl.BlockSpec((tm, IDX_DIM), row_map),
        pl.BlockSpec((tm, 128), row_map),
        pl.BlockSpec((tm, 128), row_map),
        pl.BlockSpec((512, tm), col_map),
        pl.BlockSpec((512, tm), col_map),
        pl.BlockSpec((tm // KEY_TILE, 128, KEY_TILE), lambda r: (r, 0, 0)),
        pl.BlockSpec((8, tm), col_map),
    )
    in_specs = [
        pl.BlockSpec((tm, D_MODEL), row_map),
        pl.BlockSpec(wstd.shape, const),
        pl.BlockSpec(wt.shape, const),
        pl.BlockSpec((tm, LANES), lambda r: (r % n_pos_tiles, 0)),
        pl.BlockSpec((tm, LANES), lambda r: (r % n_pos_tiles, 0)),
        pl.BlockSpec((HEAD_DIM // 2, tm), lambda r: (0, r % n_pos_tiles)),
        pl.BlockSpec((HEAD_DIM // 2, tm), lambda r: (0, r % n_pos_tiles)),
    ]
    return pl.pallas_call(
        _proj_kernel, grid=grid, in_specs=in_specs, out_specs=out_specs, out_shape=out_shape,
        compiler_params=pltpu.CompilerParams(dimension_semantics=("arbitrary",),
                                             vmem_limit_bytes=VMEM_LIMIT),
        name="proj_attn_side",
    )(x2d, wstd, wt, cosl, sinl, cost, sint)


def _rope_tables(pos):
    half = HEAD_DIM // 2
    inv = ROPE_THETA ** (-jnp.arange(half, dtype=F32) / half)
    ang = pos.astype(F32)[:, None] * inv[None, :]
    cos = jnp.cos(ang)
    sin = jnp.sin(ang)
    cosl = jnp.tile(cos, (1, LANES // half))
    sinl = jnp.tile(jnp.concatenate([-sin, sin], axis=1), (1, LANES // HEAD_DIM))
    return cosl, sinl, cos.T, sin.T


def _select_threshold(count_ge, count_gt, count_eq_le, lo0, hi0, n_allowed, n_keys, row_ok):
    k = float(TOPK)

    def step(carry):
        lo, hi, clo = carry
        mid = 0.5 * lo + 0.5 * hi
        c = count_ge(mid)
        take = c >= k
        return jnp.where(take, mid, lo), jnp.where(take, hi, mid), jnp.where(take, c, clo)

    lo, hi, clo = lax.fori_loop(0, BISECT_ITERS, lambda _, c: step(c), (lo0, hi0, n_allowed))
    big = jnp.full_like(lo, float(4 * n_keys))

    def unresolved(clo_):
        return jnp.logical_and(clo_ > k, row_ok)

    def any_true(mask):
        return jnp.max(jnp.where(mask, 1.0, 0.0)) > 0.5

    def exact(args):
        lo, hi, clo = args

        def cond(c):
            lo, hi, clo, it = c
            mid = 0.5 * lo + 0.5 * hi
            moving = jnp.logical_and(mid > lo, mid < hi)
            return jnp.logical_and(any_true(jnp.logical_and(unresolved(clo), moving)), it < 400)

        def body(c):
            lo, hi, clo, it = c
            lo, hi, clo = step((lo, hi, clo))
            return lo, hi, clo, it + 1

        lo, hi, clo, _ = lax.while_loop(cond, body, (lo, hi, clo, jnp.int32(0)))
        need = k - count_gt(lo)

        def jstep(_, c):
            jlo, jhi = c
            mid = jnp.floor(0.5 * (jlo + jhi))
            ok = count_eq_le(lo, mid) >= need
            return jnp.where(ok, jlo, mid), jnp.where(ok, mid, jhi)

        jlo0 = jnp.full_like(lo, -1.0)
        jhi0 = jnp.full_like(lo, float(n_keys - 1))
        _, jhi = lax.fori_loop(0, 13, jstep, (jlo0, jhi0))
        return lo, jnp.where(clo > k, jhi, big)

    def fast(args):
        return args[0], big

    return lax.cond(any_true(unresolved(clo)), exact, fast, (lo, hi, clo))


def _attn_prompt_kernel(qt_ref, qit_ref, wit_ref, kbf_ref, kiw_ref, vt_ref, o_ref,
                        s_ref, bias_ref, ri_ref, rq_ref):
    i = pl.program_id(1)
    nt = i + 1
    w = wit_ref[...]
    zeros_half = jnp.zeros((HEAD_DIM, LANES), BF16)

    for hp in range(4):
        a = qit_ref[(2 * hp) * 64:(2 * hp + 1) * 64, :]
        b = qit_ref[(2 * hp + 1) * 64:(2 * hp + 2) * 64, :]
        ri_ref[hp] = jnp.concatenate(
            [jnp.concatenate([a, zeros_half], axis=0), jnp.concatenate([b, zeros_half], axis=0)], axis=1)
        g = hp // 2
        a = qt_ref[(2 * hp) * 64:(2 * hp + 1) * 64, :]
        b = qt_ref[(2 * hp + 1) * 64:(2 * hp + 2) * 64, :]
        if g == 0:
            ab = jnp.concatenate([jnp.concatenate([a, zeros_half], axis=0),
                                  jnp.concatenate([b, zeros_half], axis=0)], axis=1)
        else:
            ab = jnp.concatenate([jnp.concatenate([zeros_half, a], axis=0),
                                  jnp.concatenate([zeros_half, b], axis=0)], axis=1)
        rq_ref[hp] = ab

    row = lax.broadcasted_iota(jnp.int32, (KEY_TILE, LANES), 0)
    col = lax.broadcasted_iota(jnp.int32, (KEY_TILE, LANES), 1)
    qpos = col + i * LANES
    inf = jnp.float32(jnp.inf)

    def idx_body(j, carry):
        mn, mx = carry
        off = pl.multiple_of(j * KEY_TILE, KEY_TILE)
        kt = kiw_ref[pl.ds(off, KEY_TILE), :]
        acc = jnp.zeros((KEY_TILE, LANES), F32)
        for hp in range(4):
            s2 = jnp.dot(kt, ri_ref[hp], preferred_element_type=F32)
            acc = acc + jnp.maximum(s2[:, :LANES], 0.0) * w[2 * hp:2 * hp + 1, :]
            acc = acc + jnp.maximum(s2[:, LANES:], 0.0) * w[2 * hp + 1:2 * hp + 2, :]
        allowed = (row + j * KEY_TILE) <= qpos
        s_ref[pl.ds(off, KEY_TILE), :] = jnp.where(allowed, acc, -inf)
        mn = jnp.minimum(mn, jnp.where(allowed, acc, inf))
        mx = jnp.maximum(mx, jnp.where(allowed, acc, -inf))
        return mn, mx

    mn, mx = lax.fori_loop(0, nt, idx_body, (jnp.full((KEY_TILE, LANES), inf, F32),
                                             jnp.full((KEY_TILE, LANES), -inf, F32)))
    lo0 = jnp.min(mn, axis=0, keepdims=True)
    mx1 = jnp.max(mx, axis=0, keepdims=True)
    hi0 = mx1 + (jnp.abs(mx1) * (2.0 ** -20) + 1e-30)
    n_allowed = (qpos[0:1, :] + 1).astype(F32)

    def tile_count(pred):
        def body(j, c8):
            off = pl.multiple_of(j * KEY_TILE, KEY_TILE)
            hit = pred(s_ref[pl.ds(off, KEY_TILE), :], j)
            return c8 + jnp.sum(hit.reshape(KEY_TILE // 8, 8, LANES), axis=0)
        c8 = lax.fori_loop(0, nt, body, jnp.zeros((8, LANES), F32))
        return jnp.sum(c8, axis=0, keepdims=True)

    def kidx(j):
        return (row + j * KEY_TILE).astype(F32)

    count_ge = lambda thr: tile_count(lambda t, j: jnp.where(t >= thr, 1.0, 0.0))
    count_gt = lambda thr: tile_count(lambda t, j: jnp.where(t > thr, 1.0, 0.0))
    count_eq_le = lambda thr, jm: tile_count(
        lambda t, j: jnp.where(t == thr, jnp.where(kidx(j) <= jm, 1.0, 0.0), 0.0))

    lo, jmax = _select_threshold(count_ge, count_gt, count_eq_le, lo0, hi0, n_allowed, SEQ,
                                 jnp.full((1, LANES), True))

    def bias_body(j, _):
        off = pl.multiple_of(j * KEY_TILE, KEY_TILE)
        t = s_ref[pl.ds(off, KEY_TILE), :]
        tie = jnp.where(t == lo, jnp.where(kidx(j) <= jmax, 0.0, NEG), NEG)
        bias_ref[pl.ds(off, KEY_TILE), :] = jnp.where(t > lo, 0.0, tie)
        return 0

    lax.fori_loop(0, nt, bias_body, 0)

    def att_body(j, carry):
        off = pl.multiple_of(j * KEY_TILE, KEY_TILE)
        kt = kbf_ref[pl.ds(off, KEY_TILE), :]
        b = bias_ref[pl.ds(off, KEY_TILE), :]
        b2 = jnp.concatenate([b, b], axis=1)
        vt = vt_ref[j]
        out = []
        for hp in range(4):
            g = hp // 2
            m, l, acc = carry[hp]
            s2 = jnp.dot(kt, rq_ref[hp], preferred_element_type=F32) + b2
            m_new = jnp.maximum(m, jnp.max(s2, axis=0, keepdims=True))
            alpha = jnp.exp(m - m_new)
            p = jnp.exp(s2 - m_new)
            l = alpha * l + jnp.sum(p, axis=0, keepdims=True)
            pv = jnp.dot(vt[g * 64:(g + 1) * 64, :], p.astype(BF16), preferred_element_type=F32)
            out.append((m_new, l, alpha * acc + pv))
        return tuple(out)

    init = tuple((jnp.full((1, 2 * LANES), -inf, F32), jnp.zeros((1, 2 * LANES), F32),
                  jnp.zeros((HEAD_DIM, 2 * LANES), F32)) for _ in range(4))
    res = lax.fori_loop(0, nt, att_body, init)
    pieces = []
    for hp in range(4):
        _, l, acc = res[hp]
        o2 = acc / l
        pieces.append(o2[:, :LANES])
        pieces.append(o2[:, LANES:])
    o_ref[...] = jnp.concatenate(pieces, axis=0).T


def _attn_prompt(qt, qit, wit, kbf, kiw, vt3):
    n_blocks = SEQ // LANES
    grid = (BATCH, n_blocks)
    qmap = lambda n, i: (0, n * n_blocks + i)
    in_specs = [
        pl.BlockSpec((512, LANES), qmap),
        pl.BlockSpec((512, LANES), qmap),
        pl.BlockSpec((8, LANES), qmap),
        pl.BlockSpec((SEQ, 128), lambda n, i: (n, 0)),
        pl.BlockSpec((SEQ, 128), lambda n, i: (n, 0)),
        pl.BlockSpec((n_blocks, 128, KEY_TILE), lambda n, i: (n, 0, 0)),
    ]
    return pl.pallas_call(
        _attn_prompt_kernel, grid=grid, in_specs=in_specs,
        out_specs=pl.BlockSpec((LANES, W_A), lambda n, i: (n * n_blocks + i, 0)),
        out_shape=jax.ShapeDtypeStruct((BATCH * SEQ, W_A), F32),
        scratch_shapes=[pltpu.VMEM((SEQ, LANES), F32), pltpu.VMEM((SEQ, LANES), F32),
                        pltpu.VMEM((4, 128, 2 * LANES), BF16), pltpu.VMEM((4, 128, 2 * LANES), BF16)],
        compiler_params=pltpu.CompilerParams(dimension_semantics=("arbitrary", "arbitrary"),
                                             vmem_limit_bytes=VMEM_LIMIT),
        name="attn_prompt",
    )(qt, qit, wit, kbf, kiw, vt3)


def _attn_sample_kernel(pt_ref, qi_ref, q_ref, w_ref, kin_ref, kn_ref, vn_ref,
                        ck_hbm, cv_hbm, cki_hbm, o_ref,
                        kbuf, vbuf, kibuf, sems, s_ref, bias_ref):
    step = pl.program_id(0)
    n_steps = pl.num_programs(0)
    gsz = SAMPLE_GROUP
    slot = step % 2

    def copies(group, slot_):
        out = []
        for g in range(gsz):
            seq = group * gsz + g
            for p in range(N_PAGES):
                page = pt_ref[seq, p]
                rows = pl.ds(p * PAGE_SIZE, PAGE_SIZE)
                out.append(pltpu.make_async_copy(ck_hbm.at[page], kbuf.at[slot_, g, rows, :], sems.at[slot_, 0]))
                out.append(pltpu.make_async_copy(cv_hbm.at[page], vbuf.at[slot_, g, rows, :], sems.at[slot_, 1]))
                out.append(pltpu.make_async_copy(cki_hbm.at[page], kibuf.at[slot_, g, rows, :], sems.at[slot_, 2]))
        return out

    @pl.when(step == 0)
    def _():
        tail = pl.ds(PAST_LEN + SAMPLE_Q, KEY_TILE - SAMPLE_Q)
        for s_ in range(2):
            for g in range(gsz):
                kbuf[s_, g, tail, :] = jnp.zeros((KEY_TILE - SAMPLE_Q, 128), F32)
                vbuf[s_, g, tail, :] = jnp.zeros((KEY_TILE - SAMPLE_Q, 128), F32)
                kibuf[s_, g, tail, :] = jnp.zeros((KEY_TILE - SAMPLE_Q, IDX_DIM), F32)
        for c in copies(0, 0):
            c.start()

    @pl.when(step + 1 < n_steps)
    def _():
        for c in copies(step + 1, 1 - slot):
            c.start()

    for c in copies(step, slot):
        c.wait()

    new_rows = pl.ds(PAST_LEN, SAMPLE_Q)
    rows8 = lax.broadcasted_iota(jnp.int32, (SAMPLE_Q, SAMPLE_KEYS), 0)
    keyi = lax.broadcasted_iota(jnp.int32, (SAMPLE_Q, SAMPLE_KEYS), 1)
    allowed = keyi <= (PAST_LEN + rows8)
    inf = jnp.float32(jnp.inf)
    row_ok8 = lax.broadcasted_iota(jnp.int32, (SAMPLE_Q, 1), 0) < DEC_SEQ

    for g in range(gsz):
        kbuf[slot, g, new_rows, :] = kn_ref[g]
        vbuf[slot, g, new_rows, :] = vn_ref[g]
        kibuf[slot, g, new_rows, :] = kin_ref[g]
        kib = kibuf[slot, g].astype(BF16)
        s = lax.dot_general(qi_ref[g], kib, (((1,), (1,)), ((), ())),
                            preferred_element_type=F32)
        sw = jnp.maximum(s, 0.0) * w_ref[g][:, 0:1]
        acc = jnp.sum(sw.reshape(N_IDX_HEADS, SAMPLE_Q, SAMPLE_KEYS), axis=0)
        s_ref[g * SAMPLE_Q:(g + 1) * SAMPLE_Q, :] = jnp.where(allowed, acc, -inf)

    rows_all = gsz * SAMPLE_Q
    allowed_all = jnp.concatenate([allowed] * gsz, axis=0)
    row_ok = jnp.concatenate([row_ok8] * gsz, axis=0)
    keyf = lax.broadcasted_iota(jnp.int32, (rows_all, SAMPLE_KEYS), 1).astype(F32)

    sc = s_ref[...]
    lo0 = jnp.min(jnp.where(allowed_all, sc, inf), axis=1, keepdims=True)
    mx1 = jnp.max(sc, axis=1, keepdims=True)
    hi0 = mx1 + (jnp.abs(mx1) * (2.0 ** -20) + 1e-30)
    t8 = lax.broadcasted_iota(jnp.int32, (SAMPLE_Q, 1), 0)
    n_allowed = jnp.concatenate([(PAST_LEN + 1 + t8).astype(F32)] * gsz, axis=0)

    def lane_count(hit):
        return jnp.sum(hit, axis=1, keepdims=True)

    count_ge = lambda thr: lane_count(jnp.where(s_ref[...] >= thr, 1.0, 0.0))
    count_gt = lambda thr: lane_count(jnp.where(s_ref[...] > thr, 1.0, 0.0))
    count_eq_le = lambda thr, jm: lane_count(
        jnp.where(s_ref[...] == thr, jnp.where(keyf <= jm, 1.0, 0.0), 0.0))
    lo, jmax = _select_threshold(count_ge, count_gt, count_eq_le, lo0, hi0, n_allowed,
                                 SAMPLE_KEYS, row_ok)
    sc = s_ref[...]
    tie = jnp.where(sc == lo, jnp.where(keyf <= jmax, 0.0, NEG), NEG)
    bias_ref[...] = jnp.where(sc > lo, 0.0, tie)

    lane = lax.broadcasted_iota(jnp.int32, (SAMPLE_Q, LANES), 1)
    for g in range(gsz):
        kb = kbuf[slot, g].astype(BF16)
        vb = vbuf[slot, g].astype(BF16)
        b = bias_ref[g * SAMPLE_Q:(g + 1) * SAMPLE_Q, :]
        s = lax.dot_general(q_ref[g], kb, (((1,), (1,)), ((), ())),
                            preferred_element_type=F32)
        s = s + jnp.concatenate([b] * N_HEADS, axis=0)
        m = jnp.max(s, axis=1, keepdims=True)
        p = jnp.exp(s - m)
        l = jnp.sum(p, axis=1, keepdims=True)
        o = jnp.dot(p.astype(BF16), vb, preferred_element_type=F32) / l
        o_r = pltpu.roll(o, HEAD_DIM, 1)
        tiles = []
        for c in range(4):
            ha, hb = 2 * c, 2 * c + 1
            src_a = o if ha // 4 == 0 else o_r
            src_b = o if hb // 4 == 1 else o_r
            tiles.append(jnp.where(lane < HEAD_DIM, src_a[ha * SAMPLE_Q:(ha + 1) * SAMPLE_Q, :],
                                   src_b[hb * SAMPLE_Q:(hb + 1) * SAMPLE_Q, :]))
        o_ref[g] = jnp.concatenate(tiles, axis=1)


def _attn_sample(page_table, qi_s, q_s, w_s, ki_new, k_new, v_new, ck, cv, cki):
    gsz = SAMPLE_GROUP
    grid = (DEC_BATCH // gsz,)
    blk = lambda shape: pl.BlockSpec((gsz,) + shape, lambda s, pt: (s, 0, 0))
    grid_spec = pltpu.PrefetchScalarGridSpec(
        num_scalar_prefetch=1, grid=grid,
        in_specs=[blk((64, IDX_DIM)), blk((64, 128)), blk((64, LANES)),
                  blk((SAMPLE_Q, IDX_DIM)), blk((SAMPLE_Q, 128)), blk((SAMPLE_Q, 128)),
                  pl.BlockSpec(memory_space=pl.ANY), pl.BlockSpec(memory_space=pl.ANY),
                  pl.BlockSpec(memory_space=pl.ANY)],
        out_specs=blk((SAMPLE_Q, W_A)),
        scratch_shapes=[pltpu.VMEM((2, gsz, SAMPLE_KEYS, 128), F32),
                        pltpu.VMEM((2, gsz, SAMPLE_KEYS, 128), F32),
                        pltpu.VMEM((2, gsz, SAMPLE_KEYS, IDX_DIM), F32),
                        pltpu.SemaphoreType.DMA((2, 3)),
                        pltpu.VMEM((gsz * SAMPLE_Q, SAMPLE_KEYS), F32),
                        pltpu.VMEM((gsz * SAMPLE_Q, SAMPLE_KEYS), F32)])
    return pl.pallas_call(
        _attn_sample_kernel, grid_spec=grid_spec,
        out_shape=jax.ShapeDtypeStruct((DEC_BATCH, SAMPLE_Q, W_A), F32),
        compiler_params=pltpu.CompilerParams(dimension_semantics=("arbitrary",),
                                             vmem_limit_bytes=VMEM_LIMIT),
        name="attn_sample",
    )(page_table, qi_s, q_s, w_s, ki_new, k_new, v_new, ck, cv, cki)


def _layer_norm(x, g, b):
    mu = jnp.mean(x, axis=-1, keepdims=True)
    xc = x - mu
    var = jnp.mean(xc * xc, axis=-1, keepdims=True)
    return xc * lax.rsqrt(var + LN_EPS) * g + b


def _silu(x):
    return x / (1.0 + jnp.exp(-x))


def _mixer_kernel(x_ref, a_ref, wb_ref, wmix_ref, bsb_ref, lnvg_ref, lnvb_ref, wout_ref,
                  lng_ref, lnb_ref, y_ref, vn_ref, *, chunk, vn_last_only, tiles_per_seq):
    tm = x_ref.shape[0]
    x = x_ref[...]
    zb = jnp.dot(x.astype(BF16), wb_ref[...], preferred_element_type=F32)
    ga = zb[:, 0:512]
    u = zb[:, 512:1024]
    vb = zb[:, 1024:1536]
    gb = zb[:, 1536:2048]
    vn = _layer_norm(vb, lnvg_ref[...], lnvb_ref[...])
    if vn_last_only:
        @pl.when(pl.program_id(0) % tiles_per_seq == tiles_per_seq - 1)
        def _():
            vn_ref[0] = vn[tm - CHUNK:, :]
    else:
        vn_ref[...] = vn
    vnb = vn.astype(BF16)
    rows = []
    for c in range(tm // chunk):
        cols = []
        for g in range(N_GROUPS_B):
            blk = vnb[c * chunk:(c + 1) * chunk, g * GROUP_W_B:(g + 1) * GROUP_W_B]
            cols.append(jnp.dot(wmix_ref[g], blk, preferred_element_type=F32) + bsb_ref[g])
        rows.append(jnp.concatenate(cols, axis=1))
    mixed = jnp.concatenate(rows, axis=0) if len(rows) > 1 else rows[0]
    a_out = a_ref[...] * _silu(ga)
    b_out = u * mixed * _silu(gb)
    cat = jnp.concatenate([a_out, b_out], axis=1).astype(BF16)
    out = jnp.dot(cat, wout_ref[...], preferred_element_type=F32)
    y_ref[...] = _layer_norm(ALPHA * x + out, lng_ref[...], lnb_ref[...])


def _mixer(x2d, attn, wb, wmix, bsb, lnvg, lnvb, wout, lng, lnb, *, chunk, vn_last_only):
    rows = x2d.shape[0]
    tm = ROW_TILE
    tiles_per_seq = SEQ // tm
    grid = (rows // tm,)
    row_map = lambda r: (r, 0)
    const2 = lambda r: (0, 0)
    const3 = lambda r: (0, 0, 0)
    if vn_last_only:
        vn_shape = jax.ShapeDtypeStruct((rows // SEQ, CHUNK, W_B), F32)
        vn_spec = pl.BlockSpec((1, CHUNK, W_B), lambda r: (r // tiles_per_seq, 0, 0))
    else:
        vn_shape = jax.ShapeDtypeStruct((rows, W_B), F32)
        vn_spec = pl.BlockSpec((tm, W_B), row_map)
    kern = functools.partial(_mixer_kernel, chunk=chunk, vn_last_only=vn_last_only,
                             tiles_per_seq=tiles_per_seq)
    return pl.pallas_call(
        kern, grid=grid,
        in_specs=[pl.BlockSpec((tm, D_MODEL), row_map), pl.BlockSpec((tm, W_A), row_map),
                  pl.BlockSpec(wb.shape, const2), pl.BlockSpec(wmix.shape, const3),
                  pl.BlockSpec(bsb.shape, const3), pl.BlockSpec((1, W_B), const2),
                  pl.BlockSpec((1, W_B), const2), pl.BlockSpec(wout.shape, const2),
                  pl.BlockSpec((1, D_MODEL), const2), pl.BlockSpec((1, D_MODEL), const2)],
        out_specs=(pl.BlockSpec((tm, D_MODEL), row_map), vn_spec),
        out_shape=(jax.ShapeDtypeStruct((rows, D_MODEL), F32), vn_shape),
        compiler_params=pltpu.CompilerParams(dimension_semantics=("arbitrary",),
                                             vmem_limit_bytes=VMEM_LIMIT),
        name="mixer_chunk%d" % chunk,
    )(x2d, attn, wb, wmix, bsb, lnvg, lnvb, wout, lng, lnb)


def kernel(x_prompt, x_sample, cache_k, cache_v, cache_k_idx, page_table, w_in, w_s, b_s,
           ln_v_g, ln_v_b, w_out, ln_g, ln_b):
    w = w_in[0]
    wq, wk, wv = w[:, 0:512], w[:, 512:640], w[:, 640:768]
    wga, wqi, wki, wwi = w[:, 768:1280], w[:, 1280:1792], w[:, 1792:1856], w[:, 1856:1864]
    wu, wvb, wgb = w[:, 1864:2376], w[:, 2376:2888], w[:, 2888:3400]
    wstd = jnp.concatenate([wk, wv, wki, jnp.zeros((D_MODEL, 64), F32)], axis=1).astype(BF16)
    wt = jnp.concatenate([wq * (HEAD_DIM ** -0.5), wqi, wv, wwi,
                          jnp.zeros((D_MODEL, WT_ROWS - 1160), F32)], axis=1).T.astype(BF16)
    wb = jnp.concatenate([wga, wu, wvb, wgb], axis=1).astype(BF16)
    wout = w_out[0].astype(BF16)
    lnvg, lnvb = ln_v_g[0][None, :], ln_v_b[0][None, :]
    lng, lnb = ln_g[0][None, :], ln_b[0][None, :]

    xp = x_prompt.reshape(BATCH * SEQ, D_MODEL)
    tabs_p = _rope_tables(jnp.arange(SEQ, dtype=jnp.int32))
    k_p, v_p, ki_p, kbf, kiw, qt, qit, vt3, wit = _project(xp, wstd, wt, *tabs_p, SEQ // ROW_TILE)
    attn_p = _attn_prompt(qt, qit, wit, kbf, kiw, vt3)
    tril = jnp.tril(jnp.ones((CHUNK, CHUNK), F32))
    wmix_p = (w_s[0] * tril[None]).astype(BF16)
    bsb_p = jnp.broadcast_to(b_s[0][:, :, None], (N_GROUPS_B, CHUNK, GROUP_W_B))
    y_p, vn_p = _mixer(xp, attn_p, wb, wmix_p, bsb_p, lnvg, lnvb, wout, lng, lnb,
                       chunk=CHUNK, vn_last_only=True)

    rows_s = DEC_BATCH * DEC_SEQ
    xs = x_sample.reshape(rows_s, D_MODEL)
    pos_s = PAST_LEN + (jnp.arange(rows_s, dtype=jnp.int32) % DEC_SEQ)
    tabs_s = _rope_tables(pos_s)
    k_s, v_s, ki_s, _, _, qt_s, qit_s, _, wit_s = _project(xs, wstd, wt, *tabs_s, 1)

    def to_rows(a_t):
        a = a_t.reshape(N_HEADS, HEAD_DIM, DEC_BATCH, DEC_SEQ).transpose(2, 0, 3, 1)
        return jnp.pad(a, ((0, 0), (0, 0), (0, SAMPLE_Q - DEC_SEQ), (0, 0)))

    qi_rows = to_rows(qit_s).reshape(DEC_BATCH, N_IDX_HEADS * SAMPLE_Q, IDX_DIM)
    q5 = to_rows(qt_s).reshape(DEC_BATCH, N_KV, N_HEADS // N_KV, SAMPLE_Q, HEAD_DIM)
    zq = jnp.zeros_like(q5[:, 0])
    q_rows = jnp.stack([jnp.concatenate([q5[:, 0], zq], axis=-1),
                        jnp.concatenate([zq, q5[:, 1]], axis=-1)], axis=1)
    q_rows = q_rows.reshape(DEC_BATCH, N_HEADS * SAMPLE_Q, 2 * HEAD_DIM)
    w_rows = jnp.pad(wit_s.reshape(N_IDX_HEADS, DEC_BATCH, DEC_SEQ).transpose(1, 0, 2),
                     ((0, 0), (0, 0), (0, SAMPLE_Q - DEC_SEQ)))
    w_rows = jnp.broadcast_to(w_rows.reshape(DEC_BATCH, N_IDX_HEADS * SAMPLE_Q, 1),
                              (DEC_BATCH, N_IDX_HEADS * SAMPLE_Q, LANES))
    pad_new = lambda a: jnp.pad(a.reshape(DEC_BATCH, DEC_SEQ, a.shape[-1]),
                                ((0, 0), (0, SAMPLE_Q - DEC_SEQ), (0, 0)))
    n_pool = cache_k.shape[1]
    attn_s8 = _attn_sample(page_table, qi_rows, q_rows, w_rows, pad_new(ki_s), pad_new(k_s),
                           pad_new(v_s), cache_k[0].reshape(n_pool, PAGE_SIZE, 128),
                           cache_v[0].reshape(n_pool, PAGE_SIZE, 128), cache_k_idx[0])
    attn_s = attn_s8[:, :DEC_SEQ, :].reshape(rows_s, W_A)
    tril4 = jnp.tril(jnp.ones((DEC_SEQ, DEC_SEQ), F32))
    eye = jnp.eye(DEC_BATCH, dtype=F32)
    wmix_s = jnp.stack([jnp.kron(eye, w_s[0, g, :DEC_SEQ, :DEC_SEQ] * tril4)
                        for g in range(N_GROUPS_B)]).astype(BF16)
    bsb_s = jnp.broadcast_to(jnp.tile(b_s[0][:, :DEC_SEQ], (1, DEC_BATCH))[:, :, None],
                             (N_GROUPS_B, rows_s, GROUP_W_B))
    y_s, vn_s = _mixer(xs, attn_s, wb, wmix_s, bsb_s, lnvg, lnvb, wout, lng, lnb,
                       chunk=rows_s, vn_last_only=False)

    return (y_p.reshape(BATCH, SEQ, D_MODEL),
            y_s.reshape(DEC_BATCH, DEC_SEQ, D_MODEL),
            k_p.reshape(1, BATCH, SEQ, N_KV, HEAD_DIM),
            v_p.reshape(1, BATCH, SEQ, N_KV, HEAD_DIM),
            ki_p.reshape(1, BATCH, SEQ, IDX_DIM),
            vn_p.reshape(1, BATCH, CHUNK, W_B),
            k_s.reshape(1, DEC_BATCH, DEC_SEQ, N_KV, HEAD_DIM),
            v_s.reshape(1, DEC_BATCH, DEC_SEQ, N_KV, HEAD_DIM),
            ki_s.reshape(1, DEC_BATCH, DEC_SEQ, IDX_DIM),
            vn_s.reshape(1, DEC_BATCH, DEC_SEQ, W_B))
```

```python
import functools

import jax
import jax.numpy as jnp
from jax import lax
from jax.experimental import pallas as pl
from jax.experimental.pallas import tpu as pltpu

F32 = jnp.float32
BF16 = jnp.bfloat16

D_MODEL = 1024
SEQ = 2048
BATCH = 8
DEC_BATCH = 128
DEC_SEQ = 4
PAST_LEN = 2048
PAGE_SIZE = 128
N_PAGES = PAST_LEN // PAGE_SIZE
W_A = 512
W_B = 512
HEAD_DIM = 64
N_HEADS = 8
N_KV = 2
N_IDX_HEADS = 8
IDX_DIM = 64
TOPK = 256
CHUNK = 128
N_GROUPS_B = 4
GROUP_W_B = 128
ROPE_THETA = 10000.0
LN_EPS = 1e-5
ALPHA = 2.0 ** 0.25
NEG = -1e30
IDX_SCALE = float((N_IDX_HEADS * IDX_DIM) ** -0.5)

LANES = 128
KEY_TILE = 128
KEY_STEP = 512
ROW_TILE = 512
WT_ROWS = 1168
BISECT_ITERS = 24
VMEM_LIMIT = 48 * 1024 * 1024

SAMPLE_GROUP = 4
SAMPLE_KEYS = PAST_LEN + KEY_TILE
SAMPLE_Q = 8


def _proj_kernel(x_ref, wstd_ref, wt_ref, cosl_ref, sinl_ref, cost_ref, sint_ref,
                 k_ref, v_ref, ki_ref, kbf_ref, kiw_ref, qt_ref, qit_ref, vt_ref, wit_ref):
    tm = x_ref.shape[0]
    xb = x_ref[...].astype(BF16)
    zs = jnp.dot(xb, wstd_ref[...], preferred_element_type=F32)
    cosl = cosl_ref[...]
    sinl = sinl_ref[...]
    lane = lax.broadcasted_iota(jnp.int32, (tm, LANES), 1)
    first_half = (lane % HEAD_DIM) < (HEAD_DIM // 2)

    def rope_lanes(z):
        partner = jnp.where(first_half, pltpu.roll(z, 96, 1), pltpu.roll(z, 32, 1))
        return z * cosl + partner * sinl

    k = rope_lanes(zs[:, 0:128])
    v = zs[:, 128:256]
    kiw = rope_lanes(zs[:, 256:384])
    k_ref[...] = k
    v_ref[...] = v
    ki_ref[...] = kiw[:, 0:IDX_DIM]
    kbf_ref[...] = k.astype(BF16)
    kiw_ref[...] = kiw.astype(BF16)

    zt = lax.dot_general(wt_ref[...], xb, (((1,), (1,)), ((), ())),
                         preferred_element_type=F32)
    cost = cost_ref[...]
    sint = sint_ref[...]
    half = HEAD_DIM // 2

    def rope_rows(base):
        pieces = []
        for h in range(N_HEADS):
            x1 = zt[base + h * HEAD_DIM: base + h * HEAD_DIM + half]
            x2 = zt[base + h * HEAD_DIM + half: base + (h + 1) * HEAD_DIM]
            pieces.append(x1 * cost - x2 * sint)
            pieces.append(x2 * cost + x1 * sint)
        return jnp.concatenate(pieces, axis=0)

    qt_ref[...] = rope_rows(0).astype(BF16)
    qit_ref[...] = rope_rows(512).astype(BF16)
    vtb = zt[1024:1152].astype(BF16)
    for c in range(tm // KEY_STEP):
        vt_ref[c] = vtb[:, c * KEY_STEP:(c + 1) * KEY_STEP]
    wit_ref[...] = zt[1152:1160] * IDX_SCALE


def _project(x2d, wstd, wt, cosl, sinl, cost, sint, n_pos_tiles):
    rows = x2d.shape[0]
    tm = ROW_TILE
    grid = (rows // tm,)
    row_map = lambda r: (r, 0)
    col_map = lambda r: (0, r)
    const = lambda r: (0, 0)
    out_shape = (
        jax.ShapeDtypeStruct((rows, 128), F32),
        jax.ShapeDtypeStruct((rows, 128), F32),
        jax.ShapeDtypeStruct((rows, IDX_DIM), F32),
        jax.ShapeDtypeStruct((rows, 128), BF16),
        jax.ShapeDtypeStruct((rows, 128), BF16),
        jax.ShapeDtypeStruct((512, rows), BF16),
        jax.ShapeDtypeStruct((512, rows), BF16),
        jax.ShapeDtypeStruct((rows // KEY_STEP, 128, KEY_STEP), BF16),
        jax.ShapeDtypeStruct((8, rows), F32),
    )
    out_specs = (
        pl.BlockSpec((tm, 128), row_map),
        pl.BlockSpec((tm, 128), row_map),
        pl.BlockSpec((tm, IDX_DIM), row_map),
        pl.BlockSpec((tm, 128), row_map),
        pl.BlockSpec((tm, 128), row_map),
        pl.BlockSpec((512, tm), col_map),
        pl.BlockSpec((512, tm), col_map),
        pl.BlockSpec((tm // KEY_STEP, 128, KEY_STEP), lambda r: (r, 0, 0)),
        pl.BlockSpec((8, tm), col_map),
    )
    in_specs = [
        pl.BlockSpec((tm, D_MODEL), row_map),
        pl.BlockSpec(wstd.shape, const),
        pl.BlockSpec(wt.shape, const),
        pl.BlockSpec((tm, LANES), lambda r: (r % n_pos_tiles, 0)),
        pl.BlockSpec((tm, LANES), lambda r: (r % n_pos_tiles, 0)),
        pl.BlockSpec((HEAD_DIM // 2, tm), lambda r: (0, r % n_pos_tiles)),
        pl.BlockSpec((HEAD_DIM // 2, tm), lambda r: (0, r % n_pos_tiles)),
    ]
    return pl.pallas_call(
        _proj_kernel, grid=grid, in_specs=in_specs, out_specs=out_specs, out_shape=out_shape,
        compiler_params=pltpu.CompilerParams(dimension_semantics=("arbitrary",),
                                             vmem_limit_bytes=VMEM_LIMIT),
        name="proj_attn_side",
    )(x2d, wstd, wt, cosl, sinl, cost, sint)


def _rope_tables(pos):
    half = HEAD_DIM // 2
    inv = ROPE_THETA ** (-jnp.arange(half, dtype=F32) / half)
    ang = pos.astype(F32)[:, None] * inv[None, :]
    cos = jnp.cos(ang)
    sin = jnp.sin(ang)
    cosl = jnp.tile(cos, (1, LANES // half))
    sinl = jnp.tile(jnp.concatenate([-sin, sin], axis=1), (1, LANES // HEAD_DIM))
    return cosl, sinl, cos.T, sin.T


def _select_threshold(count_ge, count_gt, count_eq_le, min_ge, max_lt, lo0, hi0, n_allowed, n_keys,
                      row_ok):
    k = float(TOPK)

    def step(carry):
        lo, hi, clo = carry
        mid = 0.5 * lo + 0.5 * hi
        c = count_ge(mid)
        take = c >= k
        return jnp.where(take, mid, lo), jnp.where(take, hi, mid), jnp.where(take, c, clo)

    lo, hi, clo = lax.fori_loop(0, BISECT_ITERS, lambda _, c: step(c), (lo0, hi0, n_allowed))
    big = jnp.full_like(lo, float(4 * n_keys))

    def unresolved(clo_):
        return jnp.logical_and(clo_ > k, row_ok)

    def any_true(mask):
        return jnp.max(jnp.where(mask, 1.0, 0.0)) > 0.5

    def exact(args):
        lo, hi, clo = args

        def cond(c):
            _, _, clo, vlo, vhi, it = c
            open_ = jnp.logical_and(unresolved(clo), vlo < vhi)
            return jnp.logical_and(any_true(open_), it < 2 * n_keys)

        def body(c):
            lo, hi, clo, vlo, vhi, it = c
            mid = 0.5 * vlo + 0.5 * vhi
            mid = jnp.where(mid > vlo, mid, vhi)
            cnt = count_ge(mid)
            take = cnt >= k
            lo, hi, clo = jnp.where(take, mid, lo), jnp.where(take, hi, mid), jnp.where(take, cnt, clo)
            return lo, hi, clo, min_ge(lo), max_lt(hi), it + 1

        lo, hi, clo, vlo, _, _ = lax.while_loop(
            cond, body, (lo, hi, clo, min_ge(lo), max_lt(hi), jnp.int32(0)))
        lo = vlo
        need = k - count_gt(lo)

        def jstep(_, c):
            jlo, jhi = c
            mid = jnp.floor(0.5 * (jlo + jhi))
            ok = count_eq_le(lo, mid) >= need
            return jnp.where(ok, jlo, mid), jnp.where(ok, mid, jhi)

        jlo0 = jnp.full_like(lo, -1.0)
        jhi0 = jnp.full_like(lo, float(n_keys - 1))
        _, jhi = lax.fori_loop(0, 13, jstep, (jlo0, jhi0))
        return lo, jnp.where(clo > k, jhi, big)

    def fast(args):
        return args[0], big

    return lax.cond(any_true(unresolved(clo)), exact, fast, (lo, hi, clo))


def _fold_rows(x, op):
    parts = [x[r:r + 8] for r in range(0, x.shape[0], 8)]
    while len(parts) > 1:
        nxt = [op(parts[a], parts[a + 1]) for a in range(0, len(parts) - 1, 2)]
        if len(parts) % 2:
            nxt.append(parts[-1])
        parts = nxt
    return parts[0]


def _attn_prompt_kernel(qt_ref, qit_ref, wit_ref, kbf_ref, kiw_ref, vt_ref, o_ref,
                        s_ref, sc_ref, acc_ref, ri_ref, rq_ref):
    i = pl.program_id(1)
    w = wit_ref[...]
    zeros_half = jnp.zeros((HEAD_DIM, LANES), BF16)

    for hp in range(4):
        a = qit_ref[(2 * hp) * 64:(2 * hp + 1) * 64, :]
        b = qit_ref[(2 * hp + 1) * 64:(2 * hp + 2) * 64, :]
        ri_ref[hp] = jnp.concatenate(
            [jnp.concatenate([a, zeros_half], axis=0), jnp.concatenate([b, zeros_half], axis=0)], axis=1)
        g = hp // 2
        a = qt_ref[(2 * hp) * 64:(2 * hp + 1) * 64, :]
        b = qt_ref[(2 * hp + 1) * 64:(2 * hp + 2) * 64, :]
        if g == 0:
            ab = jnp.concatenate([jnp.concatenate([a, zeros_half], axis=0),
                                  jnp.concatenate([b, zeros_half], axis=0)], axis=1)
        else:
            ab = jnp.concatenate([jnp.concatenate([zeros_half, a], axis=0),
                                  jnp.concatenate([zeros_half, b], axis=0)], axis=1)
        rq_ref[hp] = ab

    row = lax.broadcasted_iota(jnp.int32, (KEY_STEP, LANES), 0)
    col = lax.broadcasted_iota(jnp.int32, (KEY_STEP, LANES), 1)
    qpos = col + i * LANES
    inf = jnp.float32(jnp.inf)
    groups = KEY_STEP // 8
    n_steps = (i * LANES + LANES + KEY_STEP - 1) // KEY_STEP

    def key_rows(j):
        return pl.ds(pl.multiple_of(j * KEY_STEP, KEY_STEP), KEY_STEP)

    def idx_body(j, carry):
        mn, mx = carry
        kt = kiw_ref[key_rows(j), :]
        acc = jnp.zeros((KEY_STEP, LANES), F32)
        for hp in range(4):
            s2 = jnp.dot(kt, ri_ref[hp], preferred_element_type=F32)
            acc = acc + jnp.maximum(s2[:, :LANES], 0.0) * w[2 * hp:2 * hp + 1, :]
            acc = acc + jnp.maximum(s2[:, LANES:], 0.0) * w[2 * hp + 1:2 * hp + 2, :]
        allowed = (row + j * KEY_STEP) <= qpos
        s_ref[key_rows(j), :] = jnp.where(allowed, acc, -inf)
        lo_part = _fold_rows(jnp.where(allowed, acc, inf), jnp.minimum)
        hi_part = _fold_rows(jnp.where(allowed, acc, -inf), jnp.maximum)
        return jnp.minimum(mn, lo_part), jnp.maximum(mx, hi_part)

    mn, mx = lax.fori_loop(0, n_steps, idx_body, (jnp.full((8, LANES), inf, F32),
                                                  jnp.full((8, LANES), -inf, F32)))
    lo0 = jnp.min(mn, axis=0, keepdims=True)
    mx1 = jnp.max(mx, axis=0, keepdims=True)
    hi0 = mx1 + (jnp.abs(mx1) * (2.0 ** -20) + 1e-30)
    n_allowed = (qpos[0:1, :] + 1).astype(F32)

    def key_reduce(fn, op, init, finish):
        def body(j, r8):
            return op(r8, _fold_rows(fn(s_ref[key_rows(j), :], j), op))
        r8 = lax.fori_loop(0, n_steps, body, jnp.full((8, LANES), init, F32))
        return finish(r8, axis=0, keepdims=True)

    def kidx(j):
        return (row + j * KEY_STEP).astype(F32)

    def tile_count(fn):
        return key_reduce(fn, jnp.add, 0.0, jnp.sum)

    count_ge = lambda thr: tile_count(lambda t, j: jnp.where(t >= thr, 1.0, 0.0))
    count_gt = lambda thr: tile_count(lambda t, j: jnp.where(t > thr, 1.0, 0.0))
    count_eq_le = lambda thr, jm: tile_count(
        lambda t, j: jnp.where(t == thr, jnp.where(kidx(j) <= jm, 1.0, 0.0), 0.0))
    min_ge = lambda thr: key_reduce(lambda t, j: jnp.where(t >= thr, t, inf), jnp.minimum, inf, jnp.min)
    max_lt = lambda thr: key_reduce(lambda t, j: jnp.where(t < thr, t, -inf), jnp.maximum, -inf, jnp.max)

    lo, jmax = _select_threshold(count_ge, count_gt, count_eq_le, min_ge, max_lt, lo0, hi0,
                                 n_allowed, SEQ, jnp.full((1, LANES), True))

    def bias_body(j, _):
        t = s_ref[key_rows(j), :]
        tie = jnp.where(t == lo, jnp.where(kidx(j) <= jmax, 0.0, NEG), NEG)
        s_ref[key_rows(j), :] = jnp.where(t > lo, 0.0, tie)
        return 0

    lax.fori_loop(0, n_steps, bias_body, 0)

    def score_body(j, m8s):
        kt = kbf_ref[key_rows(j), :]
        b = s_ref[key_rows(j), :]
        b2 = jnp.concatenate([b, b], axis=1)
        out = []
        for hp in range(4):
            s2 = jnp.dot(kt, rq_ref[hp], preferred_element_type=F32) + b2
            sc_ref[hp, key_rows(j), :] = s2
            out.append(jnp.maximum(m8s[hp], _fold_rows(s2, jnp.maximum)))
        return tuple(out)

    m8s = lax.fori_loop(0, n_steps, score_body,
                        tuple(jnp.full((8, 2 * LANES), -inf, F32) for _ in range(4)))
    ms = [jnp.max(m8, axis=0, keepdims=True) for m8 in m8s]
    acc_ref[...] = jnp.zeros(acc_ref.shape, F32)

    def pv_body(j, l8s):
        vt = vt_ref[j]
        out = []
        for hp in range(4):
            g = hp // 2
            p = jnp.exp(sc_ref[hp, key_rows(j), :] - ms[hp])
            acc_ref[hp] += jnp.dot(vt[g * 64:(g + 1) * 64, :], p.astype(BF16),
                                   preferred_element_type=F32)
            out.append(l8s[hp] + _fold_rows(p, jnp.add))
        return tuple(out)

    l8s = lax.fori_loop(0, n_steps, pv_body,
                        tuple(jnp.zeros((8, 2 * LANES), F32) for _ in range(4)))
    pieces = []
    for hp in range(4):
        o2 = acc_ref[hp] / jnp.sum(l8s[hp], axis=0, keepdims=True)
        pieces.append(o2[:, :LANES])
        pieces.append(o2[:, LANES:])
    o_ref[...] = jnp.concatenate(pieces, axis=0).T


def _attn_prompt(qt, qit, wit, kbf, kiw, vt3):
    n_blocks = SEQ // LANES
    grid = (BATCH, n_blocks)
    qmap = lambda n, i: (0, n * n_blocks + i)
    in_specs = [
        pl.BlockSpec((512, LANES), qmap),
        pl.BlockSpec((512, LANES), qmap),
        pl.BlockSpec((8, LANES), qmap),
        pl.BlockSpec((SEQ, 128), lambda n, i: (n, 0)),
        pl.BlockSpec((SEQ, 128), lambda n, i: (n, 0)),
        pl.BlockSpec((SEQ // KEY_STEP, 128, KEY_STEP), lambda n, i: (n, 0, 0)),
    ]
    return pl.pallas_call(
        _attn_prompt_kernel, grid=grid, in_specs=in_specs,
        out_specs=pl.BlockSpec((LANES, W_A), lambda n, i: (n * n_blocks + i, 0)),
        out_shape=jax.ShapeDtypeStruct((BATCH * SEQ, W_A), F32),
        scratch_shapes=[pltpu.VMEM((SEQ, LANES), F32), pltpu.VMEM((4, SEQ, 2 * LANES), F32),
                        pltpu.VMEM((4, HEAD_DIM, 2 * LANES), F32),
                        pltpu.VMEM((4, 128, 2 * LANES), BF16), pltpu.VMEM((4, 128, 2 * LANES), BF16)],
        compiler_params=pltpu.CompilerParams(dimension_semantics=("arbitrary", "arbitrary"),
                                             vmem_limit_bytes=VMEM_LIMIT),
        name="attn_prompt",
    )(qt, qit, wit, kbf, kiw, vt3)


def _attn_sample_kernel(pt_ref, qi_ref, q_ref, w_ref, kin_ref, kn_ref, vn_ref,
                        ck_hbm, cv_hbm, cki_hbm, o_ref,
                        kbuf, vbuf, kibuf, sems, s_ref, bias_ref):
    step = pl.program_id(0)
    n_steps = pl.num_programs(0)
    gsz = SAMPLE_GROUP
    slot = step % 2

    def copies(group, slot_):
        out = []
        for g in range(gsz):
            seq = group * gsz + g
            for p in range(N_PAGES):
                page = pt_ref[seq, p]
                rows = pl.ds(p * PAGE_SIZE, PAGE_SIZE)
                out.append(pltpu.make_async_copy(ck_hbm.at[page], kbuf.at[slot_, g, rows, :], sems.at[slot_, 0]))
                out.append(pltpu.make_async_copy(cv_hbm.at[page], vbuf.at[slot_, g, rows, :], sems.at[slot_, 1]))
                out.append(pltpu.make_async_copy(cki_hbm.at[page], kibuf.at[slot_, g, rows, :], sems.at[slot_, 2]))
        return out

    @pl.when(step == 0)
    def _():
        tail = pl.ds(PAST_LEN + SAMPLE_Q, KEY_TILE - SAMPLE_Q)
        for s_ in range(2):
            for g in range(gsz):
                kbuf[s_, g, tail, :] = jnp.zeros((KEY_TILE - SAMPLE_Q, 128), F32)
                vbuf[s_, g, tail, :] = jnp.zeros((KEY_TILE - SAMPLE_Q, 128), F32)
                kibuf[s_, g, tail, :] = jnp.zeros((KEY_TILE - SAMPLE_Q, IDX_DIM), F32)
        for c in copies(0, 0):
            c.start()

    @pl.when(step + 1 < n_steps)
    def _():
        for c in copies(step + 1, 1 - slot):
            c.start()

    for c in copies(step, slot):
        c.wait()

    new_rows = pl.ds(PAST_LEN, SAMPLE_Q)
    rows8 = lax.broadcasted_iota(jnp.int32, (SAMPLE_Q, SAMPLE_KEYS), 0)
    keyi = lax.broadcasted_iota(jnp.int32, (SAMPLE_Q, SAMPLE_KEYS), 1)
    allowed = keyi <= (PAST_LEN + rows8)
    inf = jnp.float32(jnp.inf)
    row_ok8 = lax.broadcasted_iota(jnp.int32, (SAMPLE_Q, 1), 0) < DEC_SEQ

    for g in range(gsz):
        kbuf[slot, g, new_rows, :] = kn_ref[g]
        vbuf[slot, g, new_rows, :] = vn_ref[g]
        kibuf[slot, g, new_rows, :] = kin_ref[g]
        kib = kibuf[slot, g].astype(BF16)
        s = lax.dot_general(qi_ref[g], kib, (((1,), (1,)), ((), ())),
                            preferred_element_type=F32)
        sw = jnp.maximum(s, 0.0) * w_ref[g][:, 0:1]
        acc = jnp.sum(sw.reshape(N_IDX_HEADS, SAMPLE_Q, SAMPLE_KEYS), axis=0)
        s_ref[g * SAMPLE_Q:(g + 1) * SAMPLE_Q, :] = jnp.where(allowed, acc, -inf)

    rows_all = gsz * SAMPLE_Q
    allowed_all = jnp.concatenate([allowed] * gsz, axis=0)
    row_ok = jnp.concatenate([row_ok8] * gsz, axis=0)
    keyf = lax.broadcasted_iota(jnp.int32, (rows_all, SAMPLE_KEYS), 1).astype(F32)

    sc = s_ref[...]
    lo0 = jnp.min(jnp.where(allowed_all, sc, inf), axis=1, keepdims=True)
    mx1 = jnp.max(sc, axis=1, keepdims=True)
    hi0 = mx1 + (jnp.abs(mx1) * (2.0 ** -20) + 1e-30)
    t8 = lax.broadcasted_iota(jnp.int32, (SAMPLE_Q, 1), 0)
    n_allowed = jnp.concatenate([(PAST_LEN + 1 + t8).astype(F32)] * gsz, axis=0)

    def lane_count(hit):
        return jnp.sum(hit, axis=1, keepdims=True)

    count_ge = lambda thr: lane_count(jnp.where(s_ref[...] >= thr, 1.0, 0.0))
    count_gt = lambda thr: lane_count(jnp.where(s_ref[...] > thr, 1.0, 0.0))
    count_eq_le = lambda thr, jm: lane_count(
        jnp.where(s_ref[...] == thr, jnp.where(keyf <= jm, 1.0, 0.0), 0.0))
    min_ge = lambda thr: jnp.min(jnp.where(s_ref[...] >= thr, s_ref[...], inf), axis=1, keepdims=True)
    max_lt = lambda thr: jnp.max(jnp.where(s_ref[...] < thr, s_ref[...], -inf), axis=1, keepdims=True)
    lo, jmax = _select_threshold(count_ge, count_gt, count_eq_le, min_ge, max_lt, lo0, hi0,
                                 n_allowed, SAMPLE_KEYS, row_ok)
    sc = s_ref[...]
    tie = jnp.where(sc == lo, jnp.where(keyf <= jmax, 0.0, NEG), NEG)
    bias_ref[...] = jnp.where(sc > lo, 0.0, tie)

    lane = lax.broadcasted_iota(jnp.int32, (SAMPLE_Q, LANES), 1)
    for g in range(gsz):
        kb = kbuf[slot, g].astype(BF16)
        vb = vbuf[slot, g].astype(BF16)
        b = bias_ref[g * SAMPLE_Q:(g + 1) * SAMPLE_Q, :]
        s = lax.dot_general(q_ref[g], kb, (((1,), (1,)), ((), ())),
                            preferred_element_type=F32)
        s = s + jnp.concatenate([b] * N_HEADS, axis=0)
        m = jnp.max(s, axis=1, keepdims=True)
        p = jnp.exp(s - m)
        l = jnp.sum(p, axis=1, keepdims=True)
        o = jnp.dot(p.astype(BF16), vb, preferred_element_type=F32) / l
        o_r = pltpu.roll(o, HEAD_DIM, 1)
        tiles = []
        for c in range(4):
            ha, hb = 2 * c, 2 * c + 1
            src_a = o if ha // 4 == 0 else o_r
            src_b = o if hb // 4 == 1 else o_r
            tiles.append(jnp.where(lane < HEAD_DIM, src_a[ha * SAMPLE_Q:(ha + 1) * SAMPLE_Q, :],
                                   src_b[hb * SAMPLE_Q:(hb + 1) * SAMPLE_Q, :]))
        o_ref[g] = jnp.concatenate(tiles, axis=1)


def _attn_sample(page_table, qi_s, q_s, w_s, ki_new, k_new, v_new, ck, cv, cki):
    gsz = SAMPLE_GROUP
    grid = (DEC_BATCH // gsz,)
    blk = lambda shape: pl.BlockSpec((gsz,) + shape, lambda s, pt: (s, 0, 0))
    grid_spec = pltpu.PrefetchScalarGridSpec(
        num_scalar_prefetch=1, grid=grid,
        in_specs=[blk((64, IDX_DIM)), blk((64, 128)), blk((64, LANES)),
                  blk((SAMPLE_Q, IDX_DIM)), blk((SAMPLE_Q, 128)), blk((SAMPLE_Q, 128)),
                  pl.BlockSpec(memory_space=pl.ANY), pl.BlockSpec(memory_space=pl.ANY),
                  pl.BlockSpec(memory_space=pl.ANY)],
        out_specs=blk((SAMPLE_Q, W_A)),
        scratch_shapes=[pltpu.VMEM((2, gsz, SAMPLE_KEYS, 128), F32),
                        pltpu.VMEM((2, gsz, SAMPLE_KEYS, 128), F32),
                        pltpu.VMEM((2, gsz, SAMPLE_KEYS, IDX_DIM), F32),
                        pltpu.SemaphoreType.DMA((2, 3)),
                        pltpu.VMEM((gsz * SAMPLE_Q, SAMPLE_KEYS), F32),
                        pltpu.VMEM((gsz * SAMPLE_Q, SAMPLE_KEYS), F32)])
    return pl.pallas_call(
        _attn_sample_kernel, grid_spec=grid_spec,
        out_shape=jax.ShapeDtypeStruct((DEC_BATCH, SAMPLE_Q, W_A), F32),
        compiler_params=pltpu.CompilerParams(dimension_semantics=("arbitrary",),
                                             vmem_limit_bytes=VMEM_LIMIT),
        name="attn_sample",
    )(page_table, qi_s, q_s, w_s, ki_new, k_new, v_new, ck, cv, cki)


def _layer_norm(x, g, b):
    mu = jnp.mean(x, axis=-1, keepdims=True)
    xc = x - mu
    var = jnp.mean(xc * xc, axis=-1, keepdims=True)
    return xc * lax.rsqrt(var + LN_EPS) * g + b


def _silu(x):
    return x / (1.0 + jnp.exp(-x))


def _mixer_kernel(x_ref, a_ref, wb_ref, wmix_ref, bsb_ref, lnvg_ref, lnvb_ref, wout_ref,
                  lng_ref, lnb_ref, y_ref, vn_ref, *, chunk, vn_last_only, tiles_per_seq):
    tm = x_ref.shape[0]
    x = x_ref[...]
    zb = jnp.dot(x.astype(BF16), wb_ref[...], preferred_element_type=F32)
    ga = zb[:, 0:512]
    u = zb[:, 512:1024]
    vb = zb[:, 1024:1536]
    gb = zb[:, 1536:2048]
    vn = _layer_norm(vb, lnvg_ref[...], lnvb_ref[...])
    if vn_last_only:
        @pl.when(pl.program_id(0) % tiles_per_seq == tiles_per_seq - 1)
        def _():
            vn_ref[0] = vn[tm - CHUNK:, :]
    else:
        vn_ref[...] = vn
    vnb = vn.astype(BF16)
    rows = []
    for c in range(tm // chunk):
        cols = []
        for g in range(N_GROUPS_B):
            blk = vnb[c * chunk:(c + 1) * chunk, g * GROUP_W_B:(g + 1) * GROUP_W_B]
            cols.append(jnp.dot(wmix_ref[g], blk, preferred_element_type=F32) + bsb_ref[g])
        rows.append(jnp.concatenate(cols, axis=1))
    mixed = jnp.concatenate(rows, axis=0) if len(rows) > 1 else rows[0]
    a_out = a_ref[...] * _silu(ga)
    b_out = u * mixed * _silu(gb)
    cat = jnp.concatenate([a_out, b_out], axis=1).astype(BF16)
    out = jnp.dot(cat, wout_ref[...], preferred_element_type=F32)
    y_ref[...] = _layer_norm(ALPHA * x + out, lng_ref[...], lnb_ref[...])


def _mixer(x2d, attn, wb, wmix, bsb, lnvg, lnvb, wout, lng, lnb, *, chunk, vn_last_only):
    rows = x2d.shape[0]
    tm = ROW_TILE
    tiles_per_seq = SEQ // tm
    grid = (rows // tm,)
    row_map = lambda r: (r, 0)
    const2 = lambda r: (0, 0)
    const3 = lambda r: (0, 0, 0)
    if vn_last_only:
        vn_shape = jax.ShapeDtypeStruct((rows // SEQ, CHUNK, W_B), F32)
        vn_spec = pl.BlockSpec((1, CHUNK, W_B), lambda r: (r // tiles_per_seq, 0, 0))
    else:
        vn_shape = jax.ShapeDtypeStruct((rows, W_B), F32)
        vn_spec = pl.BlockSpec((tm, W_B), row_map)
    kern = functools.partial(_mixer_kernel, chunk=chunk, vn_last_only=vn_last_only,
                             tiles_per_seq=tiles_per_seq)
    return pl.pallas_call(
        kern, grid=grid,
        in_specs=[pl.BlockSpec((tm, D_MODEL), row_map), pl.BlockSpec((tm, W_A), row_map),
                  pl.BlockSpec(wb.shape, const2), pl.BlockSpec(wmix.shape, const3),
                  pl.BlockSpec(bsb.shape, const3), pl.BlockSpec((1, W_B), const2),
                  pl.BlockSpec((1, W_B), const2), pl.BlockSpec(wout.shape, const2),
                  pl.BlockSpec((1, D_MODEL), const2), pl.BlockSpec((1, D_MODEL), const2)],
        out_specs=(pl.BlockSpec((tm, D_MODEL), row_map), vn_spec),
        out_shape=(jax.ShapeDtypeStruct((rows, D_MODEL), F32), vn_shape),
        compiler_params=pltpu.CompilerParams(dimension_semantics=("arbitrary",),
                                             vmem_limit_bytes=VMEM_LIMIT),
        name="mixer_chunk%d" % chunk,
    )(x2d, attn, wb, wmix, bsb, lnvg, lnvb, wout, lng, lnb)


def kernel(x_prompt, x_sample, cache_k, cache_v, cache_k_idx, page_table, w_in, w_s, b_s,
           ln_v_g, ln_v_b, w_out, ln_g, ln_b):
    w = w_in[0]
    wq, wk, wv = w[:, 0:512], w[:, 512:640], w[:, 640:768]
    wga, wqi, wki, wwi = w[:, 768:1280], w[:, 1280:1792], w[:, 1792:1856], w[:, 1856:1864]
    wu, wvb, wgb = w[:, 1864:2376], w[:, 2376:2888], w[:, 2888:3400]
    wstd = jnp.concatenate([wk, wv, wki, jnp.zeros((D_MODEL, 64), F32)], axis=1).astype(BF16)
    wt = jnp.concatenate([wq * (HEAD_DIM ** -0.5), wqi, wv, wwi,
                          jnp.zeros((D_MODEL, WT_ROWS - 1160), F32)], axis=1).T.astype(BF16)
    wb = jnp.concatenate([wga, wu, wvb, wgb], axis=1).astype(BF16)
    wout = w_out[0].astype(BF16)
    lnvg, lnvb = ln_v_g[0][None, :], ln_v_b[0][None, :]
    lng, lnb = ln_g[0][None, :], ln_b[0][None, :]

    xp = x_prompt.reshape(BATCH * SEQ, D_MODEL)
    tabs_p = _rope_tables(jnp.arange(SEQ, dtype=jnp.int32))
    k_p, v_p, ki_p, kbf, kiw, qt, qit, vt3, wit = _project(xp, wstd, wt, *tabs_p, SEQ // ROW_TILE)
    attn_p = _attn_prompt(qt, qit, wit, kbf, kiw, vt3)
    tril = jnp.tril(jnp.ones((CHUNK, CHUNK), F32))
    wmix_p = (w_s[0] * tril[None]).astype(BF16)
    bsb_p = jnp.broadcast_to(b_s[0][:, :, None], (N_GROUPS_B, CHUNK, GROUP_W_B))
    y_p, vn_p = _mixer(xp, attn_p, wb, wmix_p, bsb_p, lnvg, lnvb, wout, lng, lnb,
                       chunk=CHUNK, vn_last_only=True)

    rows_s = DEC_BATCH * DEC_SEQ
    xs = x_sample.reshape(rows_s, D_MODEL)
    pos_s = PAST_LEN + (jnp.arange(rows_s, dtype=jnp.int32) % DEC_SEQ)
    tabs_s = _rope_tables(pos_s)
    k_s, v_s, ki_s, _, _, qt_s, qit_s, _, wit_s = _project(xs, wstd, wt, *tabs_s, 1)

    def to_rows(a_t):
        a = a_t.reshape(N_HEADS, HEAD_DIM, DEC_BATCH, DEC_SEQ).transpose(2, 0, 3, 1)
        return jnp.pad(a, ((0, 0), (0, 0), (0, SAMPLE_Q - DEC_SEQ), (0, 0)))

    qi_rows = to_rows(qit_s).reshape(DEC_BATCH, N_IDX_HEADS * SAMPLE_Q, IDX_DIM)
    q5 = to_rows(qt_s).reshape(DEC_BATCH, N_KV, N_HEADS // N_KV, SAMPLE_Q, HEAD_DIM)
    zq = jnp.zeros_like(q5[:, 0])
    q_rows = jnp.stack([jnp.concatenate([q5[:, 0], zq], axis=-1),
                        jnp.concatenate([zq, q5[:, 1]], axis=-1)], axis=1)
    q_rows = q_rows.reshape(DEC_BATCH, N_HEADS * SAMPLE_Q, 2 * HEAD_DIM)
    w_rows = jnp.pad(wit_s.reshape(N_IDX_HEADS, DEC_BATCH, DEC_SEQ).transpose(1, 0, 2),
                     ((0, 0), (0, 0), (0, SAMPLE_Q - DEC_SEQ)))
    w_rows = jnp.broadcast_to(w_rows.reshape(DEC_BATCH, N_IDX_HEADS * SAMPLE_Q, 1),
                              (DEC_BATCH, N_IDX_HEADS * SAMPLE_Q, LANES))
    pad_new = lambda a: jnp.pad(a.reshape(DEC_BATCH, DEC_SEQ, a.shape[-1]),
                                ((0, 0), (0, SAMPLE_Q - DEC_SEQ), (0, 0)))
    n_pool = cache_k.shape[1]
    attn_s8 = _attn_sample(page_table, qi_rows, q_rows, w_rows, pad_new(ki_s), pad_new(k_s),
                           pad_new(v_s), cache_k[0].reshape(n_pool, PAGE_SIZE, 128),
                           cache_v[0].reshape(n_pool, PAGE_SIZE, 128), cache_k_idx[0])
    attn_s = attn_s8[:, :DEC_SEQ, :].reshape(rows_s, W_A)
    tril4 = jnp.tril(jnp.ones((DEC_SEQ, DEC_SEQ), F32))
    eye = jnp.eye(DEC_BATCH, dtype=F32)
    wmix_s = jnp.stack([jnp.kron(eye, w_s[0, g, :DEC_SEQ, :DEC_SEQ] * tril4)
                        for g in range(N_GROUPS_B)]).astype(BF16)
    bsb_s = jnp.broadcast_to(jnp.tile(b_s[0][:, :DEC_SEQ], (1, DEC_BATCH))[:, :, None],
                             (N_GROUPS_B, rows_s, GROUP_W_B))
    y_s, vn_s = _mixer(xs, attn_s, wb, wmix_s, bsb_s, lnvg, lnvb, wout, lng, lnb,
                       chunk=rows_s, vn_last_only=False)

    return (y_p.reshape(BATCH, SEQ, D_MODEL),
            y_s.reshape(DEC_BATCH, DEC_SEQ, D_MODEL),
            k_p.reshape(1, BATCH, SEQ, N_KV, HEAD_DIM),
            v_p.reshape(1, BATCH, SEQ, N_KV, HEAD_DIM),
            ki_p.reshape(1, BATCH, SEQ, IDX_DIM),
            vn_p.reshape(1, BATCH, CHUNK, W_B),
            k_s.reshape(1, DEC_BATCH, DEC_SEQ, N_KV, HEAD_DIM),
            v_s.reshape(1, DEC_BATCH, DEC_SEQ, N_KV, HEAD_DIM),
            ki_s.reshape(1, DEC_BATCH, DEC_SEQ, IDX_DIM),
            vn_s.reshape(1, DEC_BATCH, DEC_SEQ, W_B))
```

```python
import functools

import jax
import jax.numpy as jnp
from jax import lax
from jax.experimental import pallas as pl
from jax.experimental.pallas import tpu as pltpu

F32 = jnp.float32
BF16 = jnp.bfloat16

D_MODEL = 1024
SEQ = 2048
BATCH = 8
DEC_BATCH = 128
DEC_SEQ = 4
PAST_LEN = 2048
PAGE_SIZE = 128
N_PAGES = PAST_LEN // PAGE_SIZE
W_A = 512
W_B = 512
HEAD_DIM = 64
N_HEADS = 8
N_KV = 2
N_IDX_HEADS = 8
IDX_DIM = 64
TOPK = 256
CHUNK = 128
N_GROUPS_B = 4
GROUP_W_B = 128
ROPE_THETA = 10000.0
LN_EPS = 1e-5
ALPHA = 2.0 ** 0.25
NEG = -1e30
IDX_SCALE = float((N_IDX_HEADS * IDX_DIM) ** -0.5)

LANES = 128
KEY_TILE = 128
KEY_STEP = 512
ROW_TILE = 512
WT_ROWS = 1168
BISECT_ITERS = 24
VMEM_LIMIT = 48 * 1024 * 1024

SAMPLE_GROUP = 4
SAMPLE_KEYS = PAST_LEN + KEY_TILE
SAMPLE_Q = 8


def _proj_kernel(x_ref, wstd_ref, wt_ref, cosl_ref, sinl_ref, cost_ref, sint_ref,
                 k_ref, v_ref, ki_ref, kbf_ref, kiw_ref, qt_ref, qit_ref, vt_ref, wit_ref):
    tm = x_ref.shape[0]
    xb = x_ref[...].astype(BF16)
    zs = jnp.dot(xb, wstd_ref[...], preferred_element_type=F32)
    cosl = cosl_ref[...]
    sinl = sinl_ref[...]
    lane = lax.broadcasted_iota(jnp.int32, (tm, LANES), 1)
    first_half = (lane % HEAD_DIM) < (HEAD_DIM // 2)

    def rope_lanes(z):
        partner = jnp.where(first_half, pltpu.roll(z, 96, 1), pltpu.roll(z, 32, 1))
        return z * cosl + partner * sinl

    k = rope_lanes(zs[:, 0:128])
    v = zs[:, 128:256]
    kiw = rope_lanes(zs[:, 256:384])
    k_ref[...] = k
    v_ref[...] = v
    ki_ref[...] = kiw[:, 0:IDX_DIM]
    kbf_ref[...] = k.astype(BF16)
    kiw_ref[...] = kiw.astype(BF16)

    zt = lax.dot_general(wt_ref[...], xb, (((1,), (1,)), ((), ())),
                         preferred_element_type=F32)
    cost = cost_ref[...]
    sint = sint_ref[...]
    half = HEAD_DIM // 2

    def rope_rows(base):
        pieces = []
        for h in range(N_HEADS):
            x1 = zt[base + h * HEAD_DIM: base + h * HEAD_DIM + half]
            x2 = zt[base + h * HEAD_DIM + half: base + (h + 1) * HEAD_DIM]
            pieces.append(x1 * cost - x2 * sint)
            pieces.append(x2 * cost + x1 * sint)
        return jnp.concatenate(pieces, axis=0)

    qt_ref[...] = rope_rows(0).astype(BF16)
    qit_ref[...] = rope_rows(512).astype(BF16)
    vtb = zt[1024:1152].astype(BF16)
    for c in range(tm // KEY_STEP):
        vt_ref[c] = vtb[:, c * KEY_STEP:(c + 1) * KEY_STEP]
    wit_ref[...] = zt[1152:1160] * IDX_SCALE


def _project(x2d, wstd, wt, cosl, sinl, cost, sint, n_pos_tiles):
    rows = x2d.shape[0]
    tm = ROW_TILE
    grid = (rows // tm,)
    row_map = lambda r: (r, 0)
    col_map = lambda r: (0, r)
    const = lambda r: (0, 0)
    out_shape = (
        jax.ShapeDtypeStruct((rows, 128), F32),
        jax.ShapeDtypeStruct((rows, 128), F32),
        jax.ShapeDtypeStruct((rows, IDX_DIM), F32),
        jax.ShapeDtypeStruct((rows, 128), BF16),
        jax.ShapeDtypeStruct((rows, 128), BF16),
        jax.ShapeDtypeStruct((512, rows), BF16),
        jax.ShapeDtypeStruct((512, rows), BF16),
        jax.ShapeDtypeStruct((rows // KEY_STEP, 128, KEY_STEP), BF16),
        jax.ShapeDtypeStruct((8, rows), F32),
    )
    out_specs = (
        pl.BlockSpec((tm, 128), row_map),
        pl.BlockSpec((tm, 128), row_map),
        pl.BlockSpec((tm, IDX_DIM), row_map),
        pl.BlockSpec((tm, 128), row_map),
        pl.BlockSpec((tm, 128), row_map),
        pl.BlockSpec((512, tm), col_map),
        pl.BlockSpec((512, tm), col_map),
        pl.BlockSpec((tm // KEY_STEP, 128, KEY_STEP), lambda r: (r, 0, 0)),
        pl.BlockSpec((8, tm), col_map),
    )
    in_specs = [
        pl.BlockSpec((tm, D_MODEL), row_map),
        pl.BlockSpec(wstd.shape, const),
        pl.BlockSpec(wt.shape, const),
        pl.BlockSpec((tm, LANES), lambda r: (r % n_pos_tiles, 0)),
        pl.BlockSpec((tm, LANES), lambda r: (r % n_pos_tiles, 0)),
        pl.BlockSpec((HEAD_DIM // 2, tm), lambda r: (0, r % n_pos_tiles)),
        pl.BlockSpec((HEAD_DIM // 2, tm), lambda r: (0, r % n_pos_tiles)),
    ]
    return pl.pallas_call(
        _proj_kernel, grid=grid, in_specs=in_specs, out_specs=out_specs, out_shape=out_shape,
        compiler_params=pltpu.CompilerParams(dimension_semantics=("arbitrary",),
                                             vmem_limit_bytes=VMEM_LIMIT),
        name="proj_attn_side",
    )(x2d, wstd, wt, cosl, sinl, cost, sint)


def _rope_tables(pos):
    half = HEAD_DIM // 2
    inv = ROPE_THETA ** (-jnp.arange(half, dtype=F32) / half)
    ang = pos.astype(F32)[:, None] * inv[None, :]
    cos = jnp.cos(ang)
    sin = jnp.sin(ang)
    cosl = jnp.tile(cos, (1, LANES // half))
    sinl = jnp.tile(jnp.concatenate([-sin, sin], axis=1), (1, LANES // HEAD_DIM))
    return cosl, sinl, cos.T, sin.T


def _select_threshold(count_ge, count_gt, count_eq_le, min_ge, max_lt, lo0, hi0, n_allowed, n_keys,
                      row_ok):
    k = float(TOPK)

    def step(carry):
        lo, hi, clo = carry
        mid = 0.5 * lo + 0.5 * hi
        c = count_ge(mid)
        take = c >= k
        return jnp.where(take, mid, lo), jnp.where(take, hi, mid), jnp.where(take, c, clo)

    lo, hi, clo = lax.fori_loop(0, BISECT_ITERS, lambda _, c: step(c), (lo0, hi0, n_allowed))
    big = jnp.full_like(lo, float(4 * n_keys))

    def unresolved(clo_):
        return jnp.logical_and(clo_ > k, row_ok)

    def any_true(mask):
        return jnp.max(jnp.where(mask, 1.0, 0.0)) > 0.5

    def exact(args):
        lo, hi, clo = args

        def cond(c):
            _, _, clo, vlo, vhi, it = c
            open_ = jnp.logical_and(unresolved(clo), vlo < vhi)
            return jnp.logical_and(any_true(open_), it < 2 * n_keys)

        def body(c):
            lo, hi, clo, vlo, vhi, it = c
            mid = 0.5 * vlo + 0.5 * vhi
            mid = jnp.where(mid > vlo, mid, vhi)
            cnt = count_ge(mid)
            take = cnt >= k
            lo, hi, clo = jnp.where(take, mid, lo), jnp.where(take, hi, mid), jnp.where(take, cnt, clo)
            return lo, hi, clo, min_ge(lo), max_lt(hi), it + 1

        lo, hi, clo, vlo, _, _ = lax.while_loop(
            cond, body, (lo, hi, clo, min_ge(lo), max_lt(hi), jnp.int32(0)))
        lo = vlo
        need = k - count_gt(lo)

        def jstep(_, c):
            jlo, jhi = c
            mid = jnp.floor(0.5 * (jlo + jhi))
            ok = count_eq_le(lo, mid) >= need
            return jnp.where(ok, jlo, mid), jnp.where(ok, mid, jhi)

        jlo0 = jnp.full_like(lo, -1.0)
        jhi0 = jnp.full_like(lo, float(n_keys - 1))
        _, jhi = lax.fori_loop(0, 13, jstep, (jlo0, jhi0))
        return lo, jnp.where(clo > k, jhi, big)

    def fast(args):
        return args[0], big

    return lax.cond(any_true(unresolved(clo)), exact, fast, (lo, hi, clo))


def _fold_rows(x, op):
    parts = [x[r:r + 8] for r in range(0, x.shape[0], 8)]
    while len(parts) > 1:
        nxt = [op(parts[a], parts[a + 1]) for a in range(0, len(parts) - 1, 2)]
        if len(parts) % 2:
            nxt.append(parts[-1])
        parts = nxt
    return parts[0]


def _attn_prompt_kernel(qt_ref, qit_ref, wit_ref, kbf_ref, kiw_ref, vt_ref, o_ref,
                        s_ref, sc_ref, acc_ref, ri_ref, rq_ref):
    i = pl.program_id(1)
    w = wit_ref[...]
    zeros_half = jnp.zeros((HEAD_DIM, LANES), BF16)

    for hp in range(4):
        a = qit_ref[(2 * hp) * 64:(2 * hp + 1) * 64, :]
        b = qit_ref[(2 * hp + 1) * 64:(2 * hp + 2) * 64, :]
        ri_ref[hp] = jnp.concatenate(
            [jnp.concatenate([a, zeros_half], axis=0), jnp.concatenate([b, zeros_half], axis=0)], axis=1)
        g = hp // 2
        a = qt_ref[(2 * hp) * 64:(2 * hp + 1) * 64, :]
        b = qt_ref[(2 * hp + 1) * 64:(2 * hp + 2) * 64, :]
        if g == 0:
            ab = jnp.concatenate([jnp.concatenate([a, zeros_half], axis=0),
                                  jnp.concatenate([b, zeros_half], axis=0)], axis=1)
        else:
            ab = jnp.concatenate([jnp.concatenate([zeros_half, a], axis=0),
                                  jnp.concatenate([zeros_half, b], axis=0)], axis=1)
        rq_ref[hp] = ab

    row = lax.broadcasted_iota(jnp.int32, (KEY_STEP, LANES), 0)
    col = lax.broadcasted_iota(jnp.int32, (KEY_STEP, LANES), 1)
    qpos = col + i * LANES
    inf = jnp.float32(jnp.inf)
    groups = KEY_STEP // 8
    n_steps = (i * LANES + LANES + KEY_STEP - 1) // KEY_STEP

    def key_rows(j):
        return pl.ds(pl.multiple_of(j * KEY_STEP, KEY_STEP), KEY_STEP)

    def idx_body(j, carry):
        mn, mx = carry
        kt = kiw_ref[key_rows(j), :]
        acc = jnp.zeros((KEY_STEP, LANES), F32)
        for hp in range(4):
            s2 = jnp.dot(kt, ri_ref[hp], preferred_element_type=F32)
            acc = acc + jnp.maximum(s2[:, :LANES], 0.0) * w[2 * hp:2 * hp + 1, :]
            acc = acc + jnp.maximum(s2[:, LANES:], 0.0) * w[2 * hp + 1:2 * hp + 2, :]
        allowed = (row + j * KEY_STEP) <= qpos
        s_ref[key_rows(j), :] = jnp.where(allowed, acc, -inf)
        lo_part = _fold_rows(jnp.where(allowed, acc, inf), jnp.minimum)
        hi_part = _fold_rows(jnp.where(allowed, acc, -inf), jnp.maximum)
        return jnp.minimum(mn, lo_part), jnp.maximum(mx, hi_part)

    mn, mx = lax.fori_loop(0, n_steps, idx_body, (jnp.full((8, LANES), inf, F32),
                                                  jnp.full((8, LANES), -inf, F32)))
    lo0 = jnp.min(mn, axis=0, keepdims=True)
    mx1 = jnp.max(mx, axis=0, keepdims=True)
    hi0 = mx1 + (jnp.abs(mx1) * (2.0 ** -20) + 1e-30)
    n_allowed = (qpos[0:1, :] + 1).astype(F32)

    def key_reduce(fn, op, init, finish):
        def body(j, r8):
            return op(r8, _fold_rows(fn(s_ref[key_rows(j), :], j), op))
        r8 = lax.fori_loop(0, n_steps, body, jnp.full((8, LANES), init, F32))
        return finish(r8, axis=0, keepdims=True)

    def kidx(j):
        return (row + j * KEY_STEP).astype(F32)

    def tile_count(fn):
        return key_reduce(fn, jnp.add, 0.0, jnp.sum)

    count_ge = lambda thr: tile_count(lambda t, j: jnp.where(t >= thr, 1.0, 0.0))
    count_gt = lambda thr: tile_count(lambda t, j: jnp.where(t > thr, 1.0, 0.0))
    count_eq_le = lambda thr, jm: tile_count(
        lambda t, j: jnp.where(t == thr, jnp.where(kidx(j) <= jm, 1.0, 0.0), 0.0))
    min_ge = lambda thr: key_reduce(lambda t, j: jnp.where(t >= thr, t, inf), jnp.minimum, inf, jnp.min)
    max_lt = lambda thr: key_reduce(lambda t, j: jnp.where(t < thr, t, -inf), jnp.maximum, -inf, jnp.max)

    lo, jmax = _select_threshold(count_ge, count_gt, count_eq_le, min_ge, max_lt, lo0, hi0,
                                 n_allowed, SEQ, jnp.full((1, LANES), True))

    def bias_body(j, _):
        t = s_ref[key_rows(j), :]
        tie = jnp.where(t == lo, jnp.where(kidx(j) <= jmax, 0.0, NEG), NEG)
        s_ref[key_rows(j), :] = jnp.where(t > lo, 0.0, tie)
        return 0

    lax.fori_loop(0, n_steps, bias_body, 0)

    def score_body(j, m8s):
        kt = kbf_ref[key_rows(j), :]
        b = s_ref[key_rows(j), :]
        b2 = jnp.concatenate([b, b], axis=1)
        out = []
        for hp in range(4):
            s2 = jnp.dot(kt, rq_ref[hp], preferred_element_type=F32) + b2
            sc_ref[hp, key_rows(j), :] = s2
            out.append(jnp.maximum(m8s[hp], _fold_rows(s2, jnp.maximum)))
        return tuple(out)

    m8s = lax.fori_loop(0, n_steps, score_body,
                        tuple(jnp.full((8, 2 * LANES), -inf, F32) for _ in range(4)))
    ms = [jnp.max(m8, axis=0, keepdims=True) for m8 in m8s]
    acc_ref[...] = jnp.zeros(acc_ref.shape, F32)

    def pv_body(j, l8s):
        vt = vt_ref[j]
        out = []
        for hp in range(4):
            g = hp // 2
            p = jnp.exp(sc_ref[hp, key_rows(j), :] - ms[hp])
            acc_ref[hp] += jnp.dot(vt[g * 64:(g + 1) * 64, :], p.astype(BF16),
                                   preferred_element_type=F32)
            out.append(l8s[hp] + _fold_rows(p, jnp.add))
        return tuple(out)

    l8s = lax.fori_loop(0, n_steps, pv_body,
                        tuple(jnp.zeros((8, 2 * LANES), F32) for _ in range(4)))
    pieces = []
    for hp in range(4):
        o2 = acc_ref[hp] / jnp.sum(l8s[hp], axis=0, keepdims=True)
        pieces.append(o2[:, :LANES])
        pieces.append(o2[:, LANES:])
    o_ref[...] = jnp.concatenate(pieces, axis=0).T


def _attn_prompt(qt, qit, wit, kbf, kiw, vt3):
    n_blocks = SEQ // LANES
    grid = (BATCH, n_blocks)
    qmap = lambda n, i: (0, n * n_blocks + i)
    in_specs = [
        pl.BlockSpec((512, LANES), qmap),
        pl.BlockSpec((512, LANES), qmap),
        pl.BlockSpec((8, LANES), qmap),
        pl.BlockSpec((SEQ, 128), lambda n, i: (n, 0)),
        pl.BlockSpec((SEQ, 128), lambda n, i: (n, 0)),
        pl.BlockSpec((SEQ // KEY_STEP, 128, KEY_STEP), lambda n, i: (n, 0, 0)),
    ]
    return pl.pallas_call(
        _attn_prompt_kernel, grid=grid, in_specs=in_specs,
        out_specs=pl.BlockSpec((LANES, W_A), lambda n, i: (n * n_blocks + i, 0)),
        out_shape=jax.ShapeDtypeStruct((BATCH * SEQ, W_A), F32),
        scratch_shapes=[pltpu.VMEM((SEQ, LANES), F32), pltpu.VMEM((4, SEQ, 2 * LANES), F32),
                        pltpu.VMEM((4, HEAD_DIM, 2 * LANES), F32),
                        pltpu.VMEM((4, 128, 2 * LANES), BF16), pltpu.VMEM((4, 128, 2 * LANES), BF16)],
        compiler_params=pltpu.CompilerParams(dimension_semantics=("arbitrary", "arbitrary"),
                                             vmem_limit_bytes=VMEM_LIMIT),
        name="attn_prompt",
    )(qt, qit, wit, kbf, kiw, vt3)


def _attn_sample_kernel(pt_ref, qi_ref, q_ref, w_ref, kin_ref, kn_ref, vn_ref,
                        ck_hbm, cv_hbm, cki_hbm, o_ref,
                        kbuf, vbuf, kibuf, sems, s_ref, bias_ref):
    step = pl.program_id(0)
    n_steps = pl.num_programs(0)
    gsz = SAMPLE_GROUP
    slot = step % 2

    def copies(group, slot_):
        out = []
        for g in range(gsz):
            seq = group * gsz + g
            for p in range(N_PAGES):
                page = pt_ref[seq, p]
                keys = pl.ds(p * PAGE_SIZE, PAGE_SIZE)
                out.append(pltpu.make_async_copy(ck_hbm.at[page], kbuf.at[slot_, g, :, keys], sems.at[slot_, 0]))
                out.append(pltpu.make_async_copy(cv_hbm.at[page], vbuf.at[slot_, g, :, keys], sems.at[slot_, 1]))
                out.append(pltpu.make_async_copy(cki_hbm.at[page], kibuf.at[slot_, g, :, keys], sems.at[slot_, 2]))
        return out

    new_keys = pl.ds(PAST_LEN, SAMPLE_Q)

    @pl.when(step == 0)
    def _():
        tail = pl.ds(PAST_LEN, KEY_TILE)
        for s_ in range(2):
            for g in range(gsz):
                kbuf[s_, g, :, tail] = jnp.zeros((128, KEY_TILE), F32)
                vbuf[s_, g, :, tail] = jnp.zeros((128, KEY_TILE), F32)
                kibuf[s_, g, :, tail] = jnp.zeros((IDX_DIM, KEY_TILE), F32)
        for c in copies(0, 0):
            c.start()

    @pl.when(step + 1 < n_steps)
    def _():
        for c in copies(step + 1, 1 - slot):
            c.start()

    for c in copies(step, slot):
        c.wait()

    rows8 = lax.broadcasted_iota(jnp.int32, (SAMPLE_Q, SAMPLE_KEYS), 0)
    keyi = lax.broadcasted_iota(jnp.int32, (SAMPLE_Q, SAMPLE_KEYS), 1)
    allowed = keyi <= (PAST_LEN + rows8)
    inf = jnp.float32(jnp.inf)
    row_ok8 = lax.broadcasted_iota(jnp.int32, (SAMPLE_Q, 1), 0) < DEC_SEQ

    for g in range(gsz):
        kbuf[slot, g, :, new_keys] = kn_ref[g]
        vbuf[slot, g, :, new_keys] = vn_ref[g]
        kibuf[slot, g, :, new_keys] = kin_ref[g]
        kib = kibuf[slot, g].astype(BF16)
        s = jnp.dot(qi_ref[g], kib, preferred_element_type=F32)
        sw = jnp.maximum(s, 0.0) * w_ref[g][:, 0:1]
        acc = jnp.sum(sw.reshape(N_IDX_HEADS, SAMPLE_Q, SAMPLE_KEYS), axis=0)
        s_ref[g * SAMPLE_Q:(g + 1) * SAMPLE_Q, :] = jnp.where(allowed, acc, -inf)

    rows_all = gsz * SAMPLE_Q
    allowed_all = jnp.concatenate([allowed] * gsz, axis=0)
    row_ok = jnp.concatenate([row_ok8] * gsz, axis=0)
    keyf = lax.broadcasted_iota(jnp.int32, (rows_all, SAMPLE_KEYS), 1).astype(F32)

    sc = s_ref[...]
    lo0 = jnp.min(jnp.where(allowed_all, sc, inf), axis=1, keepdims=True)
    mx1 = jnp.max(sc, axis=1, keepdims=True)
    hi0 = mx1 + (jnp.abs(mx1) * (2.0 ** -20) + 1e-30)
    t8 = lax.broadcasted_iota(jnp.int32, (SAMPLE_Q, 1), 0)
    n_allowed = jnp.concatenate([(PAST_LEN + 1 + t8).astype(F32)] * gsz, axis=0)

    def lane_count(hit):
        return jnp.sum(hit, axis=1, keepdims=True)

    count_ge = lambda thr: lane_count(jnp.where(s_ref[...] >= thr, 1.0, 0.0))
    count_gt = lambda thr: lane_count(jnp.where(s_ref[...] > thr, 1.0, 0.0))
    count_eq_le = lambda thr, jm: lane_count(
        jnp.where(s_ref[...] == thr, jnp.where(keyf <= jm, 1.0, 0.0), 0.0))
    min_ge = lambda thr: jnp.min(jnp.where(s_ref[...] >= thr, s_ref[...], inf), axis=1, keepdims=True)
    max_lt = lambda thr: jnp.max(jnp.where(s_ref[...] < thr, s_ref[...], -inf), axis=1, keepdims=True)
    lo, jmax = _select_threshold(count_ge, count_gt, count_eq_le, min_ge, max_lt, lo0, hi0,
                                 n_allowed, SAMPLE_KEYS, row_ok)
    sc = s_ref[...]
    tie = jnp.where(sc == lo, jnp.where(keyf <= jmax, 0.0, NEG), NEG)
    bias_ref[...] = jnp.where(sc > lo, 0.0, tie)

    lane = lax.broadcasted_iota(jnp.int32, (SAMPLE_Q, LANES), 1)
    for g in range(gsz):
        kb = kbuf[slot, g].astype(BF16)
        vb = vbuf[slot, g].astype(BF16)
        b = bias_ref[g * SAMPLE_Q:(g + 1) * SAMPLE_Q, :]
        s = jnp.dot(q_ref[g], kb, preferred_element_type=F32)
        s = s + jnp.concatenate([b] * N_HEADS, axis=0)
        m = jnp.max(s, axis=1, keepdims=True)
        p = jnp.exp(s - m)
        l = jnp.sum(p, axis=1, keepdims=True)
        o = lax.dot_general(p.astype(BF16), vb, (((1,), (1,)), ((), ())),
                            preferred_element_type=F32) / l
        o_r = pltpu.roll(o, HEAD_DIM, 1)
        tiles = []
        for c in range(4):
            ha, hb = 2 * c, 2 * c + 1
            src_a = o if ha // 4 == 0 else o_r
            src_b = o if hb // 4 == 1 else o_r
            tiles.append(jnp.where(lane < HEAD_DIM, src_a[ha * SAMPLE_Q:(ha + 1) * SAMPLE_Q, :],
                                   src_b[hb * SAMPLE_Q:(hb + 1) * SAMPLE_Q, :]))
        o_ref[g] = jnp.concatenate(tiles, axis=1)


def _attn_sample(page_table, qi_s, q_s, w_s, ki_new, k_new, v_new, ck, cv, cki):
    gsz = SAMPLE_GROUP
    grid = (DEC_BATCH // gsz,)
    blk = lambda shape: pl.BlockSpec((gsz,) + shape, lambda s, pt: (s, 0, 0))
    grid_spec = pltpu.PrefetchScalarGridSpec(
        num_scalar_prefetch=1, grid=grid,
        in_specs=[blk((64, IDX_DIM)), blk((64, 128)), blk((64, LANES)),
                  blk((IDX_DIM, SAMPLE_Q)), blk((128, SAMPLE_Q)), blk((128, SAMPLE_Q)),
                  pl.BlockSpec(memory_space=pl.ANY), pl.BlockSpec(memory_space=pl.ANY),
                  pl.BlockSpec(memory_space=pl.ANY)],
        out_specs=blk((SAMPLE_Q, W_A)),
        scratch_shapes=[pltpu.VMEM((2, gsz, 128, SAMPLE_KEYS), F32),
                        pltpu.VMEM((2, gsz, 128, SAMPLE_KEYS), F32),
                        pltpu.VMEM((2, gsz, IDX_DIM, SAMPLE_KEYS), F32),
                        pltpu.SemaphoreType.DMA((2, 3)),
                        pltpu.VMEM((gsz * SAMPLE_Q, SAMPLE_KEYS), F32),
                        pltpu.VMEM((gsz * SAMPLE_Q, SAMPLE_KEYS), F32)])
    return pl.pallas_call(
        _attn_sample_kernel, grid_spec=grid_spec,
        out_shape=jax.ShapeDtypeStruct((DEC_BATCH, SAMPLE_Q, W_A), F32),
        compiler_params=pltpu.CompilerParams(dimension_semantics=("arbitrary",),
                                             vmem_limit_bytes=VMEM_LIMIT),
        name="attn_sample",
    )(page_table, qi_s, q_s, w_s, ki_new, k_new, v_new, ck, cv, cki)


def _layer_norm(x, g, b):
    mu = jnp.mean(x, axis=-1, keepdims=True)
    xc = x - mu
    var = jnp.mean(xc * xc, axis=-1, keepdims=True)
    return xc * lax.rsqrt(var + LN_EPS) * g + b


def _silu(x):
    return x / (1.0 + jnp.exp(-x))


def _mixer_kernel(x_ref, a_ref, wb_ref, wmix_ref, bsb_ref, lnvg_ref, lnvb_ref, wout_ref,
                  lng_ref, lnb_ref, y_ref, vn_ref, *, chunk, vn_last_only, tiles_per_seq):
    tm = x_ref.shape[0]
    x = x_ref[...]
    zb = jnp.dot(x.astype(BF16), wb_ref[...], preferred_element_type=F32)
    ga = zb[:, 0:512]
    u = zb[:, 512:1024]
    vb = zb[:, 1024:1536]
    gb = zb[:, 1536:2048]
    vn = _layer_norm(vb, lnvg_ref[...], lnvb_ref[...])
    if vn_last_only:
        @pl.when(pl.program_id(0) % tiles_per_seq == tiles_per_seq - 1)
        def _():
            vn_ref[0] = vn[tm - CHUNK:, :]
    else:
        vn_ref[...] = vn
    vnb = vn.astype(BF16)
    rows = []
    for c in range(tm // chunk):
        cols = []
        for g in range(N_GROUPS_B):
            blk = vnb[c * chunk:(c + 1) * chunk, g * GROUP_W_B:(g + 1) * GROUP_W_B]
            cols.append(jnp.dot(wmix_ref[g], blk, preferred_element_type=F32) + bsb_ref[g])
        rows.append(jnp.concatenate(cols, axis=1))
    mixed = jnp.concatenate(rows, axis=0) if len(rows) > 1 else rows[0]
    a_out = a_ref[...] * _silu(ga)
    b_out = u * mixed * _silu(gb)
    cat = jnp.concatenate([a_out, b_out], axis=1).astype(BF16)
    out = jnp.dot(cat, wout_ref[...], preferred_element_type=F32)
    y_ref[...] = _layer_norm(ALPHA * x + out, lng_ref[...], lnb_ref[...])


def _mixer(x2d, attn, wb, wmix, bsb, lnvg, lnvb, wout, lng, lnb, *, chunk, vn_last_only):
    rows = x2d.shape[0]
    tm = ROW_TILE
    tiles_per_seq = SEQ // tm
    grid = (rows // tm,)
    row_map = lambda r: (r, 0)
    const2 = lambda r: (0, 0)
    const3 = lambda r: (0, 0, 0)
    if vn_last_only:
        vn_shape = jax.ShapeDtypeStruct((rows // SEQ, CHUNK, W_B), F32)
        vn_spec = pl.BlockSpec((1, CHUNK, W_B), lambda r: (r // tiles_per_seq, 0, 0))
    else:
        vn_shape = jax.ShapeDtypeStruct((rows, W_B), F32)
        vn_spec = pl.BlockSpec((tm, W_B), row_map)
    kern = functools.partial(_mixer_kernel, chunk=chunk, vn_last_only=vn_last_only,
                             tiles_per_seq=tiles_per_seq)
    return pl.pallas_call(
        kern, grid=grid,
        in_specs=[pl.BlockSpec((tm, D_MODEL), row_map), pl.BlockSpec((tm, W_A), row_map),
                  pl.BlockSpec(wb.shape, const2), pl.BlockSpec(wmix.shape, const3),
                  pl.BlockSpec(bsb.shape, const3), pl.BlockSpec((1, W_B), const2),
                  pl.BlockSpec((1, W_B), const2), pl.BlockSpec(wout.shape, const2),
                  pl.BlockSpec((1, D_MODEL), const2), pl.BlockSpec((1, D_MODEL), const2)],
        out_specs=(pl.BlockSpec((tm, D_MODEL), row_map), vn_spec),
        out_shape=(jax.ShapeDtypeStruct((rows, D_MODEL), F32), vn_shape),
        compiler_params=pltpu.CompilerParams(dimension_semantics=("arbitrary",),
                                             vmem_limit_bytes=VMEM_LIMIT),
        name="mixer_chunk%d" % chunk,
    )(x2d, attn, wb, wmix, bsb, lnvg, lnvb, wout, lng, lnb)


def kernel(x_prompt, x_sample, cache_k, cache_v, cache_k_idx, page_table, w_in, w_s, b_s,
           ln_v_g, ln_v_b, w_out, ln_g, ln_b):
    w = w_in[0]
    wq, wk, wv = w[:, 0:512], w[:, 512:640], w[:, 640:768]
    wga, wqi, wki, wwi = w[:, 768:1280], w[:, 1280:1792], w[:, 1792:1856], w[:, 1856:1864]
    wu, wvb, wgb = w[:, 1864:2376], w[:, 2376:2888], w[:, 2888:3400]
    wstd = jnp.concatenate([wk, wv, wki, jnp.zeros((D_MODEL, 64), F32)], axis=1).astype(BF16)
    wt = jnp.concatenate([wq * (HEAD_DIM ** -0.5), wqi, wv, wwi,
                          jnp.zeros((D_MODEL, WT_ROWS - 1160), F32)], axis=1).T.astype(BF16)
    wb = jnp.concatenate([wga, wu, wvb, wgb], axis=1).astype(BF16)
    wout = w_out[0].astype(BF16)
    lnvg, lnvb = ln_v_g[0][None, :], ln_v_b[0][None, :]
    lng, lnb = ln_g[0][None, :], ln_b[0][None, :]

    xp = x_prompt.reshape(BATCH * SEQ, D_MODEL)
    tabs_p = _rope_tables(jnp.arange(SEQ, dtype=jnp.int32))
    k_p, v_p, ki_p, kbf, kiw, qt, qit, vt3, wit = _project(xp, wstd, wt, *tabs_p, SEQ // ROW_TILE)
    attn_p = _attn_prompt(qt, qit, wit, kbf, kiw, vt3)
    tril = jnp.tril(jnp.ones((CHUNK, CHUNK), F32))
    wmix_p = (w_s[0] * tril[None]).astype(BF16)
    bsb_p = jnp.broadcast_to(b_s[0][:, :, None], (N_GROUPS_B, CHUNK, GROUP_W_B))
    y_p, vn_p = _mixer(xp, attn_p, wb, wmix_p, bsb_p, lnvg, lnvb, wout, lng, lnb,
                       chunk=CHUNK, vn_last_only=True)

    rows_s = DEC_BATCH * DEC_SEQ
    xs = x_sample.reshape(rows_s, D_MODEL)
    pos_s = PAST_LEN + (jnp.arange(rows_s, dtype=jnp.int32) % DEC_SEQ)
    tabs_s = _rope_tables(pos_s)
    k_s, v_s, ki_s, _, _, qt_s, qit_s, _, wit_s = _project(xs, wstd, wt, *tabs_s, 1)

    def to_rows(a_t):
        a = a_t.reshape(N_HEADS, HEAD_DIM, DEC_BATCH, DEC_SEQ).transpose(2, 0, 3, 1)
        return jnp.pad(a, ((0, 0), (0, 0), (0, SAMPLE_Q - DEC_SEQ), (0, 0)))

    qi_rows = to_rows(qit_s).reshape(DEC_BATCH, N_IDX_HEADS * SAMPLE_Q, IDX_DIM)
    q5 = to_rows(qt_s).reshape(DEC_BATCH, N_KV, N_HEADS // N_KV, SAMPLE_Q, HEAD_DIM)
    zq = jnp.zeros_like(q5[:, 0])
    q_rows = jnp.stack([jnp.concatenate([q5[:, 0], zq], axis=-1),
                        jnp.concatenate([zq, q5[:, 1]], axis=-1)], axis=1)
    q_rows = q_rows.reshape(DEC_BATCH, N_HEADS * SAMPLE_Q, 2 * HEAD_DIM)
    w_rows = jnp.pad(wit_s.reshape(N_IDX_HEADS, DEC_BATCH, DEC_SEQ).transpose(1, 0, 2),
                     ((0, 0), (0, 0), (0, SAMPLE_Q - DEC_SEQ)))
    w_rows = jnp.broadcast_to(w_rows.reshape(DEC_BATCH, N_IDX_HEADS * SAMPLE_Q, 1),
                              (DEC_BATCH, N_IDX_HEADS * SAMPLE_Q, LANES))
    new_cols = lambda a: jnp.pad(a.reshape(DEC_BATCH, DEC_SEQ, a.shape[-1]).transpose(0, 2, 1),
                                 ((0, 0), (0, 0), (0, SAMPLE_Q - DEC_SEQ)))
    n_pool = cache_k.shape[1]
    ck_t = cache_k[0].transpose(0, 2, 3, 1).reshape(n_pool, N_KV * HEAD_DIM, PAGE_SIZE)
    cv_t = cache_v[0].transpose(0, 2, 3, 1).reshape(n_pool, N_KV * HEAD_DIM, PAGE_SIZE)
    cki_t = cache_k_idx[0].transpose(0, 2, 1)
    attn_s8 = _attn_sample(page_table, qi_rows, q_rows, w_rows, new_cols(ki_s), new_cols(k_s),
                           new_cols(v_s), ck_t, cv_t, cki_t)
    attn_s = attn_s8[:, :DEC_SEQ, :].reshape(rows_s, W_A)
    ri = jnp.arange(rows_s, dtype=jnp.int32)
    same_seq = (ri[:, None] // DEC_SEQ) == (ri[None, :] // DEC_SEQ)
    wmix_s = jnp.zeros((N_GROUPS_B, rows_s, rows_s), F32)
    for t in range(DEC_SEQ):
        for s in range(t + 1):
            hit = same_seq & (ri[:, None] % DEC_SEQ == t) & (ri[None, :] % DEC_SEQ == s)
            wmix_s = wmix_s + jnp.where(hit[None], w_s[0, :, t, s][:, None, None], 0.0)
    wmix_s = wmix_s.astype(BF16)
    bsb_s = jnp.broadcast_to(jnp.tile(b_s[0][:, :DEC_SEQ], (1, DEC_BATCH))[:, :, None],
                             (N_GROUPS_B, rows_s, GROUP_W_B))
    y_s, vn_s = _mixer(xs, attn_s, wb, wmix_s, bsb_s, lnvg, lnvb, wout, lng, lnb,
                       chunk=rows_s, vn_last_only=False)

    return (y_p.reshape(BATCH, SEQ, D_MODEL),
            y_s.reshape(DEC_BATCH, DEC_SEQ, D_MODEL),
            k_p.reshape(1, BATCH, SEQ, N_KV, HEAD_DIM),
            v_p.reshape(1, BATCH, SEQ, N_KV, HEAD_DIM),
            ki_p.reshape(1, BATCH, SEQ, IDX_DIM),
            vn_p.reshape(1, BATCH, CHUNK, W_B),
            k_s.reshape(1, DEC_BATCH, DEC_SEQ, N_KV, HEAD_DIM),
            v_s.reshape(1, DEC_BATCH, DEC_SEQ, N_KV, HEAD_DIM),
            ki_s.reshape(1, DEC_BATCH, DEC_SEQ, IDX_DIM),
            vn_s.reshape(1, DEC_BATCH, DEC_SEQ, W_B))
```

```python
import functools

import jax
import jax.numpy as jnp
from jax import lax
from jax.experimental import pallas as pl
from jax.experimental.pallas import tpu as pltpu

F32 = jnp.float32
BF16 = jnp.bfloat16

D_MODEL = 1024
SEQ = 2048
BATCH = 8
DEC_BATCH = 128
DEC_SEQ = 4
PAST_LEN = 2048
PAGE_SIZE = 128
N_PAGES = PAST_LEN // PAGE_SIZE
W_A = 512
W_B = 512
HEAD_DIM = 64
N_HEADS = 8
N_KV = 2
N_IDX_HEADS = 8
IDX_DIM = 64
TOPK = 256
CHUNK = 128
N_GROUPS_B = 4
GROUP_W_B = 128
ROPE_THETA = 10000.0
LN_EPS = 1e-5
ALPHA = 2.0 ** 0.25
NEG = -1e30
IDX_SCALE = float((N_IDX_HEADS * IDX_DIM) ** -0.5)

LANES = 128
KEY_TILE = 128
KEY_STEP = 512
ROW_TILE = 512
WT_ROWS = 1168
BISECT_ITERS = 24
FOLD_CHAINS = 8
ONES_ROWS = 16
Q_SCALE = float(HEAD_DIM ** -0.5 * 1.4426950408889634)
VMEM_LIMIT = 48 * 1024 * 1024

SAMPLE_GROUP = 4
SAMPLE_KEYS = PAST_LEN + KEY_TILE
SAMPLE_Q = 8


def _proj_kernel(x_ref, wstd_ref, wt_ref, cosl_ref, sinl_ref, cost_ref, sint_ref,
                 k_ref, v_ref, ki_ref, kbf_ref, kiw_ref, qt_ref, qit_ref, vt_ref, wit_ref):
    tm = x_ref.shape[0]
    xb = x_ref[...].astype(BF16)
    zs = jnp.dot(xb, wstd_ref[...], preferred_element_type=F32)
    cosl = cosl_ref[...]
    sinl = sinl_ref[...]
    lane = lax.broadcasted_iota(jnp.int32, (tm, LANES), 1)
    first_half = (lane % HEAD_DIM) < (HEAD_DIM // 2)

    def rope_lanes(z):
        partner = jnp.where(first_half, pltpu.roll(z, 96, 1), pltpu.roll(z, 32, 1))
        return z * cosl + partner * sinl

    k = rope_lanes(zs[:, 0:128])
    v = zs[:, 128:256]
    kiw = rope_lanes(zs[:, 256:384])
    k_ref[...] = k
    v_ref[...] = v
    ki_ref[...] = kiw[:, 0:IDX_DIM]
    kbf_ref[...] = k.astype(BF16)
    kiw_ref[...] = kiw.astype(BF16)

    zt = lax.dot_general(wt_ref[...], xb, (((1,), (1,)), ((), ())),
                         preferred_element_type=F32)
    cost = cost_ref[...]
    sint = sint_ref[...]
    half = HEAD_DIM // 2

    def rope_rows(base):
        pieces = []
        for h in range(N_HEADS):
            x1 = zt[base + h * HEAD_DIM: base + h * HEAD_DIM + half]
            x2 = zt[base + h * HEAD_DIM + half: base + (h + 1) * HEAD_DIM]
            pieces.append(x1 * cost - x2 * sint)
            pieces.append(x2 * cost + x1 * sint)
        return jnp.concatenate(pieces, axis=0)

    qt_ref[...] = rope_rows(0).astype(BF16)
    qit_ref[...] = rope_rows(512).astype(BF16)
    vtb = zt[1024:1152].astype(BF16)
    for c in range(tm // KEY_STEP):
        vt_ref[c] = vtb[:, c * KEY_STEP:(c + 1) * KEY_STEP]
    wit_ref[...] = zt[1152:1160] * IDX_SCALE


def _project(x2d, wstd, wt, cosl, sinl, cost, sint, n_pos_tiles):
    rows = x2d.shape[0]
    tm = ROW_TILE
    grid = (rows // tm,)
    row_map = lambda r: (r, 0)
    col_map = lambda r: (0, r)
    const = lambda r: (0, 0)
    out_shape = (
        jax.ShapeDtypeStruct((rows, 128), F32),
        jax.ShapeDtypeStruct((rows, 128), F32),
        jax.ShapeDtypeStruct((rows, IDX_DIM), F32),
        jax.ShapeDtypeStruct((rows, 128), BF16),
        jax.ShapeDtypeStruct((rows, 128), BF16),
        jax.ShapeDtypeStruct((512, rows), BF16),
        jax.ShapeDtypeStruct((512, rows), BF16),
        jax.ShapeDtypeStruct((rows // KEY_STEP, 128, KEY_STEP), BF16),
        jax.ShapeDtypeStruct((8, rows), F32),
    )
    out_specs = (
        pl.BlockSpec((tm, 128), row_map),
        pl.BlockSpec((tm, 128), row_map),
        pl.BlockSpec((tm, IDX_DIM), row_map),
        pl.BlockSpec((tm, 128), row_map),
        pl.BlockSpec((tm, 128), row_map),
        pl.BlockSpec((512, tm), col_map),
        pl.BlockSpec((512, tm), col_map),
        pl.BlockSpec((tm // KEY_STEP, 128, KEY_STEP), lambda r: (r, 0, 0)),
        pl.BlockSpec((8, tm), col_map),
    )
    in_specs = [
        pl.BlockSpec((tm, D_MODEL), row_map),
        pl.BlockSpec(wstd.shape, const),
        pl.BlockSpec(wt.shape, const),
        pl.BlockSpec((tm, LANES), lambda r: (r % n_pos_tiles, 0)),
        pl.BlockSpec((tm, LANES), lambda r: (r % n_pos_tiles, 0)),
        pl.BlockSpec((HEAD_DIM // 2, tm), lambda r: (0, r % n_pos_tiles)),
        pl.BlockSpec((HEAD_DIM // 2, tm), lambda r: (0, r % n_pos_tiles)),
    ]
    return pl.pallas_call(
        _proj_kernel, grid=grid, in_specs=in_specs, out_specs=out_specs, out_shape=out_shape,
        compiler_params=pltpu.CompilerParams(dimension_semantics=("arbitrary",),
                                             vmem_limit_bytes=VMEM_LIMIT),
        name="proj_attn_side",
    )(x2d, wstd, wt, cosl, sinl, cost, sint)


def _rope_tables(pos):
    half = HEAD_DIM // 2
    inv = ROPE_THETA ** (-jnp.arange(half, dtype=F32) / half)
    ang = pos.astype(F32)[:, None] * inv[None, :]
    cos = jnp.cos(ang)
    sin = jnp.sin(ang)
    cosl = jnp.tile(cos, (1, LANES // half))
    sinl = jnp.tile(jnp.concatenate([-sin, sin], axis=1), (1, LANES // HEAD_DIM))
    return cosl, sinl, cos.T, sin.T


def _select_threshold(count_ge, count_gt, count_eq_le, min_ge, max_lt, lo0, hi0, n_allowed, n_keys,
                      row_ok):
    k = float(TOPK)

    def step(carry):
        lo, hi, clo = carry
        mid = 0.5 * lo + 0.5 * hi
        c = count_ge(mid)
        take = c >= k
        return jnp.where(take, mid, lo), jnp.where(take, hi, mid), jnp.where(take, c, clo)

    lo, hi, clo = lax.fori_loop(0, BISECT_ITERS, lambda _, c: step(c), (lo0, hi0, n_allowed))
    big = jnp.full_like(lo, float(4 * n_keys))

    def unresolved(clo_):
        return jnp.logical_and(clo_ > k, row_ok)

    def any_true(mask):
        return jnp.max(jnp.where(mask, 1.0, 0.0)) > 0.5

    def exact(args):
        lo, hi, clo = args

        def cond(c):
            _, _, clo, vlo, vhi, it = c
            open_ = jnp.logical_and(unresolved(clo), vlo < vhi)
            return jnp.logical_and(any_true(open_), it < 2 * n_keys)

        def body(c):
            lo, hi, clo, vlo, vhi, it = c
            mid = 0.5 * vlo + 0.5 * vhi
            mid = jnp.where(mid > vlo, mid, vhi)
            cnt = count_ge(mid)
            take = cnt >= k
            lo, hi, clo = jnp.where(take, mid, lo), jnp.where(take, hi, mid), jnp.where(take, cnt, clo)
            return lo, hi, clo, min_ge(lo), max_lt(hi), it + 1

        lo, hi, clo, vlo, _, _ = lax.while_loop(
            cond, body, (lo, hi, clo, min_ge(lo), max_lt(hi), jnp.int32(0)))
        lo = vlo
        need = k - count_gt(lo)

        def jstep(_, c):
            jlo, jhi = c
            mid = jnp.floor(0.5 * (jlo + jhi))
            ok = count_eq_le(lo, mid) >= need
            return jnp.where(ok, jlo, mid), jnp.where(ok, mid, jhi)

        jlo0 = jnp.full_like(lo, -1.0)
        jhi0 = jnp.full_like(lo, float(n_keys - 1))
        _, jhi = lax.fori_loop(0, 13, jstep, (jlo0, jhi0))
        return lo, jnp.where(clo > k, jhi, big)

    def fast(args):
        return args[0], big

    return lax.cond(any_true(unresolved(clo)), exact, fast, (lo, hi, clo))


def _fold_rows(x, op):
    groups = [x[r:r + 8] for r in range(0, x.shape[0], 8)]
    parts = groups[:FOLD_CHAINS]
    for n, grp in enumerate(groups[FOLD_CHAINS:]):
        parts[n % FOLD_CHAINS] = op(parts[n % FOLD_CHAINS], grp)
    while len(parts) > 1:
        nxt = [op(parts[a], parts[a + 1]) for a in range(0, len(parts) - 1, 2)]
        if len(parts) % 2:
            nxt.append(parts[-1])
        parts = nxt
    return parts[0]


def _attn_prompt_kernel(qt_ref, qit_ref, wit_ref, kbf_ref, kiw_ref, vt_ref, o_ref,
                        s_ref, sc_ref, ri_ref, rq_ref):
    i = pl.program_id(1)
    w = wit_ref[...]
    zeros_half = jnp.zeros((HEAD_DIM, LANES), BF16)

    for hp in range(4):
        a = qit_ref[(2 * hp) * 64:(2 * hp + 1) * 64, :]
        b = qit_ref[(2 * hp + 1) * 64:(2 * hp + 2) * 64, :]
        ri_ref[hp] = jnp.concatenate(
            [jnp.concatenate([a, zeros_half], axis=0), jnp.concatenate([b, zeros_half], axis=0)], axis=1)
        g = hp // 2
        a = qt_ref[(2 * hp) * 64:(2 * hp + 1) * 64, :]
        b = qt_ref[(2 * hp + 1) * 64:(2 * hp + 2) * 64, :]
        if g == 0:
            ab = jnp.concatenate([jnp.concatenate([a, zeros_half], axis=0),
                                  jnp.concatenate([b, zeros_half], axis=0)], axis=1)
        else:
            ab = jnp.concatenate([jnp.concatenate([zeros_half, a], axis=0),
                                  jnp.concatenate([zeros_half, b], axis=0)], axis=1)
        rq_ref[hp] = ab

    inf = jnp.float32(jnp.inf)
    n_steps = (i * LANES + LANES + KEY_STEP - 1) // KEY_STEP

    def process(ns):
        n_keys = ns * KEY_STEP
        keys = slice(0, n_keys)
        row = lax.broadcasted_iota(jnp.int32, (n_keys, LANES), 0)
        col = lax.broadcasted_iota(jnp.int32, (n_keys, LANES), 1)
        qpos = col + i * LANES
        allowed = row <= qpos

        kt = kiw_ref[keys, :]
        acc = jnp.zeros((n_keys, LANES), F32)
        for hp in range(4):
            s2 = jnp.dot(kt, ri_ref[hp], preferred_element_type=F32)
            acc = acc + jnp.maximum(s2[:, :LANES], 0.0) * w[2 * hp:2 * hp + 1, :]
            acc = acc + jnp.maximum(s2[:, LANES:], 0.0) * w[2 * hp + 1:2 * hp + 2, :]
        s_ref[keys, :] = jnp.where(allowed, acc, -inf)
        lo0 = jnp.min(_fold_rows(jnp.where(allowed, acc, inf), jnp.minimum), axis=0, keepdims=True)
        mx1 = jnp.max(_fold_rows(jnp.where(allowed, acc, -inf), jnp.maximum), axis=0, keepdims=True)
        hi0 = mx1 + (jnp.abs(mx1) * (2.0 ** -20) + 1e-30)
        n_allowed = (qpos[0:1, :] + 1).astype(F32)

        def key_reduce(fn, op, finish):
            return finish(_fold_rows(fn(s_ref[keys, :]), op), axis=0, keepdims=True)

        kidx = row.astype(F32)
        count_ge = lambda thr: key_reduce(lambda t: jnp.where(t >= thr, 1.0, 0.0), jnp.add, jnp.sum)
        count_gt = lambda thr: key_reduce(lambda t: jnp.where(t > thr, 1.0, 0.0), jnp.add, jnp.sum)
        count_eq_le = lambda thr, jm: key_reduce(
            lambda t: jnp.where(t == thr, jnp.where(kidx <= jm, 1.0, 0.0), 0.0), jnp.add, jnp.sum)
        min_ge = lambda thr: key_reduce(lambda t: jnp.where(t >= thr, t, inf), jnp.minimum, jnp.min)
        max_lt = lambda thr: key_reduce(lambda t: jnp.where(t < thr, t, -inf), jnp.maximum, jnp.max)
        lo, jmax = _select_threshold(count_ge, count_gt, count_eq_le, min_ge, max_lt, lo0, hi0,
                                     n_allowed, SEQ, jnp.full((1, LANES), True))

        t = s_ref[keys, :]
        tie = jnp.where(t == lo, jnp.where(kidx <= jmax, 0.0, NEG), NEG)
        s_ref[keys, :] = jnp.where(t > lo, 0.0, tie)

        kt = kbf_ref[keys, :]
        b = s_ref[keys, :]
        b2 = jnp.concatenate([b, b], axis=1)
        ms = []
        for hp in range(4):
            s2 = jnp.dot(kt, rq_ref[hp], preferred_element_type=F32) + b2
            sc_ref[hp, keys, :] = s2
            ms.append(jnp.max(_fold_rows(s2, jnp.maximum), axis=0, keepdims=True))
        vt = jnp.concatenate([vt_ref[j] for j in range(ns)], axis=1)
        ones_rows = jnp.ones((ONES_ROWS, n_keys), BF16)
        lhs = [jnp.concatenate([vt[g * 64:(g + 1) * 64, :], ones_rows], axis=0) for g in range(N_KV)]
        pieces = []
        for hp in range(4):
            p = jnp.exp2((sc_ref[hp, keys, :] - ms[hp]).astype(BF16))
            o = jnp.dot(lhs[hp // 2], p, preferred_element_type=F32)
            o2 = o[0:HEAD_DIM, :] / o[HEAD_DIM:HEAD_DIM + 1, :]
            pieces.append(o2[:, :LANES])
            pieces.append(o2[:, LANES:])
        o_ref[...] = jnp.concatenate(pieces, axis=0).T

    for ns in range(1, SEQ // KEY_STEP + 1):
        pl.when(n_steps == ns)(functools.partial(process, ns))


def _attn_prompt(qt, qit, wit, kbf, kiw, vt3):
    n_blocks = SEQ // LANES
    grid = (BATCH, n_blocks)
    qmap = lambda n, i: (0, n * n_blocks + i)
    in_specs = [
        pl.BlockSpec((512, LANES), qmap),
        pl.BlockSpec((512, LANES), qmap),
        pl.BlockSpec((8, LANES), qmap),
        pl.BlockSpec((SEQ, 128), lambda n, i: (n, 0)),
        pl.BlockSpec((SEQ, 128), lambda n, i: (n, 0)),
        pl.BlockSpec((SEQ // KEY_STEP, 128, KEY_STEP), lambda n, i: (n, 0, 0)),
    ]
    return pl.pallas_call(
        _attn_prompt_kernel, grid=grid, in_specs=in_specs,
        out_specs=pl.BlockSpec((LANES, W_A), lambda n, i: (n * n_blocks + i, 0)),
        out_shape=jax.ShapeDtypeStruct((BATCH * SEQ, W_A), F32),
        scratch_shapes=[pltpu.VMEM((SEQ, LANES), F32), pltpu.VMEM((4, SEQ, 2 * LANES), F32),
                        pltpu.VMEM((4, 128, 2 * LANES), BF16), pltpu.VMEM((4, 128, 2 * LANES), BF16)],
        compiler_params=pltpu.CompilerParams(dimension_semantics=("arbitrary", "arbitrary"),
                                             vmem_limit_bytes=VMEM_LIMIT),
        name="attn_prompt",
    )(qt, qit, wit, kbf, kiw, vt3)


def _attn_sample_kernel(pt_ref, qi_ref, q_ref, w_ref, kin_ref, kn_ref, vn_ref,
                        ck_hbm, cv_hbm, cki_hbm, o_ref,
                        kbuf, vbuf, kibuf, sems, s_ref, bias_ref):
    step = pl.program_id(0)
    n_steps = pl.num_programs(0)
    gsz = SAMPLE_GROUP
    slot = step % 2

    def copies(group, slot_):
        out = []
        for g in range(gsz):
            seq = group * gsz + g
            for p in range(N_PAGES):
                page = pt_ref[seq, p]
                keys = pl.ds(p * PAGE_SIZE, PAGE_SIZE)
                out.append(pltpu.make_async_copy(ck_hbm.at[page], kbuf.at[slot_, g, :, keys], sems.at[slot_, 0]))
                out.append(pltpu.make_async_copy(cv_hbm.at[page], vbuf.at[slot_, g, :, keys], sems.at[slot_, 1]))
                out.append(pltpu.make_async_copy(cki_hbm.at[page], kibuf.at[slot_, g, :, keys], sems.at[slot_, 2]))
        return out

    new_keys = pl.ds(PAST_LEN, SAMPLE_Q)

    @pl.when(step == 0)
    def _():
        tail = pl.ds(PAST_LEN, KEY_TILE)
        for s_ in range(2):
            for g in range(gsz):
                kbuf[s_, g, :, tail] = jnp.zeros((128, KEY_TILE), F32)
                vbuf[s_, g, :, tail] = jnp.zeros((128, KEY_TILE), F32)
                kibuf[s_, g, :, tail] = jnp.zeros((IDX_DIM, KEY_TILE), F32)
        for c in copies(0, 0):
            c.start()

    @pl.when(step + 1 < n_steps)
    def _():
        for c in copies(step + 1, 1 - slot):
            c.start()

    for c in copies(step, slot):
        c.wait()

    rows8 = lax.broadcasted_iota(jnp.int32, (SAMPLE_Q, SAMPLE_KEYS), 0)
    keyi = lax.broadcasted_iota(jnp.int32, (SAMPLE_Q, SAMPLE_KEYS), 1)
    allowed = keyi <= (PAST_LEN + rows8)
    inf = jnp.float32(jnp.inf)
    row_ok8 = lax.broadcasted_iota(jnp.int32, (SAMPLE_Q, 1), 0) < DEC_SEQ

    for g in range(gsz):
        kbuf[slot, g, :, new_keys] = kn_ref[g]
        vbuf[slot, g, :, new_keys] = vn_ref[g]
        kibuf[slot, g, :, new_keys] = kin_ref[g]
        kib = kibuf[slot, g].astype(BF16)
        s = jnp.dot(qi_ref[g], kib, preferred_element_type=F32)
        sw = jnp.maximum(s, 0.0) * w_ref[g][:, 0:1]
        acc = jnp.sum(sw.reshape(N_IDX_HEADS, SAMPLE_Q, SAMPLE_KEYS), axis=0)
        s_ref[g * SAMPLE_Q:(g + 1) * SAMPLE_Q, :] = jnp.where(allowed, acc, -inf)

    rows_all = gsz * SAMPLE_Q
    allowed_all = jnp.concatenate([allowed] * gsz, axis=0)
    row_ok = jnp.concatenate([row_ok8] * gsz, axis=0)
    keyf = lax.broadcasted_iota(jnp.int32, (rows_all, SAMPLE_KEYS), 1).astype(F32)

    sc = s_ref[...]
    lo0 = jnp.min(jnp.where(allowed_all, sc, inf), axis=1, keepdims=True)
    mx1 = jnp.max(sc, axis=1, keepdims=True)
    hi0 = mx1 + (jnp.abs(mx1) * (2.0 ** -20) + 1e-30)
    t8 = lax.broadcasted_iota(jnp.int32, (SAMPLE_Q, 1), 0)
    n_allowed = jnp.concatenate([(PAST_LEN + 1 + t8).astype(F32)] * gsz, axis=0)

    def lane_count(hit):
        return jnp.sum(hit, axis=1, keepdims=True)

    count_ge = lambda thr: lane_count(jnp.where(s_ref[...] >= thr, 1.0, 0.0))
    count_gt = lambda thr: lane_count(jnp.where(s_ref[...] > thr, 1.0, 0.0))
    count_eq_le = lambda thr, jm: lane_count(
        jnp.where(s_ref[...] == thr, jnp.where(keyf <= jm, 1.0, 0.0), 0.0))
    min_ge = lambda thr: jnp.min(jnp.where(s_ref[...] >= thr, s_ref[...], inf), axis=1, keepdims=True)
    max_lt = lambda thr: jnp.max(jnp.where(s_ref[...] < thr, s_ref[...], -inf), axis=1, keepdims=True)
    lo, jmax = _select_threshold(count_ge, count_gt, count_eq_le, min_ge, max_lt, lo0, hi0,
                                 n_allowed, SAMPLE_KEYS, row_ok)
    sc = s_ref[...]
    tie = jnp.where(sc == lo, jnp.where(keyf <= jmax, 0.0, NEG), NEG)
    bias_ref[...] = jnp.where(sc > lo, 0.0, tie)

    lane = lax.broadcasted_iota(jnp.int32, (SAMPLE_Q, LANES), 1)
    for g in range(gsz):
        kb = kbuf[slot, g].astype(BF16)
        vb = vbuf[slot, g].astype(BF16)
        b = bias_ref[g * SAMPLE_Q:(g + 1) * SAMPLE_Q, :]
        s = jnp.dot(q_ref[g], kb, preferred_element_type=F32)
        s = s + jnp.concatenate([b] * N_HEADS, axis=0)
        m = jnp.max(s, axis=1, keepdims=True)
        p = jnp.exp2(s - m)
        l = jnp.sum(p, axis=1, keepdims=True)
        o = lax.dot_general(p.astype(BF16), vb, (((1,), (1,)), ((), ())),
                            preferred_element_type=F32) / l
        o_r = pltpu.roll(o, HEAD_DIM, 1)
        tiles = []
        for c in range(4):
            ha, hb = 2 * c, 2 * c + 1
            src_a = o if ha // 4 == 0 else o_r
            src_b = o if hb // 4 == 1 else o_r
            tiles.append(jnp.where(lane < HEAD_DIM, src_a[ha * SAMPLE_Q:(ha + 1) * SAMPLE_Q, :],
                                   src_b[hb * SAMPLE_Q:(hb + 1) * SAMPLE_Q, :]))
        o_ref[g] = jnp.concatenate(tiles, axis=1)


def _attn_sample(page_table, qi_s, q_s, w_s, ki_new, k_new, v_new, ck, cv, cki):
    gsz = SAMPLE_GROUP
    grid = (DEC_BATCH // gsz,)
    blk = lambda shape: pl.BlockSpec((gsz,) + shape, lambda s, pt: (s, 0, 0))
    grid_spec = pltpu.PrefetchScalarGridSpec(
        num_scalar_prefetch=1, grid=grid,
        in_specs=[blk((64, IDX_DIM)), blk((64, 128)), blk((64, LANES)),
                  blk((IDX_DIM, SAMPLE_Q)), blk((128, SAMPLE_Q)), blk((128, SAMPLE_Q)),
                  pl.BlockSpec(memory_space=pl.ANY), pl.BlockSpec(memory_space=pl.ANY),
                  pl.BlockSpec(memory_space=pl.ANY)],
        out_specs=blk((SAMPLE_Q, W_A)),
        scratch_shapes=[pltpu.VMEM((2, gsz, 128, SAMPLE_KEYS), F32),
                        pltpu.VMEM((2, gsz, 128, SAMPLE_KEYS), F32),
                        pltpu.VMEM((2, gsz, IDX_DIM, SAMPLE_KEYS), F32),
                        pltpu.SemaphoreType.DMA((2, 3)),
                        pltpu.VMEM((gsz * SAMPLE_Q, SAMPLE_KEYS), F32),
                        pltpu.VMEM((gsz * SAMPLE_Q, SAMPLE_KEYS), F32)])
    return pl.pallas_call(
        _attn_sample_kernel, grid_spec=grid_spec,
        out_shape=jax.ShapeDtypeStruct((DEC_BATCH, SAMPLE_Q, W_A), F32),
        compiler_params=pltpu.CompilerParams(dimension_semantics=("arbitrary",),
                                             vmem_limit_bytes=VMEM_LIMIT),
        name="attn_sample",
    )(page_table, qi_s, q_s, w_s, ki_new, k_new, v_new, ck, cv, cki)


def _layer_norm(x, g, b):
    mu = jnp.mean(x, axis=-1, keepdims=True)
    xc = x - mu
    var = jnp.mean(xc * xc, axis=-1, keepdims=True)
    return xc * lax.rsqrt(var + LN_EPS) * g + b


def _silu(x):
    return x / (1.0 + jnp.exp(-x))


def _mixer_kernel(x_ref, a_ref, wb_ref, wmix_ref, bsb_ref, lnvg_ref, lnvb_ref, wout_ref,
                  lng_ref, lnb_ref, y_ref, vn_ref, *, chunk, vn_last_only, tiles_per_seq):
    tm = x_ref.shape[0]
    x = x_ref[...]
    zb = jnp.dot(x.astype(BF16), wb_ref[...], preferred_element_type=F32)
    ga = zb[:, 0:512]
    u = zb[:, 512:1024]
    vb = zb[:, 1024:1536]
    gb = zb[:, 1536:2048]
    vn = _layer_norm(vb, lnvg_ref[...], lnvb_ref[...])
    if vn_last_only:
        @pl.when(pl.program_id(0) % tiles_per_seq == tiles_per_seq - 1)
        def _():
            vn_ref[0] = vn[tm - CHUNK:, :]
    else:
        vn_ref[...] = vn
    vnb = vn.astype(BF16)
    rows = []
    for c in range(tm // chunk):
        cols = []
        for g in range(N_GROUPS_B):
            blk = vnb[c * chunk:(c + 1) * chunk, g * GROUP_W_B:(g + 1) * GROUP_W_B]
            cols.append(jnp.dot(wmix_ref[g], blk, preferred_element_type=F32) + bsb_ref[g])
        rows.append(jnp.concatenate(cols, axis=1))
    mixed = jnp.concatenate(rows, axis=0) if len(rows) > 1 else rows[0]
    a_out = a_ref[...] * _silu(ga)
    b_out = u * mixed * _silu(gb)
    cat = jnp.concatenate([a_out, b_out], axis=1).astype(BF16)
    out = jnp.dot(cat, wout_ref[...], preferred_element_type=F32)
    y_ref[...] = _layer_norm(ALPHA * x + out, lng_ref[...], lnb_ref[...])


def _mixer(x2d, attn, wb, wmix, bsb, lnvg, lnvb, wout, lng, lnb, *, chunk, vn_last_only):
    rows = x2d.shape[0]
    tm = ROW_TILE
    tiles_per_seq = SEQ // tm
    grid = (rows // tm,)
    row_map = lambda r: (r, 0)
    const2 = lambda r: (0, 0)
    const3 = lambda r: (0, 0, 0)
    if vn_last_only:
        vn_shape = jax.ShapeDtypeStruct((rows // SEQ, CHUNK, W_B), F32)
        vn_spec = pl.BlockSpec((1, CHUNK, W_B), lambda r: (r // tiles_per_seq, 0, 0))
    else:
        vn_shape = jax.ShapeDtypeStruct((rows, W_B), F32)
        vn_spec = pl.BlockSpec((tm, W_B), row_map)
    kern = functools.partial(_mixer_kernel, chunk=chunk, vn_last_only=vn_last_only,
                             tiles_per_seq=tiles_per_seq)
    return pl.pallas_call(
        kern, grid=grid,
        in_specs=[pl.BlockSpec((tm, D_MODEL), row_map), pl.BlockSpec((tm, W_A), row_map),
                  pl.BlockSpec(wb.shape, const2), pl.BlockSpec(wmix.shape, const3),
                  pl.BlockSpec(bsb.shape, const3), pl.BlockSpec((1, W_B), const2),
                  pl.BlockSpec((1, W_B), const2), pl.BlockSpec(wout.shape, const2),
                  pl.BlockSpec((1, D_MODEL), const2), pl.BlockSpec((1, D_MODEL), const2)],
        out_specs=(pl.BlockSpec((tm, D_MODEL), row_map), vn_spec),
        out_shape=(jax.ShapeDtypeStruct((rows, D_MODEL), F32), vn_shape),
        compiler_params=pltpu.CompilerParams(dimension_semantics=("arbitrary",),
                                             vmem_limit_bytes=VMEM_LIMIT),
        name="mixer_chunk%d" % chunk,
    )(x2d, attn, wb, wmix, bsb, lnvg, lnvb, wout, lng, lnb)


def kernel(x_prompt, x_sample, cache_k, cache_v, cache_k_idx, page_table, w_in, w_s, b_s,
           ln_v_g, ln_v_b, w_out, ln_g, ln_b):
    w = w_in[0]
    wq, wk, wv = w[:, 0:512], w[:, 512:640], w[:, 640:768]
    wga, wqi, wki, wwi = w[:, 768:1280], w[:, 1280:1792], w[:, 1792:1856], w[:, 1856:1864]
    wu, wvb, wgb = w[:, 1864:2376], w[:, 2376:2888], w[:, 2888:3400]
    wstd = jnp.concatenate([wk, wv, wki, jnp.zeros((D_MODEL, 64), F32)], axis=1).astype(BF16)
    wt = jnp.concatenate([wq * Q_SCALE, wqi, wv, wwi,
                          jnp.zeros((D_MODEL, WT_ROWS - 1160), F32)], axis=1).T.astype(BF16)
    wb = jnp.concatenate([wga, wu, wvb, wgb], axis=1).astype(BF16)
    wout = w_out[0].astype(BF16)
    lnvg, lnvb = ln_v_g[0][None, :], ln_v_b[0][None, :]
    lng, lnb = ln_g[0][None, :], ln_b[0][None, :]

    xp = x_prompt.reshape(BATCH * SEQ, D_MODEL)
    tabs_p = _rope_tables(jnp.arange(SEQ, dtype=jnp.int32))
    k_p, v_p, ki_p, kbf, kiw, qt, qit, vt3, wit = _project(xp, wstd, wt, *tabs_p, SEQ // ROW_TILE)
    attn_p = _attn_prompt(qt, qit, wit, kbf, kiw, vt3)
    tril = jnp.tril(jnp.ones((CHUNK, CHUNK), F32))
    wmix_p = (w_s[0] * tril[None]).astype(BF16)
    bsb_p = jnp.broadcast_to(b_s[0][:, :, None], (N_GROUPS_B, CHUNK, GROUP_W_B))
    y_p, vn_p = _mixer(xp, attn_p, wb, wmix_p, bsb_p, lnvg, lnvb, wout, lng, lnb,
                       chunk=CHUNK, vn_last_only=True)

    rows_s = DEC_BATCH * DEC_SEQ
    xs = x_sample.reshape(rows_s, D_MODEL)
    pos_s = PAST_LEN + (jnp.arange(rows_s, dtype=jnp.int32) % DEC_SEQ)
    tabs_s = _rope_tables(pos_s)
    k_s, v_s, ki_s, _, _, qt_s, qit_s, _, wit_s = _project(xs, wstd, wt, *tabs_s, 1)

    def to_rows(a_t):
        a = a_t.reshape(N_HEADS, HEAD_DIM, DEC_BATCH, DEC_SEQ).transpose(2, 0, 3, 1)
        return jnp.pad(a, ((0, 0), (0, 0), (0, SAMPLE_Q - DEC_SEQ), (0, 0)))

    qi_rows = to_rows(qit_s).reshape(DEC_BATCH, N_IDX_HEADS * SAMPLE_Q, IDX_DIM)
    q5 = to_rows(qt_s).reshape(DEC_BATCH, N_KV, N_HEADS // N_KV, SAMPLE_Q, HEAD_DIM)
    zq = jnp.zeros_like(q5[:, 0])
    q_rows = jnp.stack([jnp.concatenate([q5[:, 0], zq], axis=-1),
                        jnp.concatenate([zq, q5[:, 1]], axis=-1)], axis=1)
    q_rows = q_rows.reshape(DEC_BATCH, N_HEADS * SAMPLE_Q, 2 * HEAD_DIM)
    w_rows = jnp.pad(wit_s.reshape(N_IDX_HEADS, DEC_BATCH, DEC_SEQ).transpose(1, 0, 2),
                     ((0, 0), (0, 0), (0, SAMPLE_Q - DEC_SEQ)))
    w_rows = jnp.broadcast_to(w_rows.reshape(DEC_BATCH, N_IDX_HEADS * SAMPLE_Q, 1),
                              (DEC_BATCH, N_IDX_HEADS * SAMPLE_Q, LANES))
    new_cols = lambda a: jnp.pad(a.reshape(DEC_BATCH, DEC_SEQ, a.shape[-1]).transpose(0, 2, 1),
                                 ((0, 0), (0, 0), (0, SAMPLE_Q - DEC_SEQ)))
    n_pool = cache_k.shape[1]
    ck_t = cache_k[0].transpose(0, 2, 3, 1).reshape(n_pool, N_KV * HEAD_DIM, PAGE_SIZE)
    cv_t = cache_v[0].transpose(0, 2, 3, 1).reshape(n_pool, N_KV * HEAD_DIM, PAGE_SIZE)
    cki_t = cache_k_idx[0].transpose(0, 2, 1)
    attn_s8 = _attn_sample(page_table, qi_rows, q_rows, w_rows, new_cols(ki_s), new_cols(k_s),
                           new_cols(v_s), ck_t, cv_t, cki_t)
    attn_s = attn_s8[:, :DEC_SEQ, :].reshape(rows_s, W_A)
    ri = jnp.arange(rows_s, dtype=jnp.int32)
    same_seq = (ri[:, None] // DEC_SEQ) == (ri[None, :] // DEC_SEQ)
    wmix_s = jnp.zeros((N_GROUPS_B, rows_s, rows_s), F32)
    for t in range(DEC_SEQ):
        for s in range(t + 1):
            hit = same_seq & (ri[:, None] % DEC_SEQ == t) & (ri[None, :] % DEC_SEQ == s)
            wmix_s = wmix_s + jnp.where(hit[None], w_s[0, :, t, s][:, None, None], 0.0)
    wmix_s = wmix_s.astype(BF16)
    bsb_s = jnp.broadcast_to(jnp.tile(b_s[0][:, :DEC_SEQ], (1, DEC_BATCH))[:, :, None],
                             (N_GROUPS_B, rows_s, GROUP_W_B))
    y_s, vn_s = _mixer(xs, attn_s, wb, wmix_s, bsb_s, lnvg, lnvb, wout, lng, lnb,
                       chunk=rows_s, vn_last_only=False)

    return (y_p.reshape(BATCH, SEQ, D_MODEL),
            y_s.reshape(DEC_BATCH, DEC_SEQ, D_MODEL),
            k_p.reshape(1, BATCH, SEQ, N_KV, HEAD_DIM),
            v_p.reshape(1, BATCH, SEQ, N_KV, HEAD_DIM),
            ki_p.reshape(1, BATCH, SEQ, IDX_DIM),
            vn_p.reshape(1, BATCH, CHUNK, W_B),
            k_s.reshape(1, DEC_BATCH, DEC_SEQ, N_KV, HEAD_DIM),
            v_s.reshape(1, DEC_BATCH, DEC_SEQ, N_KV, HEAD_DIM),
            ki_s.reshape(1, DEC_BATCH, DEC_SEQ, IDX_DIM),
            vn_s.reshape(1, DEC_BATCH, DEC_SEQ, W_B))
```

```python
import functools

import jax
import jax.numpy as jnp
from jax import lax
from jax.experimental import pallas as pl
from jax.experimental.pallas import tpu as pltpu

F32 = jnp.float32
BF16 = jnp.bfloat16

D_MODEL = 1024
SEQ = 2048
BATCH = 8
DEC_BATCH = 128
DEC_SEQ = 4
PAST_LEN = 2048
PAGE_SIZE = 128
N_PAGES = PAST_LEN // PAGE_SIZE
W_A = 512
W_B = 512
HEAD_DIM = 64
N_HEADS = 8
N_KV = 2
N_IDX_HEADS = 8
IDX_DIM = 64
TOPK = 256
CHUNK = 128
N_GROUPS_B = 4
GROUP_W_B = 128
ROPE_THETA = 10000.0
LN_EPS = 1e-5
ALPHA = 2.0 ** 0.25
NEG = -1e30
IDX_SCALE = float((N_IDX_HEADS * IDX_DIM) ** -0.5)

LANES = 128
KEY_TILE = 128
KEY_STEP = 128
ROW_TILE = 512
WT_ROWS = 1168
BISECT_ITERS = 24
FOLD_CHAINS = 8
ONES_ROWS = 16
Q_SCALE = float(HEAD_DIM ** -0.5 * 1.4426950408889634)
VMEM_LIMIT = 48 * 1024 * 1024

SAMPLE_GROUP = 4
SAMPLE_KEYS = PAST_LEN + KEY_TILE
SAMPLE_Q = 8


def _proj_kernel(x_ref, wstd_ref, wt_ref, cosl_ref, sinl_ref, cost_ref, sint_ref,
                 k_ref, v_ref, ki_ref, kbf_ref, kiw_ref, qt_ref, qit_ref, vt_ref, wit_ref):
    tm = x_ref.shape[0]
    xb = x_ref[...].astype(BF16)
    zs = jnp.dot(xb, wstd_ref[...], preferred_element_type=F32)
    cosl = cosl_ref[...]
    sinl = sinl_ref[...]
    lane = lax.broadcasted_iota(jnp.int32, (tm, LANES), 1)
    first_half = (lane % HEAD_DIM) < (HEAD_DIM // 2)

    def rope_lanes(z):
        partner = jnp.where(first_half, pltpu.roll(z, 96, 1), pltpu.roll(z, 32, 1))
        return z * cosl + partner * sinl

    k = rope_lanes(zs[:, 0:128])
    v = zs[:, 128:256]
    kiw = rope_lanes(zs[:, 256:384])
    k_ref[...] = k
    v_ref[...] = v
    ki_ref[...] = kiw[:, 0:IDX_DIM]
    kbf_ref[...] = k.astype(BF16)
    kiw_ref[...] = kiw.astype(BF16)

    zt = lax.dot_general(wt_ref[...], xb, (((1,), (1,)), ((), ())),
                         preferred_element_type=F32)
    cost = cost_ref[...]
    sint = sint_ref[...]
    half = HEAD_DIM // 2

    def rope_rows(base):
        pieces = []
        for h in range(N_HEADS):
            x1 = zt[base + h * HEAD_DIM: base + h * HEAD_DIM + half]
            x2 = zt[base + h * HEAD_DIM + half: base + (h + 1) * HEAD_DIM]
            pieces.append(x1 * cost - x2 * sint)
            pieces.append(x2 * cost + x1 * sint)
        return jnp.concatenate(pieces, axis=0)

    qt_ref[...] = rope_rows(0).astype(BF16)
    qit_ref[...] = rope_rows(512).astype(BF16)
    vtb = zt[1024:1152].astype(BF16)
    for c in range(tm // KEY_STEP):
        vt_ref[c] = vtb[:, c * KEY_STEP:(c + 1) * KEY_STEP]
    wit_ref[...] = zt[1152:1160] * IDX_SCALE


def _project(x2d, wstd, wt, cosl, sinl, cost, sint, n_pos_tiles):
    rows = x2d.shape[0]
    tm = ROW_TILE
    grid = (rows // tm,)
    row_map = lambda r: (r, 0)
    col_map = lambda r: (0, r)
    const = lambda r: (0, 0)
    out_shape = (
        jax.ShapeDtypeStruct((rows, 128), F32),
        jax.ShapeDtypeStruct((rows, 128), F32),
        jax.ShapeDtypeStruct((rows, IDX_DIM), F32),
        jax.ShapeDtypeStruct((rows, 128), BF16),
        jax.ShapeDtypeStruct((rows, 128), BF16),
        jax.ShapeDtypeStruct((512, rows), BF16),
        jax.ShapeDtypeStruct((512, rows), BF16),
        jax.ShapeDtypeStruct((rows // KEY_STEP, 128, KEY_STEP), BF16),
        jax.ShapeDtypeStruct((8, rows), F32),
    )
    out_specs = (
        pl.BlockSpec((tm, 128), row_map),
        pl.BlockSpec((tm, 128), row_map),
        pl.BlockSpec((tm, IDX_DIM), row_map),
        pl.BlockSpec((tm, 128), row_map),
        pl.BlockSpec((tm, 128), row_map),
        pl.BlockSpec((512, tm), col_map),
        pl.BlockSpec((512, tm), col_map),
        pl.BlockSpec((tm // KEY_STEP, 128, KEY_STEP), lambda r: (r, 0, 0)),
        pl.BlockSpec((8, tm), col_map),
    )
    in_specs = [
        pl.BlockSpec((tm, D_MODEL), row_map),
        pl.BlockSpec(wstd.shape, const),
        pl.BlockSpec(wt.shape, const),
        pl.BlockSpec((tm, LANES), lambda r: (r % n_pos_tiles, 0)),
        pl.BlockSpec((tm, LANES), lambda r: (r % n_pos_tiles, 0)),
        pl.BlockSpec((HEAD_DIM // 2, tm), lambda r: (0, r % n_pos_tiles)),
        pl.BlockSpec((HEAD_DIM // 2, tm), lambda r: (0, r % n_pos_tiles)),
    ]
    return pl.pallas_call(
        _proj_kernel, grid=grid, in_specs=in_specs, out_specs=out_specs, out_shape=out_shape,
        compiler_params=pltpu.CompilerParams(dimension_semantics=("arbitrary",),
                                             vmem_limit_bytes=VMEM_LIMIT),
        name="proj_attn_side",
    )(x2d, wstd, wt, cosl, sinl, cost, sint)


def _rope_tables(pos):
    half = HEAD_DIM // 2
    inv = ROPE_THETA ** (-jnp.arange(half, dtype=F32) / half)
    ang = pos.astype(F32)[:, None] * inv[None, :]
    cos = jnp.cos(ang)
    sin = jnp.sin(ang)
    cosl = jnp.tile(cos, (1, LANES // half))
    sinl = jnp.tile(jnp.concatenate([-sin, sin], axis=1), (1, LANES // HEAD_DIM))
    return cosl, sinl, cos.T, sin.T


def _select_threshold(count_ge, count_gt, count_eq_le, min_ge, max_lt, lo0, hi0, n_allowed, n_keys,
                      row_ok):
    k = float(TOPK)

    def step(carry):
        lo, hi, clo = carry
        mid = 0.5 * lo + 0.5 * hi
        c = count_ge(mid)
        take = c >= k
        return jnp.where(take, mid, lo), jnp.where(take, hi, mid), jnp.where(take, c, clo)

    lo, hi, clo = lax.fori_loop(0, BISECT_ITERS, lambda _, c: step(c), (lo0, hi0, n_allowed))
    big = jnp.full_like(lo, float(4 * n_keys))

    def unresolved(clo_):
        return jnp.logical_and(clo_ > k, row_ok)

    def any_true(mask):
        return jnp.max(jnp.where(mask, 1.0, 0.0)) > 0.5

    def exact(args):
        lo, hi, clo = args

        def cond(c):
            _, _, clo, vlo, vhi, it = c
            open_ = jnp.logical_and(unresolved(clo), vlo < vhi)
            return jnp.logical_and(any_true(open_), it < 2 * n_keys)

        def body(c):
            lo, hi, clo, vlo, vhi, it = c
            mid = 0.5 * vlo + 0.5 * vhi
            mid = jnp.where(mid > vlo, mid, vhi)
            cnt = count_ge(mid)
            take = cnt >= k
            lo, hi, clo = jnp.where(take, mid, lo), jnp.where(take, hi, mid), jnp.where(take, cnt, clo)
            return lo, hi, clo, min_ge(lo), max_lt(hi), it + 1

        lo, hi, clo, vlo, _, _ = lax.while_loop(
            cond, body, (lo, hi, clo, min_ge(lo), max_lt(hi), jnp.int32(0)))
        lo = vlo
        need = k - count_gt(lo)

        def jstep(_, c):
            jlo, jhi = c
            mid = jnp.floor(0.5 * (jlo + jhi))
            ok = count_eq_le(lo, mid) >= need
            return jnp.where(ok, jlo, mid), jnp.where(ok, mid, jhi)

        jlo0 = jnp.full_like(lo, -1.0)
        jhi0 = jnp.full_like(lo, float(n_keys - 1))
        _, jhi = lax.fori_loop(0, 13, jstep, (jlo0, jhi0))
        return lo, jnp.where(clo > k, jhi, big)

    def fast(args):
        return args[0], big

    return lax.cond(any_true(unresolved(clo)), exact, fast, (lo, hi, clo))


def _fold_rows(x, op):
    groups = [x[r:r + 8] for r in range(0, x.shape[0], 8)]
    parts = groups[:FOLD_CHAINS]
    for n, grp in enumerate(groups[FOLD_CHAINS:]):
        parts[n % FOLD_CHAINS] = op(parts[n % FOLD_CHAINS], grp)
    while len(parts) > 1:
        nxt = [op(parts[a], parts[a + 1]) for a in range(0, len(parts) - 1, 2)]
        if len(parts) % 2:
            nxt.append(parts[-1])
        parts = nxt
    return parts[0]


def _attn_prompt_kernel(qt_ref, qit_ref, wit_ref, kbf_ref, kiw_ref, vt_ref, o_ref,
                        s_ref, sc_ref, ri_ref, rq_ref):
    i = pl.program_id(1)
    w = wit_ref[...]
    zeros_half = jnp.zeros((HEAD_DIM, LANES), BF16)

    for hp in range(4):
        a = qit_ref[(2 * hp) * 64:(2 * hp + 1) * 64, :]
        b = qit_ref[(2 * hp + 1) * 64:(2 * hp + 2) * 64, :]
        ri_ref[hp] = jnp.concatenate(
            [jnp.concatenate([a, zeros_half], axis=0), jnp.concatenate([b, zeros_half], axis=0)], axis=1)
        g = hp // 2
        a = qt_ref[(2 * hp) * 64:(2 * hp + 1) * 64, :]
        b = qt_ref[(2 * hp + 1) * 64:(2 * hp + 2) * 64, :]
        if g == 0:
            ab = jnp.concatenate([jnp.concatenate([a, zeros_half], axis=0),
                                  jnp.concatenate([b, zeros_half], axis=0)], axis=1)
        else:
            ab = jnp.concatenate([jnp.concatenate([zeros_half, a], axis=0),
                                  jnp.concatenate([zeros_half, b], axis=0)], axis=1)
        rq_ref[hp] = ab

    inf = jnp.float32(jnp.inf)
    n_steps = (i * LANES + LANES + KEY_STEP - 1) // KEY_STEP

    def process(ns):
        n_keys = ns * KEY_STEP
        keys = slice(0, n_keys)
        row = lax.broadcasted_iota(jnp.int32, (n_keys, LANES), 0)
        col = lax.broadcasted_iota(jnp.int32, (n_keys, LANES), 1)
        qpos = col + i * LANES
        allowed = row <= qpos

        kt = kiw_ref[keys, :]
        acc = jnp.zeros((n_keys, LANES), F32)
        for hp in range(4):
            s2 = jnp.dot(kt, ri_ref[hp], preferred_element_type=F32)
            acc = acc + jnp.maximum(s2[:, :LANES], 0.0) * w[2 * hp:2 * hp + 1, :]
            acc = acc + jnp.maximum(s2[:, LANES:], 0.0) * w[2 * hp + 1:2 * hp + 2, :]
        s_ref[keys, :] = jnp.where(allowed, acc, -inf)
        lo0 = jnp.min(_fold_rows(jnp.where(allowed, acc, inf), jnp.minimum), axis=0, keepdims=True)
        mx1 = jnp.max(_fold_rows(jnp.where(allowed, acc, -inf), jnp.maximum), axis=0, keepdims=True)
        hi0 = mx1 + (jnp.abs(mx1) * (2.0 ** -20) + 1e-30)
        n_allowed = (qpos[0:1, :] + 1).astype(F32)

        def key_reduce(fn, op, finish):
            return finish(_fold_rows(fn(s_ref[keys, :]), op), axis=0, keepdims=True)

        kidx = row.astype(F32)
        count_ge = lambda thr: key_reduce(lambda t: jnp.where(t >= thr, 1.0, 0.0), jnp.add, jnp.sum)
        count_gt = lambda thr: key_reduce(lambda t: jnp.where(t > thr, 1.0, 0.0), jnp.add, jnp.sum)
        count_eq_le = lambda thr, jm: key_reduce(
            lambda t: jnp.where(t == thr, jnp.where(kidx <= jm, 1.0, 0.0), 0.0), jnp.add, jnp.sum)
        min_ge = lambda thr: key_reduce(lambda t: jnp.where(t >= thr, t, inf), jnp.minimum, jnp.min)
        max_lt = lambda thr: key_reduce(lambda t: jnp.where(t < thr, t, -inf), jnp.maximum, jnp.max)
        lo, jmax = _select_threshold(count_ge, count_gt, count_eq_le, min_ge, max_lt, lo0, hi0,
                                     n_allowed, SEQ, jnp.full((1, LANES), True))

        t = s_ref[keys, :]
        tie = jnp.where(t == lo, jnp.where(kidx <= jmax, 0.0, NEG), NEG)
        s_ref[keys, :] = jnp.where(t > lo, 0.0, tie)

        kt = kbf_ref[keys, :]
        b = s_ref[keys, :]
        b2 = jnp.concatenate([b, b], axis=1)
        ms = []
        for hp in range(4):
            s2 = jnp.dot(kt, rq_ref[hp], preferred_element_type=F32) + b2
            sc_ref[hp, keys, :] = s2
            ms.append(jnp.max(_fold_rows(s2, jnp.maximum), axis=0, keepdims=True))
        vt = jnp.concatenate([vt_ref[j] for j in range(ns)], axis=1)
        ones_rows = jnp.ones((ONES_ROWS, n_keys), BF16)
        lhs = [jnp.concatenate([vt[g * 64:(g + 1) * 64, :], ones_rows], axis=0) for g in range(N_KV)]
        pieces = []
        for hp in range(4):
            p = jnp.exp2((sc_ref[hp, keys, :] - ms[hp]).astype(BF16))
            o = jnp.dot(lhs[hp // 2], p, preferred_element_type=F32)
            o2 = o[0:HEAD_DIM, :] / o[HEAD_DIM:HEAD_DIM + 1, :]
            pieces.append(o2[:, :LANES])
            pieces.append(o2[:, LANES:])
        o_ref[...] = jnp.concatenate(pieces, axis=0).T

    for ns in range(1, SEQ // KEY_STEP + 1):
        pl.when(n_steps == ns)(functools.partial(process, ns))


def _attn_prompt(qt, qit, wit, kbf, kiw, vt3):
    n_blocks = SEQ // LANES
    grid = (BATCH, n_blocks)
    qmap = lambda n, i: (0, n * n_blocks + i)
    in_specs = [
        pl.BlockSpec((512, LANES), qmap),
        pl.BlockSpec((512, LANES), qmap),
        pl.BlockSpec((8, LANES), qmap),
        pl.BlockSpec((SEQ, 128), lambda n, i: (n, 0)),
        pl.BlockSpec((SEQ, 128), lambda n, i: (n, 0)),
        pl.BlockSpec((SEQ // KEY_STEP, 128, KEY_STEP), lambda n, i: (n, 0, 0)),
    ]
    return pl.pallas_call(
        _attn_prompt_kernel, grid=grid, in_specs=in_specs,
        out_specs=pl.BlockSpec((LANES, W_A), lambda n, i: (n * n_blocks + i, 0)),
        out_shape=jax.ShapeDtypeStruct((BATCH * SEQ, W_A), F32),
        scratch_shapes=[pltpu.VMEM((SEQ, LANES), F32), pltpu.VMEM((4, SEQ, 2 * LANES), F32),
                        pltpu.VMEM((4, 128, 2 * LANES), BF16), pltpu.VMEM((4, 128, 2 * LANES), BF16)],
        compiler_params=pltpu.CompilerParams(dimension_semantics=("arbitrary", "arbitrary"),
                                             vmem_limit_bytes=VMEM_LIMIT),
        name="attn_prompt",
    )(qt, qit, wit, kbf, kiw, vt3)


def _attn_sample_kernel(pt_ref, qi_ref, q_ref, w_ref, kin_ref, kn_ref, vn_ref,
                        ck_hbm, cv_hbm, cki_hbm, o_ref,
                        kbuf, vbuf, kibuf, sems, s_ref, bias_ref):
    step = pl.program_id(0)
    n_steps = pl.num_programs(0)
    gsz = SAMPLE_GROUP
    slot = step % 2

    def copies(group, slot_):
        out = []
        for g in range(gsz):
            seq = group * gsz + g
            for p in range(N_PAGES):
                page = pt_ref[seq, p]
                keys = pl.ds(p * PAGE_SIZE, PAGE_SIZE)
                out.append(pltpu.make_async_copy(ck_hbm.at[page], kbuf.at[slot_, g, :, keys], sems.at[slot_, 0]))
                out.append(pltpu.make_async_copy(cv_hbm.at[page], vbuf.at[slot_, g, :, keys], sems.at[slot_, 1]))
                out.append(pltpu.make_async_copy(cki_hbm.at[page], kibuf.at[slot_, g, :, keys], sems.at[slot_, 2]))
        return out

    new_keys = pl.ds(PAST_LEN, SAMPLE_Q)

    @pl.when(step == 0)
    def _():
        tail = pl.ds(PAST_LEN, KEY_TILE)
        for s_ in range(2):
            for g in range(gsz):
                kbuf[s_, g, :, tail] = jnp.zeros((128, KEY_TILE), F32)
                vbuf[s_, g, :, tail] = jnp.zeros((128, KEY_TILE), F32)
                kibuf[s_, g, :, tail] = jnp.zeros((IDX_DIM, KEY_TILE), F32)
        for c in copies(0, 0):
            c.start()

    @pl.when(step + 1 < n_steps)
    def _():
        for c in copies(step + 1, 1 - slot):
            c.start()

    for c in copies(step, slot):
        c.wait()

    rows8 = lax.broadcasted_iota(jnp.int32, (SAMPLE_Q, SAMPLE_KEYS), 0)
    keyi = lax.broadcasted_iota(jnp.int32, (SAMPLE_Q, SAMPLE_KEYS), 1)
    allowed = keyi <= (PAST_LEN + rows8)
    inf = jnp.float32(jnp.inf)
    row_ok8 = lax.broadcasted_iota(jnp.int32, (SAMPLE_Q, 1), 0) < DEC_SEQ

    for g in range(gsz):
        kbuf[slot, g, :, new_keys] = kn_ref[g]
        vbuf[slot, g, :, new_keys] = vn_ref[g]
        kibuf[slot, g, :, new_keys] = kin_ref[g]
        kib = kibuf[slot, g].astype(BF16)
        s = jnp.dot(qi_ref[g], kib, preferred_element_type=F32)
        sw = jnp.maximum(s, 0.0) * w_ref[g][:, 0:1]
        acc = jnp.sum(sw.reshape(N_IDX_HEADS, SAMPLE_Q, SAMPLE_KEYS), axis=0)
        s_ref[g * SAMPLE_Q:(g + 1) * SAMPLE_Q, :] = jnp.where(allowed, acc, -inf)

    rows_all = gsz * SAMPLE_Q
    allowed_all = jnp.concatenate([allowed] * gsz, axis=0)
    row_ok = jnp.concatenate([row_ok8] * gsz, axis=0)
    keyf = lax.broadcasted_iota(jnp.int32, (rows_all, SAMPLE_KEYS), 1).astype(F32)

    sc = s_ref[...]
    lo0 = jnp.min(jnp.where(allowed_all, sc, inf), axis=1, keepdims=True)
    mx1 = jnp.max(sc, axis=1, keepdims=True)
    hi0 = mx1 + (jnp.abs(mx1) * (2.0 ** -20) + 1e-30)
    t8 = lax.broadcasted_iota(jnp.int32, (SAMPLE_Q, 1), 0)
    n_allowed = jnp.concatenate([(PAST_LEN + 1 + t8).astype(F32)] * gsz, axis=0)

    def lane_count(hit):
        return jnp.sum(hit, axis=1, keepdims=True)

    count_ge = lambda thr: lane_count(jnp.where(s_ref[...] >= thr, 1.0, 0.0))
    count_gt = lambda thr: lane_count(jnp.where(s_ref[...] > thr, 1.0, 0.0))
    count_eq_le = lambda thr, jm: lane_count(
        jnp.where(s_ref[...] == thr, jnp.where(keyf <= jm, 1.0, 0.0), 0.0))
    min_ge = lambda thr: jnp.min(jnp.where(s_ref[...] >= thr, s_ref[...], inf), axis=1, keepdims=True)
    max_lt = lambda thr: jnp.max(jnp.where(s_ref[...] < thr, s_ref[...], -inf), axis=1, keepdims=True)
    lo, jmax = _select_threshold(count_ge, count_gt, count_eq_le, min_ge, max_lt, lo0, hi0,
                                 n_allowed, SAMPLE_KEYS, row_ok)
    sc = s_ref[...]
    tie = jnp.where(sc == lo, jnp.where(keyf <= jmax, 0.0, NEG), NEG)
    bias_ref[...] = jnp.where(sc > lo, 0.0, tie)

    lane = lax.broadcasted_iota(jnp.int32, (SAMPLE_Q, LANES), 1)
    for g in range(gsz):
        kb = kbuf[slot, g].astype(BF16)
        vb = vbuf[slot, g].astype(BF16)
        b = bias_ref[g * SAMPLE_Q:(g + 1) * SAMPLE_Q, :]
        s = jnp.dot(q_ref[g], kb, preferred_element_type=F32)
        s = s + jnp.concatenate([b] * N_HEADS, axis=0)
        m = jnp.max(s, axis=1, keepdims=True)
        p = jnp.exp2(s - m)
        l = jnp.sum(p, axis=1, keepdims=True)
        o = lax.dot_general(p.astype(BF16), vb, (((1,), (1,)), ((), ())),
                            preferred_element_type=F32) / l
        o_r = pltpu.roll(o, HEAD_DIM, 1)
        tiles = []
        for c in range(4):
            ha, hb = 2 * c, 2 * c + 1
            src_a = o if ha // 4 == 0 else o_r
            src_b = o if hb // 4 == 1 else o_r
            tiles.append(jnp.where(lane < HEAD_DIM, src_a[ha * SAMPLE_Q:(ha + 1) * SAMPLE_Q, :],
                                   src_b[hb * SAMPLE_Q:(hb + 1) * SAMPLE_Q, :]))
        o_ref[g] = jnp.concatenate(tiles, axis=1)


def _attn_sample(page_table, qi_s, q_s, w_s, ki_new, k_new, v_new, ck, cv, cki):
    gsz = SAMPLE_GROUP
    grid = (DEC_BATCH // gsz,)
    blk = lambda shape: pl.BlockSpec((gsz,) + shape, lambda s, pt: (s, 0, 0))
    grid_spec = pltpu.PrefetchScalarGridSpec(
        num_scalar_prefetch=1, grid=grid,
        in_specs=[blk((64, IDX_DIM)), blk((64, 128)), blk((64, LANES)),
                  blk((IDX_DIM, SAMPLE_Q)), blk((128, SAMPLE_Q)), blk((128, SAMPLE_Q)),
                  pl.BlockSpec(memory_space=pl.ANY), pl.BlockSpec(memory_space=pl.ANY),
                  pl.BlockSpec(memory_space=pl.ANY)],
        out_specs=blk((SAMPLE_Q, W_A)),
        scratch_shapes=[pltpu.VMEM((2, gsz, 128, SAMPLE_KEYS), F32),
                        pltpu.VMEM((2, gsz, 128, SAMPLE_KEYS), F32),
                        pltpu.VMEM((2, gsz, IDX_DIM, SAMPLE_KEYS), F32),
                        pltpu.SemaphoreType.DMA((2, 3)),
                        pltpu.VMEM((gsz * SAMPLE_Q, SAMPLE_KEYS), F32),
                        pltpu.VMEM((gsz * SAMPLE_Q, SAMPLE_KEYS), F32)])
    return pl.pallas_call(
        _attn_sample_kernel, grid_spec=grid_spec,
        out_shape=jax.ShapeDtypeStruct((DEC_BATCH, SAMPLE_Q, W_A), F32),
        compiler_params=pltpu.CompilerParams(dimension_semantics=("arbitrary",),
                                             vmem_limit_bytes=VMEM_LIMIT),
        name="attn_sample",
    )(page_table, qi_s, q_s, w_s, ki_new, k_new, v_new, ck, cv, cki)


def _layer_norm(x, g, b):
    mu = jnp.mean(x, axis=-1, keepdims=True)
    xc = x - mu
    var = jnp.mean(xc * xc, axis=-1, keepdims=True)
    return xc * lax.rsqrt(var + LN_EPS) * g + b


def _silu(x):
    return x / (1.0 + jnp.exp(-x))


def _mixer_kernel(x_ref, a_ref, wb_ref, wmix_ref, bsb_ref, lnvg_ref, lnvb_ref, wout_ref,
                  lng_ref, lnb_ref, y_ref, vn_ref, *, chunk, vn_last_only, tiles_per_seq):
    tm = x_ref.shape[0]
    x = x_ref[...]
    zb = jnp.dot(x.astype(BF16), wb_ref[...], preferred_element_type=F32)
    ga = zb[:, 0:512]
    u = zb[:, 512:1024]
    vb = zb[:, 1024:1536]
    gb = zb[:, 1536:2048]
    vn = _layer_norm(vb, lnvg_ref[...], lnvb_ref[...])
    if vn_last_only:
        @pl.when(pl.program_id(0) % tiles_per_seq == tiles_per_seq - 1)
        def _():
            vn_ref[0] = vn[tm - CHUNK:, :]
    else:
        vn_ref[...] = vn
    vnb = vn.astype(BF16)
    rows = []
    for c in range(tm // chunk):
        cols = []
        for g in range(N_GROUPS_B):
            blk = vnb[c * chunk:(c + 1) * chunk, g * GROUP_W_B:(g + 1) * GROUP_W_B]
            cols.append(jnp.dot(wmix_ref[g], blk, preferred_element_type=F32) + bsb_ref[g])
        rows.append(jnp.concatenate(cols, axis=1))
    mixed = jnp.concatenate(rows, axis=0) if len(rows) > 1 else rows[0]
    a_out = a_ref[...] * _silu(ga)
    b_out = u * mixed * _silu(gb)
    cat = jnp.concatenate([a_out, b_out], axis=1).astype(BF16)
    out = jnp.dot(cat, wout_ref[...], preferred_element_type=F32)
    y_ref[...] = _layer_norm(ALPHA * x + out, lng_ref[...], lnb_ref[...])


def _mixer(x2d, attn, wb, wmix, bsb, lnvg, lnvb, wout, lng, lnb, *, chunk, vn_last_only):
    rows = x2d.shape[0]
    tm = ROW_TILE
    tiles_per_seq = SEQ // tm
    grid = (rows // tm,)
    row_map = lambda r: (r, 0)
    const2 = lambda r: (0, 0)
    const3 = lambda r: (0, 0, 0)
    if vn_last_only:
        vn_shape = jax.ShapeDtypeStruct((rows // SEQ, CHUNK, W_B), F32)
        vn_spec = pl.BlockSpec((1, CHUNK, W_B), lambda r: (r // tiles_per_seq, 0, 0))
    else:
        vn_shape = jax.ShapeDtypeStruct((rows, W_B), F32)
        vn_spec = pl.BlockSpec((tm, W_B), row_map)
    kern = functools.partial(_mixer_kernel, chunk=chunk, vn_last_only=vn_last_only,
                             tiles_per_seq=tiles_per_seq)
    return pl.pallas_call(
        kern, grid=grid,
        in_specs=[pl.BlockSpec((tm, D_MODEL), row_map), pl.BlockSpec((tm, W_A), row_map),
                  pl.BlockSpec(wb.shape, const2), pl.BlockSpec(wmix.shape, const3),
                  pl.BlockSpec(bsb.shape, const3), pl.BlockSpec((1, W_B), const2),
                  pl.BlockSpec((1, W_B), const2), pl.BlockSpec(wout.shape, const2),
                  pl.BlockSpec((1, D_MODEL), const2), pl.BlockSpec((1, D_MODEL), const2)],
        out_specs=(pl.BlockSpec((tm, D_MODEL), row_map), vn_spec),
        out_shape=(jax.ShapeDtypeStruct((rows, D_MODEL), F32), vn_shape),
        compiler_params=pltpu.CompilerParams(dimension_semantics=("arbitrary",),
                                             vmem_limit_bytes=VMEM_LIMIT),
        name="mixer_chunk%d" % chunk,
    )(x2d, attn, wb, wmix, bsb, lnvg, lnvb, wout, lng, lnb)


def kernel(x_prompt, x_sample, cache_k, cache_v, cache_k_idx, page_table, w_in, w_s, b_s,
           ln_v_g, ln_v_b, w_out, ln_g, ln_b):
    w = w_in[0]
    wq, wk, wv = w[:, 0:512], w[:, 512:640], w[:, 640:768]
    wga, wqi, wki, wwi = w[:, 768:1280], w[:, 1280:1792], w[:, 1792:1856], w[:, 1856:1864]
    wu, wvb, wgb = w[:, 1864:2376], w[:, 2376:2888], w[:, 2888:3400]
    wstd = jnp.concatenate([wk, wv, wki, jnp.zeros((D_MODEL, 64), F32)], axis=1).astype(BF16)
    wt = jnp.concatenate([wq * Q_SCALE, wqi, wv, wwi,
                          jnp.zeros((D_MODEL, WT_ROWS - 1160), F32)], axis=1).T.astype(BF16)
    wb = jnp.concatenate([wga, wu, wvb, wgb], axis=1).astype(BF16)
    wout = w_out[0].astype(BF16)
    lnvg, lnvb = ln_v_g[0][None, :], ln_v_b[0][None, :]
    lng, lnb = ln_g[0][None, :], ln_b[0][None, :]

    xp = x_prompt.reshape(BATCH * SEQ, D_MODEL)
    tabs_p = _rope_tables(jnp.arange(SEQ, dtype=jnp.int32))
    k_p, v_p, ki_p, kbf, kiw, qt, qit, vt3, wit = _project(xp, wstd, wt, *tabs_p, SEQ // ROW_TILE)
    attn_p = _attn_prompt(qt, qit, wit, kbf, kiw, vt3)
    tril = jnp.tril(jnp.ones((CHUNK, CHUNK), F32))
    wmix_p = (w_s[0] * tril[None]).astype(BF16)
    bsb_p = jnp.broadcast_to(b_s[0][:, :, None], (N_GROUPS_B, CHUNK, GROUP_W_B))
    y_p, vn_p = _mixer(xp, attn_p, wb, wmix_p, bsb_p, lnvg, lnvb, wout, lng, lnb,
                       chunk=CHUNK, vn_last_only=True)

    rows_s = DEC_BATCH * DEC_SEQ
    xs = x_sample.reshape(rows_s, D_MODEL)
    pos_s = PAST_LEN + (jnp.arange(rows_s, dtype=jnp.int32) % DEC_SEQ)
    tabs_s = _rope_tables(pos_s)
    k_s, v_s, ki_s, _, _, qt_s, qit_s, _, wit_s = _project(xs, wstd, wt, *tabs_s, 1)

    def to_rows(a_t):
        a = a_t.reshape(N_HEADS, HEAD_DIM, DEC_BATCH, DEC_SEQ).transpose(2, 0, 3, 1)
        return jnp.pad(a, ((0, 0), (0, 0), (0, SAMPLE_Q - DEC_SEQ), (0, 0)))

    qi_rows = to_rows(qit_s).reshape(DEC_BATCH, N_IDX_HEADS * SAMPLE_Q, IDX_DIM)
    q5 = to_rows(qt_s).reshape(DEC_BATCH, N_KV, N_HEADS // N_KV, SAMPLE_Q, HEAD_DIM)
    zq = jnp.zeros_like(q5[:, 0])
    q_rows = jnp.stack([jnp.concatenate([q5[:, 0], zq], axis=-1),
                        jnp.concatenate([zq, q5[:, 1]], axis=-1)], axis=1)
    q_rows = q_rows.reshape(DEC_BATCH, N_HEADS * SAMPLE_Q, 2 * HEAD_DIM)
    w_rows = jnp.pad(wit_s.reshape(N_IDX_HEADS, DEC_BATCH, DEC_SEQ).transpose(1, 0, 2),
                     ((0, 0), (0, 0), (0, SAMPLE_Q - DEC_SEQ)))
    w_rows = jnp.broadcast_to(w_rows.reshape(DEC_BATCH, N_IDX_HEADS * SAMPLE_Q, 1),
                              (DEC_BATCH, N_IDX_HEADS * SAMPLE_Q, LANES))
    new_cols = lambda a: jnp.pad(a.reshape(DEC_BATCH, DEC_SEQ, a.shape[-1]).transpose(0, 2, 1),
                                 ((0, 0), (0, 0), (0, SAMPLE_Q - DEC_SEQ)))
    n_pool = cache_k.shape[1]
    ck_t = cache_k[0].transpose(0, 2, 3, 1).reshape(n_pool, N_KV * HEAD_DIM, PAGE_SIZE)
    cv_t = cache_v[0].transpose(0, 2, 3, 1).reshape(n_pool, N_KV * HEAD_DIM, PAGE_SIZE)
    cki_t = cache_k_idx[0].transpose(0, 2, 1)
    attn_s8 = _attn_sample(page_table, qi_rows, q_rows, w_rows, new_cols(ki_s), new_cols(k_s),
                           new_cols(v_s), ck_t, cv_t, cki_t)
    attn_s = attn_s8[:, :DEC_SEQ, :].reshape(rows_s, W_A)
    ri = jnp.arange(rows_s, dtype=jnp.int32)
    same_seq = (ri[:, None] // DEC_SEQ) == (ri[None, :] // DEC_SEQ)
    wmix_s = jnp.zeros((N_GROUPS_B, rows_s, rows_s), F32)
    for t in range(DEC_SEQ):
        for s in range(t + 1):
            hit = same_seq & (ri[:, None] % DEC_SEQ == t) & (ri[None, :] % DEC_SEQ == s)
            wmix_s = wmix_s + jnp.where(hit[None], w_s[0, :, t, s][:, None, None], 0.0)
    wmix_s = wmix_s.astype(BF16)
    bsb_s = jnp.broadcast_to(jnp.tile(b_s[0][:, :DEC_SEQ], (1, DEC_BATCH))[:, :, None],
                             (N_GROUPS_B, rows_s, GROUP_W_B))
    y_s, vn_s = _mixer(xs, attn_s, wb, wmix_s, bsb_s, lnvg, lnvb, wout, lng, lnb,
                       chunk=rows_s, vn_last_only=False)

    return (y_p.reshape(BATCH, SEQ, D_MODEL),
            y_s.reshape(DEC_BATCH, DEC_SEQ, D_MODEL),
            k_p.reshape(1, BATCH, SEQ, N_KV, HEAD_DIM),
            v_p.reshape(1, BATCH, SEQ, N_KV, HEAD_DIM),
            ki_p.reshape(1, BATCH, SEQ, IDX_DIM),
            vn_p.reshape(1, BATCH, CHUNK, W_B),
            k_s.reshape(1, DEC_BATCH, DEC_SEQ, N_KV, HEAD_DIM),
            v_s.reshape(1, DEC_BATCH, DEC_SEQ, N_KV, HEAD_DIM),
            ki_s.reshape(1, DEC_BATCH, DEC_SEQ, IDX_DIM),
            vn_s.reshape(1, DEC_BATCH, DEC_SEQ, W_B))
```

```python
import functools

import jax
import jax.numpy as jnp
from jax import lax
from jax.experimental import pallas as pl
from jax.experimental.pallas import tpu as pltpu

F32 = jnp.float32
BF16 = jnp.bfloat16

D_MODEL = 1024
SEQ = 2048
BATCH = 8
DEC_BATCH = 128
DEC_SEQ = 4
PAST_LEN = 2048
PAGE_SIZE = 128
N_PAGES = PAST_LEN // PAGE_SIZE
W_A = 512
W_B = 512
HEAD_DIM = 64
N_HEADS = 8
N_KV = 2
N_IDX_HEADS = 8
IDX_DIM = 64
TOPK = 256
CHUNK = 128
N_GROUPS_B = 4
GROUP_W_B = 128
ROPE_THETA = 10000.0
LN_EPS = 1e-5
ALPHA = 2.0 ** 0.25
NEG = -1e30
IDX_SCALE = float((N_IDX_HEADS * IDX_DIM) ** -0.5)

LANES = 128
KEY_TILE = 128
KEY_STEP = 256
ROW_TILE = 512
WT_ROWS = 1168
BISECT_ITERS = 24
FOLD_CHAINS = 8
ONES_ROWS = 16
Q_SCALE = float(HEAD_DIM ** -0.5 * 1.4426950408889634)
VMEM_LIMIT = 48 * 1024 * 1024

SAMPLE_GROUP = 4
SAMPLE_KEYS = PAST_LEN + KEY_TILE
SAMPLE_Q = 8


def _proj_kernel(x_ref, wstd_ref, wt_ref, cosl_ref, sinl_ref, cost_ref, sint_ref,
                 k_ref, v_ref, ki_ref, kbf_ref, kiw_ref, qt_ref, qit_ref, vt_ref, wit_ref):
    tm = x_ref.shape[0]
    xb = x_ref[...].astype(BF16)
    zs = jnp.dot(xb, wstd_ref[...], preferred_element_type=F32)
    cosl = cosl_ref[...]
    sinl = sinl_ref[...]
    lane = lax.broadcasted_iota(jnp.int32, (tm, LANES), 1)
    first_half = (lane % HEAD_DIM) < (HEAD_DIM // 2)

    def rope_lanes(z):
        partner = jnp.where(first_half, pltpu.roll(z, 96, 1), pltpu.roll(z, 32, 1))
        return z * cosl + partner * sinl

    k = rope_lanes(zs[:, 0:128])
    v = zs[:, 128:256]
    kiw = rope_lanes(zs[:, 256:384])
    k_ref[...] = k
    v_ref[...] = v
    ki_ref[...] = kiw[:, 0:IDX_DIM]
    kbf_ref[...] = k.astype(BF16)
    kiw_ref[...] = kiw.astype(BF16)

    zt = lax.dot_general(wt_ref[...], xb, (((1,), (1,)), ((), ())),
                         preferred_element_type=F32)
    cost = cost_ref[...]
    sint = sint_ref[...]
    half = HEAD_DIM // 2

    def rope_rows(base):
        pieces = []
        for h in range(N_HEADS):
            x1 = zt[base + h * HEAD_DIM: base + h * HEAD_DIM + half]
            x2 = zt[base + h * HEAD_DIM + half: base + (h + 1) * HEAD_DIM]
            pieces.append(x1 * cost - x2 * sint)
            pieces.append(x2 * cost + x1 * sint)
        return jnp.concatenate(pieces, axis=0)

    qt_ref[...] = rope_rows(0).astype(BF16)
    qit_ref[...] = rope_rows(512).astype(BF16)
    vtb = zt[1024:1152].astype(BF16)
    for c in range(tm // KEY_STEP):
        vt_ref[c] = vtb[:, c * KEY_STEP:(c + 1) * KEY_STEP]
    wit_ref[...] = zt[1152:1160] * IDX_SCALE


def _project(x2d, wstd, wt, cosl, sinl, cost, sint, n_pos_tiles):
    rows = x2d.shape[0]
    tm = ROW_TILE
    grid = (rows // tm,)
    row_map = lambda r: (r, 0)
    col_map = lambda r: (0, r)
    const = lambda r: (0, 0)
    out_shape = (
        jax.ShapeDtypeStruct((rows, 128), F32),
        jax.ShapeDtypeStruct((rows, 128), F32),
        jax.ShapeDtypeStruct((rows, IDX_DIM), F32),
        jax.ShapeDtypeStruct((rows, 128), BF16),
        jax.ShapeDtypeStruct((rows, 128), BF16),
        jax.ShapeDtypeStruct((512, rows), BF16),
        jax.ShapeDtypeStruct((512, rows), BF16),
        jax.ShapeDtypeStruct((rows // KEY_STEP, 128, KEY_STEP), BF16),
        jax.ShapeDtypeStruct((8, rows), F32),
    )
    out_specs = (
        pl.BlockSpec((tm, 128), row_map),
        pl.BlockSpec((tm, 128), row_map),
        pl.BlockSpec((tm, IDX_DIM), row_map),
        pl.BlockSpec((tm, 128), row_map),
        pl.BlockSpec((tm, 128), row_map),
        pl.BlockSpec((512, tm), col_map),
        pl.BlockSpec((512, tm), col_map),
        pl.BlockSpec((tm // KEY_STEP, 128, KEY_STEP), lambda r: (r, 0, 0)),
        pl.BlockSpec((8, tm), col_map),
    )
    in_specs = [
        pl.BlockSpec((tm, D_MODEL), row_map),
        pl.BlockSpec(wstd.shape, const),
        pl.BlockSpec(wt.shape, const),
        pl.BlockSpec((tm, LANES), lambda r: (r % n_pos_tiles, 0)),
        pl.BlockSpec((tm, LANES), lambda r: (r % n_pos_tiles, 0)),
        pl.BlockSpec((HEAD_DIM // 2, tm), lambda r: (0, r % n_pos_tiles)),
        pl.BlockSpec((HEAD_DIM // 2, tm), lambda r: (0, r % n_pos_tiles)),
    ]
    return pl.pallas_call(
        _proj_kernel, grid=grid, in_specs=in_specs, out_specs=out_specs, out_shape=out_shape,
        compiler_params=pltpu.CompilerParams(dimension_semantics=("arbitrary",),
                                             vmem_limit_bytes=VMEM_LIMIT),
        name="proj_attn_side",
    )(x2d, wstd, wt, cosl, sinl, cost, sint)


def _rope_tables(pos):
    half = HEAD_DIM // 2
    inv = ROPE_THETA ** (-jnp.arange(half, dtype=F32) / half)
    ang = pos.astype(F32)[:, None] * inv[None, :]
    cos = jnp.cos(ang)
    sin = jnp.sin(ang)
    cosl = jnp.tile(cos, (1, LANES // half))
    sinl = jnp.tile(jnp.concatenate([-sin, sin], axis=1), (1, LANES // HEAD_DIM))
    return cosl, sinl, cos.T, sin.T


def _select_threshold(count_ge, count_gt, count_eq_le, min_ge, max_lt, lo0, hi0, n_allowed, n_keys,
                      row_ok):
    k = float(TOPK)

    def step(carry):
        lo, hi, clo = carry
        mid = 0.5 * lo + 0.5 * hi
        c = count_ge(mid)
        take = c >= k
        return jnp.where(take, mid, lo), jnp.where(take, hi, mid), jnp.where(take, c, clo)

    lo, hi, clo = lax.fori_loop(0, BISECT_ITERS, lambda _, c: step(c), (lo0, hi0, n_allowed))
    big = jnp.full_like(lo, float(4 * n_keys))

    def unresolved(clo_):
        return jnp.logical_and(clo_ > k, row_ok)

    def any_true(mask):
        return jnp.max(jnp.where(mask, 1.0, 0.0)) > 0.5

    def exact(args):
        lo, hi, clo = args

        def cond(c):
            _, _, clo, vlo, vhi, it = c
            open_ = jnp.logical_and(unresolved(clo), vlo < vhi)
            return jnp.logical_and(any_true(open_), it < 2 * n_keys)

        def body(c):
            lo, hi, clo, vlo, vhi, it = c
            mid = 0.5 * vlo + 0.5 * vhi
            mid = jnp.where(mid > vlo, mid, vhi)
            cnt = count_ge(mid)
            take = cnt >= k
            lo, hi, clo = jnp.where(take, mid, lo), jnp.where(take, hi, mid), jnp.where(take, cnt, clo)
            return lo, hi, clo, min_ge(lo), max_lt(hi), it + 1

        lo, hi, clo, vlo, _, _ = lax.while_loop(
            cond, body, (lo, hi, clo, min_ge(lo), max_lt(hi), jnp.int32(0)))
        lo = vlo
        need = k - count_gt(lo)

        def jstep(_, c):
            jlo, jhi = c
            mid = jnp.floor(0.5 * (jlo + jhi))
            ok = count_eq_le(lo, mid) >= need
            return jnp.where(ok, jlo, mid), jnp.where(ok, mid, jhi)

        jlo0 = jnp.full_like(lo, -1.0)
        jhi0 = jnp.full_like(lo, float(n_keys - 1))
        _, jhi = lax.fori_loop(0, 13, jstep, (jlo0, jhi0))
        return lo, jnp.where(clo > k, jhi, big)

    def fast(args):
        return args[0], big

    return lax.cond(any_true(unresolved(clo)), exact, fast, (lo, hi, clo))


def _fold_rows(x, op):
    groups = [x[r:r + 8] for r in range(0, x.shape[0], 8)]
    parts = groups[:FOLD_CHAINS]
    for n, grp in enumerate(groups[FOLD_CHAINS:]):
        parts[n % FOLD_CHAINS] = op(parts[n % FOLD_CHAINS], grp)
    while len(parts) > 1:
        nxt = [op(parts[a], parts[a + 1]) for a in range(0, len(parts) - 1, 2)]
        if len(parts) % 2:
            nxt.append(parts[-1])
        parts = nxt
    return parts[0]


def _attn_prompt_kernel(qt_ref, qit_ref, wit_ref, kbf_ref, kiw_ref, vt_ref, o_ref,
                        s_ref, sc_ref, ri_ref, rq_ref):
    i = pl.program_id(1)
    w = wit_ref[...]
    zeros_half = jnp.zeros((HEAD_DIM, LANES), BF16)

    for hp in range(4):
        a = qit_ref[(2 * hp) * 64:(2 * hp + 1) * 64, :]
        b = qit_ref[(2 * hp + 1) * 64:(2 * hp + 2) * 64, :]
        ri_ref[hp] = jnp.concatenate(
            [jnp.concatenate([a, zeros_half], axis=0), jnp.concatenate([b, zeros_half], axis=0)], axis=1)
        g = hp // 2
        a = qt_ref[(2 * hp) * 64:(2 * hp + 1) * 64, :]
        b = qt_ref[(2 * hp + 1) * 64:(2 * hp + 2) * 64, :]
        if g == 0:
            ab = jnp.concatenate([jnp.concatenate([a, zeros_half], axis=0),
                                  jnp.concatenate([b, zeros_half], axis=0)], axis=1)
        else:
            ab = jnp.concatenate([jnp.concatenate([zeros_half, a], axis=0),
                                  jnp.concatenate([zeros_half, b], axis=0)], axis=1)
        rq_ref[hp] = ab

    inf = jnp.float32(jnp.inf)
    n_steps = (i * LANES + LANES + KEY_STEP - 1) // KEY_STEP

    def process(ns):
        n_keys = ns * KEY_STEP
        keys = slice(0, n_keys)
        row = lax.broadcasted_iota(jnp.int32, (n_keys, LANES), 0)
        col = lax.broadcasted_iota(jnp.int32, (n_keys, LANES), 1)
        qpos = col + i * LANES
        allowed = row <= qpos

        kt = kiw_ref[keys, :]
        acc = jnp.zeros((n_keys, LANES), F32)
        for hp in range(4):
            s2 = jnp.dot(kt, ri_ref[hp], preferred_element_type=F32)
            acc = acc + jnp.maximum(s2[:, :LANES], 0.0) * w[2 * hp:2 * hp + 1, :]
            acc = acc + jnp.maximum(s2[:, LANES:], 0.0) * w[2 * hp + 1:2 * hp + 2, :]
        s_ref[keys, :] = jnp.where(allowed, acc, -inf)
        lo0 = jnp.min(_fold_rows(jnp.where(allowed, acc, inf), jnp.minimum), axis=0, keepdims=True)
        mx1 = jnp.max(_fold_rows(jnp.where(allowed, acc, -inf), jnp.maximum), axis=0, keepdims=True)
        hi0 = mx1 + (jnp.abs(mx1) * (2.0 ** -20) + 1e-30)
        n_allowed = (qpos[0:1, :] + 1).astype(F32)

        def key_reduce(fn, op, finish):
            return finish(_fold_rows(fn(s_ref[keys, :]), op), axis=0, keepdims=True)

        kidx = row.astype(F32)
        count_ge = lambda thr: key_reduce(lambda t: jnp.where(t >= thr, 1.0, 0.0), jnp.add, jnp.sum)
        count_gt = lambda thr: key_reduce(lambda t: jnp.where(t > thr, 1.0, 0.0), jnp.add, jnp.sum)
        count_eq_le = lambda thr, jm: key_reduce(
            lambda t: jnp.where(t == thr, jnp.where(kidx <= jm, 1.0, 0.0), 0.0), jnp.add, jnp.sum)
        min_ge = lambda thr: key_reduce(lambda t: jnp.where(t >= thr, t, inf), jnp.minimum, jnp.min)
        max_lt = lambda thr: key_reduce(lambda t: jnp.where(t < thr, t, -inf), jnp.maximum, jnp.max)
        lo, jmax = _select_threshold(count_ge, count_gt, count_eq_le, min_ge, max_lt, lo0, hi0,
                                     n_allowed, SEQ, jnp.full((1, LANES), True))

        t = s_ref[keys, :]
        tie = jnp.where(t == lo, jnp.where(kidx <= jmax, 0.0, NEG), NEG)
        s_ref[keys, :] = jnp.where(t > lo, 0.0, tie)

        kt = kbf_ref[keys, :]
        b = s_ref[keys, :]
        b2 = jnp.concatenate([b, b], axis=1)
        ms = []
        for hp in range(4):
            s2 = jnp.dot(kt, rq_ref[hp], preferred_element_type=F32) + b2
            sc_ref[hp, keys, :] = s2
            ms.append(jnp.max(_fold_rows(s2, jnp.maximum), axis=0, keepdims=True))
        vt = jnp.concatenate([vt_ref[j] for j in range(ns)], axis=1)
        ones_rows = jnp.ones((ONES_ROWS, n_keys), BF16)
        lhs = [jnp.concatenate([vt[g * 64:(g + 1) * 64, :], ones_rows], axis=0) for g in range(N_KV)]
        pieces = []
        for hp in range(4):
            p = jnp.exp2((sc_ref[hp, keys, :] - ms[hp]).astype(BF16))
            o = jnp.dot(lhs[hp // 2], p, preferred_element_type=F32)
            o2 = o[0:HEAD_DIM, :] / o[HEAD_DIM:HEAD_DIM + 1, :]
            pieces.append(o2[:, :LANES])
            pieces.append(o2[:, LANES:])
        o_ref[...] = jnp.concatenate(pieces, axis=0).T

    for ns in range(1, SEQ // KEY_STEP + 1):
        pl.when(n_steps == ns)(functools.partial(process, ns))


def _attn_prompt(qt, qit, wit, kbf, kiw, vt3):
    n_blocks = SEQ // LANES
    grid = (BATCH, n_blocks)
    qmap = lambda n, i: (0, n * n_blocks + i)
    in_specs = [
        pl.BlockSpec((512, LANES), qmap),
        pl.BlockSpec((512, LANES), qmap),
        pl.BlockSpec((8, LANES), qmap),
        pl.BlockSpec((SEQ, 128), lambda n, i: (n, 0)),
        pl.BlockSpec((SEQ, 128), lambda n, i: (n, 0)),
        pl.BlockSpec((SEQ // KEY_STEP, 128, KEY_STEP), lambda n, i: (n, 0, 0)),
    ]
    return pl.pallas_call(
        _attn_prompt_kernel, grid=grid, in_specs=in_specs,
        out_specs=pl.BlockSpec((LANES, W_A), lambda n, i: (n * n_blocks + i, 0)),
        out_shape=jax.ShapeDtypeStruct((BATCH * SEQ, W_A), F32),
        scratch_shapes=[pltpu.VMEM((SEQ, LANES), F32), pltpu.VMEM((4, SEQ, 2 * LANES), F32),
                        pltpu.VMEM((4, 128, 2 * LANES), BF16), pltpu.VMEM((4, 128, 2 * LANES), BF16)],
        compiler_params=pltpu.CompilerParams(dimension_semantics=("arbitrary", "arbitrary"),
                                             vmem_limit_bytes=VMEM_LIMIT),
        name="attn_prompt",
    )(qt, qit, wit, kbf, kiw, vt3)


def _attn_sample_kernel(pt_ref, qi_ref, q_ref, w_ref, kin_ref, kn_ref, vn_ref,
                        ck_hbm, cv_hbm, cki_hbm, o_ref,
                        kbuf, vbuf, kibuf, sems, s_ref, bias_ref):
    step = pl.program_id(0)
    n_steps = pl.num_programs(0)
    gsz = SAMPLE_GROUP
    slot = step % 2

    def copies(group, slot_):
        out = []
        for g in range(gsz):
            seq = group * gsz + g
            for p in range(N_PAGES):
                page = pt_ref[seq, p]
                keys = pl.ds(p * PAGE_SIZE, PAGE_SIZE)
                out.append(pltpu.make_async_copy(ck_hbm.at[page], kbuf.at[slot_, g, :, keys], sems.at[slot_, 0]))
                out.append(pltpu.make_async_copy(cv_hbm.at[page], vbuf.at[slot_, g, :, keys], sems.at[slot_, 1]))
                out.append(pltpu.make_async_copy(cki_hbm.at[page], kibuf.at[slot_, g, :, keys], sems.at[slot_, 2]))
        return out

    new_keys = pl.ds(PAST_LEN, SAMPLE_Q)

    @pl.when(step == 0)
    def _():
        tail = pl.ds(PAST_LEN, KEY_TILE)
        for s_ in range(2):
            for g in range(gsz):
                kbuf[s_, g, :, tail] = jnp.zeros((128, KEY_TILE), F32)
                vbuf[s_, g, :, tail] = jnp.zeros((128, KEY_TILE), F32)
                kibuf[s_, g, :, tail] = jnp.zeros((IDX_DIM, KEY_TILE), F32)
        for c in copies(0, 0):
            c.start()

    @pl.when(step + 1 < n_steps)
    def _():
        for c in copies(step + 1, 1 - slot):
            c.start()

    for c in copies(step, slot):
        c.wait()

    rows8 = lax.broadcasted_iota(jnp.int32, (SAMPLE_Q, SAMPLE_KEYS), 0)
    keyi = lax.broadcasted_iota(jnp.int32, (SAMPLE_Q, SAMPLE_KEYS), 1)
    allowed = keyi <= (PAST_LEN + rows8)
    inf = jnp.float32(jnp.inf)
    row_ok8 = lax.broadcasted_iota(jnp.int32, (SAMPLE_Q, 1), 0) < DEC_SEQ

    for g in range(gsz):
        kbuf[slot, g, :, new_keys] = kn_ref[g]
        vbuf[slot, g, :, new_keys] = vn_ref[g]
        kibuf[slot, g, :, new_keys] = kin_ref[g]
        kib = kibuf[slot, g].astype(BF16)
        s = jnp.dot(qi_ref[g], kib, preferred_element_type=F32)
        sw = jnp.maximum(s, 0.0) * w_ref[g][:, 0:1]
        acc = jnp.sum(sw.reshape(N_IDX_HEADS, SAMPLE_Q, SAMPLE_KEYS), axis=0)
        s_ref[g * SAMPLE_Q:(g + 1) * SAMPLE_Q, :] = jnp.where(allowed, acc, -inf)

    rows_all = gsz * SAMPLE_Q
    allowed_all = jnp.concatenate([allowed] * gsz, axis=0)
    row_ok = jnp.concatenate([row_ok8] * gsz, axis=0)
    keyf = lax.broadcasted_iota(jnp.int32, (rows_all, SAMPLE_KEYS), 1).astype(F32)

    sc = s_ref[...]
    lo0 = jnp.min(jnp.where(allowed_all, sc, inf), axis=1, keepdims=True)
    mx1 = jnp.max(sc, axis=1, keepdims=True)
    hi0 = mx1 + (jnp.abs(mx1) * (2.0 ** -20) + 1e-30)
    t8 = lax.broadcasted_iota(jnp.int32, (SAMPLE_Q, 1), 0)
    n_allowed = jnp.concatenate([(PAST_LEN + 1 + t8).astype(F32)] * gsz, axis=0)

    def lane_count(hit):
        return jnp.sum(hit, axis=1, keepdims=True)

    count_ge = lambda thr: lane_count(jnp.where(s_ref[...] >= thr, 1.0, 0.0))
    count_gt = lambda thr: lane_count(jnp.where(s_ref[...] > thr, 1.0, 0.0))
    count_eq_le = lambda thr, jm: lane_count(
        jnp.where(s_ref[...] == thr, jnp.where(keyf <= jm, 1.0, 0.0), 0.0))
    min_ge = lambda thr: jnp.min(jnp.where(s_ref[...] >= thr, s_ref[...], inf), axis=1, keepdims=True)
    max_lt = lambda thr: jnp.max(jnp.where(s_ref[...] < thr, s_ref[...], -inf), axis=1, keepdims=True)
    lo, jmax = _select_threshold(count_ge, count_gt, count_eq_le, min_ge, max_lt, lo0, hi0,
                                 n_allowed, SAMPLE_KEYS, row_ok)
    sc = s_ref[...]
    tie = jnp.where(sc == lo, jnp.where(keyf <= jmax, 0.0, NEG), NEG)
    bias_ref[...] = jnp.where(sc > lo, 0.0, tie)

    lane = lax.broadcasted_iota(jnp.int32, (SAMPLE_Q, LANES), 1)
    for g in range(gsz):
        kb = kbuf[slot, g].astype(BF16)
        vb = vbuf[slot, g].astype(BF16)
        b = bias_ref[g * SAMPLE_Q:(g + 1) * SAMPLE_Q, :]
        s = jnp.dot(q_ref[g], kb, preferred_element_type=F32)
        s = s + jnp.concatenate([b] * N_HEADS, axis=0)
        m = jnp.max(s, axis=1, keepdims=True)
        p = jnp.exp2(s - m)
        l = jnp.sum(p, axis=1, keepdims=True)
        o = lax.dot_general(p.astype(BF16), vb, (((1,), (1,)), ((), ())),
                            preferred_element_type=F32) / l
        o_r = pltpu.roll(o, HEAD_DIM, 1)
        tiles = []
        for c in range(4):
            ha, hb = 2 * c, 2 * c + 1
            src_a = o if ha // 4 == 0 else o_r
            src_b = o if hb // 4 == 1 else o_r
            tiles.append(jnp.where(lane < HEAD_DIM, src_a[ha * SAMPLE_Q:(ha + 1) * SAMPLE_Q, :],
                                   src_b[hb * SAMPLE_Q:(hb + 1) * SAMPLE_Q, :]))
        o_ref[g] = jnp.concatenate(tiles, axis=1)


def _attn_sample(page_table, qi_s, q_s, w_s, ki_new, k_new, v_new, ck, cv, cki):
    gsz = SAMPLE_GROUP
    grid = (DEC_BATCH // gsz,)
    blk = lambda shape: pl.BlockSpec((gsz,) + shape, lambda s, pt: (s, 0, 0))
    grid_spec = pltpu.PrefetchScalarGridSpec(
        num_scalar_prefetch=1, grid=grid,
        in_specs=[blk((64, IDX_DIM)), blk((64, 128)), blk((64, LANES)),
                  blk((IDX_DIM, SAMPLE_Q)), blk((128, SAMPLE_Q)), blk((128, SAMPLE_Q)),
                  pl.BlockSpec(memory_space=pl.ANY), pl.BlockSpec(memory_space=pl.ANY),
                  pl.BlockSpec(memory_space=pl.ANY)],
        out_specs=blk((SAMPLE_Q, W_A)),
        scratch_shapes=[pltpu.VMEM((2, gsz, 128, SAMPLE_KEYS), F32),
                        pltpu.VMEM((2, gsz, 128, SAMPLE_KEYS), F32),
                        pltpu.VMEM((2, gsz, IDX_DIM, SAMPLE_KEYS), F32),
                        pltpu.SemaphoreType.DMA((2, 3)),
                        pltpu.VMEM((gsz * SAMPLE_Q, SAMPLE_KEYS), F32),
                        pltpu.VMEM((gsz * SAMPLE_Q, SAMPLE_KEYS), F32)])
    return pl.pallas_call(
        _attn_sample_kernel, grid_spec=grid_spec,
        out_shape=jax.ShapeDtypeStruct((DEC_BATCH, SAMPLE_Q, W_A), F32),
        compiler_params=pltpu.CompilerParams(dimension_semantics=("arbitrary",),
                                             vmem_limit_bytes=VMEM_LIMIT),
        name="attn_sample",
    )(page_table, qi_s, q_s, w_s, ki_new, k_new, v_new, ck, cv, cki)


def _layer_norm(x, g, b):
    mu = jnp.mean(x, axis=-1, keepdims=True)
    xc = x - mu
    var = jnp.mean(xc * xc, axis=-1, keepdims=True)
    return xc * lax.rsqrt(var + LN_EPS) * g + b


def _silu(x):
    return x / (1.0 + jnp.exp(-x))


def _mixer_kernel(x_ref, a_ref, wb_ref, wmix_ref, bsb_ref, lnvg_ref, lnvb_ref, wout_ref,
                  lng_ref, lnb_ref, y_ref, vn_ref, *, chunk, vn_last_only, tiles_per_seq):
    tm = x_ref.shape[0]
    x = x_ref[...]
    zb = jnp.dot(x.astype(BF16), wb_ref[...], preferred_element_type=F32)
    ga = zb[:, 0:512]
    u = zb[:, 512:1024]
    vb = zb[:, 1024:1536]
    gb = zb[:, 1536:2048]
    vn = _layer_norm(vb, lnvg_ref[...], lnvb_ref[...])
    if vn_last_only:
        @pl.when(pl.program_id(0) % tiles_per_seq == tiles_per_seq - 1)
        def _():
            vn_ref[0] = vn[tm - CHUNK:, :]
    else:
        vn_ref[...] = vn
    vnb = vn.astype(BF16)
    rows = []
    for c in range(tm // chunk):
        cols = []
        for g in range(N_GROUPS_B):
            blk = vnb[c * chunk:(c + 1) * chunk, g * GROUP_W_B:(g + 1) * GROUP_W_B]
            cols.append(jnp.dot(wmix_ref[g], blk, preferred_element_type=F32) + bsb_ref[g])
        rows.append(jnp.concatenate(cols, axis=1))
    mixed = jnp.concatenate(rows, axis=0) if len(rows) > 1 else rows[0]
    a_out = a_ref[...] * _silu(ga)
    b_out = u * mixed * _silu(gb)
    cat = jnp.concatenate([a_out, b_out], axis=1).astype(BF16)
    out = jnp.dot(cat, wout_ref[...], preferred_element_type=F32)
    y_ref[...] = _layer_norm(ALPHA * x + out, lng_ref[...], lnb_ref[...])


def _mixer(x2d, attn, wb, wmix, bsb, lnvg, lnvb, wout, lng, lnb, *, chunk, vn_last_only):
    rows = x2d.shape[0]
    tm = ROW_TILE
    tiles_per_seq = SEQ // tm
    grid = (rows // tm,)
    row_map = lambda r: (r, 0)
    const2 = lambda r: (0, 0)
    const3 = lambda r: (0, 0, 0)
    if vn_last_only:
        vn_shape = jax.ShapeDtypeStruct((rows // SEQ, CHUNK, W_B), F32)
        vn_spec = pl.BlockSpec((1, CHUNK, W_B), lambda r: (r // tiles_per_seq, 0, 0))
    else:
        vn_shape = jax.ShapeDtypeStruct((rows, W_B), F32)
        vn_spec = pl.BlockSpec((tm, W_B), row_map)
    kern = functools.partial(_mixer_kernel, chunk=chunk, vn_last_only=vn_last_only,
                             tiles_per_seq=tiles_per_seq)
    return pl.pallas_call(
        kern, grid=grid,
        in_specs=[pl.BlockSpec((tm, D_MODEL), row_map), pl.BlockSpec((tm, W_A), row_map),
                  pl.BlockSpec(wb.shape, const2), pl.BlockSpec(wmix.shape, const3),
                  pl.BlockSpec(bsb.shape, const3), pl.BlockSpec((1, W_B), const2),
                  pl.BlockSpec((1, W_B), const2), pl.BlockSpec(wout.shape, const2),
                  pl.BlockSpec((1, D_MODEL), const2), pl.BlockSpec((1, D_MODEL), const2)],
        out_specs=(pl.BlockSpec((tm, D_MODEL), row_map), vn_spec),
        out_shape=(jax.ShapeDtypeStruct((rows, D_MODEL), F32), vn_shape),
        compiler_params=pltpu.CompilerParams(dimension_semantics=("arbitrary",),
                                             vmem_limit_bytes=VMEM_LIMIT),
        name="mixer_chunk%d" % chunk,
    )(x2d, attn, wb, wmix, bsb, lnvg, lnvb, wout, lng, lnb)


def kernel(x_prompt, x_sample, cache_k, cache_v, cache_k_idx, page_table, w_in, w_s, b_s,
           ln_v_g, ln_v_b, w_out, ln_g, ln_b):
    w = w_in[0]
    wq, wk, wv = w[:, 0:512], w[:, 512:640], w[:, 640:768]
    wga, wqi, wki, wwi = w[:, 768:1280], w[:, 1280:1792], w[:, 1792:1856], w[:, 1856:1864]
    wu, wvb, wgb = w[:, 1864:2376], w[:, 2376:2888], w[:, 2888:3400]
    wstd = jnp.concatenate([wk, wv, wki, jnp.zeros((D_MODEL, 64), F32)], axis=1).astype(BF16)
    wt = jnp.concatenate([wq * Q_SCALE, wqi, wv, wwi,
                          jnp.zeros((D_MODEL, WT_ROWS - 1160), F32)], axis=1).T.astype(BF16)
    wb = jnp.concatenate([wga, wu, wvb, wgb], axis=1).astype(BF16)
    wout = w_out[0].astype(BF16)
    lnvg, lnvb = ln_v_g[0][None, :], ln_v_b[0][None, :]
    lng, lnb = ln_g[0][None, :], ln_b[0][None, :]

    xp = x_prompt.reshape(BATCH * SEQ, D_MODEL)
    tabs_p = _rope_tables(jnp.arange(SEQ, dtype=jnp.int32))
    k_p, v_p, ki_p, kbf, kiw, qt, qit, vt3, wit = _project(xp, wstd, wt, *tabs_p, SEQ // ROW_TILE)
    attn_p = _attn_prompt(qt, qit, wit, kbf, kiw, vt3)
    tril = jnp.tril(jnp.ones((CHUNK, CHUNK), F32))
    wmix_p = (w_s[0] * tril[None]).astype(BF16)
    bsb_p = jnp.broadcast_to(b_s[0][:, :, None], (N_GROUPS_B, CHUNK, GROUP_W_B))
    y_p, vn_p = _mixer(xp, attn_p, wb, wmix_p, bsb_p, lnvg, lnvb, wout, lng, lnb,
                       chunk=CHUNK, vn_last_only=True)

    rows_s = DEC_BATCH * DEC_SEQ
    xs = x_sample.reshape(rows_s, D_MODEL)
    pos_s = PAST_LEN + (jnp.arange(rows_s, dtype=jnp.int32) % DEC_SEQ)
    tabs_s = _rope_tables(pos_s)
    k_s, v_s, ki_s, _, _, qt_s, qit_s, _, wit_s = _project(xs, wstd, wt, *tabs_s, 1)

    def to_rows(a_t):
        a = a_t.reshape(N_HEADS, HEAD_DIM, DEC_BATCH, DEC_SEQ).transpose(2, 0, 3, 1)
        return jnp.pad(a, ((0, 0), (0, 0), (0, SAMPLE_Q - DEC_SEQ), (0, 0)))

    qi_rows = to_rows(qit_s).reshape(DEC_BATCH, N_IDX_HEADS * SAMPLE_Q, IDX_DIM)
    q5 = to_rows(qt_s).reshape(DEC_BATCH, N_KV, N_HEADS // N_KV, SAMPLE_Q, HEAD_DIM)
    zq = jnp.zeros_like(q5[:, 0])
    q_rows = jnp.stack([jnp.concatenate([q5[:, 0], zq], axis=-1),
                        jnp.concatenate([zq, q5[:, 1]], axis=-1)], axis=1)
    q_rows = q_rows.reshape(DEC_BATCH, N_HEADS * SAMPLE_Q, 2 * HEAD_DIM)
    w_rows = jnp.pad(wit_s.reshape(N_IDX_HEADS, DEC_BATCH, DEC_SEQ).transpose(1, 0, 2),
                     ((0, 0), (0, 0), (0, SAMPLE_Q - DEC_SEQ)))
    w_rows = jnp.broadcast_to(w_rows.reshape(DEC_BATCH, N_IDX_HEADS * SAMPLE_Q, 1),
                              (DEC_BATCH, N_IDX_HEADS * SAMPLE_Q, LANES))
    new_cols = lambda a: jnp.pad(a.reshape(DEC_BATCH, DEC_SEQ, a.shape[-1]).transpose(0, 2, 1),
                                 ((0, 0), (0, 0), (0, SAMPLE_Q - DEC_SEQ)))
    n_pool = cache_k.shape[1]
    ck_t = cache_k[0].transpose(0, 2, 3, 1).reshape(n_pool, N_KV * HEAD_DIM, PAGE_SIZE)
    cv_t = cache_v[0].transpose(0, 2, 3, 1).reshape(n_pool, N_KV * HEAD_DIM, PAGE_SIZE)
    cki_t = cache_k_idx[0].transpose(0, 2, 1)
    attn_s8 = _attn_sample(page_table, qi_rows, q_rows, w_rows, new_cols(ki_s), new_cols(k_s),
                           new_cols(v_s), ck_t, cv_t, cki_t)
    attn_s = attn_s8[:, :DEC_SEQ, :].reshape(rows_s, W_A)
    ri = jnp.arange(rows_s, dtype=jnp.int32)
    same_seq = (ri[:, None] // DEC_SEQ) == (ri[None, :] // DEC_SEQ)
    wmix_s = jnp.zeros((N_GROUPS_B, rows_s, rows_s), F32)
    for t in range(DEC_SEQ):
        for s in range(t + 1):
            hit = same_seq & (ri[:, None] % DEC_SEQ == t) & (ri[None, :] % DEC_SEQ == s)
            wmix_s = wmix_s + jnp.where(hit[None], w_s[0, :, t, s][:, None, None], 0.0)
    wmix_s = wmix_s.astype(BF16)
    bsb_s = jnp.broadcast_to(jnp.tile(b_s[0][:, :DEC_SEQ], (1, DEC_BATCH))[:, :, None],
                             (N_GROUPS_B, rows_s, GROUP_W_B))
    y_s, vn_s = _mixer(xs, attn_s, wb, wmix_s, bsb_s, lnvg, lnvb, wout, lng, lnb,
                       chunk=rows_s, vn_last_only=False)

    return (y_p.reshape(BATCH, SEQ, D_MODEL),
            y_s.reshape(DEC_BATCH, DEC_SEQ, D_MODEL),
            k_p.reshape(1, BATCH, SEQ, N_KV, HEAD_DIM),
            v_p.reshape(1, BATCH, SEQ, N_KV, HEAD_DIM),
            ki_p.reshape(1, BATCH, SEQ, IDX_DIM),
            vn_p.reshape(1, BATCH, CHUNK, W_B),
            k_s.reshape(1, DEC_BATCH, DEC_SEQ, N_KV, HEAD_DIM),
            v_s.reshape(1, DEC_BATCH, DEC_SEQ, N_KV, HEAD_DIM),
            ki_s.reshape(1, DEC_BATCH, DEC_SEQ, IDX_DIM),
            vn_s.reshape(1, DEC_BATCH, DEC_SEQ, W_B))
```

```python
import functools

import jax
import jax.numpy as jnp
from jax import lax
from jax.experimental import pallas as pl
from jax.experimental.pallas import tpu as pltpu

F32 = jnp.float32
BF16 = jnp.bfloat16

D_MODEL = 1024
SEQ = 2048
BATCH = 8
DEC_BATCH = 128
DEC_SEQ = 4
PAST_LEN = 2048
PAGE_SIZE = 128
N_PAGES = PAST_LEN // PAGE_SIZE
W_A = 512
W_B = 512
HEAD_DIM = 64
N_HEADS = 8
N_KV = 2
N_IDX_HEADS = 8
IDX_DIM = 64
TOPK = 256
CHUNK = 128
N_GROUPS_B = 4
GROUP_W_B = 128
ROPE_THETA = 10000.0
LN_EPS = 1e-5
ALPHA = 2.0 ** 0.25
NEG = -1e30
IDX_SCALE = float((N_IDX_HEADS * IDX_DIM) ** -0.5)

LANES = 128
KEY_TILE = 128
KEY_STEP = 256
ROW_TILE = 512
WT_Q, WT_QI, WT_V, WT_K, WT_KI, WT_WI, WT_ROWS = 0, 512, 1024, 1152, 1280, 1344, 1360
BISECT_ITERS = 24
FOLD_CHAINS = 8
ONES_ROWS = 16
Q_SCALE = float(HEAD_DIM ** -0.5 * 1.4426950408889634)
VMEM_LIMIT = 48 * 1024 * 1024

SAMPLE_GROUP = 4
SAMPLE_KEYS = PAST_LEN + KEY_TILE
SAMPLE_Q = 8


def _proj_kernel(x_ref, wstd_ref, wt_ref, cosl_ref, sinl_ref, cost_ref, sint_ref,
                 kt_ref, vt32_ref, kit_ref, kbf_ref, kiw_ref, qt_ref, qit_ref, vt_ref, wit_ref):
    tm = x_ref.shape[0]
    xb = x_ref[...].astype(BF16)
    zs = jnp.dot(xb, wstd_ref[...], preferred_element_type=F32)
    cosl = cosl_ref[...]
    sinl = sinl_ref[...]
    lane = lax.broadcasted_iota(jnp.int32, (tm, LANES), 1)
    first_half = (lane % HEAD_DIM) < (HEAD_DIM // 2)

    def rope_lanes(z):
        partner = jnp.where(first_half, pltpu.roll(z, 96, 1), pltpu.roll(z, 32, 1))
        return z * cosl + partner * sinl

    kbf_ref[...] = rope_lanes(zs[:, 0:128]).astype(BF16)
    kiw_ref[...] = rope_lanes(zs[:, 128:256]).astype(BF16)

    zt = lax.dot_general(wt_ref[...], xb, (((1,), (1,)), ((), ())),
                         preferred_element_type=F32)
    cost = cost_ref[...]
    sint = sint_ref[...]
    half = HEAD_DIM // 2

    def rope_rows(base, n_heads):
        pieces = []
        for h in range(n_heads):
            x1 = zt[base + h * HEAD_DIM: base + h * HEAD_DIM + half]
            x2 = zt[base + h * HEAD_DIM + half: base + (h + 1) * HEAD_DIM]
            pieces.append(x1 * cost - x2 * sint)
            pieces.append(x2 * cost + x1 * sint)
        return jnp.concatenate(pieces, axis=0)

    qt_ref[...] = rope_rows(WT_Q, N_HEADS).astype(BF16)
    qit_ref[...] = rope_rows(WT_QI, N_IDX_HEADS).astype(BF16)
    v_t = zt[WT_V:WT_V + 128]
    vt32_ref[0] = v_t
    vtb = v_t.astype(BF16)
    for c in range(tm // KEY_STEP):
        vt_ref[c] = vtb[:, c * KEY_STEP:(c + 1) * KEY_STEP]
    kt_ref[0] = rope_rows(WT_K, N_KV)
    kit_ref[0] = rope_rows(WT_KI, 1)
    wit_ref[...] = zt[WT_WI:WT_WI + N_IDX_HEADS] * IDX_SCALE


def _project(x2d, wstd, wt, cosl, sinl, cost, sint, seq_len):
    rows = x2d.shape[0]
    tm = ROW_TILE
    grid = (rows // tm,)
    n_pos_tiles = seq_len // tm
    row_map = lambda r: (r, 0)
    col_map = lambda r: (0, r)
    const = lambda r: (0, 0)
    seq_map = lambda r: (r // n_pos_tiles, 0, r % n_pos_tiles)
    out_shape = (
        jax.ShapeDtypeStruct((rows // seq_len, 128, seq_len), F32),
        jax.ShapeDtypeStruct((rows // seq_len, 128, seq_len), F32),
        jax.ShapeDtypeStruct((rows // seq_len, IDX_DIM, seq_len), F32),
        jax.ShapeDtypeStruct((rows, 128), BF16),
        jax.ShapeDtypeStruct((rows, 128), BF16),
        jax.ShapeDtypeStruct((512, rows), BF16),
        jax.ShapeDtypeStruct((512, rows), BF16),
        jax.ShapeDtypeStruct((rows // KEY_STEP, 128, KEY_STEP), BF16),
        jax.ShapeDtypeStruct((8, rows), F32),
    )
    out_specs = (
        pl.BlockSpec((1, 128, tm), seq_map),
        pl.BlockSpec((1, 128, tm), seq_map),
        pl.BlockSpec((1, IDX_DIM, tm), seq_map),
        pl.BlockSpec((tm, 128), row_map),
        pl.BlockSpec((tm, 128), row_map),
        pl.BlockSpec((512, tm), col_map),
        pl.BlockSpec((512, tm), col_map),
        pl.BlockSpec((tm // KEY_STEP, 128, KEY_STEP), lambda r: (r, 0, 0)),
        pl.BlockSpec((8, tm), col_map),
    )
    in_specs = [
        pl.BlockSpec((tm, D_MODEL), row_map),
        pl.BlockSpec(wstd.shape, const),
        pl.BlockSpec(wt.shape, const),
        pl.BlockSpec((tm, LANES), lambda r: (r % n_pos_tiles, 0)),
        pl.BlockSpec((tm, LANES), lambda r: (r % n_pos_tiles, 0)),
        pl.BlockSpec((HEAD_DIM // 2, tm), lambda r: (0, r % n_pos_tiles)),
        pl.BlockSpec((HEAD_DIM // 2, tm), lambda r: (0, r % n_pos_tiles)),
    ]
    return pl.pallas_call(
        _proj_kernel, grid=grid, in_specs=in_specs, out_specs=out_specs, out_shape=out_shape,
        compiler_params=pltpu.CompilerParams(dimension_semantics=("arbitrary",),
                                             vmem_limit_bytes=VMEM_LIMIT),
        name="proj_attn_side",
    )(x2d, wstd, wt, cosl, sinl, cost, sint)


def _rope_tables(pos):
    half = HEAD_DIM // 2
    inv = ROPE_THETA ** (-jnp.arange(half, dtype=F32) / half)
    ang = pos.astype(F32)[:, None] * inv[None, :]
    cos = jnp.cos(ang)
    sin = jnp.sin(ang)
    cosl = jnp.tile(cos, (1, LANES // half))
    sinl = jnp.tile(jnp.concatenate([-sin, sin], axis=1), (1, LANES // HEAD_DIM))
    return cosl, sinl, cos.T, sin.T


def _select_threshold(count_ge, count_gt, count_eq_le, min_ge, max_lt, lo0, hi0, n_allowed, n_keys,
                      row_ok):
    k = float(TOPK)

    def step(carry):
        lo, hi, clo = carry
        mid = 0.5 * lo + 0.5 * hi
        c = count_ge(mid)
        take = c >= k
        return jnp.where(take, mid, lo), jnp.where(take, hi, mid), jnp.where(take, c, clo)

    lo, hi, clo = lax.fori_loop(0, BISECT_ITERS, lambda _, c: step(c), (lo0, hi0, n_allowed))
    big = jnp.full_like(lo, float(4 * n_keys))

    def unresolved(clo_):
        return jnp.logical_and(clo_ > k, row_ok)

    def any_true(mask):
        return jnp.max(jnp.where(mask, 1.0, 0.0)) > 0.5

    def exact(args):
        lo, hi, clo = args

        def cond(c):
            _, _, clo, vlo, vhi, it = c
            open_ = jnp.logical_and(unresolved(clo), vlo < vhi)
            return jnp.logical_and(any_true(open_), it < 2 * n_keys)

        def body(c):
            lo, hi, clo, vlo, vhi, it = c
            mid = 0.5 * vlo + 0.5 * vhi
            mid = jnp.where(mid > vlo, mid, vhi)
            cnt = count_ge(mid)
            take = cnt >= k
            lo, hi, clo = jnp.where(take, mid, lo), jnp.where(take, hi, mid), jnp.where(take, cnt, clo)
            return lo, hi, clo, min_ge(lo), max_lt(hi), it + 1

        lo, hi, clo, vlo, _, _ = lax.while_loop(
            cond, body, (lo, hi, clo, min_ge(lo), max_lt(hi), jnp.int32(0)))
        lo = vlo
        need = k - count_gt(lo)

        def jstep(_, c):
            jlo, jhi = c
            mid = jnp.floor(0.5 * (jlo + jhi))
            ok = count_eq_le(lo, mid) >= need
            return jnp.where(ok, jlo, mid), jnp.where(ok, mid, jhi)

        jlo0 = jnp.full_like(lo, -1.0)
        jhi0 = jnp.full_like(lo, float(n_keys - 1))
        _, jhi = lax.fori_loop(0, 13, jstep, (jlo0, jhi0))
        return lo, jnp.where(clo > k, jhi, big)

    def fast(args):
        return args[0], big

    return lax.cond(any_true(unresolved(clo)), exact, fast, (lo, hi, clo))


def _fold_rows(x, op):
    groups = [x[r:r + 8] for r in range(0, x.shape[0], 8)]
    parts = groups[:FOLD_CHAINS]
    for n, grp in enumerate(groups[FOLD_CHAINS:]):
        parts[n % FOLD_CHAINS] = op(parts[n % FOLD_CHAINS], grp)
    while len(parts) > 1:
        nxt = [op(parts[a], parts[a + 1]) for a in range(0, len(parts) - 1, 2)]
        if len(parts) % 2:
            nxt.append(parts[-1])
        parts = nxt
    return parts[0]


def _attn_prompt_kernel(qt_ref, qit_ref, wit_ref, kbf_ref, kiw_ref, vt_ref, o_ref,
                        s_ref, sc_ref, ri_ref, rq_ref):
    i = pl.program_id(1)
    w = wit_ref[...]
    zeros_half = jnp.zeros((HEAD_DIM, LANES), BF16)

    for hp in range(4):
        a = qit_ref[(2 * hp) * 64:(2 * hp + 1) * 64, :]
        b = qit_ref[(2 * hp + 1) * 64:(2 * hp + 2) * 64, :]
        ri_ref[hp] = jnp.concatenate(
            [jnp.concatenate([a, zeros_half], axis=0), jnp.concatenate([b, zeros_half], axis=0)], axis=1)
        g = hp // 2
        a = qt_ref[(2 * hp) * 64:(2 * hp + 1) * 64, :]
        b = qt_ref[(2 * hp + 1) * 64:(2 * hp + 2) * 64, :]
        if g == 0:
            ab = jnp.concatenate([jnp.concatenate([a, zeros_half], axis=0),
                                  jnp.concatenate([b, zeros_half], axis=0)], axis=1)
        else:
            ab = jnp.concatenate([jnp.concatenate([zeros_half, a], axis=0),
                                  jnp.concatenate([zeros_half, b], axis=0)], axis=1)
        rq_ref[hp] = ab

    inf = jnp.float32(jnp.inf)
    n_steps = (i * LANES + LANES + KEY_STEP - 1) // KEY_STEP

    def process(ns):
        n_keys = ns * KEY_STEP
        keys = slice(0, n_keys)
        row = lax.broadcasted_iota(jnp.int32, (n_keys, LANES), 0)
        col = lax.broadcasted_iota(jnp.int32, (n_keys, LANES), 1)
        qpos = col + i * LANES
        allowed = row <= qpos

        kt = kiw_ref[keys, :]
        acc = jnp.zeros((n_keys, LANES), F32)
        for hp in range(4):
            s2 = jnp.dot(kt, ri_ref[hp], preferred_element_type=F32)
            acc = acc + jnp.maximum(s2[:, :LANES], 0.0) * w[2 * hp:2 * hp + 1, :]
            acc = acc + jnp.maximum(s2[:, LANES:], 0.0) * w[2 * hp + 1:2 * hp + 2, :]
        s_ref[keys, :] = jnp.where(allowed, acc, -inf)
        lo0 = jnp.min(_fold_rows(jnp.where(allowed, acc, inf), jnp.minimum), axis=0, keepdims=True)
        mx1 = jnp.max(_fold_rows(jnp.where(allowed, acc, -inf), jnp.maximum), axis=0, keepdims=True)
        hi0 = mx1 + (jnp.abs(mx1) * (2.0 ** -20) + 1e-30)
        n_allowed = (qpos[0:1, :] + 1).astype(F32)

        chunk = 8 * FOLD_CHAINS
        kidx0 = lax.broadcasted_iota(jnp.int32, (chunk, LANES), 0).astype(F32)

        def key_reduce(fn, op, finish):
            part = None
            for r in range(0, n_keys, chunk):
                v = fn(s_ref[r:r + chunk, :], r)
                part = v if part is None else op(part, v)
            return finish(_fold_rows(part, op), axis=0, keepdims=True)

        count_ge = lambda thr: key_reduce(lambda t, r: jnp.where(t >= thr, 1.0, 0.0), jnp.add, jnp.sum)
        count_gt = lambda thr: key_reduce(lambda t, r: jnp.where(t > thr, 1.0, 0.0), jnp.add, jnp.sum)
        count_eq_le = lambda thr, jm: key_reduce(
            lambda t, r: jnp.where(t == thr, jnp.where(kidx0 <= jm - r, 1.0, 0.0), 0.0), jnp.add, jnp.sum)
        min_ge = lambda thr: key_reduce(lambda t, r: jnp.where(t >= thr, t, inf), jnp.minimum, jnp.min)
        max_lt = lambda thr: key_reduce(lambda t, r: jnp.where(t < thr, t, -inf), jnp.maximum, jnp.max)
        kidx = row.astype(F32)
        lo, jmax = _select_threshold(count_ge, count_gt, count_eq_le, min_ge, max_lt, lo0, hi0,
                                     n_allowed, SEQ, jnp.full((1, LANES), True))

        t = s_ref[keys, :]
        tie = jnp.where(t == lo, jnp.where(kidx <= jmax, 0.0, NEG), NEG)
        s_ref[keys, :] = jnp.where(t > lo, 0.0, tie)

        kt = kbf_ref[keys, :]
        b = s_ref[keys, :]
        b2 = jnp.concatenate([b, b], axis=1)
        ms = []
        for hp in range(4):
            s2 = jnp.dot(kt, rq_ref[hp], preferred_element_type=F32) + b2
            sc_ref[hp, keys, :] = s2
            ms.append(jnp.max(_fold_rows(s2, jnp.maximum), axis=0, keepdims=True))
        vt = jnp.concatenate([vt_ref[j] for j in range(ns)], axis=1)
        ones_rows = jnp.ones((ONES_ROWS, n_keys), BF16)
        lhs = [jnp.concatenate([vt[g * 64:(g + 1) * 64, :], ones_rows], axis=0) for g in range(N_KV)]
        pieces = []
        for hp in range(4):
            p = jnp.exp2((sc_ref[hp, keys, :] - ms[hp]).astype(BF16))
            o = jnp.dot(lhs[hp // 2], p, preferred_element_type=F32)
            o2 = o[0:HEAD_DIM, :] / o[HEAD_DIM:HEAD_DIM + 1, :]
            pieces.append(o2[:, :LANES])
            pieces.append(o2[:, LANES:])
        o_ref[...] = jnp.concatenate(pieces, axis=0).T

    for ns in range(1, SEQ // KEY_STEP + 1):
        pl.when(n_steps == ns)(functools.partial(process, ns))


def _attn_prompt(qt, qit, wit, kbf, kiw, vt3):
    n_blocks = SEQ // LANES
    grid = (BATCH, n_blocks)
    qmap = lambda n, i: (0, n * n_blocks + i)
    in_specs = [
        pl.BlockSpec((512, LANES), qmap),
        pl.BlockSpec((512, LANES), qmap),
        pl.BlockSpec((8, LANES), qmap),
        pl.BlockSpec((SEQ, 128), lambda n, i: (n, 0)),
        pl.BlockSpec((SEQ, 128), lambda n, i: (n, 0)),
        pl.BlockSpec((SEQ // KEY_STEP, 128, KEY_STEP), lambda n, i: (n, 0, 0)),
    ]
    return pl.pallas_call(
        _attn_prompt_kernel, grid=grid, in_specs=in_specs,
        out_specs=pl.BlockSpec((LANES, W_A), lambda n, i: (n * n_blocks + i, 0)),
        out_shape=jax.ShapeDtypeStruct((BATCH * SEQ, W_A), F32),
        scratch_shapes=[pltpu.VMEM((SEQ, LANES), F32), pltpu.VMEM((4, SEQ, 2 * LANES), F32),
                        pltpu.VMEM((4, 128, 2 * LANES), BF16), pltpu.VMEM((4, 128, 2 * LANES), BF16)],
        compiler_params=pltpu.CompilerParams(dimension_semantics=("arbitrary", "arbitrary"),
                                             vmem_limit_bytes=VMEM_LIMIT),
        name="attn_prompt",
    )(qt, qit, wit, kbf, kiw, vt3)


def _attn_sample_kernel(pt_ref, qi_ref, q_ref, w_ref, kin_ref, kn_ref, vn_ref,
                        ck_hbm, cv_hbm, cki_hbm, o_ref,
                        kbuf, vbuf, kibuf, sems, s_ref, bias_ref):
    step = pl.program_id(0)
    n_steps = pl.num_programs(0)
    gsz = SAMPLE_GROUP
    slot = step % 2

    def copies(group, slot_):
        out = []
        for g in range(gsz):
            seq = group * gsz + g
            for p in range(N_PAGES):
                page = pt_ref[seq, p]
                keys = pl.ds(p * PAGE_SIZE, PAGE_SIZE)
                out.append(pltpu.make_async_copy(ck_hbm.at[page], kbuf.at[slot_, g, :, keys], sems.at[slot_, 0]))
                out.append(pltpu.make_async_copy(cv_hbm.at[page], vbuf.at[slot_, g, :, keys], sems.at[slot_, 1]))
                out.append(pltpu.make_async_copy(cki_hbm.at[page], kibuf.at[slot_, g, :, keys], sems.at[slot_, 2]))
        return out

    new_keys = pl.ds(PAST_LEN, SAMPLE_Q)

    @pl.when(step == 0)
    def _():
        tail = pl.ds(PAST_LEN, KEY_TILE)
        for s_ in range(2):
            for g in range(gsz):
                kbuf[s_, g, :, tail] = jnp.zeros((128, KEY_TILE), F32)
                vbuf[s_, g, :, tail] = jnp.zeros((128, KEY_TILE), F32)
                kibuf[s_, g, :, tail] = jnp.zeros((IDX_DIM, KEY_TILE), F32)
        for c in copies(0, 0):
            c.start()

    @pl.when(step + 1 < n_steps)
    def _():
        for c in copies(step + 1, 1 - slot):
            c.start()

    for c in copies(step, slot):
        c.wait()

    rows8 = lax.broadcasted_iota(jnp.int32, (SAMPLE_Q, SAMPLE_KEYS), 0)
    keyi = lax.broadcasted_iota(jnp.int32, (SAMPLE_Q, SAMPLE_KEYS), 1)
    allowed = keyi <= (PAST_LEN + rows8)
    inf = jnp.float32(jnp.inf)
    row_ok8 = lax.broadcasted_iota(jnp.int32, (SAMPLE_Q, 1), 0) < DEC_SEQ

    for g in range(gsz):
        kbuf[slot, g, :, new_keys] = kn_ref[g]
        vbuf[slot, g, :, new_keys] = vn_ref[g]
        kibuf[slot, g, :, new_keys] = kin_ref[g]
        kib = kibuf[slot, g].astype(BF16)
        s = jnp.dot(qi_ref[g], kib, preferred_element_type=F32)
        sw = jnp.maximum(s, 0.0) * w_ref[g][:, 0:1]
        acc = jnp.sum(sw.reshape(N_IDX_HEADS, SAMPLE_Q, SAMPLE_KEYS), axis=0)
        s_ref[g * SAMPLE_Q:(g + 1) * SAMPLE_Q, :] = jnp.where(allowed, acc, -inf)

    rows_all = gsz * SAMPLE_Q
    allowed_all = jnp.concatenate([allowed] * gsz, axis=0)
    row_ok = jnp.concatenate([row_ok8] * gsz, axis=0)
    keyf = lax.broadcasted_iota(jnp.int32, (rows_all, SAMPLE_KEYS), 1).astype(F32)

    sc = s_ref[...]
    lo0 = jnp.min(jnp.where(allowed_all, sc, inf), axis=1, keepdims=True)
    mx1 = jnp.max(sc, axis=1, keepdims=True)
    hi0 = mx1 + (jnp.abs(mx1) * (2.0 ** -20) + 1e-30)
    t8 = lax.broadcasted_iota(jnp.int32, (SAMPLE_Q, 1), 0)
    n_allowed = jnp.concatenate([(PAST_LEN + 1 + t8).astype(F32)] * gsz, axis=0)

    def lane_count(hit):
        return jnp.sum(hit, axis=1, keepdims=True)

    count_ge = lambda thr: lane_count(jnp.where(s_ref[...] >= thr, 1.0, 0.0))
    count_gt = lambda thr: lane_count(jnp.where(s_ref[...] > thr, 1.0, 0.0))
    count_eq_le = lambda thr, jm: lane_count(
        jnp.where(s_ref[...] == thr, jnp.where(keyf <= jm, 1.0, 0.0), 0.0))
    min_ge = lambda thr: jnp.min(jnp.where(s_ref[...] >= thr, s_ref[...], inf), axis=1, keepdims=True)
    max_lt = lambda thr: jnp.max(jnp.where(s_ref[...] < thr, s_ref[...], -inf), axis=1, keepdims=True)
    lo, jmax = _select_threshold(count_ge, count_gt, count_eq_le, min_ge, max_lt, lo0, hi0,
                                 n_allowed, SAMPLE_KEYS, row_ok)
    sc = s_ref[...]
    tie = jnp.where(sc == lo, jnp.where(keyf <= jmax, 0.0, NEG), NEG)
    bias_ref[...] = jnp.where(sc > lo, 0.0, tie)

    lane = lax.broadcasted_iota(jnp.int32, (SAMPLE_Q, LANES), 1)
    for g in range(gsz):
        kb = kbuf[slot, g].astype(BF16)
        vb = vbuf[slot, g].astype(BF16)
        b = bias_ref[g * SAMPLE_Q:(g + 1) * SAMPLE_Q, :]
        s = jnp.dot(q_ref[g], kb, preferred_element_type=F32)
        s = s + jnp.concatenate([b] * N_HEADS, axis=0)
        m = jnp.max(s, axis=1, keepdims=True)
        p = jnp.exp2(s - m)
        l = jnp.sum(p, axis=1, keepdims=True)
        o = lax.dot_general(p.astype(BF16), vb, (((1,), (1,)), ((), ())),
                            preferred_element_type=F32) / l
        o_r = pltpu.roll(o, HEAD_DIM, 1)
        tiles = []
        for c in range(4):
            ha, hb = 2 * c, 2 * c + 1
            src_a = o if ha // 4 == 0 else o_r
            src_b = o if hb // 4 == 1 else o_r
            tiles.append(jnp.where(lane < HEAD_DIM, src_a[ha * SAMPLE_Q:(ha + 1) * SAMPLE_Q, :],
                                   src_b[hb * SAMPLE_Q:(hb + 1) * SAMPLE_Q, :]))
        o_ref[g] = jnp.concatenate(tiles, axis=1)


def _attn_sample(page_table, qi_s, q_s, w_s, ki_new, k_new, v_new, ck, cv, cki):
    gsz = SAMPLE_GROUP
    grid = (DEC_BATCH // gsz,)
    blk = lambda shape: pl.BlockSpec((gsz,) + shape, lambda s, pt: (s, 0, 0))
    grid_spec = pltpu.PrefetchScalarGridSpec(
        num_scalar_prefetch=1, grid=grid,
        in_specs=[blk((64, IDX_DIM)), blk((64, 128)), blk((64, LANES)),
                  blk((IDX_DIM, SAMPLE_Q)), blk((128, SAMPLE_Q)), blk((128, SAMPLE_Q)),
                  pl.BlockSpec(memory_space=pl.ANY), pl.BlockSpec(memory_space=pl.ANY),
                  pl.BlockSpec(memory_space=pl.ANY)],
        out_specs=blk((SAMPLE_Q, W_A)),
        scratch_shapes=[pltpu.VMEM((2, gsz, 128, SAMPLE_KEYS), F32),
                        pltpu.VMEM((2, gsz, 128, SAMPLE_KEYS), F32),
                        pltpu.VMEM((2, gsz, IDX_DIM, SAMPLE_KEYS), F32),
                        pltpu.SemaphoreType.DMA((2, 3)),
                        pltpu.VMEM((gsz * SAMPLE_Q, SAMPLE_KEYS), F32),
                        pltpu.VMEM((gsz * SAMPLE_Q, SAMPLE_KEYS), F32)])
    return pl.pallas_call(
        _attn_sample_kernel, grid_spec=grid_spec,
        out_shape=jax.ShapeDtypeStruct((DEC_BATCH, SAMPLE_Q, W_A), F32),
        compiler_params=pltpu.CompilerParams(dimension_semantics=("arbitrary",),
                                             vmem_limit_bytes=VMEM_LIMIT),
        name="attn_sample",
    )(page_table, qi_s, q_s, w_s, ki_new, k_new, v_new, ck, cv, cki)


def _layer_norm(x, g, b):
    mu = jnp.mean(x, axis=-1, keepdims=True)
    xc = x - mu
    var = jnp.mean(xc * xc, axis=-1, keepdims=True)
    return xc * lax.rsqrt(var + LN_EPS) * g + b


def _silu(x):
    return x / (1.0 + jnp.exp(-x))


def _mixer_kernel(x_ref, a_ref, wb_ref, wmix_ref, bsb_ref, lnvg_ref, lnvb_ref, wout_ref,
                  lng_ref, lnb_ref, y_ref, vn_ref, *, chunk, vn_last_only, tiles_per_seq):
    tm = x_ref.shape[0]
    x = x_ref[...]
    zb = jnp.dot(x.astype(BF16), wb_ref[...], preferred_element_type=F32)
    ga = zb[:, 0:512]
    u = zb[:, 512:1024]
    vb = zb[:, 1024:1536]
    gb = zb[:, 1536:2048]
    vn = _layer_norm(vb, lnvg_ref[...], lnvb_ref[...])
    if vn_last_only:
        @pl.when(pl.program_id(0) % tiles_per_seq == tiles_per_seq - 1)
        def _():
            vn_ref[0] = vn[tm - CHUNK:, :]
    else:
        vn_ref[...] = vn
    vnb = vn.astype(BF16)
    rows = []
    for c in range(tm // chunk):
        cols = []
        for g in range(N_GROUPS_B):
            blk = vnb[c * chunk:(c + 1) * chunk, g * GROUP_W_B:(g + 1) * GROUP_W_B]
            cols.append(jnp.dot(wmix_ref[g], blk, preferred_element_type=F32) + bsb_ref[g])
        rows.append(jnp.concatenate(cols, axis=1))
    mixed = jnp.concatenate(rows, axis=0) if len(rows) > 1 else rows[0]
    a_out = a_ref[...] * _silu(ga)
    b_out = u * mixed * _silu(gb)
    cat = jnp.concatenate([a_out, b_out], axis=1).astype(BF16)
    out = jnp.dot(cat, wout_ref[...], preferred_element_type=F32)
    y_ref[...] = _layer_norm(ALPHA * x + out, lng_ref[...], lnb_ref[...])


def _mixer(x2d, attn, wb, wmix, bsb, lnvg, lnvb, wout, lng, lnb, *, chunk, vn_last_only):
    rows = x2d.shape[0]
    tm = ROW_TILE
    tiles_per_seq = SEQ // tm
    grid = (rows // tm,)
    row_map = lambda r: (r, 0)
    const2 = lambda r: (0, 0)
    const3 = lambda r: (0, 0, 0)
    if vn_last_only:
        vn_shape = jax.ShapeDtypeStruct((rows // SEQ, CHUNK, W_B), F32)
        vn_spec = pl.BlockSpec((1, CHUNK, W_B), lambda r: (r // tiles_per_seq, 0, 0))
    else:
        vn_shape = jax.ShapeDtypeStruct((rows, W_B), F32)
        vn_spec = pl.BlockSpec((tm, W_B), row_map)
    kern = functools.partial(_mixer_kernel, chunk=chunk, vn_last_only=vn_last_only,
                             tiles_per_seq=tiles_per_seq)
    return pl.pallas_call(
        kern, grid=grid,
        in_specs=[pl.BlockSpec((tm, D_MODEL), row_map), pl.BlockSpec((tm, W_A), row_map),
                  pl.BlockSpec(wb.shape, const2), pl.BlockSpec(wmix.shape, const3),
                  pl.BlockSpec(bsb.shape, const3), pl.BlockSpec((1, W_B), const2),
                  pl.BlockSpec((1, W_B), const2), pl.BlockSpec(wout.shape, const2),
                  pl.BlockSpec((1, D_MODEL), const2), pl.BlockSpec((1, D_MODEL), const2)],
        out_specs=(pl.BlockSpec((tm, D_MODEL), row_map), vn_spec),
        out_shape=(jax.ShapeDtypeStruct((rows, D_MODEL), F32), vn_shape),
        compiler_params=pltpu.CompilerParams(dimension_semantics=("arbitrary",),
                                             vmem_limit_bytes=VMEM_LIMIT),
        name="mixer_chunk%d" % chunk,
    )(x2d, attn, wb, wmix, bsb, lnvg, lnvb, wout, lng, lnb)


def kernel(x_prompt, x_sample, cache_k, cache_v, cache_k_idx, page_table, w_in, w_s, b_s,
           ln_v_g, ln_v_b, w_out, ln_g, ln_b):
    w = w_in[0]
    wq, wk, wv = w[:, 0:512], w[:, 512:640], w[:, 640:768]
    wga, wqi, wki, wwi = w[:, 768:1280], w[:, 1280:1792], w[:, 1792:1856], w[:, 1856:1864]
    wu, wvb, wgb = w[:, 1864:2376], w[:, 2376:2888], w[:, 2888:3400]
    wstd = jnp.concatenate([wk, wki, jnp.zeros((D_MODEL, 64), F32)], axis=1).astype(BF16)
    wt = jnp.concatenate([wq.T * Q_SCALE, wqi.T, wv.T, wk.T, wki.T, wwi.T,
                          jnp.zeros((WT_ROWS - WT_WI - N_IDX_HEADS, D_MODEL), F32)], axis=0).astype(BF16)
    wb = jnp.concatenate([wga, wu, wvb, wgb], axis=1).astype(BF16)
    wout = w_out[0].astype(BF16)
    lnvg, lnvb = ln_v_g[0][None, :], ln_v_b[0][None, :]
    lng, lnb = ln_g[0][None, :], ln_b[0][None, :]

    xp = x_prompt.reshape(BATCH * SEQ, D_MODEL)
    tabs_p = _rope_tables(jnp.arange(SEQ, dtype=jnp.int32))
    kt_p, vt_p, kit_p, kbf, kiw, qt, qit, vt3, wit = _project(xp, wstd, wt, *tabs_p, SEQ)
    attn_p = _attn_prompt(qt, qit, wit, kbf, kiw, vt3)
    tril = jnp.tril(jnp.ones((CHUNK, CHUNK), F32))
    wmix_p = (w_s[0] * tril[None]).astype(BF16)
    bsb_p = jnp.broadcast_to(b_s[0][:, :, None], (N_GROUPS_B, CHUNK, GROUP_W_B))
    y_p, vn_p = _mixer(xp, attn_p, wb, wmix_p, bsb_p, lnvg, lnvb, wout, lng, lnb,
                       chunk=CHUNK, vn_last_only=True)

    rows_s = DEC_BATCH * DEC_SEQ
    xs = x_sample.reshape(rows_s, D_MODEL)
    pos_s = PAST_LEN + (jnp.arange(rows_s, dtype=jnp.int32) % DEC_SEQ)
    tabs_s = _rope_tables(pos_s)
    kt_s, vt_s, kit_s, _, _, qt_s, qit_s, _, wit_s = _project(xs, wstd, wt, *tabs_s, rows_s)

    def to_rows(a_t):
        a = a_t.reshape(N_HEADS, HEAD_DIM, DEC_BATCH, DEC_SEQ).transpose(2, 0, 3, 1)
        return jnp.pad(a, ((0, 0), (0, 0), (0, SAMPLE_Q - DEC_SEQ), (0, 0)))

    qi_rows = to_rows(qit_s).reshape(DEC_BATCH, N_IDX_HEADS * SAMPLE_Q, IDX_DIM)
    q5 = to_rows(qt_s).reshape(DEC_BATCH, N_KV, N_HEADS // N_KV, SAMPLE_Q, HEAD_DIM)
    zq = jnp.zeros_like(q5[:, 0])
    q_rows = jnp.stack([jnp.concatenate([q5[:, 0], zq], axis=-1),
                        jnp.concatenate([zq, q5[:, 1]], axis=-1)], axis=1)
    q_rows = q_rows.reshape(DEC_BATCH, N_HEADS * SAMPLE_Q, 2 * HEAD_DIM)
    w_rows = jnp.pad(wit_s.reshape(N_IDX_HEADS, DEC_BATCH, DEC_SEQ).transpose(1, 0, 2),
                     ((0, 0), (0, 0), (0, SAMPLE_Q - DEC_SEQ)))
    w_rows = jnp.broadcast_to(w_rows.reshape(DEC_BATCH, N_IDX_HEADS * SAMPLE_Q, 1),
                              (DEC_BATCH, N_IDX_HEADS * SAMPLE_Q, LANES))
    new_cols = lambda a: jnp.pad(a.reshape(a.shape[0], DEC_BATCH, DEC_SEQ).transpose(1, 0, 2),
                                 ((0, 0), (0, 0), (0, SAMPLE_Q - DEC_SEQ)))
    n_pool = cache_k.shape[1]
    ck_t = cache_k[0].transpose(0, 2, 3, 1).reshape(n_pool, N_KV * HEAD_DIM, PAGE_SIZE)
    cv_t = cache_v[0].transpose(0, 2, 3, 1).reshape(n_pool, N_KV * HEAD_DIM, PAGE_SIZE)
    cki_t = cache_k_idx[0].transpose(0, 2, 1)
    attn_s8 = _attn_sample(page_table, qi_rows, q_rows, w_rows, new_cols(kit_s[0]),
                           new_cols(kt_s[0]), new_cols(vt_s[0]), ck_t, cv_t, cki_t)
    attn_s = attn_s8[:, :DEC_SEQ, :].reshape(rows_s, W_A)
    ri = jnp.arange(rows_s, dtype=jnp.int32)
    same_seq = (ri[:, None] // DEC_SEQ) == (ri[None, :] // DEC_SEQ)
    wmix_s = jnp.zeros((N_GROUPS_B, rows_s, rows_s), F32)
    for t in range(DEC_SEQ):
        for s in range(t + 1):
            hit = same_seq & (ri[:, None] % DEC_SEQ == t) & (ri[None, :] % DEC_SEQ == s)
            wmix_s = wmix_s + jnp.where(hit[None], w_s[0, :, t, s][:, None, None], 0.0)
    wmix_s = wmix_s.astype(BF16)
    bsb_s = jnp.broadcast_to(jnp.tile(b_s[0][:, :DEC_SEQ], (1, DEC_BATCH))[:, :, None],
                             (N_GROUPS_B, rows_s, GROUP_W_B))
    y_s, vn_s = _mixer(xs, attn_s, wb, wmix_s, bsb_s, lnvg, lnvb, wout, lng, lnb,
                       chunk=rows_s, vn_last_only=False)

    def heads_last(a, n, t):
        a = a.reshape(a.shape[0], N_KV, HEAD_DIM, a.shape[2]).transpose(0, 3, 1, 2)
        return a.reshape(1, n, t, N_KV, HEAD_DIM)

    return (y_p.reshape(BATCH, SEQ, D_MODEL),
            y_s.reshape(DEC_BATCH, DEC_SEQ, D_MODEL),
            heads_last(kt_p, BATCH, SEQ), heads_last(vt_p, BATCH, SEQ),
            kit_p.transpose(0, 2, 1)[None],
            vn_p.reshape(1, BATCH, CHUNK, W_B),
            heads_last(kt_s, DEC_BATCH, DEC_SEQ), heads_last(vt_s, DEC_BATCH, DEC_SEQ),
            kit_s[0].T.reshape(1, DEC_BATCH, DEC_SEQ, IDX_DIM),
            vn_s.reshape(1, DEC_BATCH, DEC_SEQ, W_B))
```

```python
import functools

import jax
import jax.numpy as jnp
from jax import lax
from jax.experimental import pallas as pl
from jax.experimental.pallas import tpu as pltpu

F32 = jnp.float32
BF16 = jnp.bfloat16

D_MODEL = 1024
SEQ = 2048
BATCH = 8
DEC_BATCH = 128
DEC_SEQ = 4
PAST_LEN = 2048
PAGE_SIZE = 128
N_PAGES = PAST_LEN // PAGE_SIZE
W_A = 512
W_B = 512
HEAD_DIM = 64
N_HEADS = 8
N_KV = 2
N_IDX_HEADS = 8
IDX_DIM = 64
TOPK = 256
CHUNK = 128
N_GROUPS_B = 4
GROUP_W_B = 128
ROPE_THETA = 10000.0
LN_EPS = 1e-5
ALPHA = 2.0 ** 0.25
NEG = -1e30
IDX_SCALE = float((N_IDX_HEADS * IDX_DIM) ** -0.5)

LANES = 128
KEY_TILE = 128
KEY_STEP = 256
ROW_TILE = 512
WT_Q, WT_QI, WT_V, WT_K, WT_KI, WT_WI, WT_ROWS = 0, 512, 1024, 1152, 1280, 1344, 1360
BISECT_ITERS = 24
FOLD_CHAINS = 8
ONES_ROWS = 16
Q_SCALE = float(HEAD_DIM ** -0.5 * 1.4426950408889634)
VMEM_LIMIT = 48 * 1024 * 1024

SAMPLE_GROUP = 4
SAMPLE_IDX_GROUP = 8
SAMPLE_KEYS = PAST_LEN + KEY_TILE
SAMPLE_Q = 8


def _proj_kernel(x_ref, wstd_ref, wt_ref, cosl_ref, sinl_ref, cost_ref, sint_ref,
                 kt_ref, vt32_ref, kit_ref, kbf_ref, kiw_ref, qt_ref, qit_ref, vt_ref, wit_ref):
    tm = x_ref.shape[0]
    xb = x_ref[...].astype(BF16)
    zs = jnp.dot(xb, wstd_ref[...], preferred_element_type=F32)
    cosl = cosl_ref[...]
    sinl = sinl_ref[...]
    lane = lax.broadcasted_iota(jnp.int32, (tm, LANES), 1)
    first_half = (lane % HEAD_DIM) < (HEAD_DIM // 2)

    def rope_lanes(z):
        partner = jnp.where(first_half, pltpu.roll(z, 96, 1), pltpu.roll(z, 32, 1))
        return z * cosl + partner * sinl

    kbf_ref[...] = rope_lanes(zs[:, 0:128]).astype(BF16)
    kiw_ref[...] = rope_lanes(zs[:, 128:256]).astype(BF16)

    zt = lax.dot_general(wt_ref[...], xb, (((1,), (1,)), ((), ())),
                         preferred_element_type=F32)
    cost = cost_ref[...]
    sint = sint_ref[...]
    half = HEAD_DIM // 2

    def rope_rows(base, n_heads):
        pieces = []
        for h in range(n_heads):
            x1 = zt[base + h * HEAD_DIM: base + h * HEAD_DIM + half]
            x2 = zt[base + h * HEAD_DIM + half: base + (h + 1) * HEAD_DIM]
            pieces.append(x1 * cost - x2 * sint)
            pieces.append(x2 * cost + x1 * sint)
        return jnp.concatenate(pieces, axis=0)

    qt_ref[...] = rope_rows(WT_Q, N_HEADS).astype(BF16)
    qit_ref[...] = rope_rows(WT_QI, N_IDX_HEADS).astype(BF16)
    v_t = zt[WT_V:WT_V + 128]
    vt32_ref[0] = v_t
    vtb = v_t.astype(BF16)
    for c in range(tm // KEY_STEP):
        vt_ref[c] = vtb[:, c * KEY_STEP:(c + 1) * KEY_STEP]
    kt_ref[0] = rope_rows(WT_K, N_KV)
    kit_ref[0] = rope_rows(WT_KI, 1)
    wit_ref[...] = zt[WT_WI:WT_WI + N_IDX_HEADS] * IDX_SCALE


def _project(x2d, wstd, wt, cosl, sinl, cost, sint, seq_len):
    rows = x2d.shape[0]
    tm = ROW_TILE
    grid = (rows // tm,)
    n_pos_tiles = seq_len // tm
    row_map = lambda r: (r, 0)
    col_map = lambda r: (0, r)
    const = lambda r: (0, 0)
    seq_map = lambda r: (r // n_pos_tiles, 0, r % n_pos_tiles)
    out_shape = (
        jax.ShapeDtypeStruct((rows // seq_len, 128, seq_len), F32),
        jax.ShapeDtypeStruct((rows // seq_len, 128, seq_len), F32),
        jax.ShapeDtypeStruct((rows // seq_len, IDX_DIM, seq_len), F32),
        jax.ShapeDtypeStruct((rows, 128), BF16),
        jax.ShapeDtypeStruct((rows, 128), BF16),
        jax.ShapeDtypeStruct((512, rows), BF16),
        jax.ShapeDtypeStruct((512, rows), BF16),
        jax.ShapeDtypeStruct((rows // KEY_STEP, 128, KEY_STEP), BF16),
        jax.ShapeDtypeStruct((8, rows), F32),
    )
    out_specs = (
        pl.BlockSpec((1, 128, tm), seq_map),
        pl.BlockSpec((1, 128, tm), seq_map),
        pl.BlockSpec((1, IDX_DIM, tm), seq_map),
        pl.BlockSpec((tm, 128), row_map),
        pl.BlockSpec((tm, 128), row_map),
        pl.BlockSpec((512, tm), col_map),
        pl.BlockSpec((512, tm), col_map),
        pl.BlockSpec((tm // KEY_STEP, 128, KEY_STEP), lambda r: (r, 0, 0)),
        pl.BlockSpec((8, tm), col_map),
    )
    in_specs = [
        pl.BlockSpec((tm, D_MODEL), row_map),
        pl.BlockSpec(wstd.shape, const),
        pl.BlockSpec(wt.shape, const),
        pl.BlockSpec((tm, LANES), lambda r: (r % n_pos_tiles, 0)),
        pl.BlockSpec((tm, LANES), lambda r: (r % n_pos_tiles, 0)),
        pl.BlockSpec((HEAD_DIM // 2, tm), lambda r: (0, r % n_pos_tiles)),
        pl.BlockSpec((HEAD_DIM // 2, tm), lambda r: (0, r % n_pos_tiles)),
    ]
    return pl.pallas_call(
        _proj_kernel, grid=grid, in_specs=in_specs, out_specs=out_specs, out_shape=out_shape,
        compiler_params=pltpu.CompilerParams(dimension_semantics=("arbitrary",),
                                             vmem_limit_bytes=VMEM_LIMIT),
        name="proj_attn_side",
    )(x2d, wstd, wt, cosl, sinl, cost, sint)


def _rope_tables(pos):
    half = HEAD_DIM // 2
    inv = ROPE_THETA ** (-jnp.arange(half, dtype=F32) / half)
    ang = pos.astype(F32)[:, None] * inv[None, :]
    cos = jnp.cos(ang)
    sin = jnp.sin(ang)
    cosl = jnp.tile(cos, (1, LANES // half))
    sinl = jnp.tile(jnp.concatenate([-sin, sin], axis=1), (1, LANES // HEAD_DIM))
    return cosl, sinl, cos.T, sin.T


def _select_threshold(count_ge, count_gt, count_eq_le, min_ge, max_lt, lo0, hi0, n_allowed, n_keys,
                      row_ok):
    k = float(TOPK)

    def step(carry):
        lo, hi, clo = carry
        mid = 0.5 * lo + 0.5 * hi
        c = count_ge(mid)
        take = c >= k
        return jnp.where(take, mid, lo), jnp.where(take, hi, mid), jnp.where(take, c, clo)

    lo, hi, clo = lax.fori_loop(0, BISECT_ITERS, lambda _, c: step(c), (lo0, hi0, n_allowed))
    big = jnp.full_like(lo, float(4 * n_keys))

    def unresolved(clo_):
        return jnp.logical_and(clo_ > k, row_ok)

    def any_true(mask):
        return jnp.max(jnp.where(mask, 1.0, 0.0)) > 0.5

    def exact(args):
        lo, hi, clo = args

        def cond(c):
            _, _, clo, vlo, vhi, it = c
            open_ = jnp.logical_and(unresolved(clo), vlo < vhi)
            return jnp.logical_and(any_true(open_), it < 2 * n_keys)

        def body(c):
            lo, hi, clo, vlo, vhi, it = c
            mid = 0.5 * vlo + 0.5 * vhi
            mid = jnp.where(mid > vlo, mid, vhi)
            cnt = count_ge(mid)
            take = cnt >= k
            lo, hi, clo = jnp.where(take, mid, lo), jnp.where(take, hi, mid), jnp.where(take, cnt, clo)
            return lo, hi, clo, min_ge(lo), max_lt(hi), it + 1

        lo, hi, clo, vlo, _, _ = lax.while_loop(
            cond, body, (lo, hi, clo, min_ge(lo), max_lt(hi), jnp.int32(0)))
        lo = vlo
        need = k - count_gt(lo)

        def jstep(_, c):
            jlo, jhi = c
            mid = jnp.floor(0.5 * (jlo + jhi))
            ok = count_eq_le(lo, mid) >= need
            return jnp.where(ok, jlo, mid), jnp.where(ok, mid, jhi)

        jlo0 = jnp.full_like(lo, -1.0)
        jhi0 = jnp.full_like(lo, float(n_keys - 1))
        _, jhi = lax.fori_loop(0, 13, jstep, (jlo0, jhi0))
        return lo, jnp.where(clo > k, jhi, big)

    def fast(args):
        return args[0], big

    return lax.cond(any_true(unresolved(clo)), exact, fast, (lo, hi, clo))


def _fold_rows(x, op):
    groups = [x[r:r + 8] for r in range(0, x.shape[0], 8)]
    parts = groups[:FOLD_CHAINS]
    for n, grp in enumerate(groups[FOLD_CHAINS:]):
        parts[n % FOLD_CHAINS] = op(parts[n % FOLD_CHAINS], grp)
    while len(parts) > 1:
        nxt = [op(parts[a], parts[a + 1]) for a in range(0, len(parts) - 1, 2)]
        if len(parts) % 2:
            nxt.append(parts[-1])
        parts = nxt
    return parts[0]


def _topk_bias_in_place(s_ref, n_keys, n_allowed, lo0=None, hi0=None):
    lanes = s_ref.shape[1]
    inf = jnp.float32(jnp.inf)
    chunk = 8 * FOLD_CHAINS
    kidx0 = lax.broadcasted_iota(jnp.int32, (chunk, lanes), 0).astype(F32)

    def key_reduce(fn, op, finish):
        part = None
        for r in range(0, n_keys, chunk):
            v = fn(s_ref[r:r + chunk, :], r)
            part = v if part is None else op(part, v)
        return finish(_fold_rows(part, op), axis=0, keepdims=True)

    count_ge = lambda thr: key_reduce(lambda t, r: jnp.where(t >= thr, 1.0, 0.0), jnp.add, jnp.sum)
    count_gt = lambda thr: key_reduce(lambda t, r: jnp.where(t > thr, 1.0, 0.0), jnp.add, jnp.sum)
    count_eq_le = lambda thr, jm: key_reduce(
        lambda t, r: jnp.where(t == thr, jnp.where(kidx0 <= jm - r, 1.0, 0.0), 0.0), jnp.add, jnp.sum)
    min_ge = lambda thr: key_reduce(lambda t, r: jnp.where(t >= thr, t, inf), jnp.minimum, jnp.min)
    max_lt = lambda thr: key_reduce(lambda t, r: jnp.where(t < thr, t, -inf), jnp.maximum, jnp.max)
    if lo0 is None:
        lo0 = key_reduce(lambda t, r: jnp.where(t == -inf, inf, t), jnp.minimum, jnp.min)
        mx1 = key_reduce(lambda t, r: t, jnp.maximum, jnp.max)
        hi0 = mx1 + (jnp.abs(mx1) * (2.0 ** -20) + 1e-30)
    lo, jmax = _select_threshold(count_ge, count_gt, count_eq_le, min_ge, max_lt, lo0, hi0,
                                 n_allowed, n_keys, jnp.full((1, lanes), True))
    for r in range(0, n_keys, chunk):
        t = s_ref[r:r + chunk, :]
        tie = jnp.where(t == lo, jnp.where(kidx0 <= jmax - r, 0.0, NEG), NEG)
        s_ref[r:r + chunk, :] = jnp.where(t > lo, 0.0, tie)


def _attn_prompt_kernel(qt_ref, qit_ref, wit_ref, kbf_ref, kiw_ref, vt_ref, o_ref,
                        s_ref, sc_ref, ri_ref, rq_ref):
    i = pl.program_id(1)
    w = wit_ref[...]
    zeros_half = jnp.zeros((HEAD_DIM, LANES), BF16)

    for hp in range(4):
        a = qit_ref[(2 * hp) * 64:(2 * hp + 1) * 64, :]
        b = qit_ref[(2 * hp + 1) * 64:(2 * hp + 2) * 64, :]
        ri_ref[hp] = jnp.concatenate(
            [jnp.concatenate([a, zeros_half], axis=0), jnp.concatenate([b, zeros_half], axis=0)], axis=1)
        g = hp // 2
        a = qt_ref[(2 * hp) * 64:(2 * hp + 1) * 64, :]
        b = qt_ref[(2 * hp + 1) * 64:(2 * hp + 2) * 64, :]
        if g == 0:
            ab = jnp.concatenate([jnp.concatenate([a, zeros_half], axis=0),
                                  jnp.concatenate([b, zeros_half], axis=0)], axis=1)
        else:
            ab = jnp.concatenate([jnp.concatenate([zeros_half, a], axis=0),
                                  jnp.concatenate([zeros_half, b], axis=0)], axis=1)
        rq_ref[hp] = ab

    inf = jnp.float32(jnp.inf)
    n_steps = (i * LANES + LANES + KEY_STEP - 1) // KEY_STEP

    def process(ns):
        n_keys = ns * KEY_STEP
        keys = slice(0, n_keys)
        row = lax.broadcasted_iota(jnp.int32, (n_keys, LANES), 0)
        col = lax.broadcasted_iota(jnp.int32, (n_keys, LANES), 1)
        qpos = col + i * LANES
        allowed = row <= qpos

        kt = kiw_ref[keys, :]
        acc = jnp.zeros((n_keys, LANES), F32)
        for hp in range(4):
            s2 = jnp.dot(kt, ri_ref[hp], preferred_element_type=F32)
            acc = acc + jnp.maximum(s2[:, :LANES], 0.0) * w[2 * hp:2 * hp + 1, :]
            acc = acc + jnp.maximum(s2[:, LANES:], 0.0) * w[2 * hp + 1:2 * hp + 2, :]
        s_ref[keys, :] = jnp.where(allowed, acc, -inf)
        lo0 = jnp.min(_fold_rows(jnp.where(allowed, acc, inf), jnp.minimum), axis=0, keepdims=True)
        mx1 = jnp.max(_fold_rows(jnp.where(allowed, acc, -inf), jnp.maximum), axis=0, keepdims=True)
        hi0 = mx1 + (jnp.abs(mx1) * (2.0 ** -20) + 1e-30)
        n_allowed = (qpos[0:1, :] + 1).astype(F32)

        _topk_bias_in_place(s_ref, n_keys, n_allowed, lo0, hi0)

        kt = kbf_ref[keys, :]
        b = s_ref[keys, :]
        b2 = jnp.concatenate([b, b], axis=1)
        ms = []
        for hp in range(4):
            s2 = jnp.dot(kt, rq_ref[hp], preferred_element_type=F32) + b2
            sc_ref[hp, keys, :] = s2
            ms.append(jnp.max(_fold_rows(s2, jnp.maximum), axis=0, keepdims=True))
        vt = jnp.concatenate([vt_ref[j] for j in range(ns)], axis=1)
        ones_rows = jnp.ones((ONES_ROWS, n_keys), BF16)
        lhs = [jnp.concatenate([vt[g * 64:(g + 1) * 64, :], ones_rows], axis=0) for g in range(N_KV)]
        pieces = []
        for hp in range(4):
            p = jnp.exp2((sc_ref[hp, keys, :] - ms[hp]).astype(BF16))
            o = jnp.dot(lhs[hp // 2], p, preferred_element_type=F32)
            o2 = o[0:HEAD_DIM, :] / o[HEAD_DIM:HEAD_DIM + 1, :]
            pieces.append(o2[:, :LANES])
            pieces.append(o2[:, LANES:])
        o_ref[...] = jnp.concatenate(pieces, axis=0).T

    for ns in range(1, SEQ // KEY_STEP + 1):
        pl.when(n_steps == ns)(functools.partial(process, ns))


def _attn_prompt(qt, qit, wit, kbf, kiw, vt3):
    n_blocks = SEQ // LANES
    grid = (BATCH, n_blocks)
    qmap = lambda n, i: (0, n * n_blocks + i)
    in_specs = [
        pl.BlockSpec((512, LANES), qmap),
        pl.BlockSpec((512, LANES), qmap),
        pl.BlockSpec((8, LANES), qmap),
        pl.BlockSpec((SEQ, 128), lambda n, i: (n, 0)),
        pl.BlockSpec((SEQ, 128), lambda n, i: (n, 0)),
        pl.BlockSpec((SEQ // KEY_STEP, 128, KEY_STEP), lambda n, i: (n, 0, 0)),
    ]
    return pl.pallas_call(
        _attn_prompt_kernel, grid=grid, in_specs=in_specs,
        out_specs=pl.BlockSpec((LANES, W_A), lambda n, i: (n * n_blocks + i, 0)),
        out_shape=jax.ShapeDtypeStruct((BATCH * SEQ, W_A), F32),
        scratch_shapes=[pltpu.VMEM((SEQ, LANES), F32), pltpu.VMEM((4, SEQ, 2 * LANES), F32),
                        pltpu.VMEM((4, 128, 2 * LANES), BF16), pltpu.VMEM((4, 128, 2 * LANES), BF16)],
        compiler_params=pltpu.CompilerParams(dimension_semantics=("arbitrary", "arbitrary"),
                                             vmem_limit_bytes=VMEM_LIMIT),
        name="attn_prompt",
    )(qt, qit, wit, kbf, kiw, vt3)


def _page_copies(pt_ref, group, slot, gsz, streams):
    out = []
    for g in range(gsz):
        seq = group * gsz + g
        for p in range(N_PAGES):
            page = pt_ref[seq, p]
            keys = pl.ds(p * PAGE_SIZE, PAGE_SIZE)
            for hbm, buf, sem in streams:
                out.append(pltpu.make_async_copy(hbm.at[page], buf.at[slot, g, :, keys], sem.at[slot]))
    return out


def _paged_prefetch(pt_ref, gsz, streams):
    step = pl.program_id(0)
    n_steps = pl.num_programs(0)
    slot = step % 2

    @pl.when(step == 0)
    def _():
        tail = pl.ds(PAST_LEN, KEY_TILE)
        for _, buf, _ in streams:
            for s_ in range(2):
                for g in range(gsz):
                    buf[s_, g, :, tail] = jnp.zeros((buf.shape[2], KEY_TILE), F32)
        for c in _page_copies(pt_ref, 0, 0, gsz, streams):
            c.start()

    @pl.when(step + 1 < n_steps)
    def _():
        for c in _page_copies(pt_ref, step + 1, 1 - slot, gsz, streams):
            c.start()

    for c in _page_copies(pt_ref, step, slot, gsz, streams):
        c.wait()
    return slot


def _sample_index_kernel(pt_ref, qi_ref, w_ref, kin_ref, cki_hbm, s_out_ref, kibuf, sem):
    gsz = SAMPLE_IDX_GROUP
    slot = _paged_prefetch(pt_ref, gsz, [(cki_hbm, kibuf, sem)])
    rows8 = lax.broadcasted_iota(jnp.int32, (SAMPLE_Q, SAMPLE_KEYS), 0)
    keyi = lax.broadcasted_iota(jnp.int32, (SAMPLE_Q, SAMPLE_KEYS), 1)
    allowed = keyi <= (PAST_LEN + rows8)
    new_keys = pl.ds(PAST_LEN, SAMPLE_Q)
    scores = []
    for g in range(gsz):
        kibuf[slot, g, :, new_keys] = kin_ref[g]
        kib = kibuf[slot, g].astype(BF16)
        s = jnp.dot(qi_ref[g], kib, preferred_element_type=F32)
        sw = jnp.maximum(s, 0.0) * w_ref[g][:, 0:1]
        acc = jnp.sum(sw.reshape(N_IDX_HEADS, SAMPLE_Q, SAMPLE_KEYS), axis=0)
        scores.append(jnp.where(allowed, acc, -jnp.inf))
    for a in range(gsz // 2):
        s_out_ref[a] = jnp.where(rows8 < DEC_SEQ, scores[2 * a],
                                 pltpu.roll(scores[2 * a + 1], DEC_SEQ, 0))


def _sample_select_kernel(s_ref, b_ref, st_ref):
    rows = s_ref.shape[0]
    st_ref[...] = s_ref[...].T
    t4 = lax.broadcasted_iota(jnp.int32, (1, rows), 1) % DEC_SEQ
    _topk_bias_in_place(st_ref, SAMPLE_KEYS, (PAST_LEN + 1 + t4).astype(F32))
    b_ref[...] = st_ref[...].T


def _sample_attend_kernel(pt_ref, q_ref, b_ref, kn_ref, vn_ref, ck_hbm, cv_hbm, o_ref,
                          kbuf, vbuf, ksem, vsem):
    gsz = SAMPLE_GROUP
    slot = _paged_prefetch(pt_ref, gsz, [(ck_hbm, kbuf, ksem), (cv_hbm, vbuf, vsem)])
    new_keys = pl.ds(PAST_LEN, SAMPLE_Q)
    lane = lax.broadcasted_iota(jnp.int32, (SAMPLE_Q, LANES), 1)
    for g in range(gsz):
        kbuf[slot, g, :, new_keys] = kn_ref[g]
        vbuf[slot, g, :, new_keys] = vn_ref[g]
        kb = kbuf[slot, g].astype(BF16)
        vb = vbuf[slot, g].astype(BF16)
        b = b_ref[g // 2]
        if g % 2:
            b = pltpu.roll(b, DEC_SEQ, 0)
        s = jnp.dot(q_ref[g], kb, preferred_element_type=F32)
        s = s + jnp.concatenate([b] * N_HEADS, axis=0)
        m = jnp.max(s, axis=1, keepdims=True)
        p = jnp.exp2(s - m)
        l = jnp.sum(p, axis=1, keepdims=True)
        o = lax.dot_general(p.astype(BF16), vb, (((1,), (1,)), ((), ())),
                            preferred_element_type=F32) / l
        o_r = pltpu.roll(o, HEAD_DIM, 1)
        tiles = []
        for c in range(4):
            ha, hb = 2 * c, 2 * c + 1
            src_a = o if ha // 4 == 0 else o_r
            src_b = o if hb // 4 == 1 else o_r
            tiles.append(jnp.where(lane < HEAD_DIM, src_a[ha * SAMPLE_Q:(ha + 1) * SAMPLE_Q, :],
                                   src_b[hb * SAMPLE_Q:(hb + 1) * SAMPLE_Q, :]))
        o_ref[g] = jnp.concatenate(tiles, axis=1)


def _attn_sample(page_table, qi_s, q_s, w_s, ki_new, k_new, v_new, ck, cv, cki):
    params = pltpu.CompilerParams(dimension_semantics=("arbitrary",), vmem_limit_bytes=VMEM_LIMIT)
    blk = lambda n, shape: pl.BlockSpec((n,) + shape, lambda s, pt: (s, 0, 0))
    hbm = pl.BlockSpec(memory_space=pl.ANY)
    n_pairs = DEC_BATCH // 2

    gi = SAMPLE_IDX_GROUP
    scores = pl.pallas_call(
        _sample_index_kernel,
        grid_spec=pltpu.PrefetchScalarGridSpec(
            num_scalar_prefetch=1, grid=(DEC_BATCH // gi,),
            in_specs=[blk(gi, (64, IDX_DIM)), blk(gi, (64, LANES)), blk(gi, (IDX_DIM, SAMPLE_Q)), hbm],
            out_specs=blk(gi // 2, (SAMPLE_Q, SAMPLE_KEYS)),
            scratch_shapes=[pltpu.VMEM((2, gi, IDX_DIM, SAMPLE_KEYS), F32),
                            pltpu.SemaphoreType.DMA((2,))]),
        out_shape=jax.ShapeDtypeStruct((n_pairs, SAMPLE_Q, SAMPLE_KEYS), F32),
        compiler_params=params, name="sample_index",
    )(page_table, qi_s, w_s, ki_new, cki)

    rows = n_pairs * SAMPLE_Q
    bias = pl.pallas_call(
        _sample_select_kernel, grid=(1,),
        in_specs=[pl.BlockSpec((rows, SAMPLE_KEYS), lambda s: (0, 0))],
        out_specs=pl.BlockSpec((rows, SAMPLE_KEYS), lambda s: (0, 0)),
        out_shape=jax.ShapeDtypeStruct((rows, SAMPLE_KEYS), F32),
        scratch_shapes=[pltpu.VMEM((SAMPLE_KEYS, rows), F32)],
        compiler_params=params, name="sample_select",
    )(scores.reshape(rows, SAMPLE_KEYS)).reshape(n_pairs, SAMPLE_Q, SAMPLE_KEYS)

    ga = SAMPLE_GROUP
    return pl.pallas_call(
        _sample_attend_kernel,
        grid_spec=pltpu.PrefetchScalarGridSpec(
            num_scalar_prefetch=1, grid=(DEC_BATCH // ga,),
            in_specs=[blk(ga, (64, 128)), blk(ga // 2, (SAMPLE_Q, SAMPLE_KEYS)),
                      blk(ga, (128, SAMPLE_Q)), blk(ga, (128, SAMPLE_Q)), hbm, hbm],
            out_specs=blk(ga, (SAMPLE_Q, W_A)),
            scratch_shapes=[pltpu.VMEM((2, ga, 128, SAMPLE_KEYS), F32),
                            pltpu.VMEM((2, ga, 128, SAMPLE_KEYS), F32),
                            pltpu.SemaphoreType.DMA((2,)), pltpu.SemaphoreType.DMA((2,))]),
        out_shape=jax.ShapeDtypeStruct((DEC_BATCH, SAMPLE_Q, W_A), F32),
        compiler_params=params, name="sample_attend",
    )(page_table, q_s, bias, k_new, v_new, ck, cv)


def _layer_norm(x, g, b):
    mu = jnp.mean(x, axis=-1, keepdims=True)
    xc = x - mu
    var = jnp.mean(xc * xc, axis=-1, keepdims=True)
    return xc * lax.rsqrt(var + LN_EPS) * g + b


def _silu(x):
    return x / (1.0 + jnp.exp(-x))


def _mixer_kernel(x_ref, a_ref, wb_ref, wmix_ref, bsb_ref, lnvg_ref, lnvb_ref, wout_ref,
                  lng_ref, lnb_ref, y_ref, vn_ref, *, chunk, vn_last_only, tiles_per_seq):
    tm = x_ref.shape[0]
    x = x_ref[...]
    zb = jnp.dot(x.astype(BF16), wb_ref[...], preferred_element_type=F32)
    ga = zb[:, 0:512]
    u = zb[:, 512:1024]
    vb = zb[:, 1024:1536]
    gb = zb[:, 1536:2048]
    vn = _layer_norm(vb, lnvg_ref[...], lnvb_ref[...])
    if vn_last_only:
        @pl.when(pl.program_id(0) % tiles_per_seq == tiles_per_seq - 1)
        def _():
            vn_ref[0] = vn[tm - CHUNK:, :]
    else:
        vn_ref[...] = vn
    vnb = vn.astype(BF16)
    rows = []
    for c in range(tm // chunk):
        cols = []
        for g in range(N_GROUPS_B):
            blk = vnb[c * chunk:(c + 1) * chunk, g * GROUP_W_B:(g + 1) * GROUP_W_B]
            cols.append(jnp.dot(wmix_ref[g], blk, preferred_element_type=F32) + bsb_ref[g])
        rows.append(jnp.concatenate(cols, axis=1))
    mixed = jnp.concatenate(rows, axis=0) if len(rows) > 1 else rows[0]
    a_out = a_ref[...] * _silu(ga)
    b_out = u * mixed * _silu(gb)
    cat = jnp.concatenate([a_out, b_out], axis=1).astype(BF16)
    out = jnp.dot(cat, wout_ref[...], preferred_element_type=F32)
    y_ref[...] = _layer_norm(ALPHA * x + out, lng_ref[...], lnb_ref[...])


def _mixer(x2d, attn, wb, wmix, bsb, lnvg, lnvb, wout, lng, lnb, *, chunk, vn_last_only):
    rows = x2d.shape[0]
    tm = ROW_TILE
    tiles_per_seq = SEQ // tm
    grid = (rows // tm,)
    row_map = lambda r: (r, 0)
    const2 = lambda r: (0, 0)
    const3 = lambda r: (0, 0, 0)
    if vn_last_only:
        vn_shape = jax.ShapeDtypeStruct((rows // SEQ, CHUNK, W_B), F32)
        vn_spec = pl.BlockSpec((1, CHUNK, W_B), lambda r: (r // tiles_per_seq, 0, 0))
    else:
        vn_shape = jax.ShapeDtypeStruct((rows, W_B), F32)
        vn_spec = pl.BlockSpec((tm, W_B), row_map)
    kern = functools.partial(_mixer_kernel, chunk=chunk, vn_last_only=vn_last_only,
                             tiles_per_seq=tiles_per_seq)
    return pl.pallas_call(
        kern, grid=grid,
        in_specs=[pl.BlockSpec((tm, D_MODEL), row_map), pl.BlockSpec((tm, W_A), row_map),
                  pl.BlockSpec(wb.shape, const2), pl.BlockSpec(wmix.shape, const3),
                  pl.BlockSpec(bsb.shape, const3), pl.BlockSpec((1, W_B), const2),
                  pl.BlockSpec((1, W_B), const2), pl.BlockSpec(wout.shape, const2),
                  pl.BlockSpec((1, D_MODEL), const2), pl.BlockSpec((1, D_MODEL), const2)],
        out_specs=(pl.BlockSpec((tm, D_MODEL), row_map), vn_spec),
        out_shape=(jax.ShapeDtypeStruct((rows, D_MODEL), F32), vn_shape),
        compiler_params=pltpu.CompilerParams(dimension_semantics=("arbitrary",),
                                             vmem_limit_bytes=VMEM_LIMIT),
        name="mixer_chunk%d" % chunk,
    )(x2d, attn, wb, wmix, bsb, lnvg, lnvb, wout, lng, lnb)


def kernel(x_prompt, x_sample, cache_k, cache_v, cache_k_idx, page_table, w_in, w_s, b_s,
           ln_v_g, ln_v_b, w_out, ln_g, ln_b):
    w = w_in[0]
    wq, wk, wv = w[:, 0:512], w[:, 512:640], w[:, 640:768]
    wga, wqi, wki, wwi = w[:, 768:1280], w[:, 1280:1792], w[:, 1792:1856], w[:, 1856:1864]
    wu, wvb, wgb = w[:, 1864:2376], w[:, 2376:2888], w[:, 2888:3400]
    wstd = jnp.concatenate([wk, wki, jnp.zeros((D_MODEL, 64), F32)], axis=1).astype(BF16)
    wt = jnp.concatenate([wq.T * Q_SCALE, wqi.T, wv.T, wk.T, wki.T, wwi.T,
                          jnp.zeros((WT_ROWS - WT_WI - N_IDX_HEADS, D_MODEL), F32)], axis=0).astype(BF16)
    wb = jnp.concatenate([wga, wu, wvb, wgb], axis=1).astype(BF16)
    wout = w_out[0].astype(BF16)
    lnvg, lnvb = ln_v_g[0][None, :], ln_v_b[0][None, :]
    lng, lnb = ln_g[0][None, :], ln_b[0][None, :]

    xp = x_prompt.reshape(BATCH * SEQ, D_MODEL)
    tabs_p = _rope_tables(jnp.arange(SEQ, dtype=jnp.int32))
    kt_p, vt_p, kit_p, kbf, kiw, qt, qit, vt3, wit = _project(xp, wstd, wt, *tabs_p, SEQ)
    attn_p = _attn_prompt(qt, qit, wit, kbf, kiw, vt3)
    tril = jnp.tril(jnp.ones((CHUNK, CHUNK), F32))
    wmix_p = (w_s[0] * tril[None]).astype(BF16)
    bsb_p = jnp.broadcast_to(b_s[0][:, :, None], (N_GROUPS_B, CHUNK, GROUP_W_B))
    y_p, vn_p = _mixer(xp, attn_p, wb, wmix_p, bsb_p, lnvg, lnvb, wout, lng, lnb,
                       chunk=CHUNK, vn_last_only=True)

    rows_s = DEC_BATCH * DEC_SEQ
    xs = x_sample.reshape(rows_s, D_MODEL)
    pos_s = PAST_LEN + (jnp.arange(rows_s, dtype=jnp.int32) % DEC_SEQ)
    tabs_s = _rope_tables(pos_s)
    kt_s, vt_s, kit_s, _, _, qt_s, qit_s, _, wit_s = _project(xs, wstd, wt, *tabs_s, rows_s)

    def to_rows(a_t):
        a = a_t.reshape(N_HEADS, HEAD_DIM, DEC_BATCH, DEC_SEQ).transpose(2, 0, 3, 1)
        return jnp.pad(a, ((0, 0), (0, 0), (0, SAMPLE_Q - DEC_SEQ), (0, 0)))

    qi_rows = to_rows(qit_s).reshape(DEC_BATCH, N_IDX_HEADS * SAMPLE_Q, IDX_DIM)
    q5 = to_rows(qt_s).reshape(DEC_BATCH, N_KV, N_HEADS // N_KV, SAMPLE_Q, HEAD_DIM)
    zq = jnp.zeros_like(q5[:, 0])
    q_rows = jnp.stack([jnp.concatenate([q5[:, 0], zq], axis=-1),
                        jnp.concatenate([zq, q5[:, 1]], axis=-1)], axis=1)
    q_rows = q_rows.reshape(DEC_BATCH, N_HEADS * SAMPLE_Q, 2 * HEAD_DIM)
    w_rows = jnp.pad(wit_s.reshape(N_IDX_HEADS, DEC_BATCH, DEC_SEQ).transpose(1, 0, 2),
                     ((0, 0), (0, 0), (0, SAMPLE_Q - DEC_SEQ)))
    w_rows = jnp.broadcast_to(w_rows.reshape(DEC_BATCH, N_IDX_HEADS * SAMPLE_Q, 1),
                              (DEC_BATCH, N_IDX_HEADS * SAMPLE_Q, LANES))
    new_cols = lambda a: jnp.pad(a.reshape(a.shape[0], DEC_BATCH, DEC_SEQ).transpose(1, 0, 2),
                                 ((0, 0), (0, 0), (0, SAMPLE_Q - DEC_SEQ)))
    n_pool = cache_k.shape[1]
    ck_t = cache_k[0].transpose(0, 2, 3, 1).reshape(n_pool, N_KV * HEAD_DIM, PAGE_SIZE)
    cv_t = cache_v[0].transpose(0, 2, 3, 1).reshape(n_pool, N_KV * HEAD_DIM, PAGE_SIZE)
    cki_t = cache_k_idx[0].transpose(0, 2, 1)
    attn_s8 = _attn_sample(page_table, qi_rows, q_rows, w_rows, new_cols(kit_s[0]),
                           new_cols(kt_s[0]), new_cols(vt_s[0]), ck_t, cv_t, cki_t)
    attn_s = attn_s8[:, :DEC_SEQ, :].reshape(rows_s, W_A)
    ri = jnp.arange(rows_s, dtype=jnp.int32)
    same_seq = (ri[:, None] // DEC_SEQ) == (ri[None, :] // DEC_SEQ)
    wmix_s = jnp.zeros((N_GROUPS_B, rows_s, rows_s), F32)
    for t in range(DEC_SEQ):
        for s in range(t + 1):
            hit = same_seq & (ri[:, None] % DEC_SEQ == t) & (ri[None, :] % DEC_SEQ == s)
            wmix_s = wmix_s + jnp.where(hit[None], w_s[0, :, t, s][:, None, None], 0.0)
    wmix_s = wmix_s.astype(BF16)
    bsb_s = jnp.broadcast_to(jnp.tile(b_s[0][:, :DEC_SEQ], (1, DEC_BATCH))[:, :, None],
                             (N_GROUPS_B, rows_s, GROUP_W_B))
    y_s, vn_s = _mixer(xs, attn_s, wb, wmix_s, bsb_s, lnvg, lnvb, wout, lng, lnb,
                       chunk=rows_s, vn_last_only=False)

    def heads_last(a, n, t):
        a = a.reshape(a.shape[0], N_KV, HEAD_DIM, a.shape[2]).transpose(0, 3, 1, 2)
        return a.reshape(1, n, t, N_KV, HEAD_DIM)

    return (y_p.reshape(BATCH, SEQ, D_MODEL),
            y_s.reshape(DEC_BATCH, DEC_SEQ, D_MODEL),
            heads_last(kt_p, BATCH, SEQ), heads_last(vt_p, BATCH, SEQ),
            kit_p.transpose(0, 2, 1)[None],
            vn_p.reshape(1, BATCH, CHUNK, W_B),
            heads_last(kt_s, DEC_BATCH, DEC_SEQ), heads_last(vt_s, DEC_BATCH, DEC_SEQ),
            kit_s[0].T.reshape(1, DEC_BATCH, DEC_SEQ, IDX_DIM),
            vn_s.reshape(1, DEC_BATCH, DEC_SEQ, W_B))
```

```python
import functools

import jax
import jax.numpy as jnp
import numpy as np
from jax import lax
from jax.experimental import pallas as pl
from jax.experimental.pallas import tpu as pltpu

F32 = jnp.float32
BF16 = jnp.bfloat16

D_MODEL = 1024
SEQ = 2048
BATCH = 8
DEC_BATCH = 128
DEC_SEQ = 4
PAST_LEN = 2048
PAGE_SIZE = 128
N_PAGES = PAST_LEN // PAGE_SIZE
W_A = 512
W_B = 512
HEAD_DIM = 64
N_HEADS = 8
N_KV = 2
N_IDX_HEADS = 8
IDX_DIM = 64
TOPK = 256
CHUNK = 128
N_GROUPS_B = 4
GROUP_W_B = 128
ROPE_THETA = 10000.0
LN_EPS = 1e-5
ALPHA = 2.0 ** 0.25
NEG = -1e30
IDX_SCALE = float((N_IDX_HEADS * IDX_DIM) ** -0.5)

LANES = 128
KEY_TILE = 128
KEY_STEP = 256
ROW_TILE = 512
WT_Q, WT_QI, WT_V, WT_K, WT_KI, WT_WI, WT_ROWS = 0, 512, 1024, 1152, 1280, 1344, 1360
BISECT_ITERS = 24
FOLD_CHAINS = 8
ONES_ROWS = 16
Q_SCALE = float(HEAD_DIM ** -0.5 * 1.4426950408889634)
VMEM_LIMIT = 48 * 1024 * 1024

SAMPLE_GROUP = 4
SAMPLE_IDX_GROUP = 8
SAMPLE_KEYS = PAST_LEN + KEY_TILE
SAMPLE_Q = 8


def _proj_kernel(x_ref, wstd_ref, wt_ref, cosl_ref, sinl_ref, cost_ref, sint_ref,
                 kt_ref, vt32_ref, kit_ref, kbf_ref, kiw_ref, qt_ref, qit_ref, vt_ref, wit_ref):
    tm = x_ref.shape[0]
    xb = x_ref[...].astype(BF16)
    zs = jnp.dot(xb, wstd_ref[...], preferred_element_type=F32)
    cosl = cosl_ref[...]
    sinl = sinl_ref[...]
    lane = lax.broadcasted_iota(jnp.int32, (tm, LANES), 1)
    first_half = (lane % HEAD_DIM) < (HEAD_DIM // 2)

    def rope_lanes(z):
        partner = jnp.where(first_half, pltpu.roll(z, 96, 1), pltpu.roll(z, 32, 1))
        return z * cosl + partner * sinl

    kbf_ref[...] = rope_lanes(zs[:, 0:128]).astype(BF16)
    kiw_ref[...] = rope_lanes(zs[:, 128:256]).astype(BF16)

    zt = lax.dot_general(wt_ref[...], xb, (((1,), (1,)), ((), ())),
                         preferred_element_type=F32)
    cost = cost_ref[...]
    sint = sint_ref[...]
    half = HEAD_DIM // 2

    def rope_rows(base, n_heads):
        pieces = []
        for h in range(n_heads):
            x1 = zt[base + h * HEAD_DIM: base + h * HEAD_DIM + half]
            x2 = zt[base + h * HEAD_DIM + half: base + (h + 1) * HEAD_DIM]
            pieces.append(x1 * cost - x2 * sint)
            pieces.append(x2 * cost + x1 * sint)
        return jnp.concatenate(pieces, axis=0)

    qt_ref[...] = rope_rows(WT_Q, N_HEADS).astype(BF16)
    qit_ref[...] = rope_rows(WT_QI, N_IDX_HEADS).astype(BF16)
    v_t = zt[WT_V:WT_V + 128]
    vt32_ref[0] = v_t
    vtb = v_t.astype(BF16)
    for c in range(tm // KEY_STEP):
        vt_ref[c] = vtb[:, c * KEY_STEP:(c + 1) * KEY_STEP]
    kt_ref[0] = rope_rows(WT_K, N_KV)
    kit_ref[0] = rope_rows(WT_KI, 1)
    wit_ref[...] = zt[WT_WI:WT_WI + N_IDX_HEADS] * IDX_SCALE


def _project(x2d, wstd, wt, cosl, sinl, cost, sint, seq_len):
    rows = x2d.shape[0]
    tm = ROW_TILE
    grid = (rows // tm,)
    n_pos_tiles = seq_len // tm
    row_map = lambda r: (r, 0)
    col_map = lambda r: (0, r)
    const = lambda r: (0, 0)
    seq_map = lambda r: (r // n_pos_tiles, 0, r % n_pos_tiles)
    out_shape = (
        jax.ShapeDtypeStruct((rows // seq_len, 128, seq_len), F32),
        jax.ShapeDtypeStruct((rows // seq_len, 128, seq_len), F32),
        jax.ShapeDtypeStruct((rows // seq_len, IDX_DIM, seq_len), F32),
        jax.ShapeDtypeStruct((rows, 128), BF16),
        jax.ShapeDtypeStruct((rows, 128), BF16),
        jax.ShapeDtypeStruct((512, rows), BF16),
        jax.ShapeDtypeStruct((512, rows), BF16),
        jax.ShapeDtypeStruct((rows // KEY_STEP, 128, KEY_STEP), BF16),
        jax.ShapeDtypeStruct((8, rows), F32),
    )
    out_specs = (
        pl.BlockSpec((1, 128, tm), seq_map),
        pl.BlockSpec((1, 128, tm), seq_map),
        pl.BlockSpec((1, IDX_DIM, tm), seq_map),
        pl.BlockSpec((tm, 128), row_map),
        pl.BlockSpec((tm, 128), row_map),
        pl.BlockSpec((512, tm), col_map),
        pl.BlockSpec((512, tm), col_map),
        pl.BlockSpec((tm // KEY_STEP, 128, KEY_STEP), lambda r: (r, 0, 0)),
        pl.BlockSpec((8, tm), col_map),
    )
    in_specs = [
        pl.BlockSpec((tm, D_MODEL), row_map),
        pl.BlockSpec(wstd.shape, const),
        pl.BlockSpec(wt.shape, const),
        pl.BlockSpec((tm, LANES), lambda r: (r % n_pos_tiles, 0)),
        pl.BlockSpec((tm, LANES), lambda r: (r % n_pos_tiles, 0)),
        pl.BlockSpec((HEAD_DIM // 2, tm), lambda r: (0, r % n_pos_tiles)),
        pl.BlockSpec((HEAD_DIM // 2, tm), lambda r: (0, r % n_pos_tiles)),
    ]
    return pl.pallas_call(
        _proj_kernel, grid=grid, in_specs=in_specs, out_specs=out_specs, out_shape=out_shape,
        compiler_params=pltpu.CompilerParams(dimension_semantics=("arbitrary",),
                                             vmem_limit_bytes=VMEM_LIMIT),
        name="proj_attn_side",
    )(x2d, wstd, wt, cosl, sinl, cost, sint)


def _rope_tables(pos):
    half = HEAD_DIM // 2
    inv = ROPE_THETA ** (-np.arange(half, dtype=np.float64) / half)
    ang = pos.astype(np.float64)[:, None] * inv[None, :]
    cos = np.cos(ang)
    sin = np.sin(ang)
    cosl = np.tile(cos, (1, LANES // half))
    sinl = np.tile(np.concatenate([-sin, sin], axis=1), (1, LANES // HEAD_DIM))
    return tuple(jnp.asarray(t, dtype=F32) for t in (cosl, sinl, cos.T, sin.T))


def _select_threshold(count_ge, count_gt, count_eq_le, min_ge, max_lt, lo0, hi0, n_allowed, n_keys,
                      row_ok):
    k = float(TOPK)

    def step(carry):
        lo, hi, clo = carry
        mid = 0.5 * lo + 0.5 * hi
        c = count_ge(mid)
        take = c >= k
        return jnp.where(take, mid, lo), jnp.where(take, hi, mid), jnp.where(take, c, clo)

    lo, hi, clo = lax.fori_loop(0, BISECT_ITERS, lambda _, c: step(c), (lo0, hi0, n_allowed))
    big = jnp.full_like(lo, float(4 * n_keys))

    def unresolved(clo_):
        return jnp.logical_and(clo_ > k, row_ok)

    def any_true(mask):
        return jnp.max(jnp.where(mask, 1.0, 0.0)) > 0.5

    def exact(args):
        lo, hi, clo = args

        def cond(c):
            _, _, clo, vlo, vhi, it = c
            open_ = jnp.logical_and(unresolved(clo), vlo < vhi)
            return jnp.logical_and(any_true(open_), it < 2 * n_keys)

        def body(c):
            lo, hi, clo, vlo, vhi, it = c
            mid = 0.5 * vlo + 0.5 * vhi
            mid = jnp.where(mid > vlo, mid, vhi)
            cnt = count_ge(mid)
            take = cnt >= k
            lo, hi, clo = jnp.where(take, mid, lo), jnp.where(take, hi, mid), jnp.where(take, cnt, clo)
            return lo, hi, clo, min_ge(lo), max_lt(hi), it + 1

        lo, hi, clo, vlo, _, _ = lax.while_loop(
            cond, body, (lo, hi, clo, min_ge(lo), max_lt(hi), jnp.int32(0)))
        lo = vlo
        need = k - count_gt(lo)

        def jstep(_, c):
            jlo, jhi = c
            mid = jnp.floor(0.5 * (jlo + jhi))
            ok = count_eq_le(lo, mid) >= need
            return jnp.where(ok, jlo, mid), jnp.where(ok, mid, jhi)

        jlo0 = jnp.full_like(lo, -1.0)
        jhi0 = jnp.full_like(lo, float(n_keys - 1))
        _, jhi = lax.fori_loop(0, 13, jstep, (jlo0, jhi0))
        return lo, jnp.where(clo > k, jhi, big)

    def fast(args):
        return args[0], big

    return lax.cond(any_true(unresolved(clo)), exact, fast, (lo, hi, clo))


def _fold_rows(x, op):
    groups = [x[r:r + 8] for r in range(0, x.shape[0], 8)]
    parts = groups[:FOLD_CHAINS]
    for n, grp in enumerate(groups[FOLD_CHAINS:]):
        parts[n % FOLD_CHAINS] = op(parts[n % FOLD_CHAINS], grp)
    while len(parts) > 1:
        nxt = [op(parts[a], parts[a + 1]) for a in range(0, len(parts) - 1, 2)]
        if len(parts) % 2:
            nxt.append(parts[-1])
        parts = nxt
    return parts[0]


def _topk_bias_in_place(s_ref, n_keys, n_allowed, lo0=None, hi0=None):
    lanes = s_ref.shape[1]
    inf = jnp.float32(jnp.inf)
    chunk = 8 * FOLD_CHAINS
    kidx0 = lax.broadcasted_iota(jnp.int32, (chunk, lanes), 0).astype(F32)

    def key_reduce(fn, op, finish):
        part = None
        for r in range(0, n_keys, chunk):
            v = fn(s_ref[r:r + chunk, :], r)
            part = v if part is None else op(part, v)
        return finish(_fold_rows(part, op), axis=0, keepdims=True)

    count_ge = lambda thr: key_reduce(lambda t, r: jnp.where(t >= thr, 1.0, 0.0), jnp.add, jnp.sum)
    count_gt = lambda thr: key_reduce(lambda t, r: jnp.where(t > thr, 1.0, 0.0), jnp.add, jnp.sum)
    count_eq_le = lambda thr, jm: key_reduce(
        lambda t, r: jnp.where(t == thr, jnp.where(kidx0 <= jm - r, 1.0, 0.0), 0.0), jnp.add, jnp.sum)
    min_ge = lambda thr: key_reduce(lambda t, r: jnp.where(t >= thr, t, inf), jnp.minimum, jnp.min)
    max_lt = lambda thr: key_reduce(lambda t, r: jnp.where(t < thr, t, -inf), jnp.maximum, jnp.max)
    if lo0 is None:
        lo0 = key_reduce(lambda t, r: jnp.where(t == -inf, inf, t), jnp.minimum, jnp.min)
        mx1 = key_reduce(lambda t, r: t, jnp.maximum, jnp.max)
        hi0 = mx1 + (jnp.abs(mx1) * (2.0 ** -20) + 1e-30)
    lo, jmax = _select_threshold(count_ge, count_gt, count_eq_le, min_ge, max_lt, lo0, hi0,
                                 n_allowed, n_keys, jnp.full((1, lanes), True))
    for r in range(0, n_keys, chunk):
        t = s_ref[r:r + chunk, :]
        tie = jnp.where(t == lo, jnp.where(kidx0 <= jmax - r, 0.0, NEG), NEG)
        s_ref[r:r + chunk, :] = jnp.where(t > lo, 0.0, tie)


def _attn_prompt_kernel(qt_ref, qit_ref, wit_ref, kbf_ref, kiw_ref, vt_ref, o_ref,
                        s_ref, sc_ref, ri_ref, rq_ref):
    i = pl.program_id(1)
    w = wit_ref[...]
    zeros_half = jnp.zeros((HEAD_DIM, LANES), BF16)

    for hp in range(4):
        a = qit_ref[(2 * hp) * 64:(2 * hp + 1) * 64, :]
        b = qit_ref[(2 * hp + 1) * 64:(2 * hp + 2) * 64, :]
        ri_ref[hp] = jnp.concatenate(
            [jnp.concatenate([a, zeros_half], axis=0), jnp.concatenate([b, zeros_half], axis=0)], axis=1)
        g = hp // 2
        a = qt_ref[(2 * hp) * 64:(2 * hp + 1) * 64, :]
        b = qt_ref[(2 * hp + 1) * 64:(2 * hp + 2) * 64, :]
        if g == 0:
            ab = jnp.concatenate([jnp.concatenate([a, zeros_half], axis=0),
                                  jnp.concatenate([b, zeros_half], axis=0)], axis=1)
        else:
            ab = jnp.concatenate([jnp.concatenate([zeros_half, a], axis=0),
                                  jnp.concatenate([zeros_half, b], axis=0)], axis=1)
        rq_ref[hp] = ab

    inf = jnp.float32(jnp.inf)
    n_steps = (i * LANES + LANES + KEY_STEP - 1) // KEY_STEP

    def process(ns):
        n_keys = ns * KEY_STEP
        keys = slice(0, n_keys)
        row = lax.broadcasted_iota(jnp.int32, (n_keys, LANES), 0)
        col = lax.broadcasted_iota(jnp.int32, (n_keys, LANES), 1)
        qpos = col + i * LANES
        allowed = row <= qpos

        kt = kiw_ref[keys, :]
        acc = jnp.zeros((n_keys, LANES), F32)
        for hp in range(4):
            s2 = jnp.dot(kt, ri_ref[hp], preferred_element_type=F32)
            acc = acc + jnp.maximum(s2[:, :LANES], 0.0) * w[2 * hp:2 * hp + 1, :]
            acc = acc + jnp.maximum(s2[:, LANES:], 0.0) * w[2 * hp + 1:2 * hp + 2, :]
        s_ref[keys, :] = jnp.where(allowed, acc, -inf)
        lo0 = jnp.min(_fold_rows(jnp.where(allowed, acc, inf), jnp.minimum), axis=0, keepdims=True)
        mx1 = jnp.max(_fold_rows(jnp.where(allowed, acc, -inf), jnp.maximum), axis=0, keepdims=True)
        hi0 = mx1 + (jnp.abs(mx1) * (2.0 ** -20) + 1e-30)
        n_allowed = (qpos[0:1, :] + 1).astype(F32)

        _topk_bias_in_place(s_ref, n_keys, n_allowed, lo0, hi0)

        kt = kbf_ref[keys, :]
        b = s_ref[keys, :]
        b2 = jnp.concatenate([b, b], axis=1)
        ms = []
        for hp in range(4):
            s2 = jnp.dot(kt, rq_ref[hp], preferred_element_type=F32) + b2
            sc_ref[hp, keys, :] = s2
            ms.append(jnp.max(_fold_rows(s2, jnp.maximum), axis=0, keepdims=True))
        vt = jnp.concatenate([vt_ref[j] for j in range(ns)], axis=1)
        ones_rows = jnp.ones((ONES_ROWS, n_keys), BF16)
        lhs = [jnp.concatenate([vt[g * 64:(g + 1) * 64, :], ones_rows], axis=0) for g in range(N_KV)]
        pieces = []
        for hp in range(4):
            p = jnp.exp2((sc_ref[hp, keys, :] - ms[hp]).astype(BF16))
            o = jnp.dot(lhs[hp // 2], p, preferred_element_type=F32)
            o2 = o[0:HEAD_DIM, :] / o[HEAD_DIM:HEAD_DIM + 1, :]
            pieces.append(o2[:, :LANES])
            pieces.append(o2[:, LANES:])
        o_ref[...] = jnp.concatenate(pieces, axis=0).T

    for ns in range(1, SEQ // KEY_STEP + 1):
        pl.when(n_steps == ns)(functools.partial(process, ns))


def _attn_prompt(qt, qit, wit, kbf, kiw, vt3):
    n_blocks = SEQ // LANES
    grid = (BATCH, n_blocks)
    qmap = lambda n, i: (0, n * n_blocks + i)
    in_specs = [
        pl.BlockSpec((512, LANES), qmap),
        pl.BlockSpec((512, LANES), qmap),
        pl.BlockSpec((8, LANES), qmap),
        pl.BlockSpec((SEQ, 128), lambda n, i: (n, 0)),
        pl.BlockSpec((SEQ, 128), lambda n, i: (n, 0)),
        pl.BlockSpec((SEQ // KEY_STEP, 128, KEY_STEP), lambda n, i: (n, 0, 0)),
    ]
    return pl.pallas_call(
        _attn_prompt_kernel, grid=grid, in_specs=in_specs,
        out_specs=pl.BlockSpec((LANES, W_A), lambda n, i: (n * n_blocks + i, 0)),
        out_shape=jax.ShapeDtypeStruct((BATCH * SEQ, W_A), F32),
        scratch_shapes=[pltpu.VMEM((SEQ, LANES), F32), pltpu.VMEM((4, SEQ, 2 * LANES), F32),
                        pltpu.VMEM((4, 128, 2 * LANES), BF16), pltpu.VMEM((4, 128, 2 * LANES), BF16)],
        compiler_params=pltpu.CompilerParams(dimension_semantics=("arbitrary", "arbitrary"),
                                             vmem_limit_bytes=VMEM_LIMIT),
        name="attn_prompt",
    )(qt, qit, wit, kbf, kiw, vt3)


def _page_copies(pt_ref, group, slot, gsz, streams):
    out = []
    for g in range(gsz):
        seq = group * gsz + g
        for p in range(N_PAGES):
            page = pt_ref[seq, p]
            keys = pl.ds(p * PAGE_SIZE, PAGE_SIZE)
            for hbm, buf, sem in streams:
                out.append(pltpu.make_async_copy(hbm.at[page], buf.at[slot, g, :, keys], sem.at[slot]))
    return out


def _paged_prefetch(pt_ref, gsz, streams):
    step = pl.program_id(0)
    n_steps = pl.num_programs(0)
    slot = step % 2

    @pl.when(step == 0)
    def _():
        tail = pl.ds(PAST_LEN, KEY_TILE)
        for _, buf, _ in streams:
            for s_ in range(2):
                for g in range(gsz):
                    buf[s_, g, :, tail] = jnp.zeros((buf.shape[2], KEY_TILE), F32)
        for c in _page_copies(pt_ref, 0, 0, gsz, streams):
            c.start()

    @pl.when(step + 1 < n_steps)
    def _():
        for c in _page_copies(pt_ref, step + 1, 1 - slot, gsz, streams):
            c.start()

    for c in _page_copies(pt_ref, step, slot, gsz, streams):
        c.wait()
    return slot


def _sample_index_kernel(pt_ref, qi_ref, w_ref, kin_ref, cki_hbm, s_out_ref, kibuf, sem):
    gsz = SAMPLE_IDX_GROUP
    slot = _paged_prefetch(pt_ref, gsz, [(cki_hbm, kibuf, sem)])
    rows8 = lax.broadcasted_iota(jnp.int32, (SAMPLE_Q, SAMPLE_KEYS), 0)
    keyi = lax.broadcasted_iota(jnp.int32, (SAMPLE_Q, SAMPLE_KEYS), 1)
    allowed = keyi <= (PAST_LEN + rows8)
    new_keys = pl.ds(PAST_LEN, DEC_SEQ)
    scores = []
    for g in range(gsz):
        kibuf[slot, g, :, new_keys] = kin_ref[g]
        kib = kibuf[slot, g].astype(BF16)
        s = jnp.dot(qi_ref[g], kib, preferred_element_type=F32)
        sw = jnp.maximum(s, 0.0) * w_ref[g][:, 0:1]
        acc = jnp.sum(sw.reshape(N_IDX_HEADS, SAMPLE_Q, SAMPLE_KEYS), axis=0)
        scores.append(jnp.where(allowed, acc, -jnp.inf))
    for a in range(gsz // 2):
        s_out_ref[a] = jnp.where(rows8 < DEC_SEQ, scores[2 * a],
                                 pltpu.roll(scores[2 * a + 1], DEC_SEQ, 0))


def _sample_select_kernel(s_ref, b_ref, st_ref):
    rows = s_ref.shape[0]
    st_ref[...] = s_ref[...].T
    t4 = lax.broadcasted_iota(jnp.int32, (1, rows), 1) % DEC_SEQ
    _topk_bias_in_place(st_ref, SAMPLE_KEYS, (PAST_LEN + 1 + t4).astype(F32))
    b_ref[...] = st_ref[...].T


def _sample_attend_kernel(pt_ref, q_ref, b_ref, kn_ref, vn_ref, ck_hbm, cv_hbm, o_ref,
                          kbuf, vbuf, ksem, vsem):
    gsz = SAMPLE_GROUP
    slot = _paged_prefetch(pt_ref, gsz, [(ck_hbm, kbuf, ksem), (cv_hbm, vbuf, vsem)])
    new_keys = pl.ds(PAST_LEN, DEC_SEQ)
    lane = lax.broadcasted_iota(jnp.int32, (SAMPLE_Q, LANES), 1)
    for g in range(gsz):
        kbuf[slot, g, :, new_keys] = kn_ref[g]
        vbuf[slot, g, :, new_keys] = vn_ref[g]
        kb = kbuf[slot, g].astype(BF16)
        vb = vbuf[slot, g].astype(BF16)
        b = b_ref[g // 2]
        if g % 2:
            b = pltpu.roll(b, DEC_SEQ, 0)
        s = jnp.dot(q_ref[g], kb, preferred_element_type=F32)
        s = s + jnp.concatenate([b] * N_HEADS, axis=0)
        m = jnp.max(s, axis=1, keepdims=True)
        p = jnp.exp2(s - m)
        l = jnp.sum(p, axis=1, keepdims=True)
        o = lax.dot_general(p.astype(BF16), vb, (((1,), (1,)), ((), ())),
                            preferred_element_type=F32) / l
        o_r = pltpu.roll(o, HEAD_DIM, 1)
        tiles = []
        for c in range(4):
            ha, hb = 2 * c, 2 * c + 1
            src_a = o if ha // 4 == 0 else o_r
            src_b = o if hb // 4 == 1 else o_r
            tiles.append(jnp.where(lane < HEAD_DIM, src_a[ha * SAMPLE_Q:(ha + 1) * SAMPLE_Q, :],
                                   src_b[hb * SAMPLE_Q:(hb + 1) * SAMPLE_Q, :]))
        o_ref[g] = jnp.concatenate(tiles, axis=1)


def _attn_sample(page_table, qi_s, q_s, w_s, ki_new, k_new, v_new, ck, cv, cki):
    params = pltpu.CompilerParams(dimension_semantics=("arbitrary",), vmem_limit_bytes=VMEM_LIMIT)
    blk = lambda n, shape: pl.BlockSpec((n,) + shape, lambda s, pt: (s, 0, 0))
    hbm = pl.BlockSpec(memory_space=pl.ANY)
    n_pairs = DEC_BATCH // 2

    gi = SAMPLE_IDX_GROUP
    scores = pl.pallas_call(
        _sample_index_kernel,
        grid_spec=pltpu.PrefetchScalarGridSpec(
            num_scalar_prefetch=1, grid=(DEC_BATCH // gi,),
            in_specs=[blk(gi, (64, IDX_DIM)), blk(gi, (64, LANES)), blk(gi, (IDX_DIM, DEC_SEQ)), hbm],
            out_specs=blk(gi // 2, (SAMPLE_Q, SAMPLE_KEYS)),
            scratch_shapes=[pltpu.VMEM((2, gi, IDX_DIM, SAMPLE_KEYS), F32),
                            pltpu.SemaphoreType.DMA((2,))]),
        out_shape=jax.ShapeDtypeStruct((n_pairs, SAMPLE_Q, SAMPLE_KEYS), F32),
        compiler_params=params, name="sample_index",
    )(page_table, qi_s, w_s, ki_new, cki)

    rows = n_pairs * SAMPLE_Q
    bias = pl.pallas_call(
        _sample_select_kernel, grid=(1,),
        in_specs=[pl.BlockSpec((rows, SAMPLE_KEYS), lambda s: (0, 0))],
        out_specs=pl.BlockSpec((rows, SAMPLE_KEYS), lambda s: (0, 0)),
        out_shape=jax.ShapeDtypeStruct((rows, SAMPLE_KEYS), F32),
        scratch_shapes=[pltpu.VMEM((SAMPLE_KEYS, rows), F32)],
        compiler_params=params, name="sample_select",
    )(scores.reshape(rows, SAMPLE_KEYS)).reshape(n_pairs, SAMPLE_Q, SAMPLE_KEYS)

    ga = SAMPLE_GROUP
    return pl.pallas_call(
        _sample_attend_kernel,
        grid_spec=pltpu.PrefetchScalarGridSpec(
            num_scalar_prefetch=1, grid=(DEC_BATCH // ga,),
            in_specs=[blk(ga, (64, 128)), blk(ga // 2, (SAMPLE_Q, SAMPLE_KEYS)),
                      blk(ga, (128, DEC_SEQ)), blk(ga, (128, DEC_SEQ)), hbm, hbm],
            out_specs=blk(ga, (SAMPLE_Q, W_A)),
            scratch_shapes=[pltpu.VMEM((2, ga, 128, SAMPLE_KEYS), F32),
                            pltpu.VMEM((2, ga, 128, SAMPLE_KEYS), F32),
                            pltpu.SemaphoreType.DMA((2,)), pltpu.SemaphoreType.DMA((2,))]),
        out_shape=jax.ShapeDtypeStruct((DEC_BATCH, SAMPLE_Q, W_A), F32),
        compiler_params=params, name="sample_attend",
    )(page_table, q_s, bias, k_new, v_new, ck, cv)


def _layer_norm(x, g, b):
    mu = jnp.mean(x, axis=-1, keepdims=True)
    xc = x - mu
    var = jnp.mean(xc * xc, axis=-1, keepdims=True)
    return xc * lax.rsqrt(var + LN_EPS) * g + b


def _silu(x):
    return x / (1.0 + jnp.exp(-x))


def _mixer_kernel(x_ref, a_ref, wb_ref, wmix_ref, bsb_ref, lnvg_ref, lnvb_ref, wout_ref,
                  lng_ref, lnb_ref, y_ref, vn_ref, *, chunk, vn_last_only, tiles_per_seq):
    tm = x_ref.shape[0]
    x = x_ref[...]
    zb = jnp.dot(x.astype(BF16), wb_ref[...], preferred_element_type=F32)
    ga = zb[:, 0:512]
    u = zb[:, 512:1024]
    vb = zb[:, 1024:1536]
    gb = zb[:, 1536:2048]
    vn = _layer_norm(vb, lnvg_ref[...], lnvb_ref[...])
    if vn_last_only:
        @pl.when(pl.program_id(0) % tiles_per_seq == tiles_per_seq - 1)
        def _():
            vn_ref[0] = vn[tm - CHUNK:, :]
    else:
        vn_ref[...] = vn
    vnb = vn.astype(BF16)
    rows = []
    for c in range(tm // chunk):
        cols = []
        for g in range(N_GROUPS_B):
            blk = vnb[c * chunk:(c + 1) * chunk, g * GROUP_W_B:(g + 1) * GROUP_W_B]
            cols.append(jnp.dot(wmix_ref[g], blk, preferred_element_type=F32) + bsb_ref[g])
        rows.append(jnp.concatenate(cols, axis=1))
    mixed = jnp.concatenate(rows, axis=0) if len(rows) > 1 else rows[0]
    a_out = a_ref[...] * _silu(ga)
    b_out = u * mixed * _silu(gb)
    cat = jnp.concatenate([a_out, b_out], axis=1).astype(BF16)
    out = jnp.dot(cat, wout_ref[...], preferred_element_type=F32)
    y_ref[...] = _layer_norm(ALPHA * x + out, lng_ref[...], lnb_ref[...])


def _mixer(x2d, attn, wb, wmix, bsb, lnvg, lnvb, wout, lng, lnb, *, chunk, vn_last_only):
    rows = x2d.shape[0]
    tm = ROW_TILE
    tiles_per_seq = SEQ // tm
    grid = (rows // tm,)
    row_map = lambda r: (r, 0)
    const2 = lambda r: (0, 0)
    const3 = lambda r: (0, 0, 0)
    if vn_last_only:
        vn_shape = jax.ShapeDtypeStruct((rows // SEQ, CHUNK, W_B), F32)
        vn_spec = pl.BlockSpec((1, CHUNK, W_B), lambda r: (r // tiles_per_seq, 0, 0))
    else:
        vn_shape = jax.ShapeDtypeStruct((rows, W_B), F32)
        vn_spec = pl.BlockSpec((tm, W_B), row_map)
    kern = functools.partial(_mixer_kernel, chunk=chunk, vn_last_only=vn_last_only,
                             tiles_per_seq=tiles_per_seq)
    return pl.pallas_call(
        kern, grid=grid,
        in_specs=[pl.BlockSpec((tm, D_MODEL), row_map), pl.BlockSpec((tm, W_A), row_map),
                  pl.BlockSpec(wb.shape, const2), pl.BlockSpec(wmix.shape, const3),
                  pl.BlockSpec(bsb.shape, const3), pl.BlockSpec((1, W_B), const2),
                  pl.BlockSpec((1, W_B), const2), pl.BlockSpec(wout.shape, const2),
                  pl.BlockSpec((1, D_MODEL), const2), pl.BlockSpec((1, D_MODEL), const2)],
        out_specs=(pl.BlockSpec((tm, D_MODEL), row_map), vn_spec),
        out_shape=(jax.ShapeDtypeStruct((rows, D_MODEL), F32), vn_shape),
        compiler_params=pltpu.CompilerParams(dimension_semantics=("arbitrary",),
                                             vmem_limit_bytes=VMEM_LIMIT),
        name="mixer_chunk%d" % chunk,
    )(x2d, attn, wb, wmix, bsb, lnvg, lnvb, wout, lng, lnb)


def kernel(x_prompt, x_sample, cache_k, cache_v, cache_k_idx, page_table, w_in, w_s, b_s,
           ln_v_g, ln_v_b, w_out, ln_g, ln_b):
    w = w_in[0]
    wq, wk, wv = w[:, 0:512], w[:, 512:640], w[:, 640:768]
    wga, wqi, wki, wwi = w[:, 768:1280], w[:, 1280:1792], w[:, 1792:1856], w[:, 1856:1864]
    wu, wvb, wgb = w[:, 1864:2376], w[:, 2376:2888], w[:, 2888:3400]
    wstd = jnp.concatenate([wk, wki, jnp.zeros((D_MODEL, 64), F32)], axis=1).astype(BF16)
    wt = jnp.concatenate([wq.T * Q_SCALE, wqi.T, wv.T, wk.T, wki.T, wwi.T,
                          jnp.zeros((WT_ROWS - WT_WI - N_IDX_HEADS, D_MODEL), F32)], axis=0).astype(BF16)
    wb = jnp.concatenate([wga, wu, wvb, wgb], axis=1).astype(BF16)
    wout = w_out[0].astype(BF16)
    lnvg, lnvb = ln_v_g[0][None, :], ln_v_b[0][None, :]
    lng, lnb = ln_g[0][None, :], ln_b[0][None, :]

    xp = x_prompt.reshape(BATCH * SEQ, D_MODEL)
    tabs_p = _rope_tables(np.arange(SEQ))
    kt_p, vt_p, kit_p, kbf, kiw, qt, qit, vt3, wit = _project(xp, wstd, wt, *tabs_p, SEQ)
    attn_p = _attn_prompt(qt, qit, wit, kbf, kiw, vt3)
    tril = np.tril(np.ones((CHUNK, CHUNK), np.float32))
    wmix_p = (w_s[0] * tril[None]).astype(BF16)
    bsb_p = jnp.broadcast_to(b_s[0][:, :, None], (N_GROUPS_B, CHUNK, GROUP_W_B))
    y_p, vn_p = _mixer(xp, attn_p, wb, wmix_p, bsb_p, lnvg, lnvb, wout, lng, lnb,
                       chunk=CHUNK, vn_last_only=True)

    rows_s = DEC_BATCH * DEC_SEQ
    xs = x_sample.reshape(rows_s, D_MODEL)
    tabs_s = _rope_tables(PAST_LEN + (np.arange(rows_s) % DEC_SEQ))
    kt_s, vt_s, kit_s, _, _, qt_s, qit_s, _, wit_s = _project(xs, wstd, wt, *tabs_s, rows_s)

    def to_rows(a_t):
        a = a_t.reshape(N_HEADS, HEAD_DIM, DEC_BATCH, DEC_SEQ).transpose(2, 0, 3, 1)
        return jnp.pad(a, ((0, 0), (0, 0), (0, SAMPLE_Q - DEC_SEQ), (0, 0)))

    qi_rows = to_rows(qit_s).reshape(DEC_BATCH, N_IDX_HEADS * SAMPLE_Q, IDX_DIM)
    q5 = to_rows(qt_s).reshape(DEC_BATCH, N_KV, N_HEADS // N_KV, SAMPLE_Q, HEAD_DIM)
    zq = jnp.zeros_like(q5[:, 0])
    q_rows = jnp.stack([jnp.concatenate([q5[:, 0], zq], axis=-1),
                        jnp.concatenate([zq, q5[:, 1]], axis=-1)], axis=1)
    q_rows = q_rows.reshape(DEC_BATCH, N_HEADS * SAMPLE_Q, 2 * HEAD_DIM)
    w_rows = jnp.pad(wit_s.reshape(N_IDX_HEADS, DEC_BATCH, DEC_SEQ).transpose(1, 0, 2),
                     ((0, 0), (0, 0), (0, SAMPLE_Q - DEC_SEQ)))
    w_rows = jnp.broadcast_to(w_rows.reshape(DEC_BATCH, N_IDX_HEADS * SAMPLE_Q, 1),
                              (DEC_BATCH, N_IDX_HEADS * SAMPLE_Q, LANES))
    new_cols = lambda a: a.reshape(a.shape[0], DEC_BATCH, DEC_SEQ).transpose(1, 0, 2)
    n_pool = cache_k.shape[1]
    ck_t = cache_k[0].transpose(0, 2, 3, 1).reshape(n_pool, N_KV * HEAD_DIM, PAGE_SIZE)
    cv_t = cache_v[0].transpose(0, 2, 3, 1).reshape(n_pool, N_KV * HEAD_DIM, PAGE_SIZE)
    cki_t = cache_k_idx[0].transpose(0, 2, 1)
    attn_s8 = _attn_sample(page_table, qi_rows, q_rows, w_rows, new_cols(kit_s[0]),
                           new_cols(kt_s[0]), new_cols(vt_s[0]), ck_t, cv_t, cki_t)
    attn_s = attn_s8[:, :DEC_SEQ, :].reshape(rows_s, W_A)
    ri = np.arange(rows_s)
    same_seq = (ri[:, None] // DEC_SEQ) == (ri[None, :] // DEC_SEQ)
    wmix_s = jnp.zeros((N_GROUPS_B, rows_s, rows_s), F32)
    for t in range(DEC_SEQ):
        for s in range(t + 1):
            hit = same_seq & (ri[:, None] % DEC_SEQ == t) & (ri[None, :] % DEC_SEQ == s)
            wmix_s = wmix_s + jnp.where(hit[None], w_s[0, :, t, s][:, None, None], 0.0)
    wmix_s = wmix_s.astype(BF16)
    bsb_s = jnp.broadcast_to(jnp.tile(b_s[0][:, :DEC_SEQ], (1, DEC_BATCH))[:, :, None],
                             (N_GROUPS_B, rows_s, GROUP_W_B))
    y_s, vn_s = _mixer(xs, attn_s, wb, wmix_s, bsb_s, lnvg, lnvb, wout, lng, lnb,
                       chunk=rows_s, vn_last_only=False)

    def heads_last(a, n, t):
        a = a.reshape(a.shape[0], N_KV, HEAD_DIM, a.shape[2]).transpose(0, 3, 1, 2)
        return a.reshape(1, n, t, N_KV, HEAD_DIM)

    return (y_p.reshape(BATCH, SEQ, D_MODEL),
            y_s.reshape(DEC_BATCH, DEC_SEQ, D_MODEL),
            heads_last(kt_p, BATCH, SEQ), heads_last(vt_p, BATCH, SEQ),
            kit_p.transpose(0, 2, 1)[None],
            vn_p.reshape(1, BATCH, CHUNK, W_B),
            heads_last(kt_s, DEC_BATCH, DEC_SEQ), heads_last(vt_s, DEC_BATCH, DEC_SEQ),
            kit_s[0].T.reshape(1, DEC_BATCH, DEC_SEQ, IDX_DIM),
            vn_s.reshape(1, DEC_BATCH, DEC_SEQ, W_B))
```

```python
import functools

import jax
import jax.numpy as jnp
import numpy as np
from jax import lax
from jax.experimental import pallas as pl
from jax.experimental.pallas import tpu as pltpu

F32 = jnp.float32
BF16 = jnp.bfloat16

D_MODEL = 1024
SEQ = 2048
BATCH = 8
DEC_BATCH = 128
DEC_SEQ = 4
PAST_LEN = 2048
PAGE_SIZE = 128
N_PAGES = PAST_LEN // PAGE_SIZE
W_A = 512
W_B = 512
HEAD_DIM = 64
N_HEADS = 8
N_KV = 2
N_IDX_HEADS = 8
IDX_DIM = 64
TOPK = 256
CHUNK = 128
N_GROUPS_B = 4
GROUP_W_B = 128
ROPE_THETA = 10000.0
LN_EPS = 1e-5
ALPHA = 2.0 ** 0.25
NEG = -1e30
IDX_SCALE = float((N_IDX_HEADS * IDX_DIM) ** -0.5)

LANES = 128
KEY_TILE = 128
KEY_STEP = 256
ROW_TILE = 512
WT_Q, WT_QI, WT_V, WT_K, WT_KI, WT_WI, WT_ROWS = 0, 512, 1024, 1152, 1280, 1344, 1360
BISECT_ITERS = 20
FOLD_CHAINS = 8
ONES_ROWS = 16
Q_SCALE = float(HEAD_DIM ** -0.5 * 1.4426950408889634)
VMEM_LIMIT = 48 * 1024 * 1024

SAMPLE_GROUP = 4
SAMPLE_IDX_GROUP = 8
SAMPLE_KEYS = PAST_LEN + KEY_TILE
SAMPLE_Q = 8


def _proj_kernel(x_ref, wstd_ref, wt_ref, cosl_ref, sinl_ref, cost_ref, sint_ref,
                 kt_ref, vt32_ref, kit_ref, kbf_ref, kiw_ref, qt_ref, qit_ref, vt_ref, wit_ref):
    tm = x_ref.shape[0]
    xb = x_ref[...].astype(BF16)
    zs = jnp.dot(xb, wstd_ref[...], preferred_element_type=F32)
    cosl = cosl_ref[...]
    sinl = sinl_ref[...]
    lane = lax.broadcasted_iota(jnp.int32, (tm, LANES), 1)
    first_half = (lane % HEAD_DIM) < (HEAD_DIM // 2)

    def rope_lanes(z):
        partner = jnp.where(first_half, pltpu.roll(z, 96, 1), pltpu.roll(z, 32, 1))
        return z * cosl + partner * sinl

    kbf_ref[...] = rope_lanes(zs[:, 0:128]).astype(BF16)
    kiw_ref[...] = rope_lanes(zs[:, 128:256]).astype(BF16)

    zt = lax.dot_general(wt_ref[...], xb, (((1,), (1,)), ((), ())),
                         preferred_element_type=F32)
    cost = cost_ref[...]
    sint = sint_ref[...]
    half = HEAD_DIM // 2

    def rope_rows(base, n_heads):
        pieces = []
        for h in range(n_heads):
            x1 = zt[base + h * HEAD_DIM: base + h * HEAD_DIM + half]
            x2 = zt[base + h * HEAD_DIM + half: base + (h + 1) * HEAD_DIM]
            pieces.append(x1 * cost - x2 * sint)
            pieces.append(x2 * cost + x1 * sint)
        return jnp.concatenate(pieces, axis=0)

    qt_ref[...] = rope_rows(WT_Q, N_HEADS).astype(BF16)
    qit_ref[...] = rope_rows(WT_QI, N_IDX_HEADS).astype(BF16)
    v_t = zt[WT_V:WT_V + 128]
    vt32_ref[0] = v_t
    vtb = v_t.astype(BF16)
    for c in range(tm // KEY_STEP):
        vt_ref[c] = vtb[:, c * KEY_STEP:(c + 1) * KEY_STEP]
    kt_ref[0] = rope_rows(WT_K, N_KV)
    kit_ref[0] = rope_rows(WT_KI, 1)
    wit_ref[...] = zt[WT_WI:WT_WI + N_IDX_HEADS] * IDX_SCALE


def _project(x2d, wstd, wt, cosl, sinl, cost, sint, seq_len):
    rows = x2d.shape[0]
    tm = ROW_TILE
    grid = (rows // tm,)
    n_pos_tiles = seq_len // tm
    row_map = lambda r: (r, 0)
    col_map = lambda r: (0, r)
    const = lambda r: (0, 0)
    seq_map = lambda r: (r // n_pos_tiles, 0, r % n_pos_tiles)
    out_shape = (
        jax.ShapeDtypeStruct((rows // seq_len, 128, seq_len), F32),
        jax.ShapeDtypeStruct((rows // seq_len, 128, seq_len), F32),
        jax.ShapeDtypeStruct((rows // seq_len, IDX_DIM, seq_len), F32),
        jax.ShapeDtypeStruct((rows, 128), BF16),
        jax.ShapeDtypeStruct((rows, 128), BF16),
        jax.ShapeDtypeStruct((512, rows), BF16),
        jax.ShapeDtypeStruct((512, rows), BF16),
        jax.ShapeDtypeStruct((rows // KEY_STEP, 128, KEY_STEP), BF16),
        jax.ShapeDtypeStruct((8, rows), F32),
    )
    out_specs = (
        pl.BlockSpec((1, 128, tm), seq_map),
        pl.BlockSpec((1, 128, tm), seq_map),
        pl.BlockSpec((1, IDX_DIM, tm), seq_map),
        pl.BlockSpec((tm, 128), row_map),
        pl.BlockSpec((tm, 128), row_map),
        pl.BlockSpec((512, tm), col_map),
        pl.BlockSpec((512, tm), col_map),
        pl.BlockSpec((tm // KEY_STEP, 128, KEY_STEP), lambda r: (r, 0, 0)),
        pl.BlockSpec((8, tm), col_map),
    )
    in_specs = [
        pl.BlockSpec((tm, D_MODEL), row_map),
        pl.BlockSpec(wstd.shape, const),
        pl.BlockSpec(wt.shape, const),
        pl.BlockSpec((tm, LANES), lambda r: (r % n_pos_tiles, 0)),
        pl.BlockSpec((tm, LANES), lambda r: (r % n_pos_tiles, 0)),
        pl.BlockSpec((HEAD_DIM // 2, tm), lambda r: (0, r % n_pos_tiles)),
        pl.BlockSpec((HEAD_DIM // 2, tm), lambda r: (0, r % n_pos_tiles)),
    ]
    return pl.pallas_call(
        _proj_kernel, grid=grid, in_specs=in_specs, out_specs=out_specs, out_shape=out_shape,
        compiler_params=pltpu.CompilerParams(dimension_semantics=("arbitrary",),
                                             vmem_limit_bytes=VMEM_LIMIT),
        name="proj_attn_side",
    )(x2d, wstd, wt, cosl, sinl, cost, sint)


def _rope_tables(pos):
    half = HEAD_DIM // 2
    inv = ROPE_THETA ** (-np.arange(half, dtype=np.float64) / half)
    ang = pos.astype(np.float64)[:, None] * inv[None, :]
    cos = np.cos(ang)
    sin = np.sin(ang)
    cosl = np.tile(cos, (1, LANES // half))
    sinl = np.tile(np.concatenate([-sin, sin], axis=1), (1, LANES // HEAD_DIM))
    return tuple(jnp.asarray(t, dtype=F32) for t in (cosl, sinl, cos.T, sin.T))


def _select_threshold(count_ge, count_gt, kth_tie_index, min_ge, max_lt, lo0, hi0, n_allowed, n_keys):
    k = float(TOPK)

    def step(carry):
        lo, hi, clo = carry
        mid = 0.5 * lo + 0.5 * hi
        c = count_ge(mid)
        take = c >= k
        return jnp.where(take, mid, lo), jnp.where(take, hi, mid), jnp.where(take, c, clo)

    lo, hi, clo = lax.fori_loop(0, BISECT_ITERS, lambda _, c: step(c), (lo0, hi0, n_allowed))
    big = jnp.full_like(lo, float(4 * n_keys))

    def unresolved(clo_):
        return clo_ > k

    def any_true(mask):
        return jnp.max(jnp.where(mask, 1.0, 0.0)) > 0.5

    def exact(args):
        lo, hi, clo = args

        def cond(c):
            _, _, clo, vlo, vhi, it = c
            open_ = jnp.logical_and(unresolved(clo), vlo < vhi)
            return jnp.logical_and(any_true(open_), it < 2 * n_keys)

        def body(c):
            lo, hi, clo, vlo, vhi, it = c
            mid = 0.5 * vlo + 0.5 * vhi
            mid = jnp.where(mid > vlo, mid, vhi)
            cnt = count_ge(mid)
            take = cnt >= k
            lo, hi, clo = jnp.where(take, mid, lo), jnp.where(take, hi, mid), jnp.where(take, cnt, clo)
            return lo, hi, clo, min_ge(lo), max_lt(hi), it + 1

        lo, hi, clo, vlo, _, _ = lax.while_loop(
            cond, body, (lo, hi, clo, min_ge(lo), max_lt(hi), jnp.int32(0)))
        lo = vlo
        need = k - count_gt(lo)
        return lo, jnp.where(clo > k, kth_tie_index(lo, need), big)

    def fast(args):
        return args[0], big

    return lax.cond(any_true(unresolved(clo)), exact, fast, (lo, hi, clo))


def _fold_rows(x, op):
    groups = [x[r:r + 8] for r in range(0, x.shape[0], 8)]
    parts = groups[:FOLD_CHAINS]
    for n, grp in enumerate(groups[FOLD_CHAINS:]):
        parts[n % FOLD_CHAINS] = op(parts[n % FOLD_CHAINS], grp)
    while len(parts) > 1:
        nxt = [op(parts[a], parts[a + 1]) for a in range(0, len(parts) - 1, 2)]
        if len(parts) % 2:
            nxt.append(parts[-1])
        parts = nxt
    return parts[0]


def _topk_bias_in_place(s_ref, n_keys, n_allowed, lo0=None, hi0=None):
    lanes = s_ref.shape[1]
    inf = jnp.float32(jnp.inf)
    chunk = 8 * FOLD_CHAINS
    kidx0 = lax.broadcasted_iota(jnp.int32, (chunk, lanes), 0).astype(F32)

    def key_reduce(fn, op, finish):
        part = None
        for r in range(0, n_keys, chunk):
            v = fn(s_ref[r:r + chunk, :], r)
            part = v if part is None else op(part, v)
        return finish(_fold_rows(part, op), axis=0, keepdims=True)

    count_ge = lambda thr: key_reduce(lambda t, r: jnp.where(t >= thr, 1.0, 0.0), jnp.add, jnp.sum)
    count_gt = lambda thr: key_reduce(lambda t, r: jnp.where(t > thr, 1.0, 0.0), jnp.add, jnp.sum)
    tri = (lax.broadcasted_iota(jnp.int32, (chunk, chunk), 0)
           >= lax.broadcasted_iota(jnp.int32, (chunk, chunk), 1)).astype(BF16)

    def kth_tie_index(thr, need):
        seen = jnp.zeros((1, lanes), F32)
        best = jnp.full((1, lanes), -1.0, F32)
        for r in range(0, n_keys, chunk):
            tie = jnp.where(s_ref[r:r + chunk, :] == thr, 1.0, 0.0)
            upto = jnp.dot(tri, tie.astype(BF16), preferred_element_type=F32) + seen
            hit = jnp.where(tie > 0.5, jnp.where(upto == need, kidx0 + r, -1.0), -1.0)
            best = jnp.maximum(best, jnp.max(_fold_rows(hit, jnp.maximum), axis=0, keepdims=True))
            seen = upto[chunk - 1:chunk, :]
        return best
    min_ge = lambda thr: key_reduce(lambda t, r: jnp.where(t >= thr, t, inf), jnp.minimum, jnp.min)
    max_lt = lambda thr: key_reduce(lambda t, r: jnp.where(t < thr, t, -inf), jnp.maximum, jnp.max)
    if lo0 is None:
        lo0 = key_reduce(lambda t, r: jnp.where(t == -inf, inf, t), jnp.minimum, jnp.min)
        mx1 = key_reduce(lambda t, r: t, jnp.maximum, jnp.max)
        hi0 = mx1 + (jnp.abs(mx1) * (2.0 ** -20) + 1e-30)
    lo, jmax = _select_threshold(count_ge, count_gt, kth_tie_index, min_ge, max_lt, lo0, hi0,
                                 n_allowed, n_keys)
    for r in range(0, n_keys, chunk):
        t = s_ref[r:r + chunk, :]
        tie = jnp.where(t == lo, jnp.where(kidx0 <= jmax - r, 0.0, NEG), NEG)
        s_ref[r:r + chunk, :] = jnp.where(t > lo, 0.0, tie)


def _attn_prompt_kernel(qt_ref, qit_ref, wit_ref, kbf_ref, kiw_ref, vt_ref, o_ref,
                        s_ref, sc_ref, ri_ref, rq_ref):
    i = pl.program_id(1)
    w = wit_ref[...]
    zeros_half = jnp.zeros((HEAD_DIM, LANES), BF16)

    for hp in range(4):
        a = qit_ref[(2 * hp) * 64:(2 * hp + 1) * 64, :]
        b = qit_ref[(2 * hp + 1) * 64:(2 * hp + 2) * 64, :]
        ri_ref[hp] = jnp.concatenate(
            [jnp.concatenate([a, zeros_half], axis=0), jnp.concatenate([b, zeros_half], axis=0)], axis=1)
        g = hp // 2
        a = qt_ref[(2 * hp) * 64:(2 * hp + 1) * 64, :]
        b = qt_ref[(2 * hp + 1) * 64:(2 * hp + 2) * 64, :]
        if g == 0:
            ab = jnp.concatenate([jnp.concatenate([a, zeros_half], axis=0),
                                  jnp.concatenate([b, zeros_half], axis=0)], axis=1)
        else:
            ab = jnp.concatenate([jnp.concatenate([zeros_half, a], axis=0),
                                  jnp.concatenate([zeros_half, b], axis=0)], axis=1)
        rq_ref[hp] = ab

    inf = jnp.float32(jnp.inf)
    n_steps = (i * LANES + LANES + KEY_STEP - 1) // KEY_STEP

    def process(ns):
        n_keys = ns * KEY_STEP
        keys = slice(0, n_keys)
        row = lax.broadcasted_iota(jnp.int32, (n_keys, LANES), 0)
        col = lax.broadcasted_iota(jnp.int32, (n_keys, LANES), 1)
        qpos = col + i * LANES
        allowed = row <= qpos

        kt = kiw_ref[keys, :]
        acc = jnp.zeros((n_keys, LANES), F32)
        for hp in range(4):
            s2 = jnp.dot(kt, ri_ref[hp], preferred_element_type=F32)
            acc = acc + jnp.maximum(s2[:, :LANES], 0.0) * w[2 * hp:2 * hp + 1, :]
            acc = acc + jnp.maximum(s2[:, LANES:], 0.0) * w[2 * hp + 1:2 * hp + 2, :]
        s_ref[keys, :] = jnp.where(allowed, acc, -inf)
        lo0 = jnp.min(_fold_rows(jnp.where(allowed, acc, inf), jnp.minimum), axis=0, keepdims=True)
        mx1 = jnp.max(_fold_rows(jnp.where(allowed, acc, -inf), jnp.maximum), axis=0, keepdims=True)
        hi0 = mx1 + (jnp.abs(mx1) * (2.0 ** -20) + 1e-30)
        n_allowed = (qpos[0:1, :] + 1).astype(F32)

        _topk_bias_in_place(s_ref, n_keys, n_allowed, lo0, hi0)

        kt = kbf_ref[keys, :]
        b = s_ref[keys, :]
        b2 = jnp.concatenate([b, b], axis=1)
        ms = []
        for hp in range(4):
            s2 = jnp.dot(kt, rq_ref[hp], preferred_element_type=F32) + b2
            sc_ref[hp, keys, :] = s2
            ms.append(jnp.max(_fold_rows(s2, jnp.maximum), axis=0, keepdims=True))
        vt = jnp.concatenate([vt_ref[j] for j in range(ns)], axis=1)
        ones_rows = jnp.ones((ONES_ROWS, n_keys), BF16)
        lhs = [jnp.concatenate([vt[g * 64:(g + 1) * 64, :], ones_rows], axis=0) for g in range(N_KV)]
        pieces = []
        for hp in range(4):
            p = jnp.exp2((sc_ref[hp, keys, :] - ms[hp]).astype(BF16))
            o = jnp.dot(lhs[hp // 2], p, preferred_element_type=F32)
            o2 = o[0:HEAD_DIM, :] / o[HEAD_DIM:HEAD_DIM + 1, :]
            pieces.append(o2[:, :LANES])
            pieces.append(o2[:, LANES:])
        o_ref[...] = jnp.concatenate(pieces, axis=0).T

    for ns in range(1, SEQ // KEY_STEP + 1):
        pl.when(n_steps == ns)(functools.partial(process, ns))


def _attn_prompt(qt, qit, wit, kbf, kiw, vt3):
    n_blocks = SEQ // LANES
    grid = (BATCH, n_blocks)
    qmap = lambda n, i: (0, n * n_blocks + i)
    in_specs = [
        pl.BlockSpec((512, LANES), qmap),
        pl.BlockSpec((512, LANES), qmap),
        pl.BlockSpec((8, LANES), qmap),
        pl.BlockSpec((SEQ, 128), lambda n, i: (n, 0)),
        pl.BlockSpec((SEQ, 128), lambda n, i: (n, 0)),
        pl.BlockSpec((SEQ // KEY_STEP, 128, KEY_STEP), lambda n, i: (n, 0, 0)),
    ]
    return pl.pallas_call(
        _attn_prompt_kernel, grid=grid, in_specs=in_specs,
        out_specs=pl.BlockSpec((LANES, W_A), lambda n, i: (n * n_blocks + i, 0)),
        out_shape=jax.ShapeDtypeStruct((BATCH * SEQ, W_A), F32),
        scratch_shapes=[pltpu.VMEM((SEQ, LANES), F32), pltpu.VMEM((4, SEQ, 2 * LANES), F32),
                        pltpu.VMEM((4, 128, 2 * LANES), BF16), pltpu.VMEM((4, 128, 2 * LANES), BF16)],
        compiler_params=pltpu.CompilerParams(dimension_semantics=("arbitrary", "arbitrary"),
                                             vmem_limit_bytes=VMEM_LIMIT),
        name="attn_prompt",
    )(qt, qit, wit, kbf, kiw, vt3)


def _page_copies(pt_ref, group, slot, gsz, streams):
    out = []
    for g in range(gsz):
        seq = group * gsz + g
        for p in range(N_PAGES):
            page = pt_ref[seq, p]
            keys = pl.ds(p * PAGE_SIZE, PAGE_SIZE)
            for hbm, buf, sem in streams:
                out.append(pltpu.make_async_copy(hbm.at[page], buf.at[slot, g, :, keys], sem.at[slot]))
    return out


def _paged_prefetch(pt_ref, gsz, streams):
    step = pl.program_id(0)
    n_steps = pl.num_programs(0)
    slot = step % 2

    @pl.when(step == 0)
    def _():
        tail = pl.ds(PAST_LEN, KEY_TILE)
        for _, buf, _ in streams:
            for s_ in range(2):
                for g in range(gsz):
                    buf[s_, g, :, tail] = jnp.zeros((buf.shape[2], KEY_TILE), F32)
        for c in _page_copies(pt_ref, 0, 0, gsz, streams):
            c.start()

    @pl.when(step + 1 < n_steps)
    def _():
        for c in _page_copies(pt_ref, step + 1, 1 - slot, gsz, streams):
            c.start()

    for c in _page_copies(pt_ref, step, slot, gsz, streams):
        c.wait()
    return slot


def _sample_index_kernel(pt_ref, qi_ref, w_ref, kin_ref, cki_hbm, s_out_ref, kibuf, sem):
    gsz = SAMPLE_IDX_GROUP
    slot = _paged_prefetch(pt_ref, gsz, [(cki_hbm, kibuf, sem)])
    rows8 = lax.broadcasted_iota(jnp.int32, (SAMPLE_Q, SAMPLE_KEYS), 0)
    keyi = lax.broadcasted_iota(jnp.int32, (SAMPLE_Q, SAMPLE_KEYS), 1)
    allowed = keyi <= (PAST_LEN + rows8)
    new_keys = pl.ds(PAST_LEN, DEC_SEQ)
    scores = []
    for g in range(gsz):
        kibuf[slot, g, :, new_keys] = kin_ref[g]
        kib = kibuf[slot, g].astype(BF16)
        s = jnp.dot(qi_ref[g], kib, preferred_element_type=F32)
        sw = jnp.maximum(s, 0.0) * w_ref[g][:, 0:1]
        acc = jnp.sum(sw.reshape(N_IDX_HEADS, SAMPLE_Q, SAMPLE_KEYS), axis=0)
        scores.append(jnp.where(allowed, acc, -jnp.inf))
    for a in range(gsz // 2):
        s_out_ref[a] = jnp.where(rows8 < DEC_SEQ, scores[2 * a],
                                 pltpu.roll(scores[2 * a + 1], DEC_SEQ, 0))


def _sample_select_kernel(s_ref, b_ref, st_ref):
    rows = s_ref.shape[0]
    st_ref[...] = s_ref[...].T
    t4 = lax.broadcasted_iota(jnp.int32, (1, rows), 1) % DEC_SEQ
    _topk_bias_in_place(st_ref, SAMPLE_KEYS, (PAST_LEN + 1 + t4).astype(F32))
    b_ref[...] = st_ref[...].T


def _sample_attend_kernel(pt_ref, q_ref, b_ref, kn_ref, vn_ref, ck_hbm, cv_hbm, o_ref,
                          kbuf, vbuf, ksem, vsem):
    gsz = SAMPLE_GROUP
    slot = _paged_prefetch(pt_ref, gsz, [(ck_hbm, kbuf, ksem), (cv_hbm, vbuf, vsem)])
    new_keys = pl.ds(PAST_LEN, DEC_SEQ)
    lane = lax.broadcasted_iota(jnp.int32, (SAMPLE_Q, LANES), 1)
    for g in range(gsz):
        kbuf[slot, g, :, new_keys] = kn_ref[g]
        vbuf[slot, g, :, new_keys] = vn_ref[g]
        kb = kbuf[slot, g].astype(BF16)
        vb = vbuf[slot, g].astype(BF16)
        b = b_ref[g // 2]
        if g % 2:
            b = pltpu.roll(b, DEC_SEQ, 0)
        s = jnp.dot(q_ref[g], kb, preferred_element_type=F32)
        s = s + jnp.concatenate([b] * N_HEADS, axis=0)
        m = jnp.max(s, axis=1, keepdims=True)
        p = jnp.exp2(s - m)
        l = jnp.sum(p, axis=1, keepdims=True)
        o = lax.dot_general(p.astype(BF16), vb, (((1,), (1,)), ((), ())),
                            preferred_element_type=F32) / l
        o_r = pltpu.roll(o, HEAD_DIM, 1)
        tiles = []
        for c in range(4):
            ha, hb = 2 * c, 2 * c + 1
            src_a = o if ha // 4 == 0 else o_r
            src_b = o if hb // 4 == 1 else o_r
            tiles.append(jnp.where(lane < HEAD_DIM, src_a[ha * SAMPLE_Q:(ha + 1) * SAMPLE_Q, :],
                                   src_b[hb * SAMPLE_Q:(hb + 1) * SAMPLE_Q, :]))
        o_ref[g] = jnp.concatenate(tiles, axis=1)


def _attn_sample(page_table, qi_s, q_s, w_s, ki_new, k_new, v_new, ck, cv, cki):
    params = pltpu.CompilerParams(dimension_semantics=("arbitrary",), vmem_limit_bytes=VMEM_LIMIT)
    blk = lambda n, shape: pl.BlockSpec((n,) + shape, lambda s, pt: (s, 0, 0))
    hbm = pl.BlockSpec(memory_space=pl.ANY)
    n_pairs = DEC_BATCH // 2

    gi = SAMPLE_IDX_GROUP
    scores = pl.pallas_call(
        _sample_index_kernel,
        grid_spec=pltpu.PrefetchScalarGridSpec(
            num_scalar_prefetch=1, grid=(DEC_BATCH // gi,),
            in_specs=[blk(gi, (64, IDX_DIM)), blk(gi, (64, LANES)), blk(gi, (IDX_DIM, DEC_SEQ)), hbm],
            out_specs=blk(gi // 2, (SAMPLE_Q, SAMPLE_KEYS)),
            scratch_shapes=[pltpu.VMEM((2, gi, IDX_DIM, SAMPLE_KEYS), F32),
                            pltpu.SemaphoreType.DMA((2,))]),
        out_shape=jax.ShapeDtypeStruct((n_pairs, SAMPLE_Q, SAMPLE_KEYS), F32),
        compiler_params=params, name="sample_index",
    )(page_table, qi_s, w_s, ki_new, cki)

    rows = n_pairs * SAMPLE_Q
    bias = pl.pallas_call(
        _sample_select_kernel, grid=(1,),
        in_specs=[pl.BlockSpec((rows, SAMPLE_KEYS), lambda s: (0, 0))],
        out_specs=pl.BlockSpec((rows, SAMPLE_KEYS), lambda s: (0, 0)),
        out_shape=jax.ShapeDtypeStruct((rows, SAMPLE_KEYS), F32),
        scratch_shapes=[pltpu.VMEM((SAMPLE_KEYS, rows), F32)],
        compiler_params=params, name="sample_select",
    )(scores.reshape(rows, SAMPLE_KEYS)).reshape(n_pairs, SAMPLE_Q, SAMPLE_KEYS)

    ga = SAMPLE_GROUP
    return pl.pallas_call(
        _sample_attend_kernel,
        grid_spec=pltpu.PrefetchScalarGridSpec(
            num_scalar_prefetch=1, grid=(DEC_BATCH // ga,),
            in_specs=[blk(ga, (64, 128)), blk(ga // 2, (SAMPLE_Q, SAMPLE_KEYS)),
                      blk(ga, (128, DEC_SEQ)), blk(ga, (128, DEC_SEQ)), hbm, hbm],
            out_specs=blk(ga, (SAMPLE_Q, W_A)),
            scratch_shapes=[pltpu.VMEM((2, ga, 128, SAMPLE_KEYS), F32),
                            pltpu.VMEM((2, ga, 128, SAMPLE_KEYS), F32),
                            pltpu.SemaphoreType.DMA((2,)), pltpu.SemaphoreType.DMA((2,))]),
        out_shape=jax.ShapeDtypeStruct((DEC_BATCH, SAMPLE_Q, W_A), F32),
        compiler_params=params, name="sample_attend",
    )(page_table, q_s, bias, k_new, v_new, ck, cv)


def _layer_norm(x, g, b):
    mu = jnp.mean(x, axis=-1, keepdims=True)
    xc = x - mu
    var = jnp.mean(xc * xc, axis=-1, keepdims=True)
    return xc * lax.rsqrt(var + LN_EPS) * g + b


def _silu(x):
    return x / (1.0 + jnp.exp(-x))


def _mixer_kernel(x_ref, a_ref, wb_ref, wmix_ref, bsb_ref, lnvg_ref, lnvb_ref, wout_ref,
                  lng_ref, lnb_ref, y_ref, vn_ref, *, chunk, vn_last_only, tiles_per_seq):
    tm = x_ref.shape[0]
    x = x_ref[...]
    zb = jnp.dot(x.astype(BF16), wb_ref[...], preferred_element_type=F32)
    ga = zb[:, 0:512]
    u = zb[:, 512:1024]
    vb = zb[:, 1024:1536]
    gb = zb[:, 1536:2048]
    vn = _layer_norm(vb, lnvg_ref[...], lnvb_ref[...])
    if vn_last_only:
        @pl.when(pl.program_id(0) % tiles_per_seq == tiles_per_seq - 1)
        def _():
            vn_ref[0] = vn[tm - CHUNK:, :]
    else:
        vn_ref[...] = vn
    vnb = vn.astype(BF16)
    rows = []
    for c in range(tm // chunk):
        cols = []
        for g in range(N_GROUPS_B):
            blk = vnb[c * chunk:(c + 1) * chunk, g * GROUP_W_B:(g + 1) * GROUP_W_B]
            cols.append(jnp.dot(wmix_ref[g], blk, preferred_element_type=F32) + bsb_ref[g])
        rows.append(jnp.concatenate(cols, axis=1))
    mixed = jnp.concatenate(rows, axis=0) if len(rows) > 1 else rows[0]
    a_out = a_ref[...] * _silu(ga)
    b_out = u * mixed * _silu(gb)
    cat = jnp.concatenate([a_out, b_out], axis=1).astype(BF16)
    out = jnp.dot(cat, wout_ref[...], preferred_element_type=F32)
    y_ref[...] = _layer_norm(ALPHA * x + out, lng_ref[...], lnb_ref[...])


def _mixer(x2d, attn, wb, wmix, bsb, lnvg, lnvb, wout, lng, lnb, *, chunk, vn_last_only):
    rows = x2d.shape[0]
    tm = ROW_TILE
    tiles_per_seq = SEQ // tm
    grid = (rows // tm,)
    row_map = lambda r: (r, 0)
    const2 = lambda r: (0, 0)
    const3 = lambda r: (0, 0, 0)
    if vn_last_only:
        vn_shape = jax.ShapeDtypeStruct((rows // SEQ, CHUNK, W_B), F32)
        vn_spec = pl.BlockSpec((1, CHUNK, W_B), lambda r: (r // tiles_per_seq, 0, 0))
    else:
        vn_shape = jax.ShapeDtypeStruct((rows, W_B), F32)
        vn_spec = pl.BlockSpec((tm, W_B), row_map)
    kern = functools.partial(_mixer_kernel, chunk=chunk, vn_last_only=vn_last_only,
                             tiles_per_seq=tiles_per_seq)
    return pl.pallas_call(
        kern, grid=grid,
        in_specs=[pl.BlockSpec((tm, D_MODEL), row_map), pl.BlockSpec((tm, W_A), row_map),
                  pl.BlockSpec(wb.shape, const2), pl.BlockSpec(wmix.shape, const3),
                  pl.BlockSpec(bsb.shape, const3), pl.BlockSpec((1, W_B), const2),
                  pl.BlockSpec((1, W_B), const2), pl.BlockSpec(wout.shape, const2),
                  pl.BlockSpec((1, D_MODEL), const2), pl.BlockSpec((1, D_MODEL), const2)],
        out_specs=(pl.BlockSpec((tm, D_MODEL), row_map), vn_spec),
        out_shape=(jax.ShapeDtypeStruct((rows, D_MODEL), F32), vn_shape),
        compiler_params=pltpu.CompilerParams(dimension_semantics=("arbitrary",),
                                             vmem_limit_bytes=VMEM_LIMIT),
        name="mixer_chunk%d" % chunk,
    )(x2d, attn, wb, wmix, bsb, lnvg, lnvb, wout, lng, lnb)


def kernel(x_prompt, x_sample, cache_k, cache_v, cache_k_idx, page_table, w_in, w_s, b_s,
           ln_v_g, ln_v_b, w_out, ln_g, ln_b):
    w = w_in[0]
    wq, wk, wv = w[:, 0:512], w[:, 512:640], w[:, 640:768]
    wga, wqi, wki, wwi = w[:, 768:1280], w[:, 1280:1792], w[:, 1792:1856], w[:, 1856:1864]
    wu, wvb, wgb = w[:, 1864:2376], w[:, 2376:2888], w[:, 2888:3400]
    wstd = jnp.concatenate([wk, wki, jnp.zeros((D_MODEL, 64), F32)], axis=1).astype(BF16)
    wt = jnp.concatenate([wq.T * Q_SCALE, wqi.T, wv.T, wk.T, wki.T, wwi.T,
                          jnp.zeros((WT_ROWS - WT_WI - N_IDX_HEADS, D_MODEL), F32)], axis=0).astype(BF16)
    wb = jnp.concatenate([wga, wu, wvb, wgb], axis=1).astype(BF16)
    wout = w_out[0].astype(BF16)
    lnvg, lnvb = ln_v_g[0][None, :], ln_v_b[0][None, :]
    lng, lnb = ln_g[0][None, :], ln_b[0][None, :]

    xp = x_prompt.reshape(BATCH * SEQ, D_MODEL)
    tabs_p = _rope_tables(np.arange(SEQ))
    kt_p, vt_p, kit_p, kbf, kiw, qt, qit, vt3, wit = _project(xp, wstd, wt, *tabs_p, SEQ)
    attn_p = _attn_prompt(qt, qit, wit, kbf, kiw, vt3)
    tril = np.tril(np.ones((CHUNK, CHUNK), np.float32))
    wmix_p = (w_s[0] * tril[None]).astype(BF16)
    bsb_p = jnp.broadcast_to(b_s[0][:, :, None], (N_GROUPS_B, CHUNK, GROUP_W_B))
    y_p, vn_p = _mixer(xp, attn_p, wb, wmix_p, bsb_p, lnvg, lnvb, wout, lng, lnb,
                       chunk=CHUNK, vn_last_only=True)

    rows_s = DEC_BATCH * DEC_SEQ
    xs = x_sample.reshape(rows_s, D_MODEL)
    tabs_s = _rope_tables(PAST_LEN + (np.arange(rows_s) % DEC_SEQ))
    kt_s, vt_s, kit_s, _, _, qt_s, qit_s, _, wit_s = _project(xs, wstd, wt, *tabs_s, rows_s)

    def to_rows(a_t):
        a = a_t.reshape(N_HEADS, HEAD_DIM, DEC_BATCH, DEC_SEQ).transpose(2, 0, 3, 1)
        return jnp.pad(a, ((0, 0), (0, 0), (0, SAMPLE_Q - DEC_SEQ), (0, 0)))

    qi_rows = to_rows(qit_s).reshape(DEC_BATCH, N_IDX_HEADS * SAMPLE_Q, IDX_DIM)
    q5 = to_rows(qt_s).reshape(DEC_BATCH, N_KV, N_HEADS // N_KV, SAMPLE_Q, HEAD_DIM)
    zq = jnp.zeros_like(q5[:, 0])
    q_rows = jnp.stack([jnp.concatenate([q5[:, 0], zq], axis=-1),
                        jnp.concatenate([zq, q5[:, 1]], axis=-1)], axis=1)
    q_rows = q_rows.reshape(DEC_BATCH, N_HEADS * SAMPLE_Q, 2 * HEAD_DIM)
    w_rows = jnp.pad(wit_s.reshape(N_IDX_HEADS, DEC_BATCH, DEC_SEQ).transpose(1, 0, 2),
                     ((0, 0), (0, 0), (0, SAMPLE_Q - DEC_SEQ)))
    w_rows = jnp.broadcast_to(w_rows.reshape(DEC_BATCH, N_IDX_HEADS * SAMPLE_Q, 1),
                              (DEC_BATCH, N_IDX_HEADS * SAMPLE_Q, LANES))
    new_cols = lambda a: a.reshape(a.shape[0], DEC_BATCH, DEC_SEQ).transpose(1, 0, 2)
    n_pool = cache_k.shape[1]
    ck_t = cache_k[0].transpose(0, 2, 3, 1).reshape(n_pool, N_KV * HEAD_DIM, PAGE_SIZE)
    cv_t = cache_v[0].transpose(0, 2, 3, 1).reshape(n_pool, N_KV * HEAD_DIM, PAGE_SIZE)
    cki_t = cache_k_idx[0].transpose(0, 2, 1)
    attn_s8 = _attn_sample(page_table, qi_rows, q_rows, w_rows, new_cols(kit_s[0]),
                           new_cols(kt_s[0]), new_cols(vt_s[0]), ck_t, cv_t, cki_t)
    attn_s = attn_s8[:, :DEC_SEQ, :].reshape(rows_s, W_A)
    ri = np.arange(rows_s)
    same_seq = (ri[:, None] // DEC_SEQ) == (ri[None, :] // DEC_SEQ)
    wmix_s = jnp.zeros((N_GROUPS_B, rows_s, rows_s), F32)
    for t in range(DEC_SEQ):
        for s in range(t + 1):
            hit = same_seq & (ri[:, None] % DEC_SEQ == t) & (ri[None, :] % DEC_SEQ == s)
            wmix_s = wmix_s + jnp.where(hit[None], w_s[0, :, t, s][:, None, None], 0.0)
    wmix_s = wmix_s.astype(BF16)
    bsb_s = jnp.broadcast_to(jnp.tile(b_s[0][:, :DEC_SEQ], (1, DEC_BATCH))[:, :, None],
                             (N_GROUPS_B, rows_s, GROUP_W_B))
    y_s, vn_s = _mixer(xs, attn_s, wb, wmix_s, bsb_s, lnvg, lnvb, wout, lng, lnb,
                       chunk=rows_s, vn_last_only=False)

    def heads_last(a, n, t):
        a = a.reshape(a.shape[0], N_KV, HEAD_DIM, a.shape[2]).transpose(0, 3, 1, 2)
        return a.reshape(1, n, t, N_KV, HEAD_DIM)

    return (y_p.reshape(BATCH, SEQ, D_MODEL),
            y_s.reshape(DEC_BATCH, DEC_SEQ, D_MODEL),
            heads_last(kt_p, BATCH, SEQ), heads_last(vt_p, BATCH, SEQ),
            kit_p.transpose(0, 2, 1)[None],
            vn_p.reshape(1, BATCH, CHUNK, W_B),
            heads_last(kt_s, DEC_BATCH, DEC_SEQ), heads_last(vt_s, DEC_BATCH, DEC_SEQ),
            kit_s[0].T.reshape(1, DEC_BATCH, DEC_SEQ, IDX_DIM),
            vn_s.reshape(1, DEC_BATCH, DEC_SEQ, W_B))
```

```python
import functools

import jax
import jax.numpy as jnp
import numpy as np
from jax import lax
from jax.experimental import pallas as pl
from jax.experimental.pallas import tpu as pltpu

F32 = jnp.float32
BF16 = jnp.bfloat16

D_MODEL = 1024
SEQ = 2048
BATCH = 8
DEC_BATCH = 128
DEC_SEQ = 4
PAST_LEN = 2048
PAGE_SIZE = 128
N_PAGES = PAST_LEN // PAGE_SIZE
W_A = 512
W_B = 512
HEAD_DIM = 64
N_HEADS = 8
N_KV = 2
N_IDX_HEADS = 8
IDX_DIM = 64
TOPK = 256
CHUNK = 128
N_GROUPS_B = 4
GROUP_W_B = 128
ROPE_THETA = 10000.0
LN_EPS = 1e-5
ALPHA = 2.0 ** 0.25
NEG = -1e30
IDX_SCALE = float((N_IDX_HEADS * IDX_DIM) ** -0.5)

LANES = 128
KEY_TILE = 128
KEY_STEP = 256
ROW_TILE = 512
WT_Q, WT_QI, WT_V, WT_K, WT_KI, WT_WI, WT_ROWS = 0, 512, 1024, 1152, 1280, 1344, 1360
BISECT_ITERS = 20
FOLD_CHAINS = 8
ONES_ROWS = 16
Q_SCALE = float(HEAD_DIM ** -0.5 * 1.4426950408889634)
VMEM_LIMIT = 48 * 1024 * 1024

SAMPLE_GROUP = 4
SAMPLE_IDX_GROUP = 8
SAMPLE_KEYS = PAST_LEN + KEY_TILE
SAMPLE_Q = 8


def _proj_kernel(x_ref, wstd_ref, wt_ref, cosl_ref, sinl_ref, cost_ref, sint_ref,
                 kt_ref, vt32_ref, kit_ref, kbf_ref, kiw_ref, qt_ref, qit_ref, vt_ref, wit_ref):
    tm = x_ref.shape[0]
    xb = x_ref[...].astype(BF16)
    zs = lax.dot_general(xb, wstd_ref[...], (((1,), (1,)), ((), ())),
                         preferred_element_type=F32)
    cosl = cosl_ref[...]
    sinl = sinl_ref[...]
    lane = lax.broadcasted_iota(jnp.int32, (tm, LANES), 1)
    first_half = (lane % HEAD_DIM) < (HEAD_DIM // 2)

    def rope_lanes(z):
        partner = jnp.where(first_half, pltpu.roll(z, 96, 1), pltpu.roll(z, 32, 1))
        return z * cosl + partner * sinl

    kbf_ref[...] = rope_lanes(zs[:, 0:128]).astype(BF16)
    kiw_ref[...] = rope_lanes(zs[:, 128:256]).astype(BF16)

    zt = lax.dot_general(wt_ref[...], xb, (((1,), (1,)), ((), ())),
                         preferred_element_type=F32)
    cost = cost_ref[...]
    sint = sint_ref[...]
    half = HEAD_DIM // 2

    def rope_rows(base, n_heads):
        pieces = []
        for h in range(n_heads):
            x1 = zt[base + h * HEAD_DIM: base + h * HEAD_DIM + half]
            x2 = zt[base + h * HEAD_DIM + half: base + (h + 1) * HEAD_DIM]
            pieces.append(x1 * cost - x2 * sint)
            pieces.append(x2 * cost + x1 * sint)
        return jnp.concatenate(pieces, axis=0)

    qt_ref[...] = rope_rows(WT_Q, N_HEADS).astype(BF16)
    qit_ref[...] = rope_rows(WT_QI, N_IDX_HEADS).astype(BF16)
    v_t = zt[WT_V:WT_V + 128]
    vt32_ref[0] = v_t
    vtb = v_t.astype(BF16)
    for c in range(tm // KEY_STEP):
        vt_ref[c] = vtb[:, c * KEY_STEP:(c + 1) * KEY_STEP]
    kt_ref[0] = rope_rows(WT_K, N_KV)
    kit_ref[0] = rope_rows(WT_KI, 1)
    wit_ref[...] = zt[WT_WI:WT_WI + N_IDX_HEADS] * IDX_SCALE


def _project(x2d, wstd, wt, cosl, sinl, cost, sint, seq_len):
    rows = x2d.shape[0]
    tm = ROW_TILE
    grid = (rows // tm,)
    n_pos_tiles = seq_len // tm
    row_map = lambda r: (r, 0)
    col_map = lambda r: (0, r)
    const = lambda r: (0, 0)
    seq_map = lambda r: (r // n_pos_tiles, 0, r % n_pos_tiles)
    out_shape = (
        jax.ShapeDtypeStruct((rows // seq_len, 128, seq_len), F32),
        jax.ShapeDtypeStruct((rows // seq_len, 128, seq_len), F32),
        jax.ShapeDtypeStruct((rows // seq_len, IDX_DIM, seq_len), F32),
        jax.ShapeDtypeStruct((rows, 128), BF16),
        jax.ShapeDtypeStruct((rows, 128), BF16),
        jax.ShapeDtypeStruct((512, rows), BF16),
        jax.ShapeDtypeStruct((512, rows), BF16),
        jax.ShapeDtypeStruct((rows // KEY_STEP, 128, KEY_STEP), BF16),
        jax.ShapeDtypeStruct((8, rows), F32),
    )
    out_specs = (
        pl.BlockSpec((1, 128, tm), seq_map),
        pl.BlockSpec((1, 128, tm), seq_map),
        pl.BlockSpec((1, IDX_DIM, tm), seq_map),
        pl.BlockSpec((tm, 128), row_map),
        pl.BlockSpec((tm, 128), row_map),
        pl.BlockSpec((512, tm), col_map),
        pl.BlockSpec((512, tm), col_map),
        pl.BlockSpec((tm // KEY_STEP, 128, KEY_STEP), lambda r: (r, 0, 0)),
        pl.BlockSpec((8, tm), col_map),
    )
    in_specs = [
        pl.BlockSpec((tm, D_MODEL), row_map),
        pl.BlockSpec(wstd.shape, const),
        pl.BlockSpec(wt.shape, const),
        pl.BlockSpec((tm, LANES), lambda r: (r % n_pos_tiles, 0)),
        pl.BlockSpec((tm, LANES), lambda r: (r % n_pos_tiles, 0)),
        pl.BlockSpec((HEAD_DIM // 2, tm), lambda r: (0, r % n_pos_tiles)),
        pl.BlockSpec((HEAD_DIM // 2, tm), lambda r: (0, r % n_pos_tiles)),
    ]
    return pl.pallas_call(
        _proj_kernel, grid=grid, in_specs=in_specs, out_specs=out_specs, out_shape=out_shape,
        compiler_params=pltpu.CompilerParams(dimension_semantics=("arbitrary",),
                                             vmem_limit_bytes=VMEM_LIMIT),
        name="proj_attn_side",
    )(x2d, wstd, wt, cosl, sinl, cost, sint)


def _rope_tables(pos):
    half = HEAD_DIM // 2
    inv = ROPE_THETA ** (-np.arange(half, dtype=np.float64) / half)
    ang = pos.astype(np.float64)[:, None] * inv[None, :]
    cos = np.cos(ang)
    sin = np.sin(ang)
    cosl = np.tile(cos, (1, LANES // half))
    sinl = np.tile(np.concatenate([-sin, sin], axis=1), (1, LANES // HEAD_DIM))
    return tuple(jnp.asarray(t, dtype=F32) for t in (cosl, sinl, cos.T, sin.T))


def _select_threshold(count_ge, count_gt, kth_tie_index, min_ge, max_lt, lo0, hi0, n_allowed, n_keys):
    k = float(TOPK)

    def step(carry):
        lo, hi, clo = carry
        mid = 0.5 * lo + 0.5 * hi
        c = count_ge(mid)
        take = c >= k
        return jnp.where(take, mid, lo), jnp.where(take, hi, mid), jnp.where(take, c, clo)

    lo, hi, clo = lax.fori_loop(0, BISECT_ITERS, lambda _, c: step(c), (lo0, hi0, n_allowed))
    big = jnp.full_like(lo, float(4 * n_keys))

    def unresolved(clo_):
        return clo_ > k

    def any_true(mask):
        return jnp.max(jnp.where(mask, 1.0, 0.0)) > 0.5

    def exact(args):
        lo, hi, clo = args

        def cond(c):
            _, _, clo, vlo, vhi, it = c
            open_ = jnp.logical_and(unresolved(clo), vlo < vhi)
            return jnp.logical_and(any_true(open_), it < 2 * n_keys)

        def body(c):
            lo, hi, clo, vlo, vhi, it = c
            mid = 0.5 * vlo + 0.5 * vhi
            mid = jnp.where(mid > vlo, mid, vhi)
            cnt = count_ge(mid)
            take = cnt >= k
            lo, hi, clo = jnp.where(take, mid, lo), jnp.where(take, hi, mid), jnp.where(take, cnt, clo)
            return lo, hi, clo, min_ge(lo), max_lt(hi), it + 1

        lo, hi, clo, vlo, _, _ = lax.while_loop(
            cond, body, (lo, hi, clo, min_ge(lo), max_lt(hi), jnp.int32(0)))
        lo = vlo
        need = k - count_gt(lo)
        return lo, jnp.where(clo > k, kth_tie_index(lo, need), big)

    def fast(args):
        return args[0], big

    return lax.cond(any_true(unresolved(clo)), exact, fast, (lo, hi, clo))


def _fold_rows(x, op):
    groups = [x[r:r + 8] for r in range(0, x.shape[0], 8)]
    parts = groups[:FOLD_CHAINS]
    for n, grp in enumerate(groups[FOLD_CHAINS:]):
        parts[n % FOLD_CHAINS] = op(parts[n % FOLD_CHAINS], grp)
    while len(parts) > 1:
        nxt = [op(parts[a], parts[a + 1]) for a in range(0, len(parts) - 1, 2)]
        if len(parts) % 2:
            nxt.append(parts[-1])
        parts = nxt
    return parts[0]


def _topk_bias_in_place(s_ref, n_keys, n_allowed, lo0=None, hi0=None):
    lanes = s_ref.shape[1]
    inf = jnp.float32(jnp.inf)
    chunk = 8 * FOLD_CHAINS
    kidx0 = lax.broadcasted_iota(jnp.int32, (chunk, lanes), 0).astype(F32)

    def key_reduce(fn, op, finish):
        part = None
        for r in range(0, n_keys, chunk):
            v = fn(s_ref[r:r + chunk, :], r)
            part = v if part is None else op(part, v)
        return finish(_fold_rows(part, op), axis=0, keepdims=True)

    count_ge = lambda thr: key_reduce(lambda t, r: jnp.where(t >= thr, 1.0, 0.0), jnp.add, jnp.sum)
    count_gt = lambda thr: key_reduce(lambda t, r: jnp.where(t > thr, 1.0, 0.0), jnp.add, jnp.sum)
    tri = (lax.broadcasted_iota(jnp.int32, (chunk, chunk), 0)
           >= lax.broadcasted_iota(jnp.int32, (chunk, chunk), 1)).astype(BF16)

    def kth_tie_index(thr, need):
        seen = jnp.zeros((1, lanes), F32)
        best = jnp.full((1, lanes), -1.0, F32)
        for r in range(0, n_keys, chunk):
            tie = jnp.where(s_ref[r:r + chunk, :] == thr, 1.0, 0.0)
            upto = jnp.dot(tri, tie.astype(BF16), preferred_element_type=F32) + seen
            hit = jnp.where(tie > 0.5, jnp.where(upto == need, kidx0 + r, -1.0), -1.0)
            best = jnp.maximum(best, jnp.max(_fold_rows(hit, jnp.maximum), axis=0, keepdims=True))
            seen = upto[chunk - 1:chunk, :]
        return best
    min_ge = lambda thr: key_reduce(lambda t, r: jnp.where(t >= thr, t, inf), jnp.minimum, jnp.min)
    max_lt = lambda thr: key_reduce(lambda t, r: jnp.where(t < thr, t, -inf), jnp.maximum, jnp.max)
    if lo0 is None:
        lo0 = key_reduce(lambda t, r: jnp.where(t == -inf, inf, t), jnp.minimum, jnp.min)
        mx1 = key_reduce(lambda t, r: t, jnp.maximum, jnp.max)
        hi0 = mx1 + (jnp.abs(mx1) * (2.0 ** -20) + 1e-30)
    lo, jmax = _select_threshold(count_ge, count_gt, kth_tie_index, min_ge, max_lt, lo0, hi0,
                                 n_allowed, n_keys)
    for r in range(0, n_keys, chunk):
        t = s_ref[r:r + chunk, :]
        tie = jnp.where(t == lo, jnp.where(kidx0 <= jmax - r, 0.0, NEG), NEG)
        s_ref[r:r + chunk, :] = jnp.where(t > lo, 0.0, tie)


def _attn_prompt_kernel(qt_ref, qit_ref, wit_ref, kbf_ref, kiw_ref, vt_ref, o_ref,
                        s_ref, sc_ref, ri_ref, rq_ref):
    i = pl.program_id(1)
    w = wit_ref[...]
    zeros_half = jnp.zeros((HEAD_DIM, LANES), BF16)

    for hp in range(4):
        a = qit_ref[(2 * hp) * 64:(2 * hp + 1) * 64, :]
        b = qit_ref[(2 * hp + 1) * 64:(2 * hp + 2) * 64, :]
        ri_ref[hp] = jnp.concatenate(
            [jnp.concatenate([a, zeros_half], axis=0), jnp.concatenate([b, zeros_half], axis=0)], axis=1)
        g = hp // 2
        a = qt_ref[(2 * hp) * 64:(2 * hp + 1) * 64, :]
        b = qt_ref[(2 * hp + 1) * 64:(2 * hp + 2) * 64, :]
        if g == 0:
            ab = jnp.concatenate([jnp.concatenate([a, zeros_half], axis=0),
                                  jnp.concatenate([b, zeros_half], axis=0)], axis=1)
        else:
            ab = jnp.concatenate([jnp.concatenate([zeros_half, a], axis=0),
                                  jnp.concatenate([zeros_half, b], axis=0)], axis=1)
        rq_ref[hp] = ab

    inf = jnp.float32(jnp.inf)
    n_steps = (i * LANES + LANES + KEY_STEP - 1) // KEY_STEP

    def process(ns):
        n_keys = ns * KEY_STEP
        keys = slice(0, n_keys)
        row = lax.broadcasted_iota(jnp.int32, (n_keys, LANES), 0)
        col = lax.broadcasted_iota(jnp.int32, (n_keys, LANES), 1)
        qpos = col + i * LANES
        allowed = row <= qpos

        kt = kiw_ref[keys, :]
        acc = jnp.zeros((n_keys, LANES), F32)
        for hp in range(4):
            s2 = jnp.dot(kt, ri_ref[hp], preferred_element_type=F32)
            acc = acc + jnp.maximum(s2[:, :LANES], 0.0) * w[2 * hp:2 * hp + 1, :]
            acc = acc + jnp.maximum(s2[:, LANES:], 0.0) * w[2 * hp + 1:2 * hp + 2, :]
        s_ref[keys, :] = jnp.where(allowed, acc, -inf)
        lo0 = jnp.min(_fold_rows(jnp.where(allowed, acc, inf), jnp.minimum), axis=0, keepdims=True)
        mx1 = jnp.max(_fold_rows(jnp.where(allowed, acc, -inf), jnp.maximum), axis=0, keepdims=True)
        hi0 = mx1 + (jnp.abs(mx1) * (2.0 ** -20) + 1e-30)
        n_allowed = (qpos[0:1, :] + 1).astype(F32)

        _topk_bias_in_place(s_ref, n_keys, n_allowed, lo0, hi0)

        kt = kbf_ref[keys, :]
        b = s_ref[keys, :]
        b2 = jnp.concatenate([b, b], axis=1)
        ms = []
        for hp in range(4):
            s2 = jnp.dot(kt, rq_ref[hp], preferred_element_type=F32) + b2
            sc_ref[hp, keys, :] = s2
            ms.append(jnp.max(_fold_rows(s2, jnp.maximum), axis=0, keepdims=True))
        vt = jnp.concatenate([vt_ref[j] for j in range(ns)], axis=1)
        ones_rows = jnp.ones((ONES_ROWS, n_keys), BF16)
        lhs = [jnp.concatenate([vt[g * 64:(g + 1) * 64, :], ones_rows], axis=0) for g in range(N_KV)]
        pieces = []
        for hp in range(4):
            p = jnp.exp2((sc_ref[hp, keys, :] - ms[hp]).astype(BF16))
            o = jnp.dot(lhs[hp // 2], p, preferred_element_type=F32)
            o2 = o[0:HEAD_DIM, :] / o[HEAD_DIM:HEAD_DIM + 1, :]
            pieces.append(o2[:, :LANES])
            pieces.append(o2[:, LANES:])
        o_ref[...] = jnp.concatenate(pieces, axis=0).T

    for ns in range(1, SEQ // KEY_STEP + 1):
        pl.when(n_steps == ns)(functools.partial(process, ns))


def _attn_prompt(qt, qit, wit, kbf, kiw, vt3):
    n_blocks = SEQ // LANES
    grid = (BATCH, n_blocks)
    qmap = lambda n, i: (0, n * n_blocks + i)
    in_specs = [
        pl.BlockSpec((512, LANES), qmap),
        pl.BlockSpec((512, LANES), qmap),
        pl.BlockSpec((8, LANES), qmap),
        pl.BlockSpec((SEQ, 128), lambda n, i: (n, 0)),
        pl.BlockSpec((SEQ, 128), lambda n, i: (n, 0)),
        pl.BlockSpec((SEQ // KEY_STEP, 128, KEY_STEP), lambda n, i: (n, 0, 0)),
    ]
    return pl.pallas_call(
        _attn_prompt_kernel, grid=grid, in_specs=in_specs,
        out_specs=pl.BlockSpec((LANES, W_A), lambda n, i: (n * n_blocks + i, 0)),
        out_shape=jax.ShapeDtypeStruct((BATCH * SEQ, W_A), F32),
        scratch_shapes=[pltpu.VMEM((SEQ, LANES), F32), pltpu.VMEM((4, SEQ, 2 * LANES), F32),
                        pltpu.VMEM((4, 128, 2 * LANES), BF16), pltpu.VMEM((4, 128, 2 * LANES), BF16)],
        compiler_params=pltpu.CompilerParams(dimension_semantics=("arbitrary", "arbitrary"),
                                             vmem_limit_bytes=VMEM_LIMIT),
        name="attn_prompt",
    )(qt, qit, wit, kbf, kiw, vt3)


def _page_copies(pt_ref, group, slot, gsz, streams):
    out = []
    for g in range(gsz):
        seq = group * gsz + g
        for p in range(N_PAGES):
            page = pt_ref[seq, p]
            keys = pl.ds(p * PAGE_SIZE, PAGE_SIZE)
            for hbm, buf, sem in streams:
                out.append(pltpu.make_async_copy(hbm.at[page], buf.at[slot, g, :, keys], sem.at[slot]))
    return out


def _paged_prefetch(pt_ref, gsz, streams):
    step = pl.program_id(0)
    n_steps = pl.num_programs(0)
    slot = step % 2

    @pl.when(step == 0)
    def _():
        tail = pl.ds(PAST_LEN, KEY_TILE)
        for _, buf, _ in streams:
            for s_ in range(2):
                for g in range(gsz):
                    buf[s_, g, :, tail] = jnp.zeros((buf.shape[2], KEY_TILE), F32)
        for c in _page_copies(pt_ref, 0, 0, gsz, streams):
            c.start()

    @pl.when(step + 1 < n_steps)
    def _():
        for c in _page_copies(pt_ref, step + 1, 1 - slot, gsz, streams):
            c.start()

    for c in _page_copies(pt_ref, step, slot, gsz, streams):
        c.wait()
    return slot


def _sample_index_kernel(pt_ref, qi_ref, w_ref, kin_ref, cki_hbm, s_out_ref, kibuf, sem):
    gsz = SAMPLE_IDX_GROUP
    slot = _paged_prefetch(pt_ref, gsz, [(cki_hbm, kibuf, sem)])
    rows8 = lax.broadcasted_iota(jnp.int32, (SAMPLE_Q, SAMPLE_KEYS), 0)
    keyi = lax.broadcasted_iota(jnp.int32, (SAMPLE_Q, SAMPLE_KEYS), 1)
    allowed = keyi <= (PAST_LEN + rows8)
    new_keys = pl.ds(PAST_LEN, DEC_SEQ)
    scores = []
    for g in range(gsz):
        kibuf[slot, g, :, new_keys] = kin_ref[g]
        kib = kibuf[slot, g].astype(BF16)
        s = jnp.dot(qi_ref[g], kib, preferred_element_type=F32)
        sw = jnp.maximum(s, 0.0) * w_ref[g][:, 0:1]
        acc = jnp.sum(sw.reshape(N_IDX_HEADS, SAMPLE_Q, SAMPLE_KEYS), axis=0)
        scores.append(jnp.where(allowed, acc, -jnp.inf))
    for a in range(gsz // 2):
        s_out_ref[a] = jnp.where(rows8 < DEC_SEQ, scores[2 * a],
                                 pltpu.roll(scores[2 * a + 1], DEC_SEQ, 0))


def _sample_select_kernel(s_ref, b_ref, st_ref):
    rows = s_ref.shape[0]
    st_ref[...] = s_ref[...].T
    t4 = lax.broadcasted_iota(jnp.int32, (1, rows), 1) % DEC_SEQ
    _topk_bias_in_place(st_ref, SAMPLE_KEYS, (PAST_LEN + 1 + t4).astype(F32))
    b_ref[...] = st_ref[...].T


def _sample_attend_kernel(pt_ref, q_ref, b_ref, kn_ref, vn_ref, ck_hbm, cv_hbm, o_ref,
                          kbuf, vbuf, ksem, vsem):
    gsz = SAMPLE_GROUP
    slot = _paged_prefetch(pt_ref, gsz, [(ck_hbm, kbuf, ksem), (cv_hbm, vbuf, vsem)])
    new_keys = pl.ds(PAST_LEN, DEC_SEQ)
    lane = lax.broadcasted_iota(jnp.int32, (SAMPLE_Q, LANES), 1)
    for g in range(gsz):
        kbuf[slot, g, :, new_keys] = kn_ref[g]
        vbuf[slot, g, :, new_keys] = vn_ref[g]
        kb = kbuf[slot, g].astype(BF16)
        vb = vbuf[slot, g].astype(BF16)
        b = b_ref[g // 2]
        if g % 2:
            b = pltpu.roll(b, DEC_SEQ, 0)
        s = jnp.dot(q_ref[g], kb, preferred_element_type=F32)
        s = s + jnp.concatenate([b] * N_HEADS, axis=0)
        m = jnp.max(s, axis=1, keepdims=True)
        p = jnp.exp2(s - m)
        l = jnp.sum(p, axis=1, keepdims=True)
        o = lax.dot_general(p.astype(BF16), vb, (((1,), (1,)), ((), ())),
                            preferred_element_type=F32) / l
        o_r = pltpu.roll(o, HEAD_DIM, 1)
        tiles = []
        for c in range(4):
            ha, hb = 2 * c, 2 * c + 1
            src_a = o if ha // 4 == 0 else o_r
            src_b = o if hb // 4 == 1 else o_r
            tiles.append(jnp.where(lane < HEAD_DIM, src_a[ha * SAMPLE_Q:(ha + 1) * SAMPLE_Q, :],
                                   src_b[hb * SAMPLE_Q:(hb + 1) * SAMPLE_Q, :]))
        o_ref[g] = jnp.concatenate(tiles, axis=1)


def _attn_sample(page_table, qi_s, q_s, w_s, ki_new, k_new, v_new, ck, cv, cki):
    params = pltpu.CompilerParams(dimension_semantics=("arbitrary",), vmem_limit_bytes=VMEM_LIMIT)
    blk = lambda n, shape: pl.BlockSpec((n,) + shape, lambda s, pt: (s, 0, 0))
    hbm = pl.BlockSpec(memory_space=pl.ANY)
    n_pairs = DEC_BATCH // 2

    gi = SAMPLE_IDX_GROUP
    scores = pl.pallas_call(
        _sample_index_kernel,
        grid_spec=pltpu.PrefetchScalarGridSpec(
            num_scalar_prefetch=1, grid=(DEC_BATCH // gi,),
            in_specs=[blk(gi, (64, IDX_DIM)), blk(gi, (64, LANES)), blk(gi, (IDX_DIM, DEC_SEQ)), hbm],
            out_specs=blk(gi // 2, (SAMPLE_Q, SAMPLE_KEYS)),
            scratch_shapes=[pltpu.VMEM((2, gi, IDX_DIM, SAMPLE_KEYS), F32),
                            pltpu.SemaphoreType.DMA((2,))]),
        out_shape=jax.ShapeDtypeStruct((n_pairs, SAMPLE_Q, SAMPLE_KEYS), F32),
        compiler_params=params, name="sample_index",
    )(page_table, qi_s, w_s, ki_new, cki)

    rows = n_pairs * SAMPLE_Q
    bias = pl.pallas_call(
        _sample_select_kernel, grid=(1,),
        in_specs=[pl.BlockSpec((rows, SAMPLE_KEYS), lambda s: (0, 0))],
        out_specs=pl.BlockSpec((rows, SAMPLE_KEYS), lambda s: (0, 0)),
        out_shape=jax.ShapeDtypeStruct((rows, SAMPLE_KEYS), F32),
        scratch_shapes=[pltpu.VMEM((SAMPLE_KEYS, rows), F32)],
        compiler_params=params, name="sample_select",
    )(scores.reshape(rows, SAMPLE_KEYS)).reshape(n_pairs, SAMPLE_Q, SAMPLE_KEYS)

    ga = SAMPLE_GROUP
    return pl.pallas_call(
        _sample_attend_kernel,
        grid_spec=pltpu.PrefetchScalarGridSpec(
            num_scalar_prefetch=1, grid=(DEC_BATCH // ga,),
            in_specs=[blk(ga, (64, 128)), blk(ga // 2, (SAMPLE_Q, SAMPLE_KEYS)),
                      blk(ga, (128, DEC_SEQ)), blk(ga, (128, DEC_SEQ)), hbm, hbm],
            out_specs=blk(ga, (SAMPLE_Q, W_A)),
            scratch_shapes=[pltpu.VMEM((2, ga, 128, SAMPLE_KEYS), F32),
                            pltpu.VMEM((2, ga, 128, SAMPLE_KEYS), F32),
                            pltpu.SemaphoreType.DMA((2,)), pltpu.SemaphoreType.DMA((2,))]),
        out_shape=jax.ShapeDtypeStruct((DEC_BATCH, SAMPLE_Q, W_A), F32),
        compiler_params=params, name="sample_attend",
    )(page_table, q_s, bias, k_new, v_new, ck, cv)


def _layer_norm(x, g, b):
    mu = jnp.mean(x, axis=-1, keepdims=True)
    xc = x - mu
    var = jnp.mean(xc * xc, axis=-1, keepdims=True)
    return xc * lax.rsqrt(var + LN_EPS) * g + b


def _silu(x):
    return x / (1.0 + jnp.exp(-x))


def _mixer_kernel(x_ref, a_ref, wb_ref, wmix_ref, bsb_ref, lnvg_ref, lnvb_ref, wout_ref,
                  lng_ref, lnb_ref, y_ref, vn_ref, *, chunk, vn_last_only, tiles_per_seq):
    tm = x_ref.shape[0]
    x = x_ref[...]
    zb = lax.dot_general(x.astype(BF16), wb_ref[...], (((1,), (1,)), ((), ())),
                         preferred_element_type=F32)
    ga = zb[:, 0:512]
    u = zb[:, 512:1024]
    vb = zb[:, 1024:1536]
    gb = zb[:, 1536:2048]
    vn = _layer_norm(vb, lnvg_ref[...], lnvb_ref[...])
    if vn_last_only:
        @pl.when(pl.program_id(0) % tiles_per_seq == tiles_per_seq - 1)
        def _():
            vn_ref[0] = vn[tm - CHUNK:, :]
    else:
        vn_ref[...] = vn
    vnb = vn.astype(BF16)
    rows = []
    for c in range(tm // chunk):
        cols = []
        for g in range(N_GROUPS_B):
            blk = vnb[c * chunk:(c + 1) * chunk, g * GROUP_W_B:(g + 1) * GROUP_W_B]
            cols.append(jnp.dot(wmix_ref[g], blk, preferred_element_type=F32) + bsb_ref[g])
        rows.append(jnp.concatenate(cols, axis=1))
    mixed = jnp.concatenate(rows, axis=0) if len(rows) > 1 else rows[0]
    a_out = a_ref[...] * _silu(ga)
    b_out = u * mixed * _silu(gb)
    cat = jnp.concatenate([a_out, b_out], axis=1).astype(BF16)
    out = jnp.dot(cat, wout_ref[...], preferred_element_type=F32)
    y_ref[...] = _layer_norm(ALPHA * x + out, lng_ref[...], lnb_ref[...])


def _mixer(x2d, attn, wb, wmix, bsb, lnvg, lnvb, wout, lng, lnb, *, chunk, vn_last_only):
    rows = x2d.shape[0]
    tm = ROW_TILE
    tiles_per_seq = SEQ // tm
    grid = (rows // tm,)
    row_map = lambda r: (r, 0)
    const2 = lambda r: (0, 0)
    const3 = lambda r: (0, 0, 0)
    if vn_last_only:
        vn_shape = jax.ShapeDtypeStruct((rows // SEQ, CHUNK, W_B), F32)
        vn_spec = pl.BlockSpec((1, CHUNK, W_B), lambda r: (r // tiles_per_seq, 0, 0))
    else:
        vn_shape = jax.ShapeDtypeStruct((rows, W_B), F32)
        vn_spec = pl.BlockSpec((tm, W_B), row_map)
    kern = functools.partial(_mixer_kernel, chunk=chunk, vn_last_only=vn_last_only,
                             tiles_per_seq=tiles_per_seq)
    return pl.pallas_call(
        kern, grid=grid,
        in_specs=[pl.BlockSpec((tm, D_MODEL), row_map), pl.BlockSpec((tm, W_A), row_map),
                  pl.BlockSpec(wb.shape, const2), pl.BlockSpec(wmix.shape, const3),
                  pl.BlockSpec(bsb.shape, const3), pl.BlockSpec((1, W_B), const2),
                  pl.BlockSpec((1, W_B), const2), pl.BlockSpec(wout.shape, const2),
                  pl.BlockSpec((1, D_MODEL), const2), pl.BlockSpec((1, D_MODEL), const2)],
        out_specs=(pl.BlockSpec((tm, D_MODEL), row_map), vn_spec),
        out_shape=(jax.ShapeDtypeStruct((rows, D_MODEL), F32), vn_shape),
        compiler_params=pltpu.CompilerParams(dimension_semantics=("arbitrary",),
                                             vmem_limit_bytes=VMEM_LIMIT),
        name="mixer_chunk%d" % chunk,
    )(x2d, attn, wb, wmix, bsb, lnvg, lnvb, wout, lng, lnb)


def kernel(x_prompt, x_sample, cache_k, cache_v, cache_k_idx, page_table, w_in, w_s, b_s,
           ln_v_g, ln_v_b, w_out, ln_g, ln_b):
    w = w_in[0]
    wq, wk, wv = w[:, 0:512], w[:, 512:640], w[:, 640:768]
    wga, wqi, wki, wwi = w[:, 768:1280], w[:, 1280:1792], w[:, 1792:1856], w[:, 1856:1864]
    wu, wvb, wgb = w[:, 1864:2376], w[:, 2376:2888], w[:, 2888:3400]
    wstd = jnp.concatenate([wk.T, wki.T, jnp.zeros((64, D_MODEL), F32)], axis=0).astype(BF16)
    wt = jnp.concatenate([wq.T * Q_SCALE, wqi.T, wv.T, wk.T, wki.T, wwi.T,
                          jnp.zeros((WT_ROWS - WT_WI - N_IDX_HEADS, D_MODEL), F32)], axis=0).astype(BF16)
    wb = jnp.concatenate([wga.T, wu.T, wvb.T, wgb.T], axis=0).astype(BF16)
    wout = w_out[0].astype(BF16)
    lnvg, lnvb = ln_v_g[0][None, :], ln_v_b[0][None, :]
    lng, lnb = ln_g[0][None, :], ln_b[0][None, :]

    xp = x_prompt.reshape(BATCH * SEQ, D_MODEL)
    tabs_p = _rope_tables(np.arange(SEQ))
    kt_p, vt_p, kit_p, kbf, kiw, qt, qit, vt3, wit = _project(xp, wstd, wt, *tabs_p, SEQ)
    attn_p = _attn_prompt(qt, qit, wit, kbf, kiw, vt3)
    tril = np.tril(np.ones((CHUNK, CHUNK), np.float32))
    wmix_p = (w_s[0] * tril[None]).astype(BF16)
    bsb_p = jnp.broadcast_to(b_s[0][:, :, None], (N_GROUPS_B, CHUNK, GROUP_W_B))
    y_p, vn_p = _mixer(xp, attn_p, wb, wmix_p, bsb_p, lnvg, lnvb, wout, lng, lnb,
                       chunk=CHUNK, vn_last_only=True)

    rows_s = DEC_BATCH * DEC_SEQ
    xs = x_sample.reshape(rows_s, D_MODEL)
    tabs_s = _rope_tables(PAST_LEN + (np.arange(rows_s) % DEC_SEQ))
    kt_s, vt_s, kit_s, _, _, qt_s, qit_s, _, wit_s = _project(xs, wstd, wt, *tabs_s, rows_s)

    def to_rows(a_t):
        a = a_t.reshape(N_HEADS, HEAD_DIM, DEC_BATCH, DEC_SEQ).transpose(2, 0, 3, 1)
        return jnp.pad(a, ((0, 0), (0, 0), (0, SAMPLE_Q - DEC_SEQ), (0, 0)))

    qi_rows = to_rows(qit_s).reshape(DEC_BATCH, N_IDX_HEADS * SAMPLE_Q, IDX_DIM)
    q5 = to_rows(qt_s).reshape(DEC_BATCH, N_KV, N_HEADS // N_KV, SAMPLE_Q, HEAD_DIM)
    zq = jnp.zeros_like(q5[:, 0])
    q_rows = jnp.stack([jnp.concatenate([q5[:, 0], zq], axis=-1),
                        jnp.concatenate([zq, q5[:, 1]], axis=-1)], axis=1)
    q_rows = q_rows.reshape(DEC_BATCH, N_HEADS * SAMPLE_Q, 2 * HEAD_DIM)
    w_rows = jnp.pad(wit_s.reshape(N_IDX_HEADS, DEC_BATCH, DEC_SEQ).transpose(1, 0, 2),
                     ((0, 0), (0, 0), (0, SAMPLE_Q - DEC_SEQ)))
    w_rows = jnp.broadcast_to(w_rows.reshape(DEC_BATCH, N_IDX_HEADS * SAMPLE_Q, 1),
                              (DEC_BATCH, N_IDX_HEADS * SAMPLE_Q, LANES))
    new_cols = lambda a: a.reshape(a.shape[0], DEC_BATCH, DEC_SEQ).transpose(1, 0, 2)
    n_pool = cache_k.shape[1]
    ck_t = cache_k[0].transpose(0, 2, 3, 1).reshape(n_pool, N_KV * HEAD_DIM, PAGE_SIZE)
    cv_t = cache_v[0].transpose(0, 2, 3, 1).reshape(n_pool, N_KV * HEAD_DIM, PAGE_SIZE)
    cki_t = cache_k_idx[0].transpose(0, 2, 1)
    attn_s8 = _attn_sample(page_table, qi_rows, q_rows, w_rows, new_cols(kit_s[0]),
                           new_cols(kt_s[0]), new_cols(vt_s[0]), ck_t, cv_t, cki_t)
    attn_s = attn_s8[:, :DEC_SEQ, :].reshape(rows_s, W_A)
    ri = np.arange(rows_s)
    same_seq = (ri[:, None] // DEC_SEQ) == (ri[None, :] // DEC_SEQ)
    wmix_s = jnp.zeros((N_GROUPS_B, rows_s, rows_s), F32)
    for t in range(DEC_SEQ):
        for s in range(t + 1):
            hit = same_seq & (ri[:, None] % DEC_SEQ == t) & (ri[None, :] % DEC_SEQ == s)
            wmix_s = wmix_s + jnp.where(hit[None], w_s[0, :, t, s][:, None, None], 0.0)
    wmix_s = wmix_s.astype(BF16)
    bsb_s = jnp.broadcast_to(jnp.tile(b_s[0][:, :DEC_SEQ], (1, DEC_BATCH))[:, :, None],
                             (N_GROUPS_B, rows_s, GROUP_W_B))
    y_s, vn_s = _mixer(xs, attn_s, wb, wmix_s, bsb_s, lnvg, lnvb, wout, lng, lnb,
                       chunk=rows_s, vn_last_only=False)

    def heads_last(a, n, t):
        a = a.reshape(a.shape[0], N_KV, HEAD_DIM, a.shape[2]).transpose(0, 3, 1, 2)
        return a.reshape(1, n, t, N_KV, HEAD_DIM)

    return (y_p.reshape(BATCH, SEQ, D_MODEL),
            y_s.reshape(DEC_BATCH, DEC_SEQ, D_MODEL),
            heads_last(kt_p, BATCH, SEQ), heads_last(vt_p, BATCH, SEQ),
            kit_p.transpose(0, 2, 1)[None],
            vn_p.reshape(1, BATCH, CHUNK, W_B),
            heads_last(kt_s, DEC_BATCH, DEC_SEQ), heads_last(vt_s, DEC_BATCH, DEC_SEQ),
            kit_s[0].T.reshape(1, DEC_BATCH, DEC_SEQ, IDX_DIM),
            vn_s.reshape(1, DEC_BATCH, DEC_SEQ, W_B))
```

```python
import functools

import jax
import jax.numpy as jnp
import numpy as np
from jax import lax
from jax.experimental import pallas as pl
from jax.experimental.pallas import tpu as pltpu

F32 = jnp.float32
BF16 = jnp.bfloat16

D_MODEL = 1024
SEQ = 2048
BATCH = 8
DEC_BATCH = 128
DEC_SEQ = 4
PAST_LEN = 2048
PAGE_SIZE = 128
N_PAGES = PAST_LEN // PAGE_SIZE
W_A = 512
W_B = 512
HEAD_DIM = 64
N_HEADS = 8
N_KV = 2
N_IDX_HEADS = 8
IDX_DIM = 64
TOPK = 256
CHUNK = 128
N_GROUPS_B = 4
GROUP_W_B = 128
ROPE_THETA = 10000.0
LN_EPS = 1e-5
ALPHA = 2.0 ** 0.25
NEG = -1e30
IDX_SCALE = float((N_IDX_HEADS * IDX_DIM) ** -0.5)

LANES = 128
KEY_TILE = 128
KEY_STEP = 256
ROW_TILE = 512
WT_Q, WT_QI, WT_V, WT_K, WT_KI, WT_WI, WT_ROWS = 0, 512, 1024, 1152, 1280, 1344, 1360
BISECT_ITERS = 20
FOLD_CHAINS = 8
ONES_ROWS = 16
Q_SCALE = float(HEAD_DIM ** -0.5 * 1.4426950408889634)
VMEM_LIMIT = 48 * 1024 * 1024

SAMPLE_GROUP = 4
SAMPLE_IDX_GROUP = 8
SAMPLE_KEYS = PAST_LEN + KEY_TILE
SAMPLE_Q = 8


def _proj_kernel(x_ref, wstd_ref, wt_ref, cosl_ref, sinl_ref, cost_ref, sint_ref,
                 kt_ref, vt32_ref, kit_ref, kbf_ref, kiw_ref, qt_ref, qit_ref, vt_ref, wit_ref):
    tm = x_ref.shape[0]
    xb = x_ref[...].astype(BF16)
    zs = lax.dot_general(xb, wstd_ref[...], (((1,), (1,)), ((), ())),
                         preferred_element_type=F32)
    cosl = cosl_ref[...]
    sinl = sinl_ref[...]
    lane = lax.broadcasted_iota(jnp.int32, (tm, LANES), 1)
    first_half = (lane % HEAD_DIM) < (HEAD_DIM // 2)

    def rope_lanes(z):
        partner = jnp.where(first_half, pltpu.roll(z, 96, 1), pltpu.roll(z, 32, 1))
        return z * cosl + partner * sinl

    kbf_ref[...] = rope_lanes(zs[:, 0:128]).astype(BF16)
    kiw_ref[...] = rope_lanes(zs[:, 128:256]).astype(BF16)

    zt = lax.dot_general(wt_ref[...], xb, (((1,), (1,)), ((), ())),
                         preferred_element_type=F32)
    cost = cost_ref[...]
    sint = sint_ref[...]
    half = HEAD_DIM // 2

    def rope_rows(base, n_heads):
        pieces = []
        for h in range(n_heads):
            x1 = zt[base + h * HEAD_DIM: base + h * HEAD_DIM + half]
            x2 = zt[base + h * HEAD_DIM + half: base + (h + 1) * HEAD_DIM]
            pieces.append(x1 * cost - x2 * sint)
            pieces.append(x2 * cost + x1 * sint)
        return jnp.concatenate(pieces, axis=0)

    qt_ref[...] = rope_rows(WT_Q, N_HEADS).astype(BF16)
    qit_ref[...] = rope_rows(WT_QI, N_IDX_HEADS).astype(BF16)
    v_t = zt[WT_V:WT_V + 128]
    vt32_ref[0] = v_t
    vtb = v_t.astype(BF16)
    for c in range(tm // KEY_STEP):
        vt_ref[c] = vtb[:, c * KEY_STEP:(c + 1) * KEY_STEP]
    kt_ref[0] = rope_rows(WT_K, N_KV)
    kit_ref[0] = rope_rows(WT_KI, 1)
    wit_ref[...] = zt[WT_WI:WT_WI + N_IDX_HEADS] * IDX_SCALE


def _project(x2d, wstd, wt, cosl, sinl, cost, sint, seq_len):
    rows = x2d.shape[0]
    tm = ROW_TILE
    grid = (rows // tm,)
    n_pos_tiles = seq_len // tm
    row_map = lambda r: (r, 0)
    col_map = lambda r: (0, r)
    const = lambda r: (0, 0)
    seq_map = lambda r: (r // n_pos_tiles, 0, r % n_pos_tiles)
    out_shape = (
        jax.ShapeDtypeStruct((rows // seq_len, 128, seq_len), F32),
        jax.ShapeDtypeStruct((rows // seq_len, 128, seq_len), F32),
        jax.ShapeDtypeStruct((rows // seq_len, IDX_DIM, seq_len), F32),
        jax.ShapeDtypeStruct((rows, 128), BF16),
        jax.ShapeDtypeStruct((rows, 128), BF16),
        jax.ShapeDtypeStruct((512, rows), BF16),
        jax.ShapeDtypeStruct((512, rows), BF16),
        jax.ShapeDtypeStruct((rows // KEY_STEP, 128, KEY_STEP), BF16),
        jax.ShapeDtypeStruct((8, rows), F32),
    )
    out_specs = (
        pl.BlockSpec((1, 128, tm), seq_map),
        pl.BlockSpec((1, 128, tm), seq_map),
        pl.BlockSpec((1, IDX_DIM, tm), seq_map),
        pl.BlockSpec((tm, 128), row_map),
        pl.BlockSpec((tm, 128), row_map),
        pl.BlockSpec((512, tm), col_map),
        pl.BlockSpec((512, tm), col_map),
        pl.BlockSpec((tm // KEY_STEP, 128, KEY_STEP), lambda r: (r, 0, 0)),
        pl.BlockSpec((8, tm), col_map),
    )
    in_specs = [
        pl.BlockSpec((tm, D_MODEL), row_map),
        pl.BlockSpec(wstd.shape, const),
        pl.BlockSpec(wt.shape, const),
        pl.BlockSpec((tm, LANES), lambda r: (r % n_pos_tiles, 0)),
        pl.BlockSpec((tm, LANES), lambda r: (r % n_pos_tiles, 0)),
        pl.BlockSpec((HEAD_DIM // 2, tm), lambda r: (0, r % n_pos_tiles)),
        pl.BlockSpec((HEAD_DIM // 2, tm), lambda r: (0, r % n_pos_tiles)),
    ]
    return pl.pallas_call(
        _proj_kernel, grid=grid, in_specs=in_specs, out_specs=out_specs, out_shape=out_shape,
        compiler_params=pltpu.CompilerParams(dimension_semantics=("arbitrary",),
                                             vmem_limit_bytes=VMEM_LIMIT),
        name="proj_attn_side",
    )(x2d, wstd, wt, cosl, sinl, cost, sint)


def _rope_tables(pos):
    half = HEAD_DIM // 2
    inv = ROPE_THETA ** (-np.arange(half, dtype=np.float64) / half)
    ang = pos.astype(np.float64)[:, None] * inv[None, :]
    cos = np.cos(ang)
    sin = np.sin(ang)
    cosl = np.tile(cos, (1, LANES // half))
    sinl = np.tile(np.concatenate([-sin, sin], axis=1), (1, LANES // HEAD_DIM))
    return tuple(jnp.asarray(t, dtype=F32) for t in (cosl, sinl, cos.T, sin.T))


def _select_threshold(count_ge, kth_tie_index, min_ge, max_lt, lo0, hi0, n_allowed, n_keys):
    k = float(TOPK)

    def split(mid, lo, hi, clo, chi):
        c = count_ge(mid)
        take = c >= k
        return (jnp.where(take, mid, lo), jnp.where(take, hi, mid),
                jnp.where(take, c, clo), jnp.where(take, chi, c))

    def step(_, carry):
        lo, hi, clo, chi = carry
        return split(0.5 * lo + 0.5 * hi, lo, hi, clo, chi)

    lo, hi, clo, chi = lax.fori_loop(0, BISECT_ITERS, step,
                                     (lo0, hi0, n_allowed, jnp.zeros_like(lo0)))
    big = jnp.full_like(lo, float(4 * n_keys))

    def any_true(mask):
        return jnp.max(jnp.where(mask, 1.0, 0.0)) > 0.5

    def exact(args):
        lo, hi, clo, chi = args

        def cond(c):
            _, _, clo, _, vlo, vhi, it = c
            open_ = jnp.logical_and(clo > k, vlo < vhi)
            return jnp.logical_and(any_true(open_), it < 2 * n_keys)

        def body(c):
            lo, hi, clo, chi, vlo, vhi, it = c
            mid = 0.5 * vlo + 0.5 * vhi
            lo, hi, clo, chi = split(jnp.where(mid > vlo, mid, vhi), lo, hi, clo, chi)
            return lo, hi, clo, chi, min_ge(lo), max_lt(hi), it + 1

        lo, hi, clo, chi, vlo, _, _ = lax.while_loop(
            cond, body, (lo, hi, clo, chi, min_ge(lo), max_lt(hi), jnp.int32(0)))
        return vlo, jnp.where(clo > k, kth_tie_index(vlo, k - chi), big)

    def fast(args):
        return args[0], big

    return lax.cond(any_true(clo > k), exact, fast, (lo, hi, clo, chi))


def _fold_rows(x, op):
    groups = [x[r:r + 8] for r in range(0, x.shape[0], 8)]
    parts = groups[:FOLD_CHAINS]
    for n, grp in enumerate(groups[FOLD_CHAINS:]):
        parts[n % FOLD_CHAINS] = op(parts[n % FOLD_CHAINS], grp)
    while len(parts) > 1:
        nxt = [op(parts[a], parts[a + 1]) for a in range(0, len(parts) - 1, 2)]
        if len(parts) % 2:
            nxt.append(parts[-1])
        parts = nxt
    return parts[0]


def _topk_bias_in_place(s_ref, n_keys, n_allowed, lo0=None, hi0=None):
    lanes = s_ref.shape[1]
    inf = jnp.float32(jnp.inf)
    chunk = 8 * FOLD_CHAINS
    kidx0 = lax.broadcasted_iota(jnp.int32, (chunk, lanes), 0).astype(F32)

    def key_reduce(fn, op, finish):
        part = None
        for r in range(0, n_keys, chunk):
            v = fn(s_ref[r:r + chunk, :], r)
            part = v if part is None else op(part, v)
        return finish(_fold_rows(part, op), axis=0, keepdims=True)

    count_ge = lambda thr: key_reduce(lambda t, r: jnp.where(t >= thr, 1.0, 0.0), jnp.add, jnp.sum)
    tri = (lax.broadcasted_iota(jnp.int32, (chunk, chunk), 0)
           >= lax.broadcasted_iota(jnp.int32, (chunk, chunk), 1)).astype(BF16)

    def kth_tie_index(thr, need):
        seen = jnp.zeros((1, lanes), F32)
        best = jnp.full((1, lanes), -1.0, F32)
        for r in range(0, n_keys, chunk):
            tie = jnp.where(s_ref[r:r + chunk, :] == thr, 1.0, 0.0)
            upto = jnp.dot(tri, tie.astype(BF16), preferred_element_type=F32) + seen
            hit = jnp.where(tie > 0.5, jnp.where(upto == need, kidx0 + r, -1.0), -1.0)
            best = jnp.maximum(best, jnp.max(_fold_rows(hit, jnp.maximum), axis=0, keepdims=True))
            seen = upto[chunk - 1:chunk, :]
        return best
    min_ge = lambda thr: key_reduce(lambda t, r: jnp.where(t >= thr, t, inf), jnp.minimum, jnp.min)
    max_lt = lambda thr: key_reduce(lambda t, r: jnp.where(t < thr, t, -inf), jnp.maximum, jnp.max)
    if lo0 is None:
        lo0 = key_reduce(lambda t, r: jnp.where(t == -inf, inf, t), jnp.minimum, jnp.min)
        mx1 = key_reduce(lambda t, r: t, jnp.maximum, jnp.max)
        hi0 = mx1 + (jnp.abs(mx1) * (2.0 ** -20) + 1e-30)
    lo, jmax = _select_threshold(count_ge, kth_tie_index, min_ge, max_lt, lo0, hi0, n_allowed, n_keys)
    for r in range(0, n_keys, chunk):
        t = s_ref[r:r + chunk, :]
        tie = jnp.where(t == lo, jnp.where(kidx0 <= jmax - r, 0.0, NEG), NEG)
        s_ref[r:r + chunk, :] = jnp.where(t > lo, 0.0, tie)


def _attn_prompt_kernel(qt_ref, qit_ref, wit_ref, kbf_ref, kiw_ref, vt_ref, o_ref,
                        s_ref, sc_ref, ri_ref, rq_ref):
    i = pl.program_id(1)
    w = wit_ref[...]
    zeros_half = jnp.zeros((HEAD_DIM, LANES), BF16)

    for hp in range(4):
        a = qit_ref[(2 * hp) * 64:(2 * hp + 1) * 64, :]
        b = qit_ref[(2 * hp + 1) * 64:(2 * hp + 2) * 64, :]
        ri_ref[hp] = jnp.concatenate(
            [jnp.concatenate([a, zeros_half], axis=0), jnp.concatenate([b, zeros_half], axis=0)], axis=1)
        g = hp // 2
        a = qt_ref[(2 * hp) * 64:(2 * hp + 1) * 64, :]
        b = qt_ref[(2 * hp + 1) * 64:(2 * hp + 2) * 64, :]
        if g == 0:
            ab = jnp.concatenate([jnp.concatenate([a, zeros_half], axis=0),
                                  jnp.concatenate([b, zeros_half], axis=0)], axis=1)
        else:
            ab = jnp.concatenate([jnp.concatenate([zeros_half, a], axis=0),
                                  jnp.concatenate([zeros_half, b], axis=0)], axis=1)
        rq_ref[hp] = ab

    inf = jnp.float32(jnp.inf)
    n_steps = (i * LANES + LANES + KEY_STEP - 1) // KEY_STEP

    def process(ns):
        n_keys = ns * KEY_STEP
        keys = slice(0, n_keys)
        row = lax.broadcasted_iota(jnp.int32, (n_keys, LANES), 0)
        col = lax.broadcasted_iota(jnp.int32, (n_keys, LANES), 1)
        qpos = col + i * LANES
        allowed = row <= qpos

        kt = kiw_ref[keys, :]
        acc = jnp.zeros((n_keys, LANES), F32)
        for hp in range(4):
            s2 = jnp.dot(kt, ri_ref[hp], preferred_element_type=F32)
            acc = acc + jnp.maximum(s2[:, :LANES], 0.0) * w[2 * hp:2 * hp + 1, :]
            acc = acc + jnp.maximum(s2[:, LANES:], 0.0) * w[2 * hp + 1:2 * hp + 2, :]
        s_ref[keys, :] = jnp.where(allowed, acc, -inf)
        lo0 = jnp.min(_fold_rows(jnp.where(allowed, acc, inf), jnp.minimum), axis=0, keepdims=True)
        mx1 = jnp.max(_fold_rows(jnp.where(allowed, acc, -inf), jnp.maximum), axis=0, keepdims=True)
        hi0 = mx1 + (jnp.abs(mx1) * (2.0 ** -20) + 1e-30)
        n_allowed = (qpos[0:1, :] + 1).astype(F32)

        _topk_bias_in_place(s_ref, n_keys, n_allowed, lo0, hi0)

        kt = kbf_ref[keys, :]
        b = s_ref[keys, :]
        b2 = jnp.concatenate([b, b], axis=1)
        ms = []
        for hp in range(4):
            s2 = jnp.dot(kt, rq_ref[hp], preferred_element_type=F32) + b2
            sc_ref[hp, keys, :] = s2
            ms.append(jnp.max(_fold_rows(s2, jnp.maximum), axis=0, keepdims=True))
        vt = jnp.concatenate([vt_ref[j] for j in range(ns)], axis=1)
        ones_rows = jnp.ones((ONES_ROWS, n_keys), BF16)
        lhs = [jnp.concatenate([vt[g * 64:(g + 1) * 64, :], ones_rows], axis=0) for g in range(N_KV)]
        pieces = []
        for hp in range(4):
            p = jnp.exp2((sc_ref[hp, keys, :] - ms[hp]).astype(BF16))
            o = jnp.dot(lhs[hp // 2], p, preferred_element_type=F32)
            o2 = o[0:HEAD_DIM, :] / o[HEAD_DIM:HEAD_DIM + 1, :]
            pieces.append(o2[:, :LANES])
            pieces.append(o2[:, LANES:])
        o_ref[...] = jnp.concatenate(pieces, axis=0).T

    for ns in range(1, SEQ // KEY_STEP + 1):
        pl.when(n_steps == ns)(functools.partial(process, ns))


def _attn_prompt(qt, qit, wit, kbf, kiw, vt3):
    n_blocks = SEQ // LANES
    grid = (BATCH, n_blocks)
    qmap = lambda n, i: (0, n * n_blocks + i)
    in_specs = [
        pl.BlockSpec((512, LANES), qmap),
        pl.BlockSpec((512, LANES), qmap),
        pl.BlockSpec((8, LANES), qmap),
        pl.BlockSpec((SEQ, 128), lambda n, i: (n, 0)),
        pl.BlockSpec((SEQ, 128), lambda n, i: (n, 0)),
        pl.BlockSpec((SEQ // KEY_STEP, 128, KEY_STEP), lambda n, i: (n, 0, 0)),
    ]
    return pl.pallas_call(
        _attn_prompt_kernel, grid=grid, in_specs=in_specs,
        out_specs=pl.BlockSpec((LANES, W_A), lambda n, i: (n * n_blocks + i, 0)),
        out_shape=jax.ShapeDtypeStruct((BATCH * SEQ, W_A), F32),
        scratch_shapes=[pltpu.VMEM((SEQ, LANES), F32), pltpu.VMEM((4, SEQ, 2 * LANES), F32),
                        pltpu.VMEM((4, 128, 2 * LANES), BF16), pltpu.VMEM((4, 128, 2 * LANES), BF16)],
        compiler_params=pltpu.CompilerParams(dimension_semantics=("arbitrary", "arbitrary"),
                                             vmem_limit_bytes=VMEM_LIMIT),
        name="attn_prompt",
    )(qt, qit, wit, kbf, kiw, vt3)


def _page_copies(pt_ref, group, slot, gsz, streams):
    out = []
    for g in range(gsz):
        seq = group * gsz + g
        for p in range(N_PAGES):
            page = pt_ref[seq, p]
            keys = pl.ds(p * PAGE_SIZE, PAGE_SIZE)
            for hbm, buf, sem in streams:
                out.append(pltpu.make_async_copy(hbm.at[page], buf.at[slot, g, :, keys], sem.at[slot]))
    return out


def _paged_prefetch(pt_ref, gsz, streams):
    step = pl.program_id(0)
    n_steps = pl.num_programs(0)
    slot = step % 2

    @pl.when(step == 0)
    def _():
        tail = pl.ds(PAST_LEN, KEY_TILE)
        for _, buf, _ in streams:
            for s_ in range(2):
                for g in range(gsz):
                    buf[s_, g, :, tail] = jnp.zeros((buf.shape[2], KEY_TILE), F32)
        for c in _page_copies(pt_ref, 0, 0, gsz, streams):
            c.start()

    @pl.when(step + 1 < n_steps)
    def _():
        for c in _page_copies(pt_ref, step + 1, 1 - slot, gsz, streams):
            c.start()

    for c in _page_copies(pt_ref, step, slot, gsz, streams):
        c.wait()
    return slot


def _sample_index_kernel(pt_ref, qi_ref, w_ref, kin_ref, cki_hbm, s_out_ref, kibuf, sem):
    gsz = SAMPLE_IDX_GROUP
    slot = _paged_prefetch(pt_ref, gsz, [(cki_hbm, kibuf, sem)])
    rows8 = lax.broadcasted_iota(jnp.int32, (SAMPLE_Q, SAMPLE_KEYS), 0)
    keyi = lax.broadcasted_iota(jnp.int32, (SAMPLE_Q, SAMPLE_KEYS), 1)
    allowed = keyi <= (PAST_LEN + rows8)
    new_keys = pl.ds(PAST_LEN, DEC_SEQ)
    scores = []
    for g in range(gsz):
        kibuf[slot, g, :, new_keys] = kin_ref[g]
        kib = kibuf[slot, g].astype(BF16)
        s = jnp.dot(qi_ref[g], kib, preferred_element_type=F32)
        sw = jnp.maximum(s, 0.0) * w_ref[g][:, 0:1]
        acc = jnp.sum(sw.reshape(N_IDX_HEADS, SAMPLE_Q, SAMPLE_KEYS), axis=0)
        scores.append(jnp.where(allowed, acc, -jnp.inf))
    for a in range(gsz // 2):
        s_out_ref[a] = jnp.where(rows8 < DEC_SEQ, scores[2 * a],
                                 pltpu.roll(scores[2 * a + 1], DEC_SEQ, 0))


def _sample_select_kernel(s_ref, b_ref, st_ref):
    rows = s_ref.shape[0]
    st_ref[...] = s_ref[...].T
    t4 = lax.broadcasted_iota(jnp.int32, (1, rows), 1) % DEC_SEQ
    _topk_bias_in_place(st_ref, SAMPLE_KEYS, (PAST_LEN + 1 + t4).astype(F32))
    b_ref[...] = st_ref[...].T


def _sample_attend_kernel(pt_ref, q_ref, b_ref, kn_ref, vn_ref, ck_hbm, cv_hbm, o_ref,
                          kbuf, vbuf, ksem, vsem):
    gsz = SAMPLE_GROUP
    slot = _paged_prefetch(pt_ref, gsz, [(ck_hbm, kbuf, ksem), (cv_hbm, vbuf, vsem)])
    new_keys = pl.ds(PAST_LEN, DEC_SEQ)
    lane = lax.broadcasted_iota(jnp.int32, (SAMPLE_Q, LANES), 1)
    for g in range(gsz):
        kbuf[slot, g, :, new_keys] = kn_ref[g]
        vbuf[slot, g, :, new_keys] = vn_ref[g]
        kb = kbuf[slot, g].astype(BF16)
        vb = vbuf[slot, g].astype(BF16)
        b = b_ref[g // 2]
        if g % 2:
            b = pltpu.roll(b, DEC_SEQ, 0)
        s = jnp.dot(q_ref[g], kb, preferred_element_type=F32)
        s = s + jnp.concatenate([b] * N_HEADS, axis=0)
        m = jnp.max(s, axis=1, keepdims=True)
        p = jnp.exp2(s - m)
        l = jnp.sum(p, axis=1, keepdims=True)
        o = lax.dot_general(p.astype(BF16), vb, (((1,), (1,)), ((), ())),
                            preferred_element_type=F32) / l
        o_r = pltpu.roll(o, HEAD_DIM, 1)
        tiles = []
        for c in range(4):
            ha, hb = 2 * c, 2 * c + 1
            src_a = o if ha // 4 == 0 else o_r
            src_b = o if hb // 4 == 1 else o_r
            tiles.append(jnp.where(lane < HEAD_DIM, src_a[ha * SAMPLE_Q:(ha + 1) * SAMPLE_Q, :],
                                   src_b[hb * SAMPLE_Q:(hb + 1) * SAMPLE_Q, :]))
        o_ref[g] = jnp.concatenate(tiles, axis=1)


def _attn_sample(page_table, qi_s, q_s, w_s, ki_new, k_new, v_new, ck, cv, cki):
    params = pltpu.CompilerParams(dimension_semantics=("arbitrary",), vmem_limit_bytes=VMEM_LIMIT)
    blk = lambda n, shape: pl.BlockSpec((n,) + shape, lambda s, pt: (s, 0, 0))
    hbm = pl.BlockSpec(memory_space=pl.ANY)
    n_pairs = DEC_BATCH // 2

    gi = SAMPLE_IDX_GROUP
    scores = pl.pallas_call(
        _sample_index_kernel,
        grid_spec=pltpu.PrefetchScalarGridSpec(
            num_scalar_prefetch=1, grid=(DEC_BATCH // gi,),
            in_specs=[blk(gi, (64, IDX_DIM)), blk(gi, (64, LANES)), blk(gi, (IDX_DIM, DEC_SEQ)), hbm],
            out_specs=blk(gi // 2, (SAMPLE_Q, SAMPLE_KEYS)),
            scratch_shapes=[pltpu.VMEM((2, gi, IDX_DIM, SAMPLE_KEYS), F32),
                            pltpu.SemaphoreType.DMA((2,))]),
        out_shape=jax.ShapeDtypeStruct((n_pairs, SAMPLE_Q, SAMPLE_KEYS), F32),
        compiler_params=params, name="sample_index",
    )(page_table, qi_s, w_s, ki_new, cki)

    rows = n_pairs * SAMPLE_Q
    bias = pl.pallas_call(
        _sample_select_kernel, grid=(1,),
        in_specs=[pl.BlockSpec((rows, SAMPLE_KEYS), lambda s: (0, 0))],
        out_specs=pl.BlockSpec((rows, SAMPLE_KEYS), lambda s: (0, 0)),
        out_shape=jax.ShapeDtypeStruct((rows, SAMPLE_KEYS), F32),
        scratch_shapes=[pltpu.VMEM((SAMPLE_KEYS, rows), F32)],
        compiler_params=params, name="sample_select",
    )(scores.reshape(rows, SAMPLE_KEYS)).reshape(n_pairs, SAMPLE_Q, SAMPLE_KEYS)

    ga = SAMPLE_GROUP
    return pl.pallas_call(
        _sample_attend_kernel,
        grid_spec=pltpu.PrefetchScalarGridSpec(
            num_scalar_prefetch=1, grid=(DEC_BATCH // ga,),
            in_specs=[blk(ga, (64, 128)), blk(ga // 2, (SAMPLE_Q, SAMPLE_KEYS)),
                      blk(ga, (128, DEC_SEQ)), blk(ga, (128, DEC_SEQ)), hbm, hbm],
            out_specs=blk(ga, (SAMPLE_Q, W_A)),
            scratch_shapes=[pltpu.VMEM((2, ga, 128, SAMPLE_KEYS), F32),
                            pltpu.VMEM((2, ga, 128, SAMPLE_KEYS), F32),
                            pltpu.SemaphoreType.DMA((2,)), pltpu.SemaphoreType.DMA((2,))]),
        out_shape=jax.ShapeDtypeStruct((DEC_BATCH, SAMPLE_Q, W_A), F32),
        compiler_params=params, name="sample_attend",
    )(page_table, q_s, bias, k_new, v_new, ck, cv)


def _layer_norm(x, g, b):
    mu = jnp.mean(x, axis=-1, keepdims=True)
    xc = x - mu
    var = jnp.mean(xc * xc, axis=-1, keepdims=True)
    return xc * lax.rsqrt(var + LN_EPS) * g + b


def _silu(x):
    return x / (1.0 + jnp.exp(-x))


def _mixer_kernel(x_ref, a_ref, wb_ref, wmix_ref, bsb_ref, lnvg_ref, lnvb_ref, wout_ref,
                  lng_ref, lnb_ref, y_ref, vn_ref, *, chunk, vn_last_only, tiles_per_seq):
    tm = x_ref.shape[0]
    x = x_ref[...]
    zb = lax.dot_general(x.astype(BF16), wb_ref[...], (((1,), (1,)), ((), ())),
                         preferred_element_type=F32)
    ga = zb[:, 0:512]
    u = zb[:, 512:1024]
    vb = zb[:, 1024:1536]
    gb = zb[:, 1536:2048]
    vn = _layer_norm(vb, lnvg_ref[...], lnvb_ref[...])
    if vn_last_only:
        @pl.when(pl.program_id(0) % tiles_per_seq == tiles_per_seq - 1)
        def _():
            vn_ref[0] = vn[tm - CHUNK:, :]
    else:
        vn_ref[...] = vn
    vnb = vn.astype(BF16)
    rows = []
    for c in range(tm // chunk):
        cols = []
        for g in range(N_GROUPS_B):
            blk = vnb[c * chunk:(c + 1) * chunk, g * GROUP_W_B:(g + 1) * GROUP_W_B]
            cols.append(jnp.dot(wmix_ref[g], blk, preferred_element_type=F32) + bsb_ref[g])
        rows.append(jnp.concatenate(cols, axis=1))
    mixed = jnp.concatenate(rows, axis=0) if len(rows) > 1 else rows[0]
    a_out = a_ref[...] * _silu(ga)
    b_out = u * mixed * _silu(gb)
    cat = jnp.concatenate([a_out, b_out], axis=1).astype(BF16)
    out = jnp.dot(cat, wout_ref[...], preferred_element_type=F32)
    y_ref[...] = _layer_norm(ALPHA * x + out, lng_ref[...], lnb_ref[...])


def _mixer(x2d, attn, wb, wmix, bsb, lnvg, lnvb, wout, lng, lnb, *, chunk, vn_last_only):
    rows = x2d.shape[0]
    tm = ROW_TILE
    tiles_per_seq = SEQ // tm
    grid = (rows // tm,)
    row_map = lambda r: (r, 0)
    const2 = lambda r: (0, 0)
    const3 = lambda r: (0, 0, 0)
    if vn_last_only:
        vn_shape = jax.ShapeDtypeStruct((rows // SEQ, CHUNK, W_B), F32)
        vn_spec = pl.BlockSpec((1, CHUNK, W_B), lambda r: (r // tiles_per_seq, 0, 0))
    else:
        vn_shape = jax.ShapeDtypeStruct((rows, W_B), F32)
        vn_spec = pl.BlockSpec((tm, W_B), row_map)
    kern = functools.partial(_mixer_kernel, chunk=chunk, vn_last_only=vn_last_only,
                             tiles_per_seq=tiles_per_seq)
    return pl.pallas_call(
        kern, grid=grid,
        in_specs=[pl.BlockSpec((tm, D_MODEL), row_map), pl.BlockSpec((tm, W_A), row_map),
                  pl.BlockSpec(wb.shape, const2), pl.BlockSpec(wmix.shape, const3),
                  pl.BlockSpec(bsb.shape, const3), pl.BlockSpec((1, W_B), const2),
                  pl.BlockSpec((1, W_B), const2), pl.BlockSpec(wout.shape, const2),
                  pl.BlockSpec((1, D_MODEL), const2), pl.BlockSpec((1, D_MODEL), const2)],
        out_specs=(pl.BlockSpec((tm, D_MODEL), row_map), vn_spec),
        out_shape=(jax.ShapeDtypeStruct((rows, D_MODEL), F32), vn_shape),
        compiler_params=pltpu.CompilerParams(dimension_semantics=("arbitrary",),
                                             vmem_limit_bytes=VMEM_LIMIT),
        name="mixer_chunk%d" % chunk,
    )(x2d, attn, wb, wmix, bsb, lnvg, lnvb, wout, lng, lnb)


def kernel(x_prompt, x_sample, cache_k, cache_v, cache_k_idx, page_table, w_in, w_s, b_s,
           ln_v_g, ln_v_b, w_out, ln_g, ln_b):
    w = w_in[0]
    wq, wk, wv = w[:, 0:512], w[:, 512:640], w[:, 640:768]
    wga, wqi, wki, wwi = w[:, 768:1280], w[:, 1280:1792], w[:, 1792:1856], w[:, 1856:1864]
    wu, wvb, wgb = w[:, 1864:2376], w[:, 2376:2888], w[:, 2888:3400]
    wstd = jnp.concatenate([wk.T, wki.T, jnp.zeros((64, D_MODEL), F32)], axis=0).astype(BF16)
    wt = jnp.concatenate([wq.T * Q_SCALE, wqi.T, wv.T, wk.T, wki.T, wwi.T,
                          jnp.zeros((WT_ROWS - WT_WI - N_IDX_HEADS, D_MODEL), F32)], axis=0).astype(BF16)
    wb = jnp.concatenate([wga.T, wu.T, wvb.T, wgb.T], axis=0).astype(BF16)
    wout = w_out[0].astype(BF16)
    lnvg, lnvb = ln_v_g[0][None, :], ln_v_b[0][None, :]
    lng, lnb = ln_g[0][None, :], ln_b[0][None, :]

    xp = x_prompt.reshape(BATCH * SEQ, D_MODEL)
    tabs_p = _rope_tables(np.arange(SEQ))
    kt_p, vt_p, kit_p, kbf, kiw, qt, qit, vt3, wit = _project(xp, wstd, wt, *tabs_p, SEQ)
    attn_p = _attn_prompt(qt, qit, wit, kbf, kiw, vt3)
    tril = np.tril(np.ones((CHUNK, CHUNK), np.float32))
    wmix_p = (w_s[0] * tril[None]).astype(BF16)
    bsb_p = jnp.broadcast_to(b_s[0][:, :, None], (N_GROUPS_B, CHUNK, GROUP_W_B))
    y_p, vn_p = _mixer(xp, attn_p, wb, wmix_p, bsb_p, lnvg, lnvb, wout, lng, lnb,
                       chunk=CHUNK, vn_last_only=True)

    rows_s = DEC_BATCH * DEC_SEQ
    xs = x_sample.reshape(rows_s, D_MODEL)
    tabs_s = _rope_tables(PAST_LEN + (np.arange(rows_s) % DEC_SEQ))
    kt_s, vt_s, kit_s, _, _, qt_s, qit_s, _, wit_s = _project(xs, wstd, wt, *tabs_s, rows_s)

    def to_rows(a_t):
        a = a_t.reshape(N_HEADS, HEAD_DIM, DEC_BATCH, DEC_SEQ).transpose(2, 0, 3, 1)
        return jnp.pad(a, ((0, 0), (0, 0), (0, SAMPLE_Q - DEC_SEQ), (0, 0)))

    qi_rows = to_rows(qit_s).reshape(DEC_BATCH, N_IDX_HEADS * SAMPLE_Q, IDX_DIM)
    q5 = to_rows(qt_s).reshape(DEC_BATCH, N_KV, N_HEADS // N_KV, SAMPLE_Q, HEAD_DIM)
    zq = jnp.zeros_like(q5[:, 0])
    q_rows = jnp.stack([jnp.concatenate([q5[:, 0], zq], axis=-1),
                        jnp.concatenate([zq, q5[:, 1]], axis=-1)], axis=1)
    q_rows = q_rows.reshape(DEC_BATCH, N_HEADS * SAMPLE_Q, 2 * HEAD_DIM)
    w_rows = jnp.pad(wit_s.reshape(N_IDX_HEADS, DEC_BATCH, DEC_SEQ).transpose(1, 0, 2),
                     ((0, 0), (0, 0), (0, SAMPLE_Q - DEC_SEQ)))
    w_rows = jnp.broadcast_to(w_rows.reshape(DEC_BATCH, N_IDX_HEADS * SAMPLE_Q, 1),
                              (DEC_BATCH, N_IDX_HEADS * SAMPLE_Q, LANES))
    new_cols = lambda a: a.reshape(a.shape[0], DEC_BATCH, DEC_SEQ).transpose(1, 0, 2)
    n_pool = cache_k.shape[1]
    ck_t = cache_k[0].transpose(0, 2, 3, 1).reshape(n_pool, N_KV * HEAD_DIM, PAGE_SIZE)
    cv_t = cache_v[0].transpose(0, 2, 3, 1).reshape(n_pool, N_KV * HEAD_DIM, PAGE_SIZE)
    cki_t = cache_k_idx[0].transpose(0, 2, 1)
    attn_s8 = _attn_sample(page_table, qi_rows, q_rows, w_rows, new_cols(kit_s[0]),
                           new_cols(kt_s[0]), new_cols(vt_s[0]), ck_t, cv_t, cki_t)
    attn_s = attn_s8[:, :DEC_SEQ, :].reshape(rows_s, W_A)
    ri = np.arange(rows_s)
    same_seq = (ri[:, None] // DEC_SEQ) == (ri[None, :] // DEC_SEQ)
    wmix_s = jnp.zeros((N_GROUPS_B, rows_s, rows_s), F32)
    for t in range(DEC_SEQ):
        for s in range(t + 1):
            hit = same_seq & (ri[:, None] % DEC_SEQ == t) & (ri[None, :] % DEC_SEQ == s)
            wmix_s = wmix_s + jnp.where(hit[None], w_s[0, :, t, s][:, None, None], 0.0)
    wmix_s = wmix_s.astype(BF16)
    bsb_s = jnp.broadcast_to(jnp.tile(b_s[0][:, :DEC_SEQ], (1, DEC_BATCH))[:, :, None],
                             (N_GROUPS_B, rows_s, GROUP_W_B))
    y_s, vn_s = _mixer(xs, attn_s, wb, wmix_s, bsb_s, lnvg, lnvb, wout, lng, lnb,
                       chunk=rows_s, vn_last_only=False)

    def heads_last(a, n, t):
        a = a.reshape(a.shape[0], N_KV, HEAD_DIM, a.shape[2]).transpose(0, 3, 1, 2)
        return a.reshape(1, n, t, N_KV, HEAD_DIM)

    return (y_p.reshape(BATCH, SEQ, D_MODEL),
            y_s.reshape(DEC_BATCH, DEC_SEQ, D_MODEL),
            heads_last(kt_p, BATCH, SEQ), heads_last(vt_p, BATCH, SEQ),
            kit_p.transpose(0, 2, 1)[None],
            vn_p.reshape(1, BATCH, CHUNK, W_B),
            heads_last(kt_s, DEC_BATCH, DEC_SEQ), heads_last(vt_s, DEC_BATCH, DEC_SEQ),
            kit_s[0].T.reshape(1, DEC_BATCH, DEC_SEQ, IDX_DIM),
            vn_s.reshape(1, DEC_BATCH, DEC_SEQ, W_B))
```

```python
import functools

import jax
import jax.numpy as jnp
import numpy as np
from jax import lax
from jax.experimental import pallas as pl
from jax.experimental.pallas import tpu as pltpu

F32 = jnp.float32
BF16 = jnp.bfloat16

D_MODEL = 1024
SEQ = 2048
BATCH = 8
DEC_BATCH = 128
DEC_SEQ = 4
PAST_LEN = 2048
PAGE_SIZE = 128
N_PAGES = PAST_LEN // PAGE_SIZE
W_A = 512
W_B = 512
HEAD_DIM = 64
N_HEADS = 8
N_KV = 2
N_IDX_HEADS = 8
IDX_DIM = 64
TOPK = 256
CHUNK = 128
N_GROUPS_B = 4
GROUP_W_B = 128
ROPE_THETA = 10000.0
LN_EPS = 1e-5
ALPHA = 2.0 ** 0.25
NEG = -1e30
IDX_SCALE = float((N_IDX_HEADS * IDX_DIM) ** -0.5)

LANES = 128
KEY_TILE = 128
KEY_STEP = 256
ROW_TILE = 512
WT_Q, WT_QI, WT_V, WT_K, WT_KI, WT_WI, WT_ROWS = 0, 512, 1024, 1152, 1280, 1344, 1360
BISECT_ITERS = 20
FOLD_CHAINS = 8
ONES_ROWS = 16
Q_SCALE = float(HEAD_DIM ** -0.5 * 1.4426950408889634)
VMEM_LIMIT = 48 * 1024 * 1024

SAMPLE_GROUP = 4
SAMPLE_IDX_GROUP = 8
SAMPLE_KEYS = PAST_LEN + KEY_TILE
SAMPLE_Q = 8


def _proj_kernel(x_ref, wstd_ref, wt_ref, cosl_ref, sinl_ref, cost_ref, sint_ref,
                 kt_ref, vt32_ref, kit_ref, kbf_ref, kiw_ref, qt_ref, qit_ref, vt_ref, wit_ref):
    tm = x_ref.shape[0]
    xb = x_ref[...].astype(BF16)
    zs = lax.dot_general(xb, wstd_ref[...], (((1,), (1,)), ((), ())),
                         preferred_element_type=F32)
    cosl = cosl_ref[...]
    sinl = sinl_ref[...]
    lane = lax.broadcasted_iota(jnp.int32, (tm, LANES), 1)
    first_half = (lane % HEAD_DIM) < (HEAD_DIM // 2)

    def rope_lanes(z):
        partner = jnp.where(first_half, pltpu.roll(z, 96, 1), pltpu.roll(z, 32, 1))
        return z * cosl + partner * sinl

    kbf_ref[...] = rope_lanes(zs[:, 0:128]).astype(BF16)
    kiw_ref[...] = rope_lanes(zs[:, 128:256]).astype(BF16)

    zt = lax.dot_general(wt_ref[...], xb, (((1,), (1,)), ((), ())),
                         preferred_element_type=F32)
    cost = cost_ref[...]
    sint = sint_ref[...]
    half = HEAD_DIM // 2

    def rope_rows(base, n_heads):
        pieces = []
        for h in range(n_heads):
            x1 = zt[base + h * HEAD_DIM: base + h * HEAD_DIM + half]
            x2 = zt[base + h * HEAD_DIM + half: base + (h + 1) * HEAD_DIM]
            pieces.append(x1 * cost - x2 * sint)
            pieces.append(x2 * cost + x1 * sint)
        return jnp.concatenate(pieces, axis=0)

    qt_ref[...] = rope_rows(WT_Q, N_HEADS).astype(BF16)
    qit_ref[...] = rope_rows(WT_QI, N_IDX_HEADS).astype(BF16)
    v_t = zt[WT_V:WT_V + 128]
    vt32_ref[0] = v_t
    vtb = v_t.astype(BF16)
    for c in range(tm // KEY_STEP):
        vt_ref[c] = vtb[:, c * KEY_STEP:(c + 1) * KEY_STEP]
    kt_ref[0] = rope_rows(WT_K, N_KV)
    kit_ref[0] = rope_rows(WT_KI, 1)
    wit_ref[...] = zt[WT_WI:WT_WI + N_IDX_HEADS] * IDX_SCALE


def _project(x2d, wstd, wt, cosl, sinl, cost, sint, seq_len):
    rows = x2d.shape[0]
    tm = ROW_TILE
    grid = (rows // tm,)
    n_pos_tiles = seq_len // tm
    row_map = lambda r: (r, 0)
    col_map = lambda r: (0, r)
    const = lambda r: (0, 0)
    seq_map = lambda r: (r // n_pos_tiles, 0, r % n_pos_tiles)
    out_shape = (
        jax.ShapeDtypeStruct((rows // seq_len, 128, seq_len), F32),
        jax.ShapeDtypeStruct((rows // seq_len, 128, seq_len), F32),
        jax.ShapeDtypeStruct((rows // seq_len, IDX_DIM, seq_len), F32),
        jax.ShapeDtypeStruct((rows, 128), BF16),
        jax.ShapeDtypeStruct((rows, 128), BF16),
        jax.ShapeDtypeStruct((512, rows), BF16),
        jax.ShapeDtypeStruct((512, rows), BF16),
        jax.ShapeDtypeStruct((rows // KEY_STEP, 128, KEY_STEP), BF16),
        jax.ShapeDtypeStruct((8, rows), F32),
    )
    out_specs = (
        pl.BlockSpec((1, 128, tm), seq_map),
        pl.BlockSpec((1, 128, tm), seq_map),
        pl.BlockSpec((1, IDX_DIM, tm), seq_map),
        pl.BlockSpec((tm, 128), row_map),
        pl.BlockSpec((tm, 128), row_map),
        pl.BlockSpec((512, tm), col_map),
        pl.BlockSpec((512, tm), col_map),
        pl.BlockSpec((tm // KEY_STEP, 128, KEY_STEP), lambda r: (r, 0, 0)),
        pl.BlockSpec((8, tm), col_map),
    )
    in_specs = [
        pl.BlockSpec((tm, D_MODEL), row_map),
        pl.BlockSpec(wstd.shape, const),
        pl.BlockSpec(wt.shape, const),
        pl.BlockSpec((tm, LANES), lambda r: (r % n_pos_tiles, 0)),
        pl.BlockSpec((tm, LANES), lambda r: (r % n_pos_tiles, 0)),
        pl.BlockSpec((HEAD_DIM // 2, tm), lambda r: (0, r % n_pos_tiles)),
        pl.BlockSpec((HEAD_DIM // 2, tm), lambda r: (0, r % n_pos_tiles)),
    ]
    return pl.pallas_call(
        _proj_kernel, grid=grid, in_specs=in_specs, out_specs=out_specs, out_shape=out_shape,
        compiler_params=pltpu.CompilerParams(dimension_semantics=("arbitrary",),
                                             vmem_limit_bytes=VMEM_LIMIT),
        name="proj_attn_side",
    )(x2d, wstd, wt, cosl, sinl, cost, sint)


def _rope_tables(pos):
    half = HEAD_DIM // 2
    inv = ROPE_THETA ** (-np.arange(half, dtype=np.float64) / half)
    ang = pos.astype(np.float64)[:, None] * inv[None, :]
    cos = np.cos(ang)
    sin = np.sin(ang)
    cosl = np.tile(cos, (1, LANES // half))
    sinl = np.tile(np.concatenate([-sin, sin], axis=1), (1, LANES // HEAD_DIM))
    return tuple(jnp.asarray(t, dtype=F32) for t in (cosl, sinl, cos.T, sin.T))


def _select_threshold(count_ge, kth_tie_index, min_ge, max_lt, lo0, hi0, n_allowed, n_keys):
    k = float(TOPK)

    def split(mid, lo, hi, clo, chi):
        c = count_ge(mid)
        take = c >= k
        return (jnp.where(take, mid, lo), jnp.where(take, hi, mid),
                jnp.where(take, c, clo), jnp.where(take, chi, c))

    def step(_, carry):
        lo, hi, clo, chi = carry
        return split(0.5 * lo + 0.5 * hi, lo, hi, clo, chi)

    lo, hi, clo, chi = lax.fori_loop(0, BISECT_ITERS, step,
                                     (lo0, hi0, n_allowed, jnp.zeros_like(lo0)))
    big = jnp.full_like(lo, float(4 * n_keys))

    def any_true(mask):
        return jnp.max(jnp.where(mask, 1.0, 0.0)) > 0.5

    def exact(args):
        lo, hi, clo, chi = args

        def cond(c):
            _, _, clo, _, vlo, vhi, it = c
            open_ = jnp.logical_and(clo > k, vlo < vhi)
            return jnp.logical_and(any_true(open_), it < 2 * n_keys)

        def body(c):
            lo, hi, clo, chi, vlo, vhi, it = c
            mid = 0.5 * vlo + 0.5 * vhi
            lo, hi, clo, chi = split(jnp.where(mid > vlo, mid, vhi), lo, hi, clo, chi)
            return lo, hi, clo, chi, min_ge(lo), max_lt(hi), it + 1

        lo, hi, clo, chi, vlo, _, _ = lax.while_loop(
            cond, body, (lo, hi, clo, chi, min_ge(lo), max_lt(hi), jnp.int32(0)))
        return vlo, jnp.where(clo > k, kth_tie_index(vlo, k - chi), big)

    def fast(args):
        return args[0], big

    return lax.cond(any_true(clo > k), exact, fast, (lo, hi, clo, chi))


def _fold_rows(x, op):
    groups = [x[r:r + 8] for r in range(0, x.shape[0], 8)]
    parts = groups[:FOLD_CHAINS]
    for n, grp in enumerate(groups[FOLD_CHAINS:]):
        parts[n % FOLD_CHAINS] = op(parts[n % FOLD_CHAINS], grp)
    while len(parts) > 1:
        nxt = [op(parts[a], parts[a + 1]) for a in range(0, len(parts) - 1, 2)]
        if len(parts) % 2:
            nxt.append(parts[-1])
        parts = nxt
    return parts[0]


def _topk_bias_in_place(s_ref, n_keys, n_allowed, lo0=None, hi0=None):
    lanes = s_ref.shape[1]
    inf = jnp.float32(jnp.inf)
    chunk = 8 * FOLD_CHAINS
    kidx0 = lax.broadcasted_iota(jnp.int32, (chunk, lanes), 0).astype(F32)

    def key_reduce(fn, op, finish):
        part = None
        for r in range(0, n_keys, chunk):
            v = fn(s_ref[r:r + chunk, :], r)
            part = v if part is None else op(part, v)
        return finish(_fold_rows(part, op), axis=0, keepdims=True)

    count_ge = lambda thr: key_reduce(lambda t, r: jnp.where(t >= thr, 1.0, 0.0), jnp.add, jnp.sum)
    tri = (lax.broadcasted_iota(jnp.int32, (chunk, chunk), 0)
           >= lax.broadcasted_iota(jnp.int32, (chunk, chunk), 1)).astype(BF16)

    def kth_tie_index(thr, need):
        seen = jnp.zeros((1, lanes), F32)
        best = jnp.full((1, lanes), -1.0, F32)
        for r in range(0, n_keys, chunk):
            tie = jnp.where(s_ref[r:r + chunk, :] == thr, 1.0, 0.0)
            upto = jnp.dot(tri, tie.astype(BF16), preferred_element_type=F32) + seen
            hit = jnp.where(tie > 0.5, jnp.where(upto == need, kidx0 + r, -1.0), -1.0)
            best = jnp.maximum(best, jnp.max(_fold_rows(hit, jnp.maximum), axis=0, keepdims=True))
            seen = upto[chunk - 1:chunk, :]
        return best
    min_ge = lambda thr: key_reduce(lambda t, r: jnp.where(t >= thr, t, inf), jnp.minimum, jnp.min)
    max_lt = lambda thr: key_reduce(lambda t, r: jnp.where(t < thr, t, -inf), jnp.maximum, jnp.max)
    if lo0 is None:
        lo0 = key_reduce(lambda t, r: jnp.where(t == -inf, inf, t), jnp.minimum, jnp.min)
        mx1 = key_reduce(lambda t, r: t, jnp.maximum, jnp.max)
        hi0 = mx1 + (jnp.abs(mx1) * (2.0 ** -20) + 1e-30)
    lo, jmax = _select_threshold(count_ge, kth_tie_index, min_ge, max_lt, lo0, hi0, n_allowed, n_keys)
    for r in range(0, n_keys, chunk):
        t = s_ref[r:r + chunk, :]
        tie = jnp.where(t == lo, jnp.where(kidx0 <= jmax - r, 0.0, NEG), NEG)
        s_ref[r:r + chunk, :] = jnp.where(t > lo, 0.0, tie)


def _attn_prompt_kernel(qt_ref, qit_ref, wit_ref, kbf_ref, kiw_ref, vt_ref, o_ref,
                        s_ref, sc_ref, ri_ref, rq_ref):
    i = pl.program_id(1)
    w = wit_ref[...]
    zeros_half = jnp.zeros((HEAD_DIM, LANES), BF16)

    for hp in range(4):
        a = qit_ref[(2 * hp) * 64:(2 * hp + 1) * 64, :]
        b = qit_ref[(2 * hp + 1) * 64:(2 * hp + 2) * 64, :]
        ri_ref[hp] = jnp.concatenate(
            [jnp.concatenate([a, zeros_half], axis=0), jnp.concatenate([b, zeros_half], axis=0)], axis=1)
        g = hp // 2
        a = qt_ref[(2 * hp) * 64:(2 * hp + 1) * 64, :]
        b = qt_ref[(2 * hp + 1) * 64:(2 * hp + 2) * 64, :]
        if g == 0:
            ab = jnp.concatenate([jnp.concatenate([a, zeros_half], axis=0),
                                  jnp.concatenate([b, zeros_half], axis=0)], axis=1)
        else:
            ab = jnp.concatenate([jnp.concatenate([zeros_half, a], axis=0),
                                  jnp.concatenate([zeros_half, b], axis=0)], axis=1)
        rq_ref[hp] = ab

    inf = jnp.float32(jnp.inf)
    n_steps = (i * LANES + LANES + KEY_STEP - 1) // KEY_STEP

    def process(ns):
        n_keys = ns * KEY_STEP
        keys = slice(0, n_keys)
        row = lax.broadcasted_iota(jnp.int32, (n_keys, LANES), 0)
        col = lax.broadcasted_iota(jnp.int32, (n_keys, LANES), 1)
        qpos = col + i * LANES
        allowed = row <= qpos

        kt = kiw_ref[keys, :]
        acc = jnp.zeros((n_keys, LANES), F32)
        for hp in range(4):
            s2 = jnp.dot(kt, ri_ref[hp], preferred_element_type=F32)
            acc = acc + jnp.maximum(s2[:, :LANES], 0.0) * w[2 * hp:2 * hp + 1, :]
            acc = acc + jnp.maximum(s2[:, LANES:], 0.0) * w[2 * hp + 1:2 * hp + 2, :]
        s_ref[keys, :] = jnp.where(allowed, acc, -inf)
        lo0 = jnp.min(_fold_rows(jnp.where(allowed, acc, inf), jnp.minimum), axis=0, keepdims=True)
        mx1 = jnp.max(_fold_rows(jnp.where(allowed, acc, -inf), jnp.maximum), axis=0, keepdims=True)
        hi0 = mx1 + (jnp.abs(mx1) * (2.0 ** -20) + 1e-30)
        n_allowed = (qpos[0:1, :] + 1).astype(F32)

        _topk_bias_in_place(s_ref, n_keys, n_allowed, lo0, hi0)

        kt = kbf_ref[keys, :]
        b = s_ref[keys, :]
        b2 = jnp.concatenate([b, b], axis=1)
        ms = []
        for hp in range(4):
            s2 = jnp.dot(kt, rq_ref[hp], preferred_element_type=F32) + b2
            sc_ref[hp, keys, :] = s2
            ms.append(jnp.max(_fold_rows(s2, jnp.maximum), axis=0, keepdims=True))
        vt = jnp.concatenate([vt_ref[j] for j in range(ns)], axis=1)
        ones_rows = jnp.ones((ONES_ROWS, n_keys), BF16)
        lhs = [jnp.concatenate([vt[g * 64:(g + 1) * 64, :], ones_rows], axis=0) for g in range(N_KV)]
        pieces = []
        for hp in range(4):
            p = jnp.exp2((sc_ref[hp, keys, :] - ms[hp]).astype(BF16))
            o = jnp.dot(lhs[hp // 2], p, preferred_element_type=F32)
            o2 = o[0:HEAD_DIM, :] / o[HEAD_DIM:HEAD_DIM + 1, :]
            pieces.append(o2[:, :LANES])
            pieces.append(o2[:, LANES:])
        o_ref[...] = jnp.concatenate(pieces, axis=0)

    for ns in range(1, SEQ // KEY_STEP + 1):
        pl.when(n_steps == ns)(functools.partial(process, ns))


def _attn_prompt(qt, qit, wit, kbf, kiw, vt3):
    n_blocks = SEQ // LANES
    grid = (BATCH, n_blocks)
    qmap = lambda n, i: (0, n * n_blocks + i)
    in_specs = [
        pl.BlockSpec((512, LANES), qmap),
        pl.BlockSpec((512, LANES), qmap),
        pl.BlockSpec((8, LANES), qmap),
        pl.BlockSpec((SEQ, 128), lambda n, i: (n, 0)),
        pl.BlockSpec((SEQ, 128), lambda n, i: (n, 0)),
        pl.BlockSpec((SEQ // KEY_STEP, 128, KEY_STEP), lambda n, i: (n, 0, 0)),
    ]
    return pl.pallas_call(
        _attn_prompt_kernel, grid=grid, in_specs=in_specs,
        out_specs=pl.BlockSpec((W_A, LANES), qmap),
        out_shape=jax.ShapeDtypeStruct((W_A, BATCH * SEQ), F32),
        scratch_shapes=[pltpu.VMEM((SEQ, LANES), F32), pltpu.VMEM((4, SEQ, 2 * LANES), F32),
                        pltpu.VMEM((4, 128, 2 * LANES), BF16), pltpu.VMEM((4, 128, 2 * LANES), BF16)],
        compiler_params=pltpu.CompilerParams(dimension_semantics=("arbitrary", "arbitrary"),
                                             vmem_limit_bytes=VMEM_LIMIT),
        name="attn_prompt",
    )(qt, qit, wit, kbf, kiw, vt3)


def _page_copies(pt_ref, group, slot, gsz, streams):
    out = []
    for g in range(gsz):
        seq = group * gsz + g
        for p in range(N_PAGES):
            page = pt_ref[seq, p]
            keys = pl.ds(p * PAGE_SIZE, PAGE_SIZE)
            for hbm, buf, sem in streams:
                out.append(pltpu.make_async_copy(hbm.at[page], buf.at[slot, g, :, keys], sem.at[slot]))
    return out


def _paged_prefetch(pt_ref, gsz, streams):
    step = pl.program_id(0)
    n_steps = pl.num_programs(0)
    slot = step % 2

    @pl.when(step == 0)
    def _():
        tail = pl.ds(PAST_LEN, KEY_TILE)
        for _, buf, _ in streams:
            for s_ in range(2):
                for g in range(gsz):
                    buf[s_, g, :, tail] = jnp.zeros((buf.shape[2], KEY_TILE), F32)
        for c in _page_copies(pt_ref, 0, 0, gsz, streams):
            c.start()

    @pl.when(step + 1 < n_steps)
    def _():
        for c in _page_copies(pt_ref, step + 1, 1 - slot, gsz, streams):
            c.start()

    for c in _page_copies(pt_ref, step, slot, gsz, streams):
        c.wait()
    return slot


def _sample_index_kernel(pt_ref, qi_ref, w_ref, kin_ref, cki_hbm, s_out_ref, kibuf, sem):
    gsz = SAMPLE_IDX_GROUP
    slot = _paged_prefetch(pt_ref, gsz, [(cki_hbm, kibuf, sem)])
    rows8 = lax.broadcasted_iota(jnp.int32, (SAMPLE_Q, SAMPLE_KEYS), 0)
    keyi = lax.broadcasted_iota(jnp.int32, (SAMPLE_Q, SAMPLE_KEYS), 1)
    allowed = keyi <= (PAST_LEN + rows8)
    new_keys = pl.ds(PAST_LEN, DEC_SEQ)
    scores = []
    for g in range(gsz):
        kibuf[slot, g, :, new_keys] = kin_ref[g]
        kib = kibuf[slot, g].astype(BF16)
        s = jnp.dot(qi_ref[g], kib, preferred_element_type=F32)
        sw = jnp.maximum(s, 0.0) * w_ref[g][:, 0:1]
        acc = jnp.sum(sw.reshape(N_IDX_HEADS, SAMPLE_Q, SAMPLE_KEYS), axis=0)
        scores.append(jnp.where(allowed, acc, -jnp.inf))
    for a in range(gsz // 2):
        s_out_ref[a] = jnp.where(rows8 < DEC_SEQ, scores[2 * a],
                                 pltpu.roll(scores[2 * a + 1], DEC_SEQ, 0))


def _sample_select_kernel(s_ref, b_ref, st_ref):
    rows = s_ref.shape[0]
    st_ref[...] = s_ref[...].T
    t4 = lax.broadcasted_iota(jnp.int32, (1, rows), 1) % DEC_SEQ
    _topk_bias_in_place(st_ref, SAMPLE_KEYS, (PAST_LEN + 1 + t4).astype(F32))
    b_ref[...] = st_ref[...].T


def _sample_attend_kernel(pt_ref, q_ref, b_ref, kn_ref, vn_ref, ck_hbm, cv_hbm, o_ref,
                          kbuf, vbuf, ksem, vsem):
    gsz = SAMPLE_GROUP
    slot = _paged_prefetch(pt_ref, gsz, [(ck_hbm, kbuf, ksem), (cv_hbm, vbuf, vsem)])
    new_keys = pl.ds(PAST_LEN, DEC_SEQ)
    lane = lax.broadcasted_iota(jnp.int32, (SAMPLE_Q, LANES), 1)
    for g in range(gsz):
        kbuf[slot, g, :, new_keys] = kn_ref[g]
        vbuf[slot, g, :, new_keys] = vn_ref[g]
        kb = kbuf[slot, g].astype(BF16)
        vb = vbuf[slot, g].astype(BF16)
        b = b_ref[g // 2]
        if g % 2:
            b = pltpu.roll(b, DEC_SEQ, 0)
        s = jnp.dot(q_ref[g], kb, preferred_element_type=F32)
        s = s + jnp.concatenate([b] * N_HEADS, axis=0)
        m = jnp.max(s, axis=1, keepdims=True)
        p = jnp.exp2(s - m)
        l = jnp.sum(p, axis=1, keepdims=True)
        o = lax.dot_general(p.astype(BF16), vb, (((1,), (1,)), ((), ())),
                            preferred_element_type=F32) / l
        o_r = pltpu.roll(o, HEAD_DIM, 1)
        tiles = []
        for c in range(4):
            ha, hb = 2 * c, 2 * c + 1
            src_a = o if ha // 4 == 0 else o_r
            src_b = o if hb // 4 == 1 else o_r
            tiles.append(jnp.where(lane < HEAD_DIM, src_a[ha * SAMPLE_Q:(ha + 1) * SAMPLE_Q, :],
                                   src_b[hb * SAMPLE_Q:(hb + 1) * SAMPLE_Q, :]))
        o_ref[g] = jnp.concatenate(tiles, axis=1)


def _attn_sample(page_table, qi_s, q_s, w_s, ki_new, k_new, v_new, ck, cv, cki):
    params = pltpu.CompilerParams(dimension_semantics=("arbitrary",), vmem_limit_bytes=VMEM_LIMIT)
    blk = lambda n, shape: pl.BlockSpec((n,) + shape, lambda s, pt: (s, 0, 0))
    hbm = pl.BlockSpec(memory_space=pl.ANY)
    n_pairs = DEC_BATCH // 2

    gi = SAMPLE_IDX_GROUP
    scores = pl.pallas_call(
        _sample_index_kernel,
        grid_spec=pltpu.PrefetchScalarGridSpec(
            num_scalar_prefetch=1, grid=(DEC_BATCH // gi,),
            in_specs=[blk(gi, (64, IDX_DIM)), blk(gi, (64, LANES)), blk(gi, (IDX_DIM, DEC_SEQ)), hbm],
            out_specs=blk(gi // 2, (SAMPLE_Q, SAMPLE_KEYS)),
            scratch_shapes=[pltpu.VMEM((2, gi, IDX_DIM, SAMPLE_KEYS), F32),
                            pltpu.SemaphoreType.DMA((2,))]),
        out_shape=jax.ShapeDtypeStruct((n_pairs, SAMPLE_Q, SAMPLE_KEYS), F32),
        compiler_params=params, name="sample_index",
    )(page_table, qi_s, w_s, ki_new, cki)

    rows = n_pairs * SAMPLE_Q
    bias = pl.pallas_call(
        _sample_select_kernel, grid=(1,),
        in_specs=[pl.BlockSpec((rows, SAMPLE_KEYS), lambda s: (0, 0))],
        out_specs=pl.BlockSpec((rows, SAMPLE_KEYS), lambda s: (0, 0)),
        out_shape=jax.ShapeDtypeStruct((rows, SAMPLE_KEYS), F32),
        scratch_shapes=[pltpu.VMEM((SAMPLE_KEYS, rows), F32)],
        compiler_params=params, name="sample_select",
    )(scores.reshape(rows, SAMPLE_KEYS)).reshape(n_pairs, SAMPLE_Q, SAMPLE_KEYS)

    ga = SAMPLE_GROUP
    return pl.pallas_call(
        _sample_attend_kernel,
        grid_spec=pltpu.PrefetchScalarGridSpec(
            num_scalar_prefetch=1, grid=(DEC_BATCH // ga,),
            in_specs=[blk(ga, (64, 128)), blk(ga // 2, (SAMPLE_Q, SAMPLE_KEYS)),
                      blk(ga, (128, DEC_SEQ)), blk(ga, (128, DEC_SEQ)), hbm, hbm],
            out_specs=blk(ga, (SAMPLE_Q, W_A)),
            scratch_shapes=[pltpu.VMEM((2, ga, 128, SAMPLE_KEYS), F32),
                            pltpu.VMEM((2, ga, 128, SAMPLE_KEYS), F32),
                            pltpu.SemaphoreType.DMA((2,)), pltpu.SemaphoreType.DMA((2,))]),
        out_shape=jax.ShapeDtypeStruct((DEC_BATCH, SAMPLE_Q, W_A), F32),
        compiler_params=params, name="sample_attend",
    )(page_table, q_s, bias, k_new, v_new, ck, cv)


def _layer_norm(x, g, b):
    mu = jnp.mean(x, axis=-1, keepdims=True)
    xc = x - mu
    var = jnp.mean(xc * xc, axis=-1, keepdims=True)
    return xc * lax.rsqrt(var + LN_EPS) * g + b


def _silu(x):
    return x / (1.0 + jnp.exp(-x))


def _mixer_kernel(x_ref, a_ref, wb_ref, wmix_ref, bsb_ref, lnvg_ref, lnvb_ref, wout_ref,
                  lng_ref, lnb_ref, y_ref, vn_ref, *, chunk, vn_last_only, tiles_per_seq):
    tm = x_ref.shape[0]
    x = x_ref[...]
    xb = x.astype(BF16)
    last = (((1,), (1,)), ((), ()))
    ga_t = lax.dot_general(wb_ref[0:W_A, :], xb, last, preferred_element_type=F32)
    zb = lax.dot_general(xb, wb_ref[W_A:, :], last, preferred_element_type=F32)
    u = zb[:, 0:512]
    vb = zb[:, 512:1024]
    gb = zb[:, 1024:1536]
    vn = _layer_norm(vb, lnvg_ref[...], lnvb_ref[...])
    if vn_last_only:
        @pl.when(pl.program_id(0) % tiles_per_seq == tiles_per_seq - 1)
        def _():
            vn_ref[0] = vn[tm - CHUNK:, :]
    else:
        vn_ref[...] = vn
    vnb = vn.astype(BF16)
    rows = []
    for c in range(tm // chunk):
        cols = []
        for g in range(N_GROUPS_B):
            blk = vnb[c * chunk:(c + 1) * chunk, g * GROUP_W_B:(g + 1) * GROUP_W_B]
            cols.append(jnp.dot(wmix_ref[g], blk, preferred_element_type=F32) + bsb_ref[g])
        rows.append(jnp.concatenate(cols, axis=1))
    mixed = jnp.concatenate(rows, axis=0) if len(rows) > 1 else rows[0]
    a_out_t = (a_ref[...] * _silu(ga_t)).astype(BF16)
    b_out = (u * mixed * _silu(gb)).astype(BF16)
    out = (lax.dot_general(a_out_t, wout_ref[0:W_A, :], (((0,), (0,)), ((), ())),
                           preferred_element_type=F32)
           + jnp.dot(b_out, wout_ref[W_A:, :], preferred_element_type=F32))
    y_ref[...] = _layer_norm(ALPHA * x + out, lng_ref[...], lnb_ref[...])


def _mixer(x2d, attn, wb, wmix, bsb, lnvg, lnvb, wout, lng, lnb, *, chunk, vn_last_only):
    rows = x2d.shape[0]
    tm = ROW_TILE
    tiles_per_seq = SEQ // tm
    grid = (rows // tm,)
    row_map = lambda r: (r, 0)
    const2 = lambda r: (0, 0)
    const3 = lambda r: (0, 0, 0)
    if vn_last_only:
        vn_shape = jax.ShapeDtypeStruct((rows // SEQ, CHUNK, W_B), F32)
        vn_spec = pl.BlockSpec((1, CHUNK, W_B), lambda r: (r // tiles_per_seq, 0, 0))
    else:
        vn_shape = jax.ShapeDtypeStruct((rows, W_B), F32)
        vn_spec = pl.BlockSpec((tm, W_B), row_map)
    kern = functools.partial(_mixer_kernel, chunk=chunk, vn_last_only=vn_last_only,
                             tiles_per_seq=tiles_per_seq)
    return pl.pallas_call(
        kern, grid=grid,
        in_specs=[pl.BlockSpec((tm, D_MODEL), row_map), pl.BlockSpec((W_A, tm), lambda r: (0, r)),
                  pl.BlockSpec(wb.shape, const2), pl.BlockSpec(wmix.shape, const3),
                  pl.BlockSpec(bsb.shape, const3), pl.BlockSpec((1, W_B), const2),
                  pl.BlockSpec((1, W_B), const2), pl.BlockSpec(wout.shape, const2),
                  pl.BlockSpec((1, D_MODEL), const2), pl.BlockSpec((1, D_MODEL), const2)],
        out_specs=(pl.BlockSpec((tm, D_MODEL), row_map), vn_spec),
        out_shape=(jax.ShapeDtypeStruct((rows, D_MODEL), F32), vn_shape),
        compiler_params=pltpu.CompilerParams(dimension_semantics=("arbitrary",),
                                             vmem_limit_bytes=VMEM_LIMIT),
        name="mixer_chunk%d" % chunk,
    )(x2d, attn, wb, wmix, bsb, lnvg, lnvb, wout, lng, lnb)


def kernel(x_prompt, x_sample, cache_k, cache_v, cache_k_idx, page_table, w_in, w_s, b_s,
           ln_v_g, ln_v_b, w_out, ln_g, ln_b):
    w = w_in[0]
    wq, wk, wv = w[:, 0:512], w[:, 512:640], w[:, 640:768]
    wga, wqi, wki, wwi = w[:, 768:1280], w[:, 1280:1792], w[:, 1792:1856], w[:, 1856:1864]
    wu, wvb, wgb = w[:, 1864:2376], w[:, 2376:2888], w[:, 2888:3400]
    wstd = jnp.concatenate([wk.T, wki.T, jnp.zeros((64, D_MODEL), F32)], axis=0).astype(BF16)
    wt = jnp.concatenate([wq.T * Q_SCALE, wqi.T, wv.T, wk.T, wki.T, wwi.T,
                          jnp.zeros((WT_ROWS - WT_WI - N_IDX_HEADS, D_MODEL), F32)], axis=0).astype(BF16)
    wb = jnp.concatenate([wga.T, wu.T, wvb.T, wgb.T], axis=0).astype(BF16)
    wout = w_out[0].astype(BF16)
    lnvg, lnvb = ln_v_g[0][None, :], ln_v_b[0][None, :]
    lng, lnb = ln_g[0][None, :], ln_b[0][None, :]

    xp = x_prompt.reshape(BATCH * SEQ, D_MODEL)
    tabs_p = _rope_tables(np.arange(SEQ))
    kt_p, vt_p, kit_p, kbf, kiw, qt, qit, vt3, wit = _project(xp, wstd, wt, *tabs_p, SEQ)
    attn_p = _attn_prompt(qt, qit, wit, kbf, kiw, vt3)
    tril = np.tril(np.ones((CHUNK, CHUNK), np.float32))
    wmix_p = (w_s[0] * tril[None]).astype(BF16)
    bsb_p = jnp.broadcast_to(b_s[0][:, :, None], (N_GROUPS_B, CHUNK, GROUP_W_B))
    y_p, vn_p = _mixer(xp, attn_p, wb, wmix_p, bsb_p, lnvg, lnvb, wout, lng, lnb,
                       chunk=CHUNK, vn_last_only=True)

    rows_s = DEC_BATCH * DEC_SEQ
    xs = x_sample.reshape(rows_s, D_MODEL)
    tabs_s = _rope_tables(PAST_LEN + (np.arange(rows_s) % DEC_SEQ))
    kt_s, vt_s, kit_s, _, _, qt_s, qit_s, _, wit_s = _project(xs, wstd, wt, *tabs_s, rows_s)

    def to_rows(a_t):
        a = a_t.reshape(N_HEADS, HEAD_DIM, DEC_BATCH, DEC_SEQ).transpose(2, 0, 3, 1)
        return jnp.pad(a, ((0, 0), (0, 0), (0, SAMPLE_Q - DEC_SEQ), (0, 0)))

    qi_rows = to_rows(qit_s).reshape(DEC_BATCH, N_IDX_HEADS * SAMPLE_Q, IDX_DIM)
    q5 = to_rows(qt_s).reshape(DEC_BATCH, N_KV, N_HEADS // N_KV, SAMPLE_Q, HEAD_DIM)
    zq = jnp.zeros_like(q5[:, 0])
    q_rows = jnp.stack([jnp.concatenate([q5[:, 0], zq], axis=-1),
                        jnp.concatenate([zq, q5[:, 1]], axis=-1)], axis=1)
    q_rows = q_rows.reshape(DEC_BATCH, N_HEADS * SAMPLE_Q, 2 * HEAD_DIM)
    w_rows = jnp.pad(wit_s.reshape(N_IDX_HEADS, DEC_BATCH, DEC_SEQ).transpose(1, 0, 2),
                     ((0, 0), (0, 0), (0, SAMPLE_Q - DEC_SEQ)))
    w_rows = jnp.broadcast_to(w_rows.reshape(DEC_BATCH, N_IDX_HEADS * SAMPLE_Q, 1),
                              (DEC_BATCH, N_IDX_HEADS * SAMPLE_Q, LANES))
    new_cols = lambda a: a.reshape(a.shape[0], DEC_BATCH, DEC_SEQ).transpose(1, 0, 2)
    n_pool = cache_k.shape[1]
    ck_t = cache_k[0].transpose(0, 2, 3, 1).reshape(n_pool, N_KV * HEAD_DIM, PAGE_SIZE)
    cv_t = cache_v[0].transpose(0, 2, 3, 1).reshape(n_pool, N_KV * HEAD_DIM, PAGE_SIZE)
    cki_t = cache_k_idx[0].transpose(0, 2, 1)
    attn_s8 = _attn_sample(page_table, qi_rows, q_rows, w_rows, new_cols(kit_s[0]),
                           new_cols(kt_s[0]), new_cols(vt_s[0]), ck_t, cv_t, cki_t)
    attn_s = attn_s8[:, :DEC_SEQ, :].reshape(rows_s, W_A).T
    ri = np.arange(rows_s)
    same_seq = (ri[:, None] // DEC_SEQ) == (ri[None, :] // DEC_SEQ)
    wmix_s = jnp.zeros((N_GROUPS_B, rows_s, rows_s), F32)
    for t in range(DEC_SEQ):
        for s in range(t + 1):
            hit = same_seq & (ri[:, None] % DEC_SEQ == t) & (ri[None, :] % DEC_SEQ == s)
            wmix_s = wmix_s + jnp.where(hit[None], w_s[0, :, t, s][:, None, None], 0.0)
    wmix_s = wmix_s.astype(BF16)
    bsb_s = jnp.broadcast_to(jnp.tile(b_s[0][:, :DEC_SEQ], (1, DEC_BATCH))[:, :, None],
                             (N_GROUPS_B, rows_s, GROUP_W_B))
    y_s, vn_s = _mixer(xs, attn_s, wb, wmix_s, bsb_s, lnvg, lnvb, wout, lng, lnb,
                       chunk=rows_s, vn_last_only=False)

    def heads_last(a, n, t):
        a = a.reshape(a.shape[0], N_KV, HEAD_DIM, a.shape[2]).transpose(0, 3, 1, 2)
        return a.reshape(1, n, t, N_KV, HEAD_DIM)

    return (y_p.reshape(BATCH, SEQ, D_MODEL),
            y_s.reshape(DEC_BATCH, DEC_SEQ, D_MODEL),
            heads_last(kt_p, BATCH, SEQ), heads_last(vt_p, BATCH, SEQ),
            kit_p.transpose(0, 2, 1)[None],
            vn_p.reshape(1, BATCH, CHUNK, W_B),
            heads_last(kt_s, DEC_BATCH, DEC_SEQ), heads_last(vt_s, DEC_BATCH, DEC_SEQ),
            kit_s[0].T.reshape(1, DEC_BATCH, DEC_SEQ, IDX_DIM),
            vn_s.reshape(1, DEC_BATCH, DEC_SEQ, W_B))
```

```python
import functools

import jax
import jax.numpy as jnp
import numpy as np
from jax import lax
from jax.experimental import pallas as pl
from jax.experimental.pallas import tpu as pltpu

F32 = jnp.float32
BF16 = jnp.bfloat16

D_MODEL = 1024
SEQ = 2048
BATCH = 8
DEC_BATCH = 128
DEC_SEQ = 4
PAST_LEN = 2048
PAGE_SIZE = 128
N_PAGES = PAST_LEN // PAGE_SIZE
W_A = 512
W_B = 512
HEAD_DIM = 64
N_HEADS = 8
N_KV = 2
N_IDX_HEADS = 8
IDX_DIM = 64
TOPK = 256
CHUNK = 128
N_GROUPS_B = 4
GROUP_W_B = 128
ROPE_THETA = 10000.0
LN_EPS = 1e-5
ALPHA = 2.0 ** 0.25
NEG = -1e30
IDX_SCALE = float((N_IDX_HEADS * IDX_DIM) ** -0.5)

LANES = 128
KEY_TILE = 128
KEY_STEP = 256
ROW_TILE = 512
WT_Q, WT_QI, WT_V, WT_K, WT_KI, WT_WI, WT_ROWS = 0, 512, 1024, 1152, 1280, 1344, 1360
BISECT_ITERS = 20
FOLD_CHAINS = 8
ONES_ROWS = 16
Q_SCALE = float(HEAD_DIM ** -0.5 * 1.4426950408889634)
VMEM_LIMIT = 48 * 1024 * 1024

SAMPLE_GROUP = 4
SAMPLE_IDX_GROUP = 8
SAMPLE_KEYS = PAST_LEN + KEY_TILE
SAMPLE_Q = 8


def _proj_kernel(x_ref, wt_ref, cost_ref, sint_ref,
                 kt_ref, vt32_ref, kit_ref, kbf_ref, kiw_ref, qt_ref, qit_ref, vt_ref, wit_ref):
    tm = x_ref.shape[0]
    xb = x_ref[...].astype(BF16)
    zt = lax.dot_general(wt_ref[...], xb, (((1,), (1,)), ((), ())),
                         preferred_element_type=F32)
    cost = cost_ref[...]
    sint = sint_ref[...]
    half = HEAD_DIM // 2

    def rope_rows(base, n_heads):
        pieces = []
        for h in range(n_heads):
            x1 = zt[base + h * HEAD_DIM: base + h * HEAD_DIM + half]
            x2 = zt[base + h * HEAD_DIM + half: base + (h + 1) * HEAD_DIM]
            pieces.append(x1 * cost - x2 * sint)
            pieces.append(x2 * cost + x1 * sint)
        return jnp.concatenate(pieces, axis=0)

    qt_ref[...] = rope_rows(WT_Q, N_HEADS).astype(BF16)
    qit_ref[...] = rope_rows(WT_QI, N_IDX_HEADS).astype(BF16)
    v_t = zt[WT_V:WT_V + 128]
    vt32_ref[0] = v_t
    vtb = v_t.astype(BF16)
    for c in range(tm // KEY_STEP):
        vt_ref[c] = vtb[:, c * KEY_STEP:(c + 1) * KEY_STEP]
    k_t = rope_rows(WT_K, N_KV)
    ki_t = rope_rows(WT_KI, 1)
    kt_ref[0] = k_t
    kit_ref[0] = ki_t
    kbf_ref[...] = k_t.T.astype(BF16)
    kiw_ref[...] = jnp.concatenate([ki_t, jnp.zeros_like(ki_t)], axis=0).T.astype(BF16)
    wit_ref[...] = zt[WT_WI:WT_WI + N_IDX_HEADS] * IDX_SCALE


def _project(x2d, wt, cost, sint, seq_len):
    rows = x2d.shape[0]
    tm = ROW_TILE
    grid = (rows // tm,)
    n_pos_tiles = seq_len // tm
    row_map = lambda r: (r, 0)
    col_map = lambda r: (0, r)
    const = lambda r: (0, 0)
    seq_map = lambda r: (r // n_pos_tiles, 0, r % n_pos_tiles)
    out_shape = (
        jax.ShapeDtypeStruct((rows // seq_len, 128, seq_len), F32),
        jax.ShapeDtypeStruct((rows // seq_len, 128, seq_len), F32),
        jax.ShapeDtypeStruct((rows // seq_len, IDX_DIM, seq_len), F32),
        jax.ShapeDtypeStruct((rows, 128), BF16),
        jax.ShapeDtypeStruct((rows, 128), BF16),
        jax.ShapeDtypeStruct((512, rows), BF16),
        jax.ShapeDtypeStruct((512, rows), BF16),
        jax.ShapeDtypeStruct((rows // KEY_STEP, 128, KEY_STEP), BF16),
        jax.ShapeDtypeStruct((8, rows), F32),
    )
    out_specs = (
        pl.BlockSpec((1, 128, tm), seq_map),
        pl.BlockSpec((1, 128, tm), seq_map),
        pl.BlockSpec((1, IDX_DIM, tm), seq_map),
        pl.BlockSpec((tm, 128), row_map),
        pl.BlockSpec((tm, 128), row_map),
        pl.BlockSpec((512, tm), col_map),
        pl.BlockSpec((512, tm), col_map),
        pl.BlockSpec((tm // KEY_STEP, 128, KEY_STEP), lambda r: (r, 0, 0)),
        pl.BlockSpec((8, tm), col_map),
    )
    in_specs = [
        pl.BlockSpec((tm, D_MODEL), row_map),
        pl.BlockSpec(wt.shape, const),
        pl.BlockSpec((HEAD_DIM // 2, tm), lambda r: (0, r % n_pos_tiles)),
        pl.BlockSpec((HEAD_DIM // 2, tm), lambda r: (0, r % n_pos_tiles)),
    ]
    return pl.pallas_call(
        _proj_kernel, grid=grid, in_specs=in_specs, out_specs=out_specs, out_shape=out_shape,
        compiler_params=pltpu.CompilerParams(dimension_semantics=("arbitrary",),
                                             vmem_limit_bytes=VMEM_LIMIT),
        name="proj_attn_side",
    )(x2d, wt, cost, sint)


def _rope_tables(pos):
    half = HEAD_DIM // 2
    inv = ROPE_THETA ** (-np.arange(half, dtype=np.float64) / half)
    ang = inv[:, None] * pos.astype(np.float64)[None, :]
    return jnp.asarray(np.cos(ang), dtype=F32), jnp.asarray(np.sin(ang), dtype=F32)


def _select_threshold(count_ge, kth_tie_index, min_ge, max_lt, lo0, hi0, n_allowed, n_keys):
    k = float(TOPK)

    def split(mid, lo, hi, clo, chi):
        c = count_ge(mid)
        take = c >= k
        return (jnp.where(take, mid, lo), jnp.where(take, hi, mid),
                jnp.where(take, c, clo), jnp.where(take, chi, c))

    def step(_, carry):
        lo, hi, clo, chi = carry
        return split(0.5 * lo + 0.5 * hi, lo, hi, clo, chi)

    lo, hi, clo, chi = lax.fori_loop(0, BISECT_ITERS, step,
                                     (lo0, hi0, n_allowed, jnp.zeros_like(lo0)))
    big = jnp.full_like(lo, float(4 * n_keys))

    def any_true(mask):
        return jnp.max(jnp.where(mask, 1.0, 0.0)) > 0.5

    def exact(args):
        lo, hi, clo, chi = args

        def cond(c):
            _, _, clo, _, vlo, vhi, it = c
            open_ = jnp.logical_and(clo > k, vlo < vhi)
            return jnp.logical_and(any_true(open_), it < 2 * n_keys)

        def body(c):
            lo, hi, clo, chi, vlo, vhi, it = c
            mid = 0.5 * vlo + 0.5 * vhi
            lo, hi, clo, chi = split(jnp.where(mid > vlo, mid, vhi), lo, hi, clo, chi)
            return lo, hi, clo, chi, min_ge(lo), max_lt(hi), it + 1

        lo, hi, clo, chi, vlo, _, _ = lax.while_loop(
            cond, body, (lo, hi, clo, chi, min_ge(lo), max_lt(hi), jnp.int32(0)))
        return vlo, jnp.where(clo > k, kth_tie_index(vlo, k - chi), big)

    def fast(args):
        return args[0], big

    return lax.cond(any_true(clo > k), exact, fast, (lo, hi, clo, chi))


def _fold_rows(x, op):
    groups = [x[r:r + 8] for r in range(0, x.shape[0], 8)]
    parts = groups[:FOLD_CHAINS]
    for n, grp in enumerate(groups[FOLD_CHAINS:]):
        parts[n % FOLD_CHAINS] = op(parts[n % FOLD_CHAINS], grp)
    while len(parts) > 1:
        nxt = [op(parts[a], parts[a + 1]) for a in range(0, len(parts) - 1, 2)]
        if len(parts) % 2:
            nxt.append(parts[-1])
        parts = nxt
    return parts[0]


def _topk_bias_in_place(s_ref, n_keys, n_allowed, lo0=None, hi0=None):
    lanes = s_ref.shape[1]
    inf = jnp.float32(jnp.inf)
    chunk = 8 * FOLD_CHAINS
    kidx0 = lax.broadcasted_iota(jnp.int32, (chunk, lanes), 0).astype(F32)

    def key_reduce(fn, op, finish):
        part = None
        for r in range(0, n_keys, chunk):
            v = fn(s_ref[r:r + chunk, :], r)
            part = v if part is None else op(part, v)
        return finish(_fold_rows(part, op), axis=0, keepdims=True)

    count_ge = lambda thr: key_reduce(lambda t, r: jnp.where(t >= thr, 1.0, 0.0), jnp.add, jnp.sum)
    tri = (lax.broadcasted_iota(jnp.int32, (chunk, chunk), 0)
           >= lax.broadcasted_iota(jnp.int32, (chunk, chunk), 1)).astype(BF16)

    def kth_tie_index(thr, need):
        seen = jnp.zeros((1, lanes), F32)
        best = jnp.full((1, lanes), -1.0, F32)
        for r in range(0, n_keys, chunk):
            tie = jnp.where(s_ref[r:r + chunk, :] == thr, 1.0, 0.0)
            upto = jnp.dot(tri, tie.astype(BF16), preferred_element_type=F32) + seen
            hit = jnp.where(tie > 0.5, jnp.where(upto == need, kidx0 + r, -1.0), -1.0)
            best = jnp.maximum(best, jnp.max(_fold_rows(hit, jnp.maximum), axis=0, keepdims=True))
            seen = upto[chunk - 1:chunk, :]
        return best
    min_ge = lambda thr: key_reduce(lambda t, r: jnp.where(t >= thr, t, inf), jnp.minimum, jnp.min)
    max_lt = lambda thr: key_reduce(lambda t, r: jnp.where(t < thr, t, -inf), jnp.maximum, jnp.max)
    if lo0 is None:
        lo0 = key_reduce(lambda t, r: jnp.where(t == -inf, inf, t), jnp.minimum, jnp.min)
        mx1 = key_reduce(lambda t, r: t, jnp.maximum, jnp.max)
        hi0 = mx1 + (jnp.abs(mx1) * (2.0 ** -20) + 1e-30)
    lo, jmax = _select_threshold(count_ge, kth_tie_index, min_ge, max_lt, lo0, hi0, n_allowed, n_keys)
    for r in range(0, n_keys, chunk):
        t = s_ref[r:r + chunk, :]
        tie = jnp.where(t == lo, jnp.where(kidx0 <= jmax - r, 0.0, NEG), NEG)
        s_ref[r:r + chunk, :] = jnp.where(t > lo, 0.0, tie)


def _attn_prompt_kernel(qt_ref, qit_ref, wit_ref, kbf_ref, kiw_ref, vt_ref, o_ref,
                        s_ref, sc_ref, ri_ref, rq_ref):
    i = pl.program_id(1)
    w = wit_ref[...]
    zeros_half = jnp.zeros((HEAD_DIM, LANES), BF16)

    for hp in range(4):
        a = qit_ref[(2 * hp) * 64:(2 * hp + 1) * 64, :]
        b = qit_ref[(2 * hp + 1) * 64:(2 * hp + 2) * 64, :]
        ri_ref[hp] = jnp.concatenate(
            [jnp.concatenate([a, zeros_half], axis=0), jnp.concatenate([b, zeros_half], axis=0)], axis=1)
        g = hp // 2
        a = qt_ref[(2 * hp) * 64:(2 * hp + 1) * 64, :]
        b = qt_ref[(2 * hp + 1) * 64:(2 * hp + 2) * 64, :]
        if g == 0:
            ab = jnp.concatenate([jnp.concatenate([a, zeros_half], axis=0),
                                  jnp.concatenate([b, zeros_half], axis=0)], axis=1)
        else:
            ab = jnp.concatenate([jnp.concatenate([zeros_half, a], axis=0),
                                  jnp.concatenate([zeros_half, b], axis=0)], axis=1)
        rq_ref[hp] = ab

    inf = jnp.float32(jnp.inf)
    n_steps = (i * LANES + LANES + KEY_STEP - 1) // KEY_STEP

    def process(ns):
        n_keys = ns * KEY_STEP
        keys = slice(0, n_keys)
        row = lax.broadcasted_iota(jnp.int32, (n_keys, LANES), 0)
        col = lax.broadcasted_iota(jnp.int32, (n_keys, LANES), 1)
        qpos = col + i * LANES
        allowed = row <= qpos

        kt = kiw_ref[keys, :]
        acc = jnp.zeros((n_keys, LANES), F32)
        for hp in range(4):
            s2 = jnp.dot(kt, ri_ref[hp], preferred_element_type=F32)
            acc = acc + jnp.maximum(s2[:, :LANES], 0.0) * w[2 * hp:2 * hp + 1, :]
            acc = acc + jnp.maximum(s2[:, LANES:], 0.0) * w[2 * hp + 1:2 * hp + 2, :]
        s_ref[keys, :] = jnp.where(allowed, acc, -inf)
        lo0 = jnp.min(_fold_rows(jnp.where(allowed, acc, inf), jnp.minimum), axis=0, keepdims=True)
        mx1 = jnp.max(_fold_rows(jnp.where(allowed, acc, -inf), jnp.maximum), axis=0, keepdims=True)
        hi0 = mx1 + (jnp.abs(mx1) * (2.0 ** -20) + 1e-30)
        n_allowed = (qpos[0:1, :] + 1).astype(F32)

        _topk_bias_in_place(s_ref, n_keys, n_allowed, lo0, hi0)

        kt = kbf_ref[keys, :]
        b = s_ref[keys, :]
        b2 = jnp.concatenate([b, b], axis=1)
        ms = []
        for hp in range(4):
            s2 = jnp.dot(kt, rq_ref[hp], preferred_element_type=F32) + b2
            sc_ref[hp, keys, :] = s2
            ms.append(jnp.max(_fold_rows(s2, jnp.maximum), axis=0, keepdims=True))
        vt = jnp.concatenate([vt_ref[j] for j in range(ns)], axis=1)
        ones_rows = jnp.ones((ONES_ROWS, n_keys), BF16)
        lhs = [jnp.concatenate([vt[g * 64:(g + 1) * 64, :], ones_rows], axis=0) for g in range(N_KV)]
        pieces = []
        for hp in range(4):
            p = jnp.exp2((sc_ref[hp, keys, :] - ms[hp]).astype(BF16))
            o = jnp.dot(lhs[hp // 2], p, preferred_element_type=F32)
            o2 = o[0:HEAD_DIM, :] / o[HEAD_DIM:HEAD_DIM + 1, :]
            pieces.append(o2[:, :LANES])
            pieces.append(o2[:, LANES:])
        o_ref[...] = jnp.concatenate(pieces, axis=0)

    for ns in range(1, SEQ // KEY_STEP + 1):
        pl.when(n_steps == ns)(functools.partial(process, ns))


def _attn_prompt(qt, qit, wit, kbf, kiw, vt3):
    n_blocks = SEQ // LANES
    grid = (BATCH, n_blocks)
    qmap = lambda n, i: (0, n * n_blocks + i)
    in_specs = [
        pl.BlockSpec((512, LANES), qmap),
        pl.BlockSpec((512, LANES), qmap),
        pl.BlockSpec((8, LANES), qmap),
        pl.BlockSpec((SEQ, 128), lambda n, i: (n, 0)),
        pl.BlockSpec((SEQ, 128), lambda n, i: (n, 0)),
        pl.BlockSpec((SEQ // KEY_STEP, 128, KEY_STEP), lambda n, i: (n, 0, 0)),
    ]
    return pl.pallas_call(
        _attn_prompt_kernel, grid=grid, in_specs=in_specs,
        out_specs=pl.BlockSpec((W_A, LANES), qmap),
        out_shape=jax.ShapeDtypeStruct((W_A, BATCH * SEQ), F32),
        scratch_shapes=[pltpu.VMEM((SEQ, LANES), F32), pltpu.VMEM((4, SEQ, 2 * LANES), F32),
                        pltpu.VMEM((4, 128, 2 * LANES), BF16), pltpu.VMEM((4, 128, 2 * LANES), BF16)],
        compiler_params=pltpu.CompilerParams(dimension_semantics=("arbitrary", "arbitrary"),
                                             vmem_limit_bytes=VMEM_LIMIT),
        name="attn_prompt",
    )(qt, qit, wit, kbf, kiw, vt3)


def _page_copies(pt_ref, group, slot, gsz, streams):
    out = []
    for g in range(gsz):
        seq = group * gsz + g
        for p in range(N_PAGES):
            page = pt_ref[seq, p]
            keys = pl.ds(p * PAGE_SIZE, PAGE_SIZE)
            for hbm, buf, sem in streams:
                out.append(pltpu.make_async_copy(hbm.at[page], buf.at[slot, g, :, keys], sem.at[slot]))
    return out


def _paged_prefetch(pt_ref, gsz, streams):
    step = pl.program_id(0)
    n_steps = pl.num_programs(0)
    slot = step % 2

    @pl.when(step == 0)
    def _():
        tail = pl.ds(PAST_LEN, KEY_TILE)
        for _, buf, _ in streams:
            for s_ in range(2):
                for g in range(gsz):
                    buf[s_, g, :, tail] = jnp.zeros((buf.shape[2], KEY_TILE), F32)
        for c in _page_copies(pt_ref, 0, 0, gsz, streams):
            c.start()

    @pl.when(step + 1 < n_steps)
    def _():
        for c in _page_copies(pt_ref, step + 1, 1 - slot, gsz, streams):
            c.start()

    for c in _page_copies(pt_ref, step, slot, gsz, streams):
        c.wait()
    return slot


def _sample_index_kernel(pt_ref, qi_ref, w_ref, kin_ref, cki_hbm, s_out_ref, kibuf, sem):
    gsz = SAMPLE_IDX_GROUP
    slot = _paged_prefetch(pt_ref, gsz, [(cki_hbm, kibuf, sem)])
    rows8 = lax.broadcasted_iota(jnp.int32, (SAMPLE_Q, SAMPLE_KEYS), 0)
    keyi = lax.broadcasted_iota(jnp.int32, (SAMPLE_Q, SAMPLE_KEYS), 1)
    allowed = keyi <= (PAST_LEN + rows8)
    new_keys = pl.ds(PAST_LEN, DEC_SEQ)
    scores = []
    for g in range(gsz):
        kibuf[slot, g, :, new_keys] = kin_ref[g]
        kib = kibuf[slot, g].astype(BF16)
        s = jnp.dot(qi_ref[g], kib, preferred_element_type=F32)
        sw = jnp.maximum(s, 0.0) * w_ref[g][:, 0:1]
        acc = jnp.sum(sw.reshape(N_IDX_HEADS, SAMPLE_Q, SAMPLE_KEYS), axis=0)
        scores.append(jnp.where(allowed, acc, -jnp.inf))
    for a in range(gsz // 2):
        s_out_ref[a] = jnp.where(rows8 < DEC_SEQ, scores[2 * a],
                                 pltpu.roll(scores[2 * a + 1], DEC_SEQ, 0))


def _sample_select_kernel(s_ref, b_ref, st_ref):
    rows = s_ref.shape[0]
    st_ref[...] = s_ref[...].T
    t4 = lax.broadcasted_iota(jnp.int32, (1, rows), 1) % DEC_SEQ
    _topk_bias_in_place(st_ref, SAMPLE_KEYS, (PAST_LEN + 1 + t4).astype(F32))
    b_ref[...] = st_ref[...].T


def _sample_attend_kernel(pt_ref, q_ref, b_ref, kn_ref, vn_ref, ck_hbm, cv_hbm, o_ref,
                          kbuf, vbuf, ksem, vsem):
    gsz = SAMPLE_GROUP
    slot = _paged_prefetch(pt_ref, gsz, [(ck_hbm, kbuf, ksem), (cv_hbm, vbuf, vsem)])
    new_keys = pl.ds(PAST_LEN, DEC_SEQ)
    lane = lax.broadcasted_iota(jnp.int32, (SAMPLE_Q, LANES), 1)
    for g in range(gsz):
        kbuf[slot, g, :, new_keys] = kn_ref[g]
        vbuf[slot, g, :, new_keys] = vn_ref[g]
        kb = kbuf[slot, g].astype(BF16)
        vb = vbuf[slot, g].astype(BF16)
        b = b_ref[g // 2]
        if g % 2:
            b = pltpu.roll(b, DEC_SEQ, 0)
        s = jnp.dot(q_ref[g], kb, preferred_element_type=F32)
        s = s + jnp.concatenate([b] * N_HEADS, axis=0)
        m = jnp.max(s, axis=1, keepdims=True)
        p = jnp.exp2(s - m)
        l = jnp.sum(p, axis=1, keepdims=True)
        o = lax.dot_general(p.astype(BF16), vb, (((1,), (1,)), ((), ())),
                            preferred_element_type=F32) / l
        o_r = pltpu.roll(o, HEAD_DIM, 1)
        tiles = []
        for c in range(4):
            ha, hb = 2 * c, 2 * c + 1
            src_a = o if ha // 4 == 0 else o_r
            src_b = o if hb // 4 == 1 else o_r
            tiles.append(jnp.where(lane < HEAD_DIM, src_a[ha * SAMPLE_Q:(ha + 1) * SAMPLE_Q, :],
                                   src_b[hb * SAMPLE_Q:(hb + 1) * SAMPLE_Q, :]))
        o_ref[g] = jnp.concatenate(tiles, axis=1)


def _attn_sample(page_table, qi_s, q_s, w_s, ki_new, k_new, v_new, ck, cv, cki):
    params = pltpu.CompilerParams(dimension_semantics=("arbitrary",), vmem_limit_bytes=VMEM_LIMIT)
    blk = lambda n, shape: pl.BlockSpec((n,) + shape, lambda s, pt: (s, 0, 0))
    hbm = pl.BlockSpec(memory_space=pl.ANY)
    n_pairs = DEC_BATCH // 2

    gi = SAMPLE_IDX_GROUP
    scores = pl.pallas_call(
        _sample_index_kernel,
        grid_spec=pltpu.PrefetchScalarGridSpec(
            num_scalar_prefetch=1, grid=(DEC_BATCH // gi,),
            in_specs=[blk(gi, (64, IDX_DIM)), blk(gi, (64, LANES)), blk(gi, (IDX_DIM, DEC_SEQ)), hbm],
            out_specs=blk(gi // 2, (SAMPLE_Q, SAMPLE_KEYS)),
            scratch_shapes=[pltpu.VMEM((2, gi, IDX_DIM, SAMPLE_KEYS), F32),
                            pltpu.SemaphoreType.DMA((2,))]),
        out_shape=jax.ShapeDtypeStruct((n_pairs, SAMPLE_Q, SAMPLE_KEYS), F32),
        compiler_params=params, name="sample_index",
    )(page_table, qi_s, w_s, ki_new, cki)

    rows = n_pairs * SAMPLE_Q
    bias = pl.pallas_call(
        _sample_select_kernel, grid=(1,),
        in_specs=[pl.BlockSpec((rows, SAMPLE_KEYS), lambda s: (0, 0))],
        out_specs=pl.BlockSpec((rows, SAMPLE_KEYS), lambda s: (0, 0)),
        out_shape=jax.ShapeDtypeStruct((rows, SAMPLE_KEYS), F32),
        scratch_shapes=[pltpu.VMEM((SAMPLE_KEYS, rows), F32)],
        compiler_params=params, name="sample_select",
    )(scores.reshape(rows, SAMPLE_KEYS)).reshape(n_pairs, SAMPLE_Q, SAMPLE_KEYS)

    ga = SAMPLE_GROUP
    return pl.pallas_call(
        _sample_attend_kernel,
        grid_spec=pltpu.PrefetchScalarGridSpec(
            num_scalar_prefetch=1, grid=(DEC_BATCH // ga,),
            in_specs=[blk(ga, (64, 128)), blk(ga // 2, (SAMPLE_Q, SAMPLE_KEYS)),
                      blk(ga, (128, DEC_SEQ)), blk(ga, (128, DEC_SEQ)), hbm, hbm],
            out_specs=blk(ga, (SAMPLE_Q, W_A)),
            scratch_shapes=[pltpu.VMEM((2, ga, 128, SAMPLE_KEYS), F32),
                            pltpu.VMEM((2, ga, 128, SAMPLE_KEYS), F32),
                            pltpu.SemaphoreType.DMA((2,)), pltpu.SemaphoreType.DMA((2,))]),
        out_shape=jax.ShapeDtypeStruct((DEC_BATCH, SAMPLE_Q, W_A), F32),
        compiler_params=params, name="sample_attend",
    )(page_table, q_s, bias, k_new, v_new, ck, cv)


def _layer_norm(x, g, b):
    mu = jnp.mean(x, axis=-1, keepdims=True)
    xc = x - mu
    var = jnp.mean(xc * xc, axis=-1, keepdims=True)
    return xc * lax.rsqrt(var + LN_EPS) * g + b


def _silu(x):
    return x / (1.0 + jnp.exp(-x))


def _mixer_kernel(x_ref, a_ref, wb_ref, wmix_ref, bsb_ref, lnvg_ref, lnvb_ref, wout_ref,
                  lng_ref, lnb_ref, y_ref, vn_ref, *, chunk, vn_last_only, tiles_per_seq):
    tm = x_ref.shape[0]
    x = x_ref[...]
    xb = x.astype(BF16)
    last = (((1,), (1,)), ((), ()))
    ga_t = lax.dot_general(wb_ref[0:W_A, :], xb, last, preferred_element_type=F32)
    zb = lax.dot_general(xb, wb_ref[W_A:, :], last, preferred_element_type=F32)
    u = zb[:, 0:512]
    vb = zb[:, 512:1024]
    gb = zb[:, 1024:1536]
    vn = _layer_norm(vb, lnvg_ref[...], lnvb_ref[...])
    if vn_last_only:
        @pl.when(pl.program_id(0) % tiles_per_seq == tiles_per_seq - 1)
        def _():
            vn_ref[0] = vn[tm - CHUNK:, :]
    else:
        vn_ref[...] = vn
    vnb = vn.astype(BF16)
    rows = []
    for c in range(tm // chunk):
        cols = []
        for g in range(N_GROUPS_B):
            blk = vnb[c * chunk:(c + 1) * chunk, g * GROUP_W_B:(g + 1) * GROUP_W_B]
            cols.append(jnp.dot(wmix_ref[g], blk, preferred_element_type=F32) + bsb_ref[g])
        rows.append(jnp.concatenate(cols, axis=1))
    mixed = jnp.concatenate(rows, axis=0) if len(rows) > 1 else rows[0]
    a_out_t = (a_ref[...] * _silu(ga_t)).astype(BF16)
    b_out = (u * mixed * _silu(gb)).astype(BF16)
    out = (lax.dot_general(a_out_t, wout_ref[0:W_A, :], (((0,), (0,)), ((), ())),
                           preferred_element_type=F32)
           + jnp.dot(b_out, wout_ref[W_A:, :], preferred_element_type=F32))
    y_ref[...] = _layer_norm(ALPHA * x + out, lng_ref[...], lnb_ref[...])


def _mixer(x2d, attn, wb, wmix, bsb, lnvg, lnvb, wout, lng, lnb, *, chunk, vn_last_only):
    rows = x2d.shape[0]
    tm = ROW_TILE
    tiles_per_seq = SEQ // tm
    grid = (rows // tm,)
    row_map = lambda r: (r, 0)
    const2 = lambda r: (0, 0)
    const3 = lambda r: (0, 0, 0)
    if vn_last_only:
        vn_shape = jax.ShapeDtypeStruct((rows // SEQ, CHUNK, W_B), F32)
        vn_spec = pl.BlockSpec((1, CHUNK, W_B), lambda r: (r // tiles_per_seq, 0, 0))
    else:
        vn_shape = jax.ShapeDtypeStruct((rows, W_B), F32)
        vn_spec = pl.BlockSpec((tm, W_B), row_map)
    kern = functools.partial(_mixer_kernel, chunk=chunk, vn_last_only=vn_last_only,
                             tiles_per_seq=tiles_per_seq)
    return pl.pallas_call(
        kern, grid=grid,
        in_specs=[pl.BlockSpec((tm, D_MODEL), row_map), pl.BlockSpec((W_A, tm), lambda r: (0, r)),
                  pl.BlockSpec(wb.shape, const2), pl.BlockSpec(wmix.shape, const3),
                  pl.BlockSpec(bsb.shape, const3), pl.BlockSpec((1, W_B), const2),
                  pl.BlockSpec((1, W_B), const2), pl.BlockSpec(wout.shape, const2),
                  pl.BlockSpec((1, D_MODEL), const2), pl.BlockSpec((1, D_MODEL), const2)],
        out_specs=(pl.BlockSpec((tm, D_MODEL), row_map), vn_spec),
        out_shape=(jax.ShapeDtypeStruct((rows, D_MODEL), F32), vn_shape),
        compiler_params=pltpu.CompilerParams(dimension_semantics=("arbitrary",),
                                             vmem_limit_bytes=VMEM_LIMIT),
        name="mixer_chunk%d" % chunk,
    )(x2d, attn, wb, wmix, bsb, lnvg, lnvb, wout, lng, lnb)


def kernel(x_prompt, x_sample, cache_k, cache_v, cache_k_idx, page_table, w_in, w_s, b_s,
           ln_v_g, ln_v_b, w_out, ln_g, ln_b):
    w = w_in[0]
    wq, wk, wv = w[:, 0:512], w[:, 512:640], w[:, 640:768]
    wga, wqi, wki, wwi = w[:, 768:1280], w[:, 1280:1792], w[:, 1792:1856], w[:, 1856:1864]
    wu, wvb, wgb = w[:, 1864:2376], w[:, 2376:2888], w[:, 2888:3400]
    wt = jnp.concatenate([wq.T * Q_SCALE, wqi.T, wv.T, wk.T, wki.T, wwi.T,
                          jnp.zeros((WT_ROWS - WT_WI - N_IDX_HEADS, D_MODEL), F32)], axis=0).astype(BF16)
    wb = jnp.concatenate([wga.T, wu.T, wvb.T, wgb.T], axis=0).astype(BF16)
    wout = w_out[0].astype(BF16)
    lnvg, lnvb = ln_v_g[0][None, :], ln_v_b[0][None, :]
    lng, lnb = ln_g[0][None, :], ln_b[0][None, :]

    xp = x_prompt.reshape(BATCH * SEQ, D_MODEL)
    tabs_p = _rope_tables(np.arange(SEQ))
    kt_p, vt_p, kit_p, kbf, kiw, qt, qit, vt3, wit = _project(xp, wt, *tabs_p, SEQ)
    attn_p = _attn_prompt(qt, qit, wit, kbf, kiw, vt3)
    tril = np.tril(np.ones((CHUNK, CHUNK), np.float32))
    wmix_p = (w_s[0] * tril[None]).astype(BF16)
    bsb_p = jnp.broadcast_to(b_s[0][:, :, None], (N_GROUPS_B, CHUNK, GROUP_W_B))
    y_p, vn_p = _mixer(xp, attn_p, wb, wmix_p, bsb_p, lnvg, lnvb, wout, lng, lnb,
                       chunk=CHUNK, vn_last_only=True)

    rows_s = DEC_BATCH * DEC_SEQ
    xs = x_sample.reshape(rows_s, D_MODEL)
    tabs_s = _rope_tables(PAST_LEN + (np.arange(rows_s) % DEC_SEQ))
    kt_s, vt_s, kit_s, _, _, qt_s, qit_s, _, wit_s = _project(xs, wt, *tabs_s, rows_s)

    def to_rows(a_t):
        a = a_t.reshape(N_HEADS, HEAD_DIM, DEC_BATCH, DEC_SEQ).transpose(2, 0, 3, 1)
        return jnp.pad(a, ((0, 0), (0, 0), (0, SAMPLE_Q - DEC_SEQ), (0, 0)))

    qi_rows = to_rows(qit_s).reshape(DEC_BATCH, N_IDX_HEADS * SAMPLE_Q, IDX_DIM)
    q5 = to_rows(qt_s).reshape(DEC_BATCH, N_KV, N_HEADS // N_KV, SAMPLE_Q, HEAD_DIM)
    zq = jnp.zeros_like(q5[:, 0])
    q_rows = jnp.stack([jnp.concatenate([q5[:, 0], zq], axis=-1),
                        jnp.concatenate([zq, q5[:, 1]], axis=-1)], axis=1)
    q_rows = q_rows.reshape(DEC_BATCH, N_HEADS * SAMPLE_Q, 2 * HEAD_DIM)
    w_rows = jnp.pad(wit_s.reshape(N_IDX_HEADS, DEC_BATCH, DEC_SEQ).transpose(1, 0, 2),
                     ((0, 0), (0, 0), (0, SAMPLE_Q - DEC_SEQ)))
    w_rows = jnp.broadcast_to(w_rows.reshape(DEC_BATCH, N_IDX_HEADS * SAMPLE_Q, 1),
                              (DEC_BATCH, N_IDX_HEADS * SAMPLE_Q, LANES))
    new_cols = lambda a: a.reshape(a.shape[0], DEC_BATCH, DEC_SEQ).transpose(1, 0, 2)
    n_pool = cache_k.shape[1]
    ck_t = cache_k[0].transpose(0, 2, 3, 1).reshape(n_pool, N_KV * HEAD_DIM, PAGE_SIZE)
    cv_t = cache_v[0].transpose(0, 2, 3, 1).reshape(n_pool, N_KV * HEAD_DIM, PAGE_SIZE)
    cki_t = cache_k_idx[0].transpose(0, 2, 1)
    attn_s8 = _attn_sample(page_table, qi_rows, q_rows, w_rows, new_cols(kit_s[0]),
                           new_cols(kt_s[0]), new_cols(vt_s[0]), ck_t, cv_t, cki_t)
    attn_s = attn_s8[:, :DEC_SEQ, :].reshape(rows_s, W_A).T
    ri = np.arange(rows_s)
    same_seq = (ri[:, None] // DEC_SEQ) == (ri[None, :] // DEC_SEQ)
    wmix_s = jnp.zeros((N_GROUPS_B, rows_s, rows_s), F32)
    for t in range(DEC_SEQ):
        for s in range(t + 1):
            hit = same_seq & (ri[:, None] % DEC_SEQ == t) & (ri[None, :] % DEC_SEQ == s)
            wmix_s = wmix_s + jnp.where(hit[None], w_s[0, :, t, s][:, None, None], 0.0)
    wmix_s = wmix_s.astype(BF16)
    bsb_s = jnp.broadcast_to(jnp.tile(b_s[0][:, :DEC_SEQ], (1, DEC_BATCH))[:, :, None],
                             (N_GROUPS_B, rows_s, GROUP_W_B))
    y_s, vn_s = _mixer(xs, attn_s, wb, wmix_s, bsb_s, lnvg, lnvb, wout, lng, lnb,
                       chunk=rows_s, vn_last_only=False)

    def heads_last(a, n, t):
        a = a.reshape(a.shape[0], N_KV, HEAD_DIM, a.shape[2]).transpose(0, 3, 1, 2)
        return a.reshape(1, n, t, N_KV, HEAD_DIM)

    return (y_p.reshape(BATCH, SEQ, D_MODEL),
            y_s.reshape(DEC_BATCH, DEC_SEQ, D_MODEL),
            heads_last(kt_p, BATCH, SEQ), heads_last(vt_p, BATCH, SEQ),
            kit_p.transpose(0, 2, 1)[None],
            vn_p.reshape(1, BATCH, CHUNK, W_B),
            heads_last(kt_s, DEC_BATCH, DEC_SEQ), heads_last(vt_s, DEC_BATCH, DEC_SEQ),
            kit_s[0].T.reshape(1, DEC_BATCH, DEC_SEQ, IDX_DIM),
            vn_s.reshape(1, DEC_BATCH, DEC_SEQ, W_B))
```

```python
import functools

import jax
import jax.numpy as jnp
import numpy as np
from jax import lax
from jax.experimental import pallas as pl
from jax.experimental.pallas import tpu as pltpu

F32 = jnp.float32
BF16 = jnp.bfloat16

D_MODEL = 1024
SEQ = 2048
BATCH = 8
DEC_BATCH = 128
DEC_SEQ = 4
PAST_LEN = 2048
PAGE_SIZE = 128
N_PAGES = PAST_LEN // PAGE_SIZE
W_A = 512
W_B = 512
HEAD_DIM = 64
N_HEADS = 8
N_KV = 2
N_IDX_HEADS = 8
IDX_DIM = 64
TOPK = 256
CHUNK = 128
N_GROUPS_B = 4
GROUP_W_B = 128
ROPE_THETA = 10000.0
LN_EPS = 1e-5
ALPHA = 2.0 ** 0.25
NEG = -1e30
IDX_SCALE = float((N_IDX_HEADS * IDX_DIM) ** -0.5)

LANES = 128
KEY_TILE = 128
KEY_STEP = 128
KEY_VARIANTS = (1, 2, 3, 4, 5, 6, 7, 8, 10, 12, 14, 16)
ROW_TILE = 512
WT_Q, WT_QI, WT_V, WT_K, WT_KI, WT_WI, WT_ROWS = 0, 512, 1024, 1152, 1280, 1344, 1360
BISECT_ITERS = 20
FOLD_CHAINS = 8
ONES_ROWS = 16
Q_SCALE = float(HEAD_DIM ** -0.5 * 1.4426950408889634)
VMEM_LIMIT = 48 * 1024 * 1024

SAMPLE_GROUP = 4
SAMPLE_IDX_GROUP = 8
SAMPLE_KEYS = PAST_LEN + KEY_TILE
SAMPLE_Q = 8


def _proj_kernel(x_ref, wt_ref, cost_ref, sint_ref,
                 kt_ref, vt32_ref, kit_ref, kbf_ref, kiw_ref, qt_ref, qit_ref, vt_ref, wit_ref):
    tm = x_ref.shape[0]
    xb = x_ref[...].astype(BF16)
    zt = lax.dot_general(wt_ref[...], xb, (((1,), (1,)), ((), ())),
                         preferred_element_type=F32)
    cost = cost_ref[...]
    sint = sint_ref[...]
    half = HEAD_DIM // 2

    def rope_rows(base, n_heads):
        pieces = []
        for h in range(n_heads):
            x1 = zt[base + h * HEAD_DIM: base + h * HEAD_DIM + half]
            x2 = zt[base + h * HEAD_DIM + half: base + (h + 1) * HEAD_DIM]
            pieces.append(x1 * cost - x2 * sint)
            pieces.append(x2 * cost + x1 * sint)
        return jnp.concatenate(pieces, axis=0)

    qt_ref[...] = rope_rows(WT_Q, N_HEADS).astype(BF16)
    qit_ref[...] = rope_rows(WT_QI, N_IDX_HEADS).astype(BF16)
    v_t = zt[WT_V:WT_V + 128]
    vt32_ref[0] = v_t
    vtb = v_t.astype(BF16)
    for c in range(tm // KEY_STEP):
        vt_ref[c] = vtb[:, c * KEY_STEP:(c + 1) * KEY_STEP]
    k_t = rope_rows(WT_K, N_KV)
    ki_t = rope_rows(WT_KI, 1)
    kt_ref[0] = k_t
    kit_ref[0] = ki_t
    kbf_ref[...] = k_t.T.astype(BF16)
    kiw_ref[...] = jnp.concatenate([ki_t, jnp.zeros_like(ki_t)], axis=0).T.astype(BF16)
    wit_ref[...] = zt[WT_WI:WT_WI + N_IDX_HEADS] * IDX_SCALE


def _project(x2d, wt, cost, sint, seq_len):
    rows = x2d.shape[0]
    tm = ROW_TILE
    grid = (rows // tm,)
    n_pos_tiles = seq_len // tm
    row_map = lambda r: (r, 0)
    col_map = lambda r: (0, r)
    const = lambda r: (0, 0)
    seq_map = lambda r: (r // n_pos_tiles, 0, r % n_pos_tiles)
    out_shape = (
        jax.ShapeDtypeStruct((rows // seq_len, 128, seq_len), F32),
        jax.ShapeDtypeStruct((rows // seq_len, 128, seq_len), F32),
        jax.ShapeDtypeStruct((rows // seq_len, IDX_DIM, seq_len), F32),
        jax.ShapeDtypeStruct((rows, 128), BF16),
        jax.ShapeDtypeStruct((rows, 128), BF16),
        jax.ShapeDtypeStruct((512, rows), BF16),
        jax.ShapeDtypeStruct((512, rows), BF16),
        jax.ShapeDtypeStruct((rows // KEY_STEP, 128, KEY_STEP), BF16),
        jax.ShapeDtypeStruct((8, rows), F32),
    )
    out_specs = (
        pl.BlockSpec((1, 128, tm), seq_map),
        pl.BlockSpec((1, 128, tm), seq_map),
        pl.BlockSpec((1, IDX_DIM, tm), seq_map),
        pl.BlockSpec((tm, 128), row_map),
        pl.BlockSpec((tm, 128), row_map),
        pl.BlockSpec((512, tm), col_map),
        pl.BlockSpec((512, tm), col_map),
        pl.BlockSpec((tm // KEY_STEP, 128, KEY_STEP), lambda r: (r, 0, 0)),
        pl.BlockSpec((8, tm), col_map),
    )
    in_specs = [
        pl.BlockSpec((tm, D_MODEL), row_map),
        pl.BlockSpec(wt.shape, const),
        pl.BlockSpec((HEAD_DIM // 2, tm), lambda r: (0, r % n_pos_tiles)),
        pl.BlockSpec((HEAD_DIM // 2, tm), lambda r: (0, r % n_pos_tiles)),
    ]
    return pl.pallas_call(
        _proj_kernel, grid=grid, in_specs=in_specs, out_specs=out_specs, out_shape=out_shape,
        compiler_params=pltpu.CompilerParams(dimension_semantics=("arbitrary",),
                                             vmem_limit_bytes=VMEM_LIMIT),
        name="proj_attn_side",
    )(x2d, wt, cost, sint)


def _rope_tables(pos):
    half = HEAD_DIM // 2
    inv = ROPE_THETA ** (-np.arange(half, dtype=np.float64) / half)
    ang = inv[:, None] * pos.astype(np.float64)[None, :]
    return jnp.asarray(np.cos(ang), dtype=F32), jnp.asarray(np.sin(ang), dtype=F32)


def _select_threshold(count_ge, kth_tie_index, min_ge, max_lt, lo0, hi0, n_allowed, n_keys):
    k = float(TOPK)

    def split(mid, lo, hi, clo, chi):
        c = count_ge(mid)
        take = c >= k
        return (jnp.where(take, mid, lo), jnp.where(take, hi, mid),
                jnp.where(take, c, clo), jnp.where(take, chi, c))

    def step(_, carry):
        lo, hi, clo, chi = carry
        return split(0.5 * lo + 0.5 * hi, lo, hi, clo, chi)

    lo, hi, clo, chi = lax.fori_loop(0, BISECT_ITERS, step,
                                     (lo0, hi0, n_allowed, jnp.zeros_like(lo0)))
    big = jnp.full_like(lo, float(4 * n_keys))

    def any_true(mask):
        return jnp.max(jnp.where(mask, 1.0, 0.0)) > 0.5

    def exact(args):
        lo, hi, clo, chi = args

        def cond(c):
            _, _, clo, _, vlo, vhi, it = c
            open_ = jnp.logical_and(clo > k, vlo < vhi)
            return jnp.logical_and(any_true(open_), it < 2 * n_keys)

        def body(c):
            lo, hi, clo, chi, vlo, vhi, it = c
            mid = 0.5 * vlo + 0.5 * vhi
            lo, hi, clo, chi = split(jnp.where(mid > vlo, mid, vhi), lo, hi, clo, chi)
            return lo, hi, clo, chi, min_ge(lo), max_lt(hi), it + 1

        lo, hi, clo, chi, vlo, _, _ = lax.while_loop(
            cond, body, (lo, hi, clo, chi, min_ge(lo), max_lt(hi), jnp.int32(0)))
        return vlo, jnp.where(clo > k, kth_tie_index(vlo, k - chi), big)

    def fast(args):
        return args[0], big

    return lax.cond(any_true(clo > k), exact, fast, (lo, hi, clo, chi))


def _fold_rows(x, op):
    groups = [x[r:r + 8] for r in range(0, x.shape[0], 8)]
    parts = groups[:FOLD_CHAINS]
    for n, grp in enumerate(groups[FOLD_CHAINS:]):
        parts[n % FOLD_CHAINS] = op(parts[n % FOLD_CHAINS], grp)
    while len(parts) > 1:
        nxt = [op(parts[a], parts[a + 1]) for a in range(0, len(parts) - 1, 2)]
        if len(parts) % 2:
            nxt.append(parts[-1])
        parts = nxt
    return parts[0]


def _topk_bias_in_place(s_ref, n_keys, n_allowed, lo0=None, hi0=None):
    lanes = s_ref.shape[1]
    inf = jnp.float32(jnp.inf)
    chunk = 8 * FOLD_CHAINS
    kidx0 = lax.broadcasted_iota(jnp.int32, (chunk, lanes), 0).astype(F32)

    def key_reduce(fn, op, finish):
        part = None
        for r in range(0, n_keys, chunk):
            v = fn(s_ref[r:r + chunk, :], r)
            part = v if part is None else op(part, v)
        return finish(_fold_rows(part, op), axis=0, keepdims=True)

    count_ge = lambda thr: key_reduce(lambda t, r: jnp.where(t >= thr, 1.0, 0.0), jnp.add, jnp.sum)
    tri = (lax.broadcasted_iota(jnp.int32, (chunk, chunk), 0)
           >= lax.broadcasted_iota(jnp.int32, (chunk, chunk), 1)).astype(BF16)

    def kth_tie_index(thr, need):
        seen = jnp.zeros((1, lanes), F32)
        best = jnp.full((1, lanes), -1.0, F32)
        for r in range(0, n_keys, chunk):
            tie = jnp.where(s_ref[r:r + chunk, :] == thr, 1.0, 0.0)
            upto = jnp.dot(tri, tie.astype(BF16), preferred_element_type=F32) + seen
            hit = jnp.where(tie > 0.5, jnp.where(upto == need, kidx0 + r, -1.0), -1.0)
            best = jnp.maximum(best, jnp.max(_fold_rows(hit, jnp.maximum), axis=0, keepdims=True))
            seen = upto[chunk - 1:chunk, :]
        return best
    min_ge = lambda thr: key_reduce(lambda t, r: jnp.where(t >= thr, t, inf), jnp.minimum, jnp.min)
    max_lt = lambda thr: key_reduce(lambda t, r: jnp.where(t < thr, t, -inf), jnp.maximum, jnp.max)
    if lo0 is None:
        lo0 = key_reduce(lambda t, r: jnp.where(t == -inf, inf, t), jnp.minimum, jnp.min)
        mx1 = key_reduce(lambda t, r: t, jnp.maximum, jnp.max)
        hi0 = mx1 + (jnp.abs(mx1) * (2.0 ** -20) + 1e-30)
    lo, jmax = _select_threshold(count_ge, kth_tie_index, min_ge, max_lt, lo0, hi0, n_allowed, n_keys)
    for r in range(0, n_keys, chunk):
        t = s_ref[r:r + chunk, :]
        tie = jnp.where(t == lo, jnp.where(kidx0 <= jmax - r, 0.0, NEG), NEG)
        s_ref[r:r + chunk, :] = jnp.where(t > lo, 0.0, tie)


def _attn_prompt_kernel(qt_ref, qit_ref, wit_ref, kbf_ref, kiw_ref, vt_ref, o_ref,
                        s_ref, sc_ref, ri_ref, rq_ref):
    i = pl.program_id(1)
    w = wit_ref[...]
    zeros_half = jnp.zeros((HEAD_DIM, LANES), BF16)

    for hp in range(4):
        a = qit_ref[(2 * hp) * 64:(2 * hp + 1) * 64, :]
        b = qit_ref[(2 * hp + 1) * 64:(2 * hp + 2) * 64, :]
        ri_ref[hp] = jnp.concatenate(
            [jnp.concatenate([a, zeros_half], axis=0), jnp.concatenate([b, zeros_half], axis=0)], axis=1)
        g = hp // 2
        a = qt_ref[(2 * hp) * 64:(2 * hp + 1) * 64, :]
        b = qt_ref[(2 * hp + 1) * 64:(2 * hp + 2) * 64, :]
        if g == 0:
            ab = jnp.concatenate([jnp.concatenate([a, zeros_half], axis=0),
                                  jnp.concatenate([b, zeros_half], axis=0)], axis=1)
        else:
            ab = jnp.concatenate([jnp.concatenate([zeros_half, a], axis=0),
                                  jnp.concatenate([zeros_half, b], axis=0)], axis=1)
        rq_ref[hp] = ab

    inf = jnp.float32(jnp.inf)
    n_units = (i * LANES + LANES + KEY_STEP - 1) // KEY_STEP
    n_steps = jnp.int32(KEY_VARIANTS[-1])
    for v in reversed(KEY_VARIANTS[:-1]):
        n_steps = jnp.where(n_units <= v, v, n_steps)

    def process(ns):
        n_keys = ns * KEY_STEP
        keys = slice(0, n_keys)
        row = lax.broadcasted_iota(jnp.int32, (n_keys, LANES), 0)
        col = lax.broadcasted_iota(jnp.int32, (n_keys, LANES), 1)
        qpos = col + i * LANES
        allowed = row <= qpos

        kt = kiw_ref[keys, :]
        acc = jnp.zeros((n_keys, LANES), F32)
        for hp in range(4):
            s2 = jnp.dot(kt, ri_ref[hp], preferred_element_type=F32)
            acc = acc + jnp.maximum(s2[:, :LANES], 0.0) * w[2 * hp:2 * hp + 1, :]
            acc = acc + jnp.maximum(s2[:, LANES:], 0.0) * w[2 * hp + 1:2 * hp + 2, :]
        s_ref[keys, :] = jnp.where(allowed, acc, -inf)
        lo0 = jnp.min(_fold_rows(jnp.where(allowed, acc, inf), jnp.minimum), axis=0, keepdims=True)
        mx1 = jnp.max(_fold_rows(jnp.where(allowed, acc, -inf), jnp.maximum), axis=0, keepdims=True)
        hi0 = mx1 + (jnp.abs(mx1) * (2.0 ** -20) + 1e-30)
        n_allowed = (qpos[0:1, :] + 1).astype(F32)

        _topk_bias_in_place(s_ref, n_keys, n_allowed, lo0, hi0)

        kt = kbf_ref[keys, :]
        b = s_ref[keys, :]
        b2 = jnp.concatenate([b, b], axis=1)
        ms = []
        for hp in range(4):
            s2 = jnp.dot(kt, rq_ref[hp], preferred_element_type=F32) + b2
            sc_ref[hp, keys, :] = s2
            ms.append(jnp.max(_fold_rows(s2, jnp.maximum), axis=0, keepdims=True))
        vt = jnp.concatenate([vt_ref[j] for j in range(ns)], axis=1)
        ones_rows = jnp.ones((ONES_ROWS, n_keys), BF16)
        lhs = [jnp.concatenate([vt[g * 64:(g + 1) * 64, :], ones_rows], axis=0) for g in range(N_KV)]
        pieces = []
        for hp in range(4):
            p = jnp.exp2((sc_ref[hp, keys, :] - ms[hp]).astype(BF16))
            o = jnp.dot(lhs[hp // 2], p, preferred_element_type=F32)
            o2 = o[0:HEAD_DIM, :] / o[HEAD_DIM:HEAD_DIM + 1, :]
            pieces.append(o2[:, :LANES])
            pieces.append(o2[:, LANES:])
        o_ref[...] = jnp.concatenate(pieces, axis=0)

    for ns in KEY_VARIANTS:
        pl.when(n_steps == ns)(functools.partial(process, ns))


def _attn_prompt(qt, qit, wit, kbf, kiw, vt3):
    n_blocks = SEQ // LANES
    grid = (BATCH, n_blocks)
    qmap = lambda n, i: (0, n * n_blocks + i)
    in_specs = [
        pl.BlockSpec((512, LANES), qmap),
        pl.BlockSpec((512, LANES), qmap),
        pl.BlockSpec((8, LANES), qmap),
        pl.BlockSpec((SEQ, 128), lambda n, i: (n, 0)),
        pl.BlockSpec((SEQ, 128), lambda n, i: (n, 0)),
        pl.BlockSpec((SEQ // KEY_STEP, 128, KEY_STEP), lambda n, i: (n, 0, 0)),
    ]
    return pl.pallas_call(
        _attn_prompt_kernel, grid=grid, in_specs=in_specs,
        out_specs=pl.BlockSpec((W_A, LANES), qmap),
        out_shape=jax.ShapeDtypeStruct((W_A, BATCH * SEQ), F32),
        scratch_shapes=[pltpu.VMEM((SEQ, LANES), F32), pltpu.VMEM((4, SEQ, 2 * LANES), F32),
                        pltpu.VMEM((4, 128, 2 * LANES), BF16), pltpu.VMEM((4, 128, 2 * LANES), BF16)],
        compiler_params=pltpu.CompilerParams(dimension_semantics=("arbitrary", "arbitrary"),
                                             vmem_limit_bytes=VMEM_LIMIT),
        name="attn_prompt",
    )(qt, qit, wit, kbf, kiw, vt3)


def _page_copies(pt_ref, group, slot, gsz, streams):
    out = []
    for g in range(gsz):
        seq = group * gsz + g
        for p in range(N_PAGES):
            page = pt_ref[seq, p]
            keys = pl.ds(p * PAGE_SIZE, PAGE_SIZE)
            for hbm, buf, sem in streams:
                out.append(pltpu.make_async_copy(hbm.at[page], buf.at[slot, g, :, keys], sem.at[slot]))
    return out


def _paged_prefetch(pt_ref, gsz, streams):
    step = pl.program_id(0)
    n_steps = pl.num_programs(0)
    slot = step % 2

    @pl.when(step == 0)
    def _():
        tail = pl.ds(PAST_LEN, KEY_TILE)
        for _, buf, _ in streams:
            for s_ in range(2):
                for g in range(gsz):
                    buf[s_, g, :, tail] = jnp.zeros((buf.shape[2], KEY_TILE), F32)
        for c in _page_copies(pt_ref, 0, 0, gsz, streams):
            c.start()

    @pl.when(step + 1 < n_steps)
    def _():
        for c in _page_copies(pt_ref, step + 1, 1 - slot, gsz, streams):
            c.start()

    for c in _page_copies(pt_ref, step, slot, gsz, streams):
        c.wait()
    return slot


def _sample_index_kernel(pt_ref, qi_ref, w_ref, kin_ref, cki_hbm, s_out_ref, kibuf, sem):
    gsz = SAMPLE_IDX_GROUP
    slot = _paged_prefetch(pt_ref, gsz, [(cki_hbm, kibuf, sem)])
    rows8 = lax.broadcasted_iota(jnp.int32, (SAMPLE_Q, SAMPLE_KEYS), 0)
    keyi = lax.broadcasted_iota(jnp.int32, (SAMPLE_Q, SAMPLE_KEYS), 1)
    allowed = keyi <= (PAST_LEN + rows8)
    new_keys = pl.ds(PAST_LEN, DEC_SEQ)
    scores = []
    for g in range(gsz):
        kibuf[slot, g, :, new_keys] = kin_ref[g]
        kib = kibuf[slot, g].astype(BF16)
        s = jnp.dot(qi_ref[g], kib, preferred_element_type=F32)
        sw = jnp.maximum(s, 0.0) * w_ref[g][:, 0:1]
        acc = jnp.sum(sw.reshape(N_IDX_HEADS, SAMPLE_Q, SAMPLE_KEYS), axis=0)
        scores.append(jnp.where(allowed, acc, -jnp.inf))
    for a in range(gsz // 2):
        s_out_ref[a] = jnp.where(rows8 < DEC_SEQ, scores[2 * a],
                                 pltpu.roll(scores[2 * a + 1], DEC_SEQ, 0))


def _sample_select_kernel(s_ref, b_ref, st_ref):
    rows = s_ref.shape[0]
    st_ref[...] = s_ref[...].T
    t4 = lax.broadcasted_iota(jnp.int32, (1, rows), 1) % DEC_SEQ
    _topk_bias_in_place(st_ref, SAMPLE_KEYS, (PAST_LEN + 1 + t4).astype(F32))
    b_ref[...] = st_ref[...].T


def _sample_attend_kernel(pt_ref, q_ref, b_ref, kn_ref, vn_ref, ck_hbm, cv_hbm, o_ref,
                          kbuf, vbuf, ksem, vsem):
    gsz = SAMPLE_GROUP
    slot = _paged_prefetch(pt_ref, gsz, [(ck_hbm, kbuf, ksem), (cv_hbm, vbuf, vsem)])
    new_keys = pl.ds(PAST_LEN, DEC_SEQ)
    lane = lax.broadcasted_iota(jnp.int32, (SAMPLE_Q, LANES), 1)
    for g in range(gsz):
        kbuf[slot, g, :, new_keys] = kn_ref[g]
        vbuf[slot, g, :, new_keys] = vn_ref[g]
        kb = kbuf[slot, g].astype(BF16)
        vb = vbuf[slot, g].astype(BF16)
        b = b_ref[g // 2]
        if g % 2:
            b = pltpu.roll(b, DEC_SEQ, 0)
        s = jnp.dot(q_ref[g], kb, preferred_element_type=F32)
        s = s + jnp.concatenate([b] * N_HEADS, axis=0)
        m = jnp.max(s, axis=1, keepdims=True)
        p = jnp.exp2(s - m)
        l = jnp.sum(p, axis=1, keepdims=True)
        o = lax.dot_general(p.astype(BF16), vb, (((1,), (1,)), ((), ())),
                            preferred_element_type=F32) / l
        o_r = pltpu.roll(o, HEAD_DIM, 1)
        tiles = []
        for c in range(4):
            ha, hb = 2 * c, 2 * c + 1
            src_a = o if ha // 4 == 0 else o_r
            src_b = o if hb // 4 == 1 else o_r
            tiles.append(jnp.where(lane < HEAD_DIM, src_a[ha * SAMPLE_Q:(ha + 1) * SAMPLE_Q, :],
                                   src_b[hb * SAMPLE_Q:(hb + 1) * SAMPLE_Q, :]))
        o_ref[g] = jnp.concatenate(tiles, axis=1)


def _attn_sample(page_table, qi_s, q_s, w_s, ki_new, k_new, v_new, ck, cv, cki):
    params = pltpu.CompilerParams(dimension_semantics=("arbitrary",), vmem_limit_bytes=VMEM_LIMIT)
    blk = lambda n, shape: pl.BlockSpec((n,) + shape, lambda s, pt: (s, 0, 0))
    hbm = pl.BlockSpec(memory_space=pl.ANY)
    n_pairs = DEC_BATCH // 2

    gi = SAMPLE_IDX_GROUP
    scores = pl.pallas_call(
        _sample_index_kernel,
        grid_spec=pltpu.PrefetchScalarGridSpec(
            num_scalar_prefetch=1, grid=(DEC_BATCH // gi,),
            in_specs=[blk(gi, (64, IDX_DIM)), blk(gi, (64, LANES)), blk(gi, (IDX_DIM, DEC_SEQ)), hbm],
            out_specs=blk(gi // 2, (SAMPLE_Q, SAMPLE_KEYS)),
            scratch_shapes=[pltpu.VMEM((2, gi, IDX_DIM, SAMPLE_KEYS), F32),
                            pltpu.SemaphoreType.DMA((2,))]),
        out_shape=jax.ShapeDtypeStruct((n_pairs, SAMPLE_Q, SAMPLE_KEYS), F32),
        compiler_params=params, name="sample_index",
    )(page_table, qi_s, w_s, ki_new, cki)

    rows = n_pairs * SAMPLE_Q
    bias = pl.pallas_call(
        _sample_select_kernel, grid=(1,),
        in_specs=[pl.BlockSpec((rows, SAMPLE_KEYS), lambda s: (0, 0))],
        out_specs=pl.BlockSpec((rows, SAMPLE_KEYS), lambda s: (0, 0)),
        out_shape=jax.ShapeDtypeStruct((rows, SAMPLE_KEYS), F32),
        scratch_shapes=[pltpu.VMEM((SAMPLE_KEYS, rows), F32)],
        compiler_params=params, name="sample_select",
    )(scores.reshape(rows, SAMPLE_KEYS)).reshape(n_pairs, SAMPLE_Q, SAMPLE_KEYS)

    ga = SAMPLE_GROUP
    return pl.pallas_call(
        _sample_attend_kernel,
        grid_spec=pltpu.PrefetchScalarGridSpec(
            num_scalar_prefetch=1, grid=(DEC_BATCH // ga,),
            in_specs=[blk(ga, (64, 128)), blk(ga // 2, (SAMPLE_Q, SAMPLE_KEYS)),
                      blk(ga, (128, DEC_SEQ)), blk(ga, (128, DEC_SEQ)), hbm, hbm],
            out_specs=blk(ga, (SAMPLE_Q, W_A)),
            scratch_shapes=[pltpu.VMEM((2, ga, 128, SAMPLE_KEYS), F32),
                            pltpu.VMEM((2, ga, 128, SAMPLE_KEYS), F32),
                            pltpu.SemaphoreType.DMA((2,)), pltpu.SemaphoreType.DMA((2,))]),
        out_shape=jax.ShapeDtypeStruct((DEC_BATCH, SAMPLE_Q, W_A), F32),
        compiler_params=params, name="sample_attend",
    )(page_table, q_s, bias, k_new, v_new, ck, cv)


def _layer_norm(x, g, b):
    mu = jnp.mean(x, axis=-1, keepdims=True)
    xc = x - mu
    var = jnp.mean(xc * xc, axis=-1, keepdims=True)
    return xc * lax.rsqrt(var + LN_EPS) * g + b


def _silu(x):
    return x / (1.0 + jnp.exp(-x))


def _mixer_kernel(x_ref, a_ref, wb_ref, wmix_ref, bsb_ref, lnvg_ref, lnvb_ref, wout_ref,
                  lng_ref, lnb_ref, y_ref, vn_ref, *, chunk, vn_last_only, tiles_per_seq):
    tm = x_ref.shape[0]
    x = x_ref[...]
    xb = x.astype(BF16)
    last = (((1,), (1,)), ((), ()))
    ga_t = lax.dot_general(wb_ref[0:W_A, :], xb, last, preferred_element_type=F32)
    zb = lax.dot_general(xb, wb_ref[W_A:, :], last, preferred_element_type=F32)
    u = zb[:, 0:512]
    vb = zb[:, 512:1024]
    gb = zb[:, 1024:1536]
    vn = _layer_norm(vb, lnvg_ref[...], lnvb_ref[...])
    if vn_last_only:
        @pl.when(pl.program_id(0) % tiles_per_seq == tiles_per_seq - 1)
        def _():
            vn_ref[0] = vn[tm - CHUNK:, :]
    else:
        vn_ref[...] = vn
    vnb = vn.astype(BF16)
    rows = []
    for c in range(tm // chunk):
        cols = []
        for g in range(N_GROUPS_B):
            blk = vnb[c * chunk:(c + 1) * chunk, g * GROUP_W_B:(g + 1) * GROUP_W_B]
            cols.append(jnp.dot(wmix_ref[g], blk, preferred_element_type=F32) + bsb_ref[g])
        rows.append(jnp.concatenate(cols, axis=1))
    mixed = jnp.concatenate(rows, axis=0) if len(rows) > 1 else rows[0]
    a_out_t = (a_ref[...] * _silu(ga_t)).astype(BF16)
    b_out = (u * mixed * _silu(gb)).astype(BF16)
    out = (lax.dot_general(a_out_t, wout_ref[0:W_A, :], (((0,), (0,)), ((), ())),
                           preferred_element_type=F32)
           + jnp.dot(b_out, wout_ref[W_A:, :], preferred_element_type=F32))
    y_ref[...] = _layer_norm(ALPHA * x + out, lng_ref[...], lnb_ref[...])


def _mixer(x2d, attn, wb, wmix, bsb, lnvg, lnvb, wout, lng, lnb, *, chunk, vn_last_only):
    rows = x2d.shape[0]
    tm = ROW_TILE
    tiles_per_seq = SEQ // tm
    grid = (rows // tm,)
    row_map = lambda r: (r, 0)
    const2 = lambda r: (0, 0)
    const3 = lambda r: (0, 0, 0)
    if vn_last_only:
        vn_shape = jax.ShapeDtypeStruct((rows // SEQ, CHUNK, W_B), F32)
        vn_spec = pl.BlockSpec((1, CHUNK, W_B), lambda r: (r // tiles_per_seq, 0, 0))
    else:
        vn_shape = jax.ShapeDtypeStruct((rows, W_B), F32)
        vn_spec = pl.BlockSpec((tm, W_B), row_map)
    kern = functools.partial(_mixer_kernel, chunk=chunk, vn_last_only=vn_last_only,
                             tiles_per_seq=tiles_per_seq)
    return pl.pallas_call(
        kern, grid=grid,
        in_specs=[pl.BlockSpec((tm, D_MODEL), row_map), pl.BlockSpec((W_A, tm), lambda r: (0, r)),
                  pl.BlockSpec(wb.shape, const2), pl.BlockSpec(wmix.shape, const3),
                  pl.BlockSpec(bsb.shape, const3), pl.BlockSpec((1, W_B), const2),
                  pl.BlockSpec((1, W_B), const2), pl.BlockSpec(wout.shape, const2),
                  pl.BlockSpec((1, D_MODEL), const2), pl.BlockSpec((1, D_MODEL), const2)],
        out_specs=(pl.BlockSpec((tm, D_MODEL), row_map), vn_spec),
        out_shape=(jax.ShapeDtypeStruct((rows, D_MODEL), F32), vn_shape),
        compiler_params=pltpu.CompilerParams(dimension_semantics=("arbitrary",),
                                             vmem_limit_bytes=VMEM_LIMIT),
        name="mixer_chunk%d" % chunk,
    )(x2d, attn, wb, wmix, bsb, lnvg, lnvb, wout, lng, lnb)


def kernel(x_prompt, x_sample, cache_k, cache_v, cache_k_idx, page_table, w_in, w_s, b_s,
           ln_v_g, ln_v_b, w_out, ln_g, ln_b):
    w = w_in[0]
    wq, wk, wv = w[:, 0:512], w[:, 512:640], w[:, 640:768]
    wga, wqi, wki, wwi = w[:, 768:1280], w[:, 1280:1792], w[:, 1792:1856], w[:, 1856:1864]
    wu, wvb, wgb = w[:, 1864:2376], w[:, 2376:2888], w[:, 2888:3400]
    wt = jnp.concatenate([wq.T * Q_SCALE, wqi.T, wv.T, wk.T, wki.T, wwi.T,
                          jnp.zeros((WT_ROWS - WT_WI - N_IDX_HEADS, D_MODEL), F32)], axis=0).astype(BF16)
    wb = jnp.concatenate([wga.T, wu.T, wvb.T, wgb.T], axis=0).astype(BF16)
    wout = w_out[0].astype(BF16)
    lnvg, lnvb = ln_v_g[0][None, :], ln_v_b[0][None, :]
    lng, lnb = ln_g[0][None, :], ln_b[0][None, :]

    xp = x_prompt.reshape(BATCH * SEQ, D_MODEL)
    tabs_p = _rope_tables(np.arange(SEQ))
    kt_p, vt_p, kit_p, kbf, kiw, qt, qit, vt3, wit = _project(xp, wt, *tabs_p, SEQ)
    attn_p = _attn_prompt(qt, qit, wit, kbf, kiw, vt3)
    tril = np.tril(np.ones((CHUNK, CHUNK), np.float32))
    wmix_p = (w_s[0] * tril[None]).astype(BF16)
    bsb_p = jnp.broadcast_to(b_s[0][:, :, None], (N_GROUPS_B, CHUNK, GROUP_W_B))
    y_p, vn_p = _mixer(xp, attn_p, wb, wmix_p, bsb_p, lnvg, lnvb, wout, lng, lnb,
                       chunk=CHUNK, vn_last_only=True)

    rows_s = DEC_BATCH * DEC_SEQ
    xs = x_sample.reshape(rows_s, D_MODEL)
    tabs_s = _rope_tables(PAST_LEN + (np.arange(rows_s) % DEC_SEQ))
    kt_s, vt_s, kit_s, _, _, qt_s, qit_s, _, wit_s = _project(xs, wt, *tabs_s, rows_s)

    def to_rows(a_t):
        a = a_t.reshape(N_HEADS, HEAD_DIM, DEC_BATCH, DEC_SEQ).transpose(2, 0, 3, 1)
        return jnp.pad(a, ((0, 0), (0, 0), (0, SAMPLE_Q - DEC_SEQ), (0, 0)))

    qi_rows = to_rows(qit_s).reshape(DEC_BATCH, N_IDX_HEADS * SAMPLE_Q, IDX_DIM)
    q5 = to_rows(qt_s).reshape(DEC_BATCH, N_KV, N_HEADS // N_KV, SAMPLE_Q, HEAD_DIM)
    zq = jnp.zeros_like(q5[:, 0])
    q_rows = jnp.stack([jnp.concatenate([q5[:, 0], zq], axis=-1),
                        jnp.concatenate([zq, q5[:, 1]], axis=-1)], axis=1)
    q_rows = q_rows.reshape(DEC_BATCH, N_HEADS * SAMPLE_Q, 2 * HEAD_DIM)
    w_rows = jnp.pad(wit_s.reshape(N_IDX_HEADS, DEC_BATCH, DEC_SEQ).transpose(1, 0, 2),
                     ((0, 0), (0, 0), (0, SAMPLE_Q - DEC_SEQ)))
    w_rows = jnp.broadcast_to(w_rows.reshape(DEC_BATCH, N_IDX_HEADS * SAMPLE_Q, 1),
                              (DEC_BATCH, N_IDX_HEADS * SAMPLE_Q, LANES))
    new_cols = lambda a: a.reshape(a.shape[0], DEC_BATCH, DEC_SEQ).transpose(1, 0, 2)
    n_pool = cache_k.shape[1]
    ck_t = cache_k[0].transpose(0, 2, 3, 1).reshape(n_pool, N_KV * HEAD_DIM, PAGE_SIZE)
    cv_t = cache_v[0].transpose(0, 2, 3, 1).reshape(n_pool, N_KV * HEAD_DIM, PAGE_SIZE)
    cki_t = cache_k_idx[0].transpose(0, 2, 1)
    attn_s8 = _attn_sample(page_table, qi_rows, q_rows, w_rows, new_cols(kit_s[0]),
                           new_cols(kt_s[0]), new_cols(vt_s[0]), ck_t, cv_t, cki_t)
    attn_s = attn_s8[:, :DEC_SEQ, :].reshape(rows_s, W_A).T
    ri = np.arange(rows_s)
    same_seq = (ri[:, None] // DEC_SEQ) == (ri[None, :] // DEC_SEQ)
    wmix_s = jnp.zeros((N_GROUPS_B, rows_s, rows_s), F32)
    for t in range(DEC_SEQ):
        for s in range(t + 1):
            hit = same_seq & (ri[:, None] % DEC_SEQ == t) & (ri[None, :] % DEC_SEQ == s)
            wmix_s = wmix_s + jnp.where(hit[None], w_s[0, :, t, s][:, None, None], 0.0)
    wmix_s = wmix_s.astype(BF16)
    bsb_s = jnp.broadcast_to(jnp.tile(b_s[0][:, :DEC_SEQ], (1, DEC_BATCH))[:, :, None],
                             (N_GROUPS_B, rows_s, GROUP_W_B))
    y_s, vn_s = _mixer(xs, attn_s, wb, wmix_s, bsb_s, lnvg, lnvb, wout, lng, lnb,
                       chunk=rows_s, vn_last_only=False)

    def heads_last(a, n, t):
        a = a.reshape(a.shape[0], N_KV, HEAD_DIM, a.shape[2]).transpose(0, 3, 1, 2)
        return a.reshape(1, n, t, N_KV, HEAD_DIM)

    return (y_p.reshape(BATCH, SEQ, D_MODEL),
            y_s.reshape(DEC_BATCH, DEC_SEQ, D_MODEL),
            heads_last(kt_p, BATCH, SEQ), heads_last(vt_p, BATCH, SEQ),
            kit_p.transpose(0, 2, 1)[None],
            vn_p.reshape(1, BATCH, CHUNK, W_B),
            heads_last(kt_s, DEC_BATCH, DEC_SEQ), heads_last(vt_s, DEC_BATCH, DEC_SEQ),
            kit_s[0].T.reshape(1, DEC_BATCH, DEC_SEQ, IDX_DIM),
            vn_s.reshape(1, DEC_BATCH, DEC_SEQ, W_B))
```

```python
import functools

import jax
import jax.numpy as jnp
import numpy as np
from jax import lax
from jax.experimental import pallas as pl
from jax.experimental.pallas import tpu as pltpu

F32 = jnp.float32
BF16 = jnp.bfloat16

D_MODEL = 1024
SEQ = 2048
BATCH = 8
DEC_BATCH = 128
DEC_SEQ = 4
PAST_LEN = 2048
PAGE_SIZE = 128
N_PAGES = PAST_LEN // PAGE_SIZE
W_A = 512
W_B = 512
HEAD_DIM = 64
N_HEADS = 8
N_KV = 2
N_IDX_HEADS = 8
IDX_DIM = 64
TOPK = 256
CHUNK = 128
N_GROUPS_B = 4
GROUP_W_B = 128
ROPE_THETA = 10000.0
LN_EPS = 1e-5
ALPHA = 2.0 ** 0.25
NEG = -1e30
IDX_SCALE = float((N_IDX_HEADS * IDX_DIM) ** -0.5)

LANES = 128
KEY_TILE = 128
KEY_STEP = 256
ROW_TILE = 512
WT_Q, WT_QI, WT_V, WT_K, WT_KI, WT_WI, WT_ROWS = 0, 512, 1024, 1152, 1280, 1344, 1360
BISECT_ITERS = 20
FOLD_CHAINS = 8
ONES_ROWS = 16
Q_SCALE = float(HEAD_DIM ** -0.5 * 1.4426950408889634)
VMEM_LIMIT = 48 * 1024 * 1024

SAMPLE_GROUP = 4
SAMPLE_IDX_GROUP = 8
SAMPLE_KEYS = PAST_LEN + KEY_TILE
SAMPLE_Q = 8


def _proj_kernel(x_ref, wt_ref, cost_ref, sint_ref,
                 kt_ref, vt32_ref, kit_ref, kbf_ref, kiw_ref, qt_ref, qit_ref, vt_ref, wit_ref):
    tm = x_ref.shape[0]
    xb = x_ref[...].astype(BF16)
    zt = lax.dot_general(wt_ref[...], xb, (((1,), (1,)), ((), ())),
                         preferred_element_type=F32)
    cost = cost_ref[...]
    sint = sint_ref[...]
    half = HEAD_DIM // 2

    def rope_rows(base, n_heads):
        pieces = []
        for h in range(n_heads):
            x1 = zt[base + h * HEAD_DIM: base + h * HEAD_DIM + half]
            x2 = zt[base + h * HEAD_DIM + half: base + (h + 1) * HEAD_DIM]
            pieces.append(x1 * cost - x2 * sint)
            pieces.append(x2 * cost + x1 * sint)
        return jnp.concatenate(pieces, axis=0)

    qt_ref[...] = rope_rows(WT_Q, N_HEADS).astype(BF16)
    qit_ref[...] = rope_rows(WT_QI, N_IDX_HEADS).astype(BF16)
    v_t = zt[WT_V:WT_V + 128]
    vt32_ref[0] = v_t
    vtb = v_t.astype(BF16)
    for c in range(tm // KEY_STEP):
        vt_ref[c] = vtb[:, c * KEY_STEP:(c + 1) * KEY_STEP]
    k_t = rope_rows(WT_K, N_KV)
    ki_t = rope_rows(WT_KI, 1)
    kt_ref[0] = k_t
    kit_ref[0] = ki_t
    kbf_ref[...] = k_t.T.astype(BF16)
    kiw_ref[...] = jnp.concatenate([ki_t, jnp.zeros_like(ki_t)], axis=0).T.astype(BF16)
    wit_ref[...] = zt[WT_WI:WT_WI + N_IDX_HEADS] * IDX_SCALE


def _project(x2d, wt, cost, sint, seq_len):
    rows = x2d.shape[0]
    tm = ROW_TILE
    grid = (rows // tm,)
    n_pos_tiles = seq_len // tm
    row_map = lambda r: (r, 0)
    col_map = lambda r: (0, r)
    const = lambda r: (0, 0)
    seq_map = lambda r: (r // n_pos_tiles, 0, r % n_pos_tiles)
    out_shape = (
        jax.ShapeDtypeStruct((rows // seq_len, 128, seq_len), F32),
        jax.ShapeDtypeStruct((rows // seq_len, 128, seq_len), F32),
        jax.ShapeDtypeStruct((rows // seq_len, IDX_DIM, seq_len), F32),
        jax.ShapeDtypeStruct((rows, 128), BF16),
        jax.ShapeDtypeStruct((rows, 128), BF16),
        jax.ShapeDtypeStruct((512, rows), BF16),
        jax.ShapeDtypeStruct((512, rows), BF16),
        jax.ShapeDtypeStruct((rows // KEY_STEP, 128, KEY_STEP), BF16),
        jax.ShapeDtypeStruct((8, rows), F32),
    )
    out_specs = (
        pl.BlockSpec((1, 128, tm), seq_map),
        pl.BlockSpec((1, 128, tm), seq_map),
        pl.BlockSpec((1, IDX_DIM, tm), seq_map),
        pl.BlockSpec((tm, 128), row_map),
        pl.BlockSpec((tm, 128), row_map),
        pl.BlockSpec((512, tm), col_map),
        pl.BlockSpec((512, tm), col_map),
        pl.BlockSpec((tm // KEY_STEP, 128, KEY_STEP), lambda r: (r, 0, 0)),
        pl.BlockSpec((8, tm), col_map),
    )
    in_specs = [
        pl.BlockSpec((tm, D_MODEL), row_map),
        pl.BlockSpec(wt.shape, const),
        pl.BlockSpec((HEAD_DIM // 2, tm), lambda r: (0, r % n_pos_tiles)),
        pl.BlockSpec((HEAD_DIM // 2, tm), lambda r: (0, r % n_pos_tiles)),
    ]
    return pl.pallas_call(
        _proj_kernel, grid=grid, in_specs=in_specs, out_specs=out_specs, out_shape=out_shape,
        compiler_params=pltpu.CompilerParams(dimension_semantics=("arbitrary",),
                                             vmem_limit_bytes=VMEM_LIMIT),
        name="proj_attn_side",
    )(x2d, wt, cost, sint)


def _rope_tables(pos):
    half = HEAD_DIM // 2
    inv = ROPE_THETA ** (-np.arange(half, dtype=np.float64) / half)
    ang = inv[:, None] * pos.astype(np.float64)[None, :]
    return jnp.asarray(np.cos(ang), dtype=F32), jnp.asarray(np.sin(ang), dtype=F32)


def _select_threshold(count_ge, kth_tie_index, min_ge, max_lt, lo0, hi0, n_allowed, n_keys):
    k = float(TOPK)

    def split(mid, lo, hi, clo, chi):
        c = count_ge(mid)
        take = c >= k
        return (jnp.where(take, mid, lo), jnp.where(take, hi, mid),
                jnp.where(take, c, clo), jnp.where(take, chi, c))

    def step(_, carry):
        lo, hi, clo, chi = carry
        return split(0.5 * lo + 0.5 * hi, lo, hi, clo, chi)

    lo, hi, clo, chi = lax.fori_loop(0, BISECT_ITERS, step,
                                     (lo0, hi0, n_allowed, jnp.zeros_like(lo0)))
    big = jnp.full_like(lo, float(4 * n_keys))

    def any_true(mask):
        return jnp.max(jnp.where(mask, 1.0, 0.0)) > 0.5

    def exact(args):
        lo, hi, clo, chi = args

        def cond(c):
            _, _, clo, _, vlo, vhi, it = c
            open_ = jnp.logical_and(clo > k, vlo < vhi)
            return jnp.logical_and(any_true(open_), it < 2 * n_keys)

        def body(c):
            lo, hi, clo, chi, vlo, vhi, it = c
            mid = 0.5 * vlo + 0.5 * vhi
            lo, hi, clo, chi = split(jnp.where(mid > vlo, mid, vhi), lo, hi, clo, chi)
            return lo, hi, clo, chi, min_ge(lo), max_lt(hi), it + 1

        lo, hi, clo, chi, vlo, _, _ = lax.while_loop(
            cond, body, (lo, hi, clo, chi, min_ge(lo), max_lt(hi), jnp.int32(0)))
        return vlo, jnp.where(clo > k, kth_tie_index(vlo, k - chi), big)

    def fast(args):
        return args[0], big

    return lax.cond(any_true(clo > k), exact, fast, (lo, hi, clo, chi))


def _fold_rows(x, op):
    groups = [x[r:r + 8] for r in range(0, x.shape[0], 8)]
    parts = groups[:FOLD_CHAINS]
    for n, grp in enumerate(groups[FOLD_CHAINS:]):
        parts[n % FOLD_CHAINS] = op(parts[n % FOLD_CHAINS], grp)
    while len(parts) > 1:
        nxt = [op(parts[a], parts[a + 1]) for a in range(0, len(parts) - 1, 2)]
        if len(parts) % 2:
            nxt.append(parts[-1])
        parts = nxt
    return parts[0]


def _topk_bias_in_place(s_ref, n_keys, n_allowed, lo0=None, hi0=None):
    lanes = s_ref.shape[1]
    inf = jnp.float32(jnp.inf)
    chunk = 8 * FOLD_CHAINS
    kidx0 = lax.broadcasted_iota(jnp.int32, (chunk, lanes), 0).astype(F32)

    def key_reduce(fn, op, finish):
        part = None
        for r in range(0, n_keys, chunk):
            v = fn(s_ref[r:r + chunk, :], r)
            part = v if part is None else op(part, v)
        return finish(_fold_rows(part, op), axis=0, keepdims=True)

    count_ge = lambda thr: key_reduce(lambda t, r: jnp.where(t >= thr, 1.0, 0.0), jnp.add, jnp.sum)
    tri = (lax.broadcasted_iota(jnp.int32, (chunk, chunk), 0)
           >= lax.broadcasted_iota(jnp.int32, (chunk, chunk), 1)).astype(BF16)

    def kth_tie_index(thr, need):
        seen = jnp.zeros((1, lanes), F32)
        best = jnp.full((1, lanes), -1.0, F32)
        for r in range(0, n_keys, chunk):
            tie = jnp.where(s_ref[r:r + chunk, :] == thr, 1.0, 0.0)
            upto = jnp.dot(tri, tie.astype(BF16), preferred_element_type=F32) + seen
            hit = jnp.where(tie > 0.5, jnp.where(upto == need, kidx0 + r, -1.0), -1.0)
            best = jnp.maximum(best, jnp.max(_fold_rows(hit, jnp.maximum), axis=0, keepdims=True))
            seen = upto[chunk - 1:chunk, :]
        return best
    min_ge = lambda thr: key_reduce(lambda t, r: jnp.where(t >= thr, t, inf), jnp.minimum, jnp.min)
    max_lt = lambda thr: key_reduce(lambda t, r: jnp.where(t < thr, t, -inf), jnp.maximum, jnp.max)
    if lo0 is None:
        lo0 = key_reduce(lambda t, r: jnp.where(t == -inf, inf, t), jnp.minimum, jnp.min)
        mx1 = key_reduce(lambda t, r: t, jnp.maximum, jnp.max)
        hi0 = mx1 + (jnp.abs(mx1) * (2.0 ** -20) + 1e-30)
    lo, jmax = _select_threshold(count_ge, kth_tie_index, min_ge, max_lt, lo0, hi0, n_allowed, n_keys)
    for r in range(0, n_keys, chunk):
        t = s_ref[r:r + chunk, :]
        tie = jnp.where(t == lo, jnp.where(kidx0 <= jmax - r, 0.0, NEG), NEG)
        s_ref[r:r + chunk, :] = jnp.where(t > lo, 0.0, tie)


def _attn_prompt_kernel(qt_ref, qit_ref, wit_ref, kbf_ref, kiw_ref, vt_ref, o_ref,
                        s_ref, sc_ref, ri_ref, rq_ref):
    i = pl.program_id(1)
    w = wit_ref[...]
    zeros_half = jnp.zeros((HEAD_DIM, LANES), BF16)

    for hp in range(4):
        a = qit_ref[(2 * hp) * 64:(2 * hp + 1) * 64, :]
        b = qit_ref[(2 * hp + 1) * 64:(2 * hp + 2) * 64, :]
        ri_ref[hp] = jnp.concatenate(
            [jnp.concatenate([a, zeros_half], axis=0), jnp.concatenate([b, zeros_half], axis=0)], axis=1)
        g = hp // 2
        a = qt_ref[(2 * hp) * 64:(2 * hp + 1) * 64, :]
        b = qt_ref[(2 * hp + 1) * 64:(2 * hp + 2) * 64, :]
        if g == 0:
            ab = jnp.concatenate([jnp.concatenate([a, zeros_half], axis=0),
                                  jnp.concatenate([b, zeros_half], axis=0)], axis=1)
        else:
            ab = jnp.concatenate([jnp.concatenate([zeros_half, a], axis=0),
                                  jnp.concatenate([zeros_half, b], axis=0)], axis=1)
        rq_ref[hp] = ab

    inf = jnp.float32(jnp.inf)
    n_steps = (i * LANES + LANES + KEY_STEP - 1) // KEY_STEP

    def process(ns):
        n_keys = ns * KEY_STEP
        keys = slice(0, n_keys)
        row = lax.broadcasted_iota(jnp.int32, (n_keys, LANES), 0)
        col = lax.broadcasted_iota(jnp.int32, (n_keys, LANES), 1)
        qpos = col + i * LANES
        allowed = row <= qpos

        kt = kiw_ref[keys, :]
        acc = jnp.zeros((n_keys, LANES), F32)
        for hp in range(4):
            s2 = jnp.dot(kt, ri_ref[hp], preferred_element_type=F32)
            acc = acc + jnp.maximum(s2[:, :LANES], 0.0) * w[2 * hp:2 * hp + 1, :]
            acc = acc + jnp.maximum(s2[:, LANES:], 0.0) * w[2 * hp + 1:2 * hp + 2, :]
        s_ref[keys, :] = jnp.where(allowed, acc, -inf)
        lo0 = jnp.min(_fold_rows(jnp.where(allowed, acc, inf), jnp.minimum), axis=0, keepdims=True)
        mx1 = jnp.max(_fold_rows(jnp.where(allowed, acc, -inf), jnp.maximum), axis=0, keepdims=True)
        hi0 = mx1 + (jnp.abs(mx1) * (2.0 ** -20) + 1e-30)
        n_allowed = (qpos[0:1, :] + 1).astype(F32)

        _topk_bias_in_place(s_ref, n_keys, n_allowed, lo0, hi0)

        kt = kbf_ref[keys, :]
        b = s_ref[keys, :]
        b2 = jnp.concatenate([b, b], axis=1)
        ms = []
        for hp in range(4):
            s2 = jnp.dot(kt, rq_ref[hp], preferred_element_type=F32) + b2
            sc_ref[hp, keys, :] = s2
            ms.append(jnp.max(_fold_rows(s2, jnp.maximum), axis=0, keepdims=True))
        vt = jnp.concatenate([vt_ref[j] for j in range(ns)], axis=1)
        ones_rows = jnp.ones((ONES_ROWS, n_keys), BF16)
        lhs = [jnp.concatenate([vt[g * 64:(g + 1) * 64, :], ones_rows], axis=0) for g in range(N_KV)]
        pieces = []
        for hp in range(4):
            p = jnp.exp2((sc_ref[hp, keys, :] - ms[hp]).astype(BF16))
            o = jnp.dot(lhs[hp // 2], p, preferred_element_type=F32)
            o2 = o[0:HEAD_DIM, :] / o[HEAD_DIM:HEAD_DIM + 1, :]
            pieces.append(o2[:, :LANES])
            pieces.append(o2[:, LANES:])
        o_ref[...] = jnp.concatenate(pieces, axis=0)

    for ns in range(1, SEQ // KEY_STEP + 1):
        pl.when(n_steps == ns)(functools.partial(process, ns))


def _attn_prompt(qt, qit, wit, kbf, kiw, vt3):
    n_blocks = SEQ // LANES
    grid = (BATCH, n_blocks)
    qmap = lambda n, i: (0, n * n_blocks + i)
    in_specs = [
        pl.BlockSpec((512, LANES), qmap),
        pl.BlockSpec((512, LANES), qmap),
        pl.BlockSpec((8, LANES), qmap),
        pl.BlockSpec((SEQ, 128), lambda n, i: (n, 0)),
        pl.BlockSpec((SEQ, 128), lambda n, i: (n, 0)),
        pl.BlockSpec((SEQ // KEY_STEP, 128, KEY_STEP), lambda n, i: (n, 0, 0)),
    ]
    return pl.pallas_call(
        _attn_prompt_kernel, grid=grid, in_specs=in_specs,
        out_specs=pl.BlockSpec((W_A, LANES), qmap),
        out_shape=jax.ShapeDtypeStruct((W_A, BATCH * SEQ), F32),
        scratch_shapes=[pltpu.VMEM((SEQ, LANES), F32), pltpu.VMEM((4, SEQ, 2 * LANES), F32),
                        pltpu.VMEM((4, 128, 2 * LANES), BF16), pltpu.VMEM((4, 128, 2 * LANES), BF16)],
        compiler_params=pltpu.CompilerParams(dimension_semantics=("arbitrary", "arbitrary"),
                                             vmem_limit_bytes=VMEM_LIMIT),
        name="attn_prompt",
    )(qt, qit, wit, kbf, kiw, vt3)


def _page_copies(pt_ref, group, slot, gsz, streams):
    out = []
    for g in range(gsz):
        seq = group * gsz + g
        for p in range(N_PAGES):
            page = pt_ref[seq, p]
            keys = pl.ds(p * PAGE_SIZE, PAGE_SIZE)
            for hbm, buf, sem in streams:
                out.append(pltpu.make_async_copy(hbm.at[page], buf.at[slot, g, :, keys], sem.at[slot]))
    return out


def _paged_prefetch(pt_ref, gsz, streams):
    step = pl.program_id(0)
    n_steps = pl.num_programs(0)
    slot = step % 2

    @pl.when(step == 0)
    def _():
        tail = pl.ds(PAST_LEN, KEY_TILE)
        for _, buf, _ in streams:
            for s_ in range(2):
                for g in range(gsz):
                    buf[s_, g, :, tail] = jnp.zeros((buf.shape[2], KEY_TILE), F32)
        for c in _page_copies(pt_ref, 0, 0, gsz, streams):
            c.start()

    @pl.when(step + 1 < n_steps)
    def _():
        for c in _page_copies(pt_ref, step + 1, 1 - slot, gsz, streams):
            c.start()

    for c in _page_copies(pt_ref, step, slot, gsz, streams):
        c.wait()
    return slot


def _sample_index_kernel(pt_ref, qi_ref, w_ref, kin_ref, cki_hbm, s_out_ref, kibuf, sem):
    gsz = SAMPLE_IDX_GROUP
    slot = _paged_prefetch(pt_ref, gsz, [(cki_hbm, kibuf, sem)])
    rows8 = lax.broadcasted_iota(jnp.int32, (SAMPLE_Q, SAMPLE_KEYS), 0)
    keyi = lax.broadcasted_iota(jnp.int32, (SAMPLE_Q, SAMPLE_KEYS), 1)
    allowed = keyi <= (PAST_LEN + rows8)
    new_keys = pl.ds(PAST_LEN, DEC_SEQ)
    scores = []
    for g in range(gsz):
        kibuf[slot, g, :, new_keys] = kin_ref[g]
        kib = kibuf[slot, g].astype(BF16)
        s = jnp.dot(qi_ref[g], kib, preferred_element_type=F32)
        sw = jnp.maximum(s, 0.0) * w_ref[g][:, 0:1]
        acc = jnp.sum(sw.reshape(N_IDX_HEADS, SAMPLE_Q, SAMPLE_KEYS), axis=0)
        scores.append(jnp.where(allowed, acc, -jnp.inf))
    for a in range(gsz // 2):
        s_out_ref[a] = jnp.where(rows8 < DEC_SEQ, scores[2 * a],
                                 pltpu.roll(scores[2 * a + 1], DEC_SEQ, 0))


def _sample_select_kernel(s_ref, b_ref, st_ref):
    rows = s_ref.shape[0]
    st_ref[...] = s_ref[...].T
    t4 = lax.broadcasted_iota(jnp.int32, (1, rows), 1) % DEC_SEQ
    _topk_bias_in_place(st_ref, SAMPLE_KEYS, (PAST_LEN + 1 + t4).astype(F32))
    b_ref[...] = st_ref[...].T


def _sample_attend_kernel(pt_ref, q_ref, b_ref, kn_ref, vn_ref, ck_hbm, cv_hbm, o_ref,
                          kbuf, vbuf, ksem, vsem):
    gsz = SAMPLE_GROUP
    slot = _paged_prefetch(pt_ref, gsz, [(ck_hbm, kbuf, ksem), (cv_hbm, vbuf, vsem)])
    new_keys = pl.ds(PAST_LEN, DEC_SEQ)
    lane = lax.broadcasted_iota(jnp.int32, (SAMPLE_Q, LANES), 1)
    scores = []
    for g in range(gsz):
        kbuf[slot, g, :, new_keys] = kn_ref[g]
        vbuf[slot, g, :, new_keys] = vn_ref[g]
        kb = kbuf[slot, g].astype(BF16)
        b = b_ref[g // 2]
        if g % 2:
            b = pltpu.roll(b, DEC_SEQ, 0)
        s = jnp.dot(q_ref[g], kb, preferred_element_type=F32)
        scores.append(s + jnp.concatenate([b] * N_HEADS, axis=0))
    probs = []
    for s in scores:
        p = jnp.exp2(s - jnp.max(s, axis=1, keepdims=True))
        probs.append((p.astype(BF16), jnp.sum(p, axis=1, keepdims=True)))
    for g in range(gsz):
        p, l = probs[g]
        vb = vbuf[slot, g].astype(BF16)
        o = lax.dot_general(p, vb, (((1,), (1,)), ((), ())),
                            preferred_element_type=F32) / l
        o_r = pltpu.roll(o, HEAD_DIM, 1)
        tiles = []
        for c in range(4):
            ha, hb = 2 * c, 2 * c + 1
            src_a = o if ha // 4 == 0 else o_r
            src_b = o if hb // 4 == 1 else o_r
            tiles.append(jnp.where(lane < HEAD_DIM, src_a[ha * SAMPLE_Q:(ha + 1) * SAMPLE_Q, :],
                                   src_b[hb * SAMPLE_Q:(hb + 1) * SAMPLE_Q, :]))
        o_ref[g] = jnp.concatenate(tiles, axis=1)


def _attn_sample(page_table, qi_s, q_s, w_s, ki_new, k_new, v_new, ck, cv, cki):
    params = pltpu.CompilerParams(dimension_semantics=("arbitrary",), vmem_limit_bytes=VMEM_LIMIT)
    blk = lambda n, shape: pl.BlockSpec((n,) + shape, lambda s, pt: (s, 0, 0))
    hbm = pl.BlockSpec(memory_space=pl.ANY)
    n_pairs = DEC_BATCH // 2

    gi = SAMPLE_IDX_GROUP
    scores = pl.pallas_call(
        _sample_index_kernel,
        grid_spec=pltpu.PrefetchScalarGridSpec(
            num_scalar_prefetch=1, grid=(DEC_BATCH // gi,),
            in_specs=[blk(gi, (64, IDX_DIM)), blk(gi, (64, LANES)), blk(gi, (IDX_DIM, DEC_SEQ)), hbm],
            out_specs=blk(gi // 2, (SAMPLE_Q, SAMPLE_KEYS)),
            scratch_shapes=[pltpu.VMEM((2, gi, IDX_DIM, SAMPLE_KEYS), F32),
                            pltpu.SemaphoreType.DMA((2,))]),
        out_shape=jax.ShapeDtypeStruct((n_pairs, SAMPLE_Q, SAMPLE_KEYS), F32),
        compiler_params=params, name="sample_index",
    )(page_table, qi_s, w_s, ki_new, cki)

    rows = n_pairs * SAMPLE_Q
    bias = pl.pallas_call(
        _sample_select_kernel, grid=(1,),
        in_specs=[pl.BlockSpec((rows, SAMPLE_KEYS), lambda s: (0, 0))],
        out_specs=pl.BlockSpec((rows, SAMPLE_KEYS), lambda s: (0, 0)),
        out_shape=jax.ShapeDtypeStruct((rows, SAMPLE_KEYS), F32),
        scratch_shapes=[pltpu.VMEM((SAMPLE_KEYS, rows), F32)],
        compiler_params=params, name="sample_select",
    )(scores.reshape(rows, SAMPLE_KEYS)).reshape(n_pairs, SAMPLE_Q, SAMPLE_KEYS)

    ga = SAMPLE_GROUP
    return pl.pallas_call(
        _sample_attend_kernel,
        grid_spec=pltpu.PrefetchScalarGridSpec(
            num_scalar_prefetch=1, grid=(DEC_BATCH // ga,),
            in_specs=[blk(ga, (64, 128)), blk(ga // 2, (SAMPLE_Q, SAMPLE_KEYS)),
                      blk(ga, (128, DEC_SEQ)), blk(ga, (128, DEC_SEQ)), hbm, hbm],
            out_specs=blk(ga, (SAMPLE_Q, W_A)),
            scratch_shapes=[pltpu.VMEM((2, ga, 128, SAMPLE_KEYS), F32),
                            pltpu.VMEM((2, ga, 128, SAMPLE_KEYS), F32),
                            pltpu.SemaphoreType.DMA((2,)), pltpu.SemaphoreType.DMA((2,))]),
        out_shape=jax.ShapeDtypeStruct((DEC_BATCH, SAMPLE_Q, W_A), F32),
        compiler_params=params, name="sample_attend",
    )(page_table, q_s, bias, k_new, v_new, ck, cv)


def _layer_norm(x, g, b):
    mu = jnp.mean(x, axis=-1, keepdims=True)
    xc = x - mu
    var = jnp.mean(xc * xc, axis=-1, keepdims=True)
    return xc * lax.rsqrt(var + LN_EPS) * g + b


def _silu(x):
    return x / (1.0 + jnp.exp(-x))


def _mixer_kernel(x_ref, a_ref, wb_ref, wmix_ref, bsb_ref, lnvg_ref, lnvb_ref, wout_ref,
                  lng_ref, lnb_ref, y_ref, vn_ref, *, chunk, vn_last_only, tiles_per_seq):
    tm = x_ref.shape[0]
    x = x_ref[...]
    xb = x.astype(BF16)
    last = (((1,), (1,)), ((), ()))
    ga_t = lax.dot_general(wb_ref[0:W_A, :], xb, last, preferred_element_type=F32)
    zb = lax.dot_general(xb, wb_ref[W_A:, :], last, preferred_element_type=F32)
    u = zb[:, 0:512]
    vb = zb[:, 512:1024]
    gb = zb[:, 1024:1536]
    vn = _layer_norm(vb, lnvg_ref[...], lnvb_ref[...])
    if vn_last_only:
        @pl.when(pl.program_id(0) % tiles_per_seq == tiles_per_seq - 1)
        def _():
            vn_ref[0] = vn[tm - CHUNK:, :]
    else:
        vn_ref[...] = vn
    vnb = vn.astype(BF16)
    rows = []
    for c in range(tm // chunk):
        cols = []
        for g in range(N_GROUPS_B):
            blk = vnb[c * chunk:(c + 1) * chunk, g * GROUP_W_B:(g + 1) * GROUP_W_B]
            cols.append(jnp.dot(wmix_ref[g], blk, preferred_element_type=F32) + bsb_ref[g])
        rows.append(jnp.concatenate(cols, axis=1))
    mixed = jnp.concatenate(rows, axis=0) if len(rows) > 1 else rows[0]
    a_out_t = (a_ref[...] * _silu(ga_t)).astype(BF16)
    b_out = (u * mixed * _silu(gb)).astype(BF16)
    out = (lax.dot_general(a_out_t, wout_ref[0:W_A, :], (((0,), (0,)), ((), ())),
                           preferred_element_type=F32)
           + jnp.dot(b_out, wout_ref[W_A:, :], preferred_element_type=F32))
    y_ref[...] = _layer_norm(ALPHA * x + out, lng_ref[...], lnb_ref[...])


def _mixer(x2d, attn, wb, wmix, bsb, lnvg, lnvb, wout, lng, lnb, *, chunk, vn_last_only):
    rows = x2d.shape[0]
    tm = ROW_TILE
    tiles_per_seq = SEQ // tm
    grid = (rows // tm,)
    row_map = lambda r: (r, 0)
    const2 = lambda r: (0, 0)
    const3 = lambda r: (0, 0, 0)
    if vn_last_only:
        vn_shape = jax.ShapeDtypeStruct((rows // SEQ, CHUNK, W_B), F32)
        vn_spec = pl.BlockSpec((1, CHUNK, W_B), lambda r: (r // tiles_per_seq, 0, 0))
    else:
        vn_shape = jax.ShapeDtypeStruct((rows, W_B), F32)
        vn_spec = pl.BlockSpec((tm, W_B), row_map)
    kern = functools.partial(_mixer_kernel, chunk=chunk, vn_last_only=vn_last_only,
                             tiles_per_seq=tiles_per_seq)
    return pl.pallas_call(
        kern, grid=grid,
        in_specs=[pl.BlockSpec((tm, D_MODEL), row_map), pl.BlockSpec((W_A, tm), lambda r: (0, r)),
                  pl.BlockSpec(wb.shape, const2), pl.BlockSpec(wmix.shape, const3),
                  pl.BlockSpec(bsb.shape, const3), pl.BlockSpec((1, W_B), const2),
                  pl.BlockSpec((1, W_B), const2), pl.BlockSpec(wout.shape, const2),
                  pl.BlockSpec((1, D_MODEL), const2), pl.BlockSpec((1, D_MODEL), const2)],
        out_specs=(pl.BlockSpec((tm, D_MODEL), row_map), vn_spec),
        out_shape=(jax.ShapeDtypeStruct((rows, D_MODEL), F32), vn_shape),
        compiler_params=pltpu.CompilerParams(dimension_semantics=("arbitrary",),
                                             vmem_limit_bytes=VMEM_LIMIT),
        name="mixer_chunk%d" % chunk,
    )(x2d, attn, wb, wmix, bsb, lnvg, lnvb, wout, lng, lnb)


def kernel(x_prompt, x_sample, cache_k, cache_v, cache_k_idx, page_table, w_in, w_s, b_s,
           ln_v_g, ln_v_b, w_out, ln_g, ln_b):
    w = w_in[0]
    wq, wk, wv = w[:, 0:512], w[:, 512:640], w[:, 640:768]
    wga, wqi, wki, wwi = w[:, 768:1280], w[:, 1280:1792], w[:, 1792:1856], w[:, 1856:1864]
    wu, wvb, wgb = w[:, 1864:2376], w[:, 2376:2888], w[:, 2888:3400]
    wt = jnp.concatenate([wq.T * Q_SCALE, wqi.T, wv.T, wk.T, wki.T, wwi.T,
                          jnp.zeros((WT_ROWS - WT_WI - N_IDX_HEADS, D_MODEL), F32)], axis=0).astype(BF16)
    wb = jnp.concatenate([wga.T, wu.T, wvb.T, wgb.T], axis=0).astype(BF16)
    wout = w_out[0].astype(BF16)
    lnvg, lnvb = ln_v_g[0][None, :], ln_v_b[0][None, :]
    lng, lnb = ln_g[0][None, :], ln_b[0][None, :]

    xp = x_prompt.reshape(BATCH * SEQ, D_MODEL)
    tabs_p = _rope_tables(np.arange(SEQ))
    kt_p, vt_p, kit_p, kbf, kiw, qt, qit, vt3, wit = _project(xp, wt, *tabs_p, SEQ)
    attn_p = _attn_prompt(qt, qit, wit, kbf, kiw, vt3)
    tril = np.tril(np.ones((CHUNK, CHUNK), np.float32))
    wmix_p = (w_s[0] * tril[None]).astype(BF16)
    bsb_p = jnp.broadcast_to(b_s[0][:, :, None], (N_GROUPS_B, CHUNK, GROUP_W_B))
    y_p, vn_p = _mixer(xp, attn_p, wb, wmix_p, bsb_p, lnvg, lnvb, wout, lng, lnb,
                       chunk=CHUNK, vn_last_only=True)

    rows_s = DEC_BATCH * DEC_SEQ
    xs = x_sample.reshape(rows_s, D_MODEL)
    tabs_s = _rope_tables(PAST_LEN + (np.arange(rows_s) % DEC_SEQ))
    kt_s, vt_s, kit_s, _, _, qt_s, qit_s, _, wit_s = _project(xs, wt, *tabs_s, rows_s)

    def to_rows(a_t):
        a = a_t.reshape(N_HEADS, HEAD_DIM, DEC_BATCH, DEC_SEQ).transpose(2, 0, 3, 1)
        return jnp.pad(a, ((0, 0), (0, 0), (0, SAMPLE_Q - DEC_SEQ), (0, 0)))

    qi_rows = to_rows(qit_s).reshape(DEC_BATCH, N_IDX_HEADS * SAMPLE_Q, IDX_DIM)
    q5 = to_rows(qt_s).reshape(DEC_BATCH, N_KV, N_HEADS // N_KV, SAMPLE_Q, HEAD_DIM)
    zq = jnp.zeros_like(q5[:, 0])
    q_rows = jnp.stack([jnp.concatenate([q5[:, 0], zq], axis=-1),
                        jnp.concatenate([zq, q5[:, 1]], axis=-1)], axis=1)
    q_rows = q_rows.reshape(DEC_BATCH, N_HEADS * SAMPLE_Q, 2 * HEAD_DIM)
    w_rows = jnp.pad(wit_s.reshape(N_IDX_HEADS, DEC_BATCH, DEC_SEQ).transpose(1, 0, 2),
                     ((0, 0), (0, 0), (0, SAMPLE_Q - DEC_SEQ)))
    w_rows = jnp.broadcast_to(w_rows.reshape(DEC_BATCH, N_IDX_HEADS * SAMPLE_Q, 1),
                              (DEC_BATCH, N_IDX_HEADS * SAMPLE_Q, LANES))
    new_cols = lambda a: a.reshape(a.shape[0], DEC_BATCH, DEC_SEQ).transpose(1, 0, 2)
    n_pool = cache_k.shape[1]
    ck_t = cache_k[0].transpose(0, 2, 3, 1).reshape(n_pool, N_KV * HEAD_DIM, PAGE_SIZE)
    cv_t = cache_v[0].transpose(0, 2, 3, 1).reshape(n_pool, N_KV * HEAD_DIM, PAGE_SIZE)
    cki_t = cache_k_idx[0].transpose(0, 2, 1)
    attn_s8 = _attn_sample(page_table, qi_rows, q_rows, w_rows, new_cols(kit_s[0]),
                           new_cols(kt_s[0]), new_cols(vt_s[0]), ck_t, cv_t, cki_t)
    attn_s = attn_s8[:, :DEC_SEQ, :].reshape(rows_s, W_A).T
    ri = np.arange(rows_s)
    same_seq = (ri[:, None] // DEC_SEQ) == (ri[None, :] // DEC_SEQ)
    wmix_s = jnp.zeros((N_GROUPS_B, rows_s, rows_s), F32)
    for t in range(DEC_SEQ):
        for s in range(t + 1):
            hit = same_seq & (ri[:, None] % DEC_SEQ == t) & (ri[None, :] % DEC_SEQ == s)
            wmix_s = wmix_s + jnp.where(hit[None], w_s[0, :, t, s][:, None, None], 0.0)
    wmix_s = wmix_s.astype(BF16)
    bsb_s = jnp.broadcast_to(jnp.tile(b_s[0][:, :DEC_SEQ], (1, DEC_BATCH))[:, :, None],
                             (N_GROUPS_B, rows_s, GROUP_W_B))
    y_s, vn_s = _mixer(xs, attn_s, wb, wmix_s, bsb_s, lnvg, lnvb, wout, lng, lnb,
                       chunk=rows_s, vn_last_only=False)

    def heads_last(a, n, t):
        a = a.reshape(a.shape[0], N_KV, HEAD_DIM, a.shape[2]).transpose(0, 3, 1, 2)
        return a.reshape(1, n, t, N_KV, HEAD_DIM)

    return (y_p.reshape(BATCH, SEQ, D_MODEL),
            y_s.reshape(DEC_BATCH, DEC_SEQ, D_MODEL),
            heads_last(kt_p, BATCH, SEQ), heads_last(vt_p, BATCH, SEQ),
            kit_p.transpose(0, 2, 1)[None],
            vn_p.reshape(1, BATCH, CHUNK, W_B),
            heads_last(kt_s, DEC_BATCH, DEC_SEQ), heads_last(vt_s, DEC_BATCH, DEC_SEQ),
            kit_s[0].T.reshape(1, DEC_BATCH, DEC_SEQ, IDX_DIM),
            vn_s.reshape(1, DEC_BATCH, DEC_SEQ, W_B))
```

```python
import functools

import jax
import jax.numpy as jnp
import numpy as np
from jax import lax
from jax.experimental import pallas as pl
from jax.experimental.pallas import tpu as pltpu

F32 = jnp.float32
BF16 = jnp.bfloat16

D_MODEL = 1024
SEQ = 2048
BATCH = 8
DEC_BATCH = 128
DEC_SEQ = 4
PAST_LEN = 2048
PAGE_SIZE = 128
N_PAGES = PAST_LEN // PAGE_SIZE
W_A = 512
W_B = 512
HEAD_DIM = 64
N_HEADS = 8
N_KV = 2
N_IDX_HEADS = 8
IDX_DIM = 64
TOPK = 256
CHUNK = 128
N_GROUPS_B = 4
GROUP_W_B = 128
ROPE_THETA = 10000.0
LN_EPS = 1e-5
ALPHA = 2.0 ** 0.25
NEG = -1e30
IDX_SCALE = float((N_IDX_HEADS * IDX_DIM) ** -0.5)

LANES = 128
KEY_TILE = 128
KEY_STEP = 256
ROW_TILE = 512
PROJ_TILE = 1024
WT_Q, WT_QI, WT_V, WT_K, WT_KI, WT_WI, WT_ROWS = 0, 512, 1024, 1152, 1280, 1344, 1360
BISECT_ITERS = 20
FOLD_CHAINS = 8
ONES_ROWS = 16
Q_SCALE = float(HEAD_DIM ** -0.5 * 1.4426950408889634)
VMEM_LIMIT = 48 * 1024 * 1024

SAMPLE_GROUP = 4
SAMPLE_IDX_GROUP = 8
SAMPLE_KEYS = PAST_LEN + KEY_TILE
SAMPLE_Q = 8


def _proj_kernel(x_ref, wt_ref, cost_ref, sint_ref,
                 kt_ref, vt32_ref, kit_ref, kbf_ref, kiw_ref, qt_ref, qit_ref, vt_ref, wit_ref):
    tm = x_ref.shape[0]
    xb = x_ref[...].astype(BF16)
    zt = lax.dot_general(wt_ref[...], xb, (((1,), (1,)), ((), ())),
                         preferred_element_type=F32)
    cost = cost_ref[...]
    sint = sint_ref[...]
    half = HEAD_DIM // 2

    def rope_rows(base, n_heads):
        pieces = []
        for h in range(n_heads):
            x1 = zt[base + h * HEAD_DIM: base + h * HEAD_DIM + half]
            x2 = zt[base + h * HEAD_DIM + half: base + (h + 1) * HEAD_DIM]
            pieces.append(x1 * cost - x2 * sint)
            pieces.append(x2 * cost + x1 * sint)
        return jnp.concatenate(pieces, axis=0)

    qt_ref[...] = rope_rows(WT_Q, N_HEADS).astype(BF16)
    qit_ref[...] = rope_rows(WT_QI, N_IDX_HEADS).astype(BF16)
    v_t = zt[WT_V:WT_V + 128]
    vt32_ref[0] = v_t
    vtb = v_t.astype(BF16)
    for c in range(tm // KEY_STEP):
        vt_ref[c] = vtb[:, c * KEY_STEP:(c + 1) * KEY_STEP]
    k_t = rope_rows(WT_K, N_KV)
    ki_t = rope_rows(WT_KI, 1)
    kt_ref[0] = k_t
    kit_ref[0] = ki_t
    kbf_ref[...] = k_t.T.astype(BF16)
    kiw_ref[...] = jnp.concatenate([ki_t, jnp.zeros_like(ki_t)], axis=0).T.astype(BF16)
    wit_ref[...] = zt[WT_WI:WT_WI + N_IDX_HEADS] * IDX_SCALE


def _project(x2d, wt, cost, sint, seq_len):
    rows = x2d.shape[0]
    tm = min(rows, PROJ_TILE)
    grid = (rows // tm,)
    n_pos_tiles = seq_len // tm
    row_map = lambda r: (r, 0)
    col_map = lambda r: (0, r)
    const = lambda r: (0, 0)
    seq_map = lambda r: (r // n_pos_tiles, 0, r % n_pos_tiles)
    out_shape = (
        jax.ShapeDtypeStruct((rows // seq_len, 128, seq_len), F32),
        jax.ShapeDtypeStruct((rows // seq_len, 128, seq_len), F32),
        jax.ShapeDtypeStruct((rows // seq_len, IDX_DIM, seq_len), F32),
        jax.ShapeDtypeStruct((rows, 128), BF16),
        jax.ShapeDtypeStruct((rows, 128), BF16),
        jax.ShapeDtypeStruct((512, rows), BF16),
        jax.ShapeDtypeStruct((512, rows), BF16),
        jax.ShapeDtypeStruct((rows // KEY_STEP, 128, KEY_STEP), BF16),
        jax.ShapeDtypeStruct((8, rows), F32),
    )
    out_specs = (
        pl.BlockSpec((1, 128, tm), seq_map),
        pl.BlockSpec((1, 128, tm), seq_map),
        pl.BlockSpec((1, IDX_DIM, tm), seq_map),
        pl.BlockSpec((tm, 128), row_map),
        pl.BlockSpec((tm, 128), row_map),
        pl.BlockSpec((512, tm), col_map),
        pl.BlockSpec((512, tm), col_map),
        pl.BlockSpec((tm // KEY_STEP, 128, KEY_STEP), lambda r: (r, 0, 0)),
        pl.BlockSpec((8, tm), col_map),
    )
    in_specs = [
        pl.BlockSpec((tm, D_MODEL), row_map),
        pl.BlockSpec(wt.shape, const),
        pl.BlockSpec((HEAD_DIM // 2, tm), lambda r: (0, r % n_pos_tiles)),
        pl.BlockSpec((HEAD_DIM // 2, tm), lambda r: (0, r % n_pos_tiles)),
    ]
    return pl.pallas_call(
        _proj_kernel, grid=grid, in_specs=in_specs, out_specs=out_specs, out_shape=out_shape,
        compiler_params=pltpu.CompilerParams(dimension_semantics=("arbitrary",),
                                             vmem_limit_bytes=VMEM_LIMIT),
        name="proj_attn_side",
    )(x2d, wt, cost, sint)


def _rope_tables(pos):
    half = HEAD_DIM // 2
    inv = ROPE_THETA ** (-np.arange(half, dtype=np.float64) / half)
    ang = inv[:, None] * pos.astype(np.float64)[None, :]
    return jnp.asarray(np.cos(ang), dtype=F32), jnp.asarray(np.sin(ang), dtype=F32)


def _select_threshold(count_ge, kth_tie_index, min_ge, max_lt, lo0, hi0, n_allowed, n_keys):
    k = float(TOPK)

    def split(mid, lo, hi, clo, chi):
        c = count_ge(mid)
        take = c >= k
        return (jnp.where(take, mid, lo), jnp.where(take, hi, mid),
                jnp.where(take, c, clo), jnp.where(take, chi, c))

    def step(_, carry):
        lo, hi, clo, chi = carry
        return split(0.5 * lo + 0.5 * hi, lo, hi, clo, chi)

    lo, hi, clo, chi = lax.fori_loop(0, BISECT_ITERS, step,
                                     (lo0, hi0, n_allowed, jnp.zeros_like(lo0)))
    big = jnp.full_like(lo, float(4 * n_keys))

    def any_true(mask):
        return jnp.max(jnp.where(mask, 1.0, 0.0)) > 0.5

    def exact(args):
        lo, hi, clo, chi = args

        def cond(c):
            _, _, clo, _, vlo, vhi, it = c
            open_ = jnp.logical_and(clo > k, vlo < vhi)
            return jnp.logical_and(any_true(open_), it < 2 * n_keys)

        def body(c):
            lo, hi, clo, chi, vlo, vhi, it = c
            mid = 0.5 * vlo + 0.5 * vhi
            lo, hi, clo, chi = split(jnp.where(mid > vlo, mid, vhi), lo, hi, clo, chi)
            return lo, hi, clo, chi, min_ge(lo), max_lt(hi), it + 1

        lo, hi, clo, chi, vlo, _, _ = lax.while_loop(
            cond, body, (lo, hi, clo, chi, min_ge(lo), max_lt(hi), jnp.int32(0)))
        return vlo, jnp.where(clo > k, kth_tie_index(vlo, k - chi), big)

    def fast(args):
        return args[0], big

    return lax.cond(any_true(clo > k), exact, fast, (lo, hi, clo, chi))


def _fold_rows(x, op):
    groups = [x[r:r + 8] for r in range(0, x.shape[0], 8)]
    parts = groups[:FOLD_CHAINS]
    for n, grp in enumerate(groups[FOLD_CHAINS:]):
        parts[n % FOLD_CHAINS] = op(parts[n % FOLD_CHAINS], grp)
    while len(parts) > 1:
        nxt = [op(parts[a], parts[a + 1]) for a in range(0, len(parts) - 1, 2)]
        if len(parts) % 2:
            nxt.append(parts[-1])
        parts = nxt
    return parts[0]


def _topk_bias_in_place(s_ref, n_keys, n_allowed, lo0=None, hi0=None):
    lanes = s_ref.shape[1]
    inf = jnp.float32(jnp.inf)
    chunk = 8 * FOLD_CHAINS
    kidx0 = lax.broadcasted_iota(jnp.int32, (chunk, lanes), 0).astype(F32)

    def key_reduce(fn, op, finish):
        part = None
        for r in range(0, n_keys, chunk):
            v = fn(s_ref[r:r + chunk, :], r)
            part = v if part is None else op(part, v)
        return finish(_fold_rows(part, op), axis=0, keepdims=True)

    count_ge = lambda thr: key_reduce(lambda t, r: jnp.where(t >= thr, 1.0, 0.0), jnp.add, jnp.sum)
    tri = (lax.broadcasted_iota(jnp.int32, (chunk, chunk), 0)
           >= lax.broadcasted_iota(jnp.int32, (chunk, chunk), 1)).astype(BF16)

    def kth_tie_index(thr, need):
        seen = jnp.zeros((1, lanes), F32)
        best = jnp.full((1, lanes), -1.0, F32)
        for r in range(0, n_keys, chunk):
            tie = jnp.where(s_ref[r:r + chunk, :] == thr, 1.0, 0.0)
            upto = jnp.dot(tri, tie.astype(BF16), preferred_element_type=F32) + seen
            hit = jnp.where(tie > 0.5, jnp.where(upto == need, kidx0 + r, -1.0), -1.0)
            best = jnp.maximum(best, jnp.max(_fold_rows(hit, jnp.maximum), axis=0, keepdims=True))
            seen = upto[chunk - 1:chunk, :]
        return best
    min_ge = lambda thr: key_reduce(lambda t, r: jnp.where(t >= thr, t, inf), jnp.minimum, jnp.min)
    max_lt = lambda thr: key_reduce(lambda t, r: jnp.where(t < thr, t, -inf), jnp.maximum, jnp.max)
    if lo0 is None:
        lo0 = key_reduce(lambda t, r: jnp.where(t == -inf, inf, t), jnp.minimum, jnp.min)
        mx1 = key_reduce(lambda t, r: t, jnp.maximum, jnp.max)
        hi0 = mx1 + (jnp.abs(mx1) * (2.0 ** -20) + 1e-30)
    lo, jmax = _select_threshold(count_ge, kth_tie_index, min_ge, max_lt, lo0, hi0, n_allowed, n_keys)
    for r in range(0, n_keys, chunk):
        t = s_ref[r:r + chunk, :]
        tie = jnp.where(t == lo, jnp.where(kidx0 <= jmax - r, 0.0, NEG), NEG)
        s_ref[r:r + chunk, :] = jnp.where(t > lo, 0.0, tie)


def _attn_prompt_kernel(qt_ref, qit_ref, wit_ref, kbf_ref, kiw_ref, vt_ref, o_ref,
                        s_ref, sc_ref, ri_ref, rq_ref):
    i = pl.program_id(1)
    w = wit_ref[...]
    zeros_half = jnp.zeros((HEAD_DIM, LANES), BF16)

    for hp in range(4):
        a = qit_ref[(2 * hp) * 64:(2 * hp + 1) * 64, :]
        b = qit_ref[(2 * hp + 1) * 64:(2 * hp + 2) * 64, :]
        ri_ref[hp] = jnp.concatenate(
            [jnp.concatenate([a, zeros_half], axis=0), jnp.concatenate([b, zeros_half], axis=0)], axis=1)
        g = hp // 2
        a = qt_ref[(2 * hp) * 64:(2 * hp + 1) * 64, :]
        b = qt_ref[(2 * hp + 1) * 64:(2 * hp + 2) * 64, :]
        if g == 0:
            ab = jnp.concatenate([jnp.concatenate([a, zeros_half], axis=0),
                                  jnp.concatenate([b, zeros_half], axis=0)], axis=1)
        else:
            ab = jnp.concatenate([jnp.concatenate([zeros_half, a], axis=0),
                                  jnp.concatenate([zeros_half, b], axis=0)], axis=1)
        rq_ref[hp] = ab

    inf = jnp.float32(jnp.inf)
    n_steps = (i * LANES + LANES + KEY_STEP - 1) // KEY_STEP

    def process(ns):
        n_keys = ns * KEY_STEP
        keys = slice(0, n_keys)
        row = lax.broadcasted_iota(jnp.int32, (n_keys, LANES), 0)
        col = lax.broadcasted_iota(jnp.int32, (n_keys, LANES), 1)
        qpos = col + i * LANES
        allowed = row <= qpos

        kt = kiw_ref[keys, :]
        acc = jnp.zeros((n_keys, LANES), F32)
        for hp in range(4):
            s2 = jnp.dot(kt, ri_ref[hp], preferred_element_type=F32)
            acc = acc + jnp.maximum(s2[:, :LANES], 0.0) * w[2 * hp:2 * hp + 1, :]
            acc = acc + jnp.maximum(s2[:, LANES:], 0.0) * w[2 * hp + 1:2 * hp + 2, :]
        s_ref[keys, :] = jnp.where(allowed, acc, -inf)
        lo0 = jnp.min(_fold_rows(jnp.where(allowed, acc, inf), jnp.minimum), axis=0, keepdims=True)
        mx1 = jnp.max(_fold_rows(jnp.where(allowed, acc, -inf), jnp.maximum), axis=0, keepdims=True)
        hi0 = mx1 + (jnp.abs(mx1) * (2.0 ** -20) + 1e-30)
        n_allowed = (qpos[0:1, :] + 1).astype(F32)

        _topk_bias_in_place(s_ref, n_keys, n_allowed, lo0, hi0)

        kt = kbf_ref[keys, :]
        b = s_ref[keys, :]
        b2 = jnp.concatenate([b, b], axis=1)
        ms = []
        for hp in range(4):
            s2 = jnp.dot(kt, rq_ref[hp], preferred_element_type=F32) + b2
            sc_ref[hp, keys, :] = s2
            ms.append(jnp.max(_fold_rows(s2, jnp.maximum), axis=0, keepdims=True))
        vt = jnp.concatenate([vt_ref[j] for j in range(ns)], axis=1)
        ones_rows = jnp.ones((ONES_ROWS, n_keys), BF16)
        lhs = [jnp.concatenate([vt[g * 64:(g + 1) * 64, :], ones_rows], axis=0) for g in range(N_KV)]
        pieces = []
        for hp in range(4):
            p = jnp.exp2((sc_ref[hp, keys, :] - ms[hp]).astype(BF16))
            o = jnp.dot(lhs[hp // 2], p, preferred_element_type=F32)
            o2 = o[0:HEAD_DIM, :] / o[HEAD_DIM:HEAD_DIM + 1, :]
            pieces.append(o2[:, :LANES])
            pieces.append(o2[:, LANES:])
        o_ref[...] = jnp.concatenate(pieces, axis=0)

    for ns in range(1, SEQ // KEY_STEP + 1):
        pl.when(n_steps == ns)(functools.partial(process, ns))


def _attn_prompt(qt, qit, wit, kbf, kiw, vt3):
    n_blocks = SEQ // LANES
    grid = (BATCH, n_blocks)
    qmap = lambda n, i: (0, n * n_blocks + i)
    in_specs = [
        pl.BlockSpec((512, LANES), qmap),
        pl.BlockSpec((512, LANES), qmap),
        pl.BlockSpec((8, LANES), qmap),
        pl.BlockSpec((SEQ, 128), lambda n, i: (n, 0)),
        pl.BlockSpec((SEQ, 128), lambda n, i: (n, 0)),
        pl.BlockSpec((SEQ // KEY_STEP, 128, KEY_STEP), lambda n, i: (n, 0, 0)),
    ]
    return pl.pallas_call(
        _attn_prompt_kernel, grid=grid, in_specs=in_specs,
        out_specs=pl.BlockSpec((W_A, LANES), qmap),
        out_shape=jax.ShapeDtypeStruct((W_A, BATCH * SEQ), F32),
        scratch_shapes=[pltpu.VMEM((SEQ, LANES), F32), pltpu.VMEM((4, SEQ, 2 * LANES), F32),
                        pltpu.VMEM((4, 128, 2 * LANES), BF16), pltpu.VMEM((4, 128, 2 * LANES), BF16)],
        compiler_params=pltpu.CompilerParams(dimension_semantics=("arbitrary", "arbitrary"),
                                             vmem_limit_bytes=VMEM_LIMIT),
        name="attn_prompt",
    )(qt, qit, wit, kbf, kiw, vt3)


def _page_copies(pt_ref, group, slot, gsz, streams):
    out = []
    for g in range(gsz):
        seq = group * gsz + g
        for p in range(N_PAGES):
            page = pt_ref[seq, p]
            keys = pl.ds(p * PAGE_SIZE, PAGE_SIZE)
            for hbm, buf, sem in streams:
                out.append(pltpu.make_async_copy(hbm.at[page], buf.at[slot, g, :, keys], sem.at[slot]))
    return out


def _paged_prefetch(pt_ref, gsz, streams):
    step = pl.program_id(0)
    n_steps = pl.num_programs(0)
    slot = step % 2

    @pl.when(step == 0)
    def _():
        tail = pl.ds(PAST_LEN, KEY_TILE)
        for _, buf, _ in streams:
            for s_ in range(2):
                for g in range(gsz):
                    buf[s_, g, :, tail] = jnp.zeros((buf.shape[2], KEY_TILE), F32)
        for c in _page_copies(pt_ref, 0, 0, gsz, streams):
            c.start()

    @pl.when(step + 1 < n_steps)
    def _():
        for c in _page_copies(pt_ref, step + 1, 1 - slot, gsz, streams):
            c.start()

    for c in _page_copies(pt_ref, step, slot, gsz, streams):
        c.wait()
    return slot


def _sample_index_kernel(pt_ref, qi_ref, w_ref, kin_ref, cki_hbm, s_out_ref, kibuf, sem):
    gsz = SAMPLE_IDX_GROUP
    slot = _paged_prefetch(pt_ref, gsz, [(cki_hbm, kibuf, sem)])
    rows8 = lax.broadcasted_iota(jnp.int32, (SAMPLE_Q, SAMPLE_KEYS), 0)
    keyi = lax.broadcasted_iota(jnp.int32, (SAMPLE_Q, SAMPLE_KEYS), 1)
    allowed = keyi <= (PAST_LEN + rows8)
    new_keys = pl.ds(PAST_LEN, DEC_SEQ)
    scores = []
    for g in range(gsz):
        kibuf[slot, g, :, new_keys] = kin_ref[g]
        kib = kibuf[slot, g].astype(BF16)
        s = jnp.dot(qi_ref[g], kib, preferred_element_type=F32)
        sw = jnp.maximum(s, 0.0) * w_ref[g][:, 0:1]
        acc = jnp.sum(sw.reshape(N_IDX_HEADS, SAMPLE_Q, SAMPLE_KEYS), axis=0)
        scores.append(jnp.where(allowed, acc, -jnp.inf))
    for a in range(gsz // 2):
        s_out_ref[a] = jnp.where(rows8 < DEC_SEQ, scores[2 * a],
                                 pltpu.roll(scores[2 * a + 1], DEC_SEQ, 0))


def _sample_select_kernel(s_ref, b_ref, st_ref):
    rows = s_ref.shape[0]
    st_ref[...] = s_ref[...].T
    t4 = lax.broadcasted_iota(jnp.int32, (1, rows), 1) % DEC_SEQ
    _topk_bias_in_place(st_ref, SAMPLE_KEYS, (PAST_LEN + 1 + t4).astype(F32))
    b_ref[...] = st_ref[...].T


def _sample_attend_kernel(pt_ref, q_ref, b_ref, kn_ref, vn_ref, ck_hbm, cv_hbm, o_ref,
                          kbuf, vbuf, ksem, vsem):
    gsz = SAMPLE_GROUP
    slot = _paged_prefetch(pt_ref, gsz, [(ck_hbm, kbuf, ksem), (cv_hbm, vbuf, vsem)])
    new_keys = pl.ds(PAST_LEN, DEC_SEQ)
    lane = lax.broadcasted_iota(jnp.int32, (SAMPLE_Q, LANES), 1)
    scores = []
    for g in range(gsz):
        kbuf[slot, g, :, new_keys] = kn_ref[g]
        vbuf[slot, g, :, new_keys] = vn_ref[g]
        kb = kbuf[slot, g].astype(BF16)
        b = b_ref[g // 2]
        if g % 2:
            b = pltpu.roll(b, DEC_SEQ, 0)
        s = jnp.dot(q_ref[g], kb, preferred_element_type=F32)
        scores.append(s + jnp.concatenate([b] * N_HEADS, axis=0))
    probs = []
    for s in scores:
        p = jnp.exp2(s - jnp.max(s, axis=1, keepdims=True))
        probs.append((p.astype(BF16), jnp.sum(p, axis=1, keepdims=True)))
    for g in range(gsz):
        p, l = probs[g]
        vb = vbuf[slot, g].astype(BF16)
        o = lax.dot_general(p, vb, (((1,), (1,)), ((), ())),
                            preferred_element_type=F32) / l
        o_r = pltpu.roll(o, HEAD_DIM, 1)
        tiles = []
        for c in range(4):
            ha, hb = 2 * c, 2 * c + 1
            src_a = o if ha // 4 == 0 else o_r
            src_b = o if hb // 4 == 1 else o_r
            tiles.append(jnp.where(lane < HEAD_DIM, src_a[ha * SAMPLE_Q:(ha + 1) * SAMPLE_Q, :],
                                   src_b[hb * SAMPLE_Q:(hb + 1) * SAMPLE_Q, :]))
        o_ref[g] = jnp.concatenate(tiles, axis=1)


def _attn_sample(page_table, qi_s, q_s, w_s, ki_new, k_new, v_new, ck, cv, cki):
    params = pltpu.CompilerParams(dimension_semantics=("arbitrary",), vmem_limit_bytes=VMEM_LIMIT)
    blk = lambda n, shape: pl.BlockSpec((n,) + shape, lambda s, pt: (s, 0, 0))
    hbm = pl.BlockSpec(memory_space=pl.ANY)
    n_pairs = DEC_BATCH // 2

    gi = SAMPLE_IDX_GROUP
    scores = pl.pallas_call(
        _sample_index_kernel,
        grid_spec=pltpu.PrefetchScalarGridSpec(
            num_scalar_prefetch=1, grid=(DEC_BATCH // gi,),
            in_specs=[blk(gi, (64, IDX_DIM)), blk(gi, (64, LANES)), blk(gi, (IDX_DIM, DEC_SEQ)), hbm],
            out_specs=blk(gi // 2, (SAMPLE_Q, SAMPLE_KEYS)),
            scratch_shapes=[pltpu.VMEM((2, gi, IDX_DIM, SAMPLE_KEYS), F32),
                            pltpu.SemaphoreType.DMA((2,))]),
        out_shape=jax.ShapeDtypeStruct((n_pairs, SAMPLE_Q, SAMPLE_KEYS), F32),
        compiler_params=params, name="sample_index",
    )(page_table, qi_s, w_s, ki_new, cki)

    rows = n_pairs * SAMPLE_Q
    bias = pl.pallas_call(
        _sample_select_kernel, grid=(1,),
        in_specs=[pl.BlockSpec((rows, SAMPLE_KEYS), lambda s: (0, 0))],
        out_specs=pl.BlockSpec((rows, SAMPLE_KEYS), lambda s: (0, 0)),
        out_shape=jax.ShapeDtypeStruct((rows, SAMPLE_KEYS), F32),
        scratch_shapes=[pltpu.VMEM((SAMPLE_KEYS, rows), F32)],
        compiler_params=params, name="sample_select",
    )(scores.reshape(rows, SAMPLE_KEYS)).reshape(n_pairs, SAMPLE_Q, SAMPLE_KEYS)

    ga = SAMPLE_GROUP
    return pl.pallas_call(
        _sample_attend_kernel,
        grid_spec=pltpu.PrefetchScalarGridSpec(
            num_scalar_prefetch=1, grid=(DEC_BATCH // ga,),
            in_specs=[blk(ga, (64, 128)), blk(ga // 2, (SAMPLE_Q, SAMPLE_KEYS)),
                      blk(ga, (128, DEC_SEQ)), blk(ga, (128, DEC_SEQ)), hbm, hbm],
            out_specs=blk(ga, (SAMPLE_Q, W_A)),
            scratch_shapes=[pltpu.VMEM((2, ga, 128, SAMPLE_KEYS), F32),
                            pltpu.VMEM((2, ga, 128, SAMPLE_KEYS), F32),
                            pltpu.SemaphoreType.DMA((2,)), pltpu.SemaphoreType.DMA((2,))]),
        out_shape=jax.ShapeDtypeStruct((DEC_BATCH, SAMPLE_Q, W_A), F32),
        compiler_params=params, name="sample_attend",
    )(page_table, q_s, bias, k_new, v_new, ck, cv)


def _layer_norm(x, g, b):
    mu = jnp.mean(x, axis=-1, keepdims=True)
    xc = x - mu
    var = jnp.mean(xc * xc, axis=-1, keepdims=True)
    return xc * lax.rsqrt(var + LN_EPS) * g + b


def _silu(x):
    return x / (1.0 + jnp.exp(-x))


def _mixer_kernel(x_ref, a_ref, wb_ref, wmix_ref, bsb_ref, lnvg_ref, lnvb_ref, wout_ref,
                  lng_ref, lnb_ref, y_ref, vn_ref, *, chunk, vn_last_only, tiles_per_seq):
    tm = x_ref.shape[0]
    x = x_ref[...]
    xb = x.astype(BF16)
    last = (((1,), (1,)), ((), ()))
    ga_t = lax.dot_general(wb_ref[0:W_A, :], xb, last, preferred_element_type=F32)
    zb = lax.dot_general(xb, wb_ref[W_A:, :], last, preferred_element_type=F32)
    u = zb[:, 0:512]
    vb = zb[:, 512:1024]
    gb = zb[:, 1024:1536]
    vn = _layer_norm(vb, lnvg_ref[...], lnvb_ref[...])
    if vn_last_only:
        @pl.when(pl.program_id(0) % tiles_per_seq == tiles_per_seq - 1)
        def _():
            vn_ref[0] = vn[tm - CHUNK:, :]
    else:
        vn_ref[...] = vn
    vnb = vn.astype(BF16)
    rows = []
    for c in range(tm // chunk):
        cols = []
        for g in range(N_GROUPS_B):
            blk = vnb[c * chunk:(c + 1) * chunk, g * GROUP_W_B:(g + 1) * GROUP_W_B]
            cols.append(jnp.dot(wmix_ref[g], blk, preferred_element_type=F32) + bsb_ref[g])
        rows.append(jnp.concatenate(cols, axis=1))
    mixed = jnp.concatenate(rows, axis=0) if len(rows) > 1 else rows[0]
    a_out_t = (a_ref[...] * _silu(ga_t)).astype(BF16)
    b_out = (u * mixed * _silu(gb)).astype(BF16)
    out = (lax.dot_general(a_out_t, wout_ref[0:W_A, :], (((0,), (0,)), ((), ())),
                           preferred_element_type=F32)
           + jnp.dot(b_out, wout_ref[W_A:, :], preferred_element_type=F32))
    y_ref[...] = _layer_norm(ALPHA * x + out, lng_ref[...], lnb_ref[...])


def _mixer(x2d, attn, wb, wmix, bsb, lnvg, lnvb, wout, lng, lnb, *, chunk, vn_last_only):
    rows = x2d.shape[0]
    tm = ROW_TILE
    tiles_per_seq = SEQ // tm
    grid = (rows // tm,)
    row_map = lambda r: (r, 0)
    const2 = lambda r: (0, 0)
    const3 = lambda r: (0, 0, 0)
    if vn_last_only:
        vn_shape = jax.ShapeDtypeStruct((rows // SEQ, CHUNK, W_B), F32)
        vn_spec = pl.BlockSpec((1, CHUNK, W_B), lambda r: (r // tiles_per_seq, 0, 0))
    else:
        vn_shape = jax.ShapeDtypeStruct((rows, W_B), F32)
        vn_spec = pl.BlockSpec((tm, W_B), row_map)
    kern = functools.partial(_mixer_kernel, chunk=chunk, vn_last_only=vn_last_only,
                             tiles_per_seq=tiles_per_seq)
    return pl.pallas_call(
        kern, grid=grid,
        in_specs=[pl.BlockSpec((tm, D_MODEL), row_map), pl.BlockSpec((W_A, tm), lambda r: (0, r)),
                  pl.BlockSpec(wb.shape, const2), pl.BlockSpec(wmix.shape, const3),
                  pl.BlockSpec(bsb.shape, const3), pl.BlockSpec((1, W_B), const2),
                  pl.BlockSpec((1, W_B), const2), pl.BlockSpec(wout.shape, const2),
                  pl.BlockSpec((1, D_MODEL), const2), pl.BlockSpec((1, D_MODEL), const2)],
        out_specs=(pl.BlockSpec((tm, D_MODEL), row_map), vn_spec),
        out_shape=(jax.ShapeDtypeStruct((rows, D_MODEL), F32), vn_shape),
        compiler_params=pltpu.CompilerParams(dimension_semantics=("arbitrary",),
                                             vmem_limit_bytes=VMEM_LIMIT),
        name="mixer_chunk%d" % chunk,
    )(x2d, attn, wb, wmix, bsb, lnvg, lnvb, wout, lng, lnb)


def kernel(x_prompt, x_sample, cache_k, cache_v, cache_k_idx, page_table, w_in, w_s, b_s,
           ln_v_g, ln_v_b, w_out, ln_g, ln_b):
    w = w_in[0]
    wq, wk, wv = w[:, 0:512], w[:, 512:640], w[:, 640:768]
    wga, wqi, wki, wwi = w[:, 768:1280], w[:, 1280:1792], w[:, 1792:1856], w[:, 1856:1864]
    wu, wvb, wgb = w[:, 1864:2376], w[:, 2376:2888], w[:, 2888:3400]
    wt = jnp.concatenate([wq.T * Q_SCALE, wqi.T, wv.T, wk.T, wki.T, wwi.T,
                          jnp.zeros((WT_ROWS - WT_WI - N_IDX_HEADS, D_MODEL), F32)], axis=0).astype(BF16)
    wb = jnp.concatenate([wga.T, wu.T, wvb.T, wgb.T], axis=0).astype(BF16)
    wout = w_out[0].astype(BF16)
    lnvg, lnvb = ln_v_g[0][None, :], ln_v_b[0][None, :]
    lng, lnb = ln_g[0][None, :], ln_b[0][None, :]

    xp = x_prompt.reshape(BATCH * SEQ, D_MODEL)
    tabs_p = _rope_tables(np.arange(SEQ))
    kt_p, vt_p, kit_p, kbf, kiw, qt, qit, vt3, wit = _project(xp, wt, *tabs_p, SEQ)
    attn_p = _attn_prompt(qt, qit, wit, kbf, kiw, vt3)
    tril = np.tril(np.ones((CHUNK, CHUNK), np.float32))
    wmix_p = (w_s[0] * tril[None]).astype(BF16)
    bsb_p = jnp.broadcast_to(b_s[0][:, :, None], (N_GROUPS_B, CHUNK, GROUP_W_B))
    y_p, vn_p = _mixer(xp, attn_p, wb, wmix_p, bsb_p, lnvg, lnvb, wout, lng, lnb,
                       chunk=CHUNK, vn_last_only=True)

    rows_s = DEC_BATCH * DEC_SEQ
    xs = x_sample.reshape(rows_s, D_MODEL)
    tabs_s = _rope_tables(PAST_LEN + (np.arange(rows_s) % DEC_SEQ))
    kt_s, vt_s, kit_s, _, _, qt_s, qit_s, _, wit_s = _project(xs, wt, *tabs_s, rows_s)

    def to_rows(a_t):
        a = a_t.reshape(N_HEADS, HEAD_DIM, DEC_BATCH, DEC_SEQ).transpose(2, 0, 3, 1)
        return jnp.pad(a, ((0, 0), (0, 0), (0, SAMPLE_Q - DEC_SEQ), (0, 0)))

    qi_rows = to_rows(qit_s).reshape(DEC_BATCH, N_IDX_HEADS * SAMPLE_Q, IDX_DIM)
    q5 = to_rows(qt_s).reshape(DEC_BATCH, N_KV, N_HEADS // N_KV, SAMPLE_Q, HEAD_DIM)
    zq = jnp.zeros_like(q5[:, 0])
    q_rows = jnp.stack([jnp.concatenate([q5[:, 0], zq], axis=-1),
                        jnp.concatenate([zq, q5[:, 1]], axis=-1)], axis=1)
    q_rows = q_rows.reshape(DEC_BATCH, N_HEADS * SAMPLE_Q, 2 * HEAD_DIM)
    w_rows = jnp.pad(wit_s.reshape(N_IDX_HEADS, DEC_BATCH, DEC_SEQ).transpose(1, 0, 2),
                     ((0, 0), (0, 0), (0, SAMPLE_Q - DEC_SEQ)))
    w_rows = jnp.broadcast_to(w_rows.reshape(DEC_BATCH, N_IDX_HEADS * SAMPLE_Q, 1),
                              (DEC_BATCH, N_IDX_HEADS * SAMPLE_Q, LANES))
    new_cols = lambda a: a.reshape(a.shape[0], DEC_BATCH, DEC_SEQ).transpose(1, 0, 2)
    n_pool = cache_k.shape[1]
    ck_t = cache_k[0].transpose(0, 2, 3, 1).reshape(n_pool, N_KV * HEAD_DIM, PAGE_SIZE)
    cv_t = cache_v[0].transpose(0, 2, 3, 1).reshape(n_pool, N_KV * HEAD_DIM, PAGE_SIZE)
    cki_t = cache_k_idx[0].transpose(0, 2, 1)
    attn_s8 = _attn_sample(page_table, qi_rows, q_rows, w_rows, new_cols(kit_s[0]),
                           new_cols(kt_s[0]), new_cols(vt_s[0]), ck_t, cv_t, cki_t)
    attn_s = attn_s8[:, :DEC_SEQ, :].reshape(rows_s, W_A).T
    ri = np.arange(rows_s)
    same_seq = (ri[:, None] // DEC_SEQ) == (ri[None, :] // DEC_SEQ)
    wmix_s = jnp.zeros((N_GROUPS_B, rows_s, rows_s), F32)
    for t in range(DEC_SEQ):
        for s in range(t + 1):
            hit = same_seq & (ri[:, None] % DEC_SEQ == t) & (ri[None, :] % DEC_SEQ == s)
            wmix_s = wmix_s + jnp.where(hit[None], w_s[0, :, t, s][:, None, None], 0.0)
    wmix_s = wmix_s.astype(BF16)
    bsb_s = jnp.broadcast_to(jnp.tile(b_s[0][:, :DEC_SEQ], (1, DEC_BATCH))[:, :, None],
                             (N_GROUPS_B, rows_s, GROUP_W_B))
    y_s, vn_s = _mixer(xs, attn_s, wb, wmix_s, bsb_s, lnvg, lnvb, wout, lng, lnb,
                       chunk=rows_s, vn_last_only=False)

    def heads_last(a, n, t):
        a = a.reshape(a.shape[0], N_KV, HEAD_DIM, a.shape[2]).transpose(0, 3, 1, 2)
        return a.reshape(1, n, t, N_KV, HEAD_DIM)

    return (y_p.reshape(BATCH, SEQ, D_MODEL),
            y_s.reshape(DEC_BATCH, DEC_SEQ, D_MODEL),
            heads_last(kt_p, BATCH, SEQ), heads_last(vt_p, BATCH, SEQ),
            kit_p.transpose(0, 2, 1)[None],
            vn_p.reshape(1, BATCH, CHUNK, W_B),
            heads_last(kt_s, DEC_BATCH, DEC_SEQ), heads_last(vt_s, DEC_BATCH, DEC_SEQ),
            kit_s[0].T.reshape(1, DEC_BATCH, DEC_SEQ, IDX_DIM),
            vn_s.reshape(1, DEC_BATCH, DEC_SEQ, W_B))
```

```python
import functools

import jax
import jax.numpy as jnp
import numpy as np
from jax import lax
from jax.experimental import pallas as pl
from jax.experimental.pallas import tpu as pltpu

F32 = jnp.float32
BF16 = jnp.bfloat16

D_MODEL = 1024
SEQ = 2048
BATCH = 8
DEC_BATCH = 128
DEC_SEQ = 4
PAST_LEN = 2048
PAGE_SIZE = 128
N_PAGES = PAST_LEN // PAGE_SIZE
W_A = 512
W_B = 512
HEAD_DIM = 64
N_HEADS = 8
N_KV = 2
N_IDX_HEADS = 8
IDX_DIM = 64
TOPK = 256
CHUNK = 128
N_GROUPS_B = 4
GROUP_W_B = 128
ROPE_THETA = 10000.0
LN_EPS = 1e-5
ALPHA = 2.0 ** 0.25
NEG = -1e30
IDX_SCALE = float((N_IDX_HEADS * IDX_DIM) ** -0.5)

LANES = 128
KEY_TILE = 128
KEY_STEP = 256
ROW_TILE = 512
PROJ_TILE = 1024
WT_Q, WT_QI, WT_V, WT_K, WT_KI, WT_WI, WT_ROWS = 0, 512, 1024, 1152, 1280, 1344, 1360
BISECT_ITERS = 20
FOLD_CHAINS = 8
ONES_ROWS = 16
Q_SCALE = float(HEAD_DIM ** -0.5 * 1.4426950408889634)
VMEM_LIMIT = 48 * 1024 * 1024

SAMPLE_GROUP = 4
PAGE_BUFS = 3
SAMPLE_IDX_GROUP = 8
SAMPLE_KEYS = PAST_LEN + KEY_TILE
SAMPLE_Q = 8


def _proj_kernel(x_ref, wt_ref, cost_ref, sint_ref,
                 kt_ref, vt32_ref, kit_ref, kbf_ref, kiw_ref, qt_ref, qit_ref, vt_ref, wit_ref):
    tm = x_ref.shape[0]
    xb = x_ref[...].astype(BF16)
    zt = lax.dot_general(wt_ref[...], xb, (((1,), (1,)), ((), ())),
                         preferred_element_type=F32)
    cost = cost_ref[...]
    sint = sint_ref[...]
    half = HEAD_DIM // 2

    def rope_rows(base, n_heads):
        pieces = []
        for h in range(n_heads):
            x1 = zt[base + h * HEAD_DIM: base + h * HEAD_DIM + half]
            x2 = zt[base + h * HEAD_DIM + half: base + (h + 1) * HEAD_DIM]
            pieces.append(x1 * cost - x2 * sint)
            pieces.append(x2 * cost + x1 * sint)
        return jnp.concatenate(pieces, axis=0)

    qt_ref[...] = rope_rows(WT_Q, N_HEADS).astype(BF16)
    qit_ref[...] = rope_rows(WT_QI, N_IDX_HEADS).astype(BF16)
    v_t = zt[WT_V:WT_V + 128]
    vt32_ref[0] = v_t
    vtb = v_t.astype(BF16)
    for c in range(tm // KEY_STEP):
        vt_ref[c] = vtb[:, c * KEY_STEP:(c + 1) * KEY_STEP]
    k_t = rope_rows(WT_K, N_KV)
    ki_t = rope_rows(WT_KI, 1)
    kt_ref[0] = k_t
    kit_ref[0] = ki_t
    kbf_ref[...] = k_t.T.astype(BF16)
    kiw_ref[...] = jnp.concatenate([ki_t, jnp.zeros_like(ki_t)], axis=0).T.astype(BF16)
    wit_ref[...] = zt[WT_WI:WT_WI + N_IDX_HEADS] * IDX_SCALE


def _project(x2d, wt, cost, sint, seq_len):
    rows = x2d.shape[0]
    tm = min(rows, PROJ_TILE)
    grid = (rows // tm,)
    n_pos_tiles = seq_len // tm
    row_map = lambda r: (r, 0)
    col_map = lambda r: (0, r)
    const = lambda r: (0, 0)
    seq_map = lambda r: (r // n_pos_tiles, 0, r % n_pos_tiles)
    out_shape = (
        jax.ShapeDtypeStruct((rows // seq_len, 128, seq_len), F32),
        jax.ShapeDtypeStruct((rows // seq_len, 128, seq_len), F32),
        jax.ShapeDtypeStruct((rows // seq_len, IDX_DIM, seq_len), F32),
        jax.ShapeDtypeStruct((rows, 128), BF16),
        jax.ShapeDtypeStruct((rows, 128), BF16),
        jax.ShapeDtypeStruct((512, rows), BF16),
        jax.ShapeDtypeStruct((512, rows), BF16),
        jax.ShapeDtypeStruct((rows // KEY_STEP, 128, KEY_STEP), BF16),
        jax.ShapeDtypeStruct((8, rows), F32),
    )
    out_specs = (
        pl.BlockSpec((1, 128, tm), seq_map),
        pl.BlockSpec((1, 128, tm), seq_map),
        pl.BlockSpec((1, IDX_DIM, tm), seq_map),
        pl.BlockSpec((tm, 128), row_map),
        pl.BlockSpec((tm, 128), row_map),
        pl.BlockSpec((512, tm), col_map),
        pl.BlockSpec((512, tm), col_map),
        pl.BlockSpec((tm // KEY_STEP, 128, KEY_STEP), lambda r: (r, 0, 0)),
        pl.BlockSpec((8, tm), col_map),
    )
    in_specs = [
        pl.BlockSpec((tm, D_MODEL), row_map),
        pl.BlockSpec(wt.shape, const),
        pl.BlockSpec((HEAD_DIM // 2, tm), lambda r: (0, r % n_pos_tiles)),
        pl.BlockSpec((HEAD_DIM // 2, tm), lambda r: (0, r % n_pos_tiles)),
    ]
    return pl.pallas_call(
        _proj_kernel, grid=grid, in_specs=in_specs, out_specs=out_specs, out_shape=out_shape,
        compiler_params=pltpu.CompilerParams(dimension_semantics=("arbitrary",),
                                             vmem_limit_bytes=VMEM_LIMIT),
        name="proj_attn_side",
    )(x2d, wt, cost, sint)


def _rope_tables(pos):
    half = HEAD_DIM // 2
    inv = ROPE_THETA ** (-np.arange(half, dtype=np.float64) / half)
    ang = inv[:, None] * pos.astype(np.float64)[None, :]
    return jnp.asarray(np.cos(ang), dtype=F32), jnp.asarray(np.sin(ang), dtype=F32)


def _select_threshold(count_ge, kth_tie_index, min_ge, max_lt, lo0, hi0, n_allowed, n_keys):
    k = float(TOPK)

    def split(mid, lo, hi, clo, chi):
        c = count_ge(mid)
        take = c >= k
        return (jnp.where(take, mid, lo), jnp.where(take, hi, mid),
                jnp.where(take, c, clo), jnp.where(take, chi, c))

    def step(_, carry):
        lo, hi, clo, chi = carry
        return split(0.5 * lo + 0.5 * hi, lo, hi, clo, chi)

    lo, hi, clo, chi = lax.fori_loop(0, BISECT_ITERS, step,
                                     (lo0, hi0, n_allowed, jnp.zeros_like(lo0)))
    big = jnp.full_like(lo, float(4 * n_keys))

    def any_true(mask):
        return jnp.max(jnp.where(mask, 1.0, 0.0)) > 0.5

    def exact(args):
        lo, hi, clo, chi = args

        def cond(c):
            _, _, clo, _, vlo, vhi, it = c
            open_ = jnp.logical_and(clo > k, vlo < vhi)
            return jnp.logical_and(any_true(open_), it < 2 * n_keys)

        def body(c):
            lo, hi, clo, chi, vlo, vhi, it = c
            mid = 0.5 * vlo + 0.5 * vhi
            lo, hi, clo, chi = split(jnp.where(mid > vlo, mid, vhi), lo, hi, clo, chi)
            return lo, hi, clo, chi, min_ge(lo), max_lt(hi), it + 1

        lo, hi, clo, chi, vlo, _, _ = lax.while_loop(
            cond, body, (lo, hi, clo, chi, min_ge(lo), max_lt(hi), jnp.int32(0)))
        return vlo, jnp.where(clo > k, kth_tie_index(vlo, k - chi), big)

    def fast(args):
        return args[0], big

    return lax.cond(any_true(clo > k), exact, fast, (lo, hi, clo, chi))


def _fold_rows(x, op):
    groups = [x[r:r + 8] for r in range(0, x.shape[0], 8)]
    parts = groups[:FOLD_CHAINS]
    for n, grp in enumerate(groups[FOLD_CHAINS:]):
        parts[n % FOLD_CHAINS] = op(parts[n % FOLD_CHAINS], grp)
    while len(parts) > 1:
        nxt = [op(parts[a], parts[a + 1]) for a in range(0, len(parts) - 1, 2)]
        if len(parts) % 2:
            nxt.append(parts[-1])
        parts = nxt
    return parts[0]


def _topk_bias_in_place(s_ref, n_keys, n_allowed, lo0=None, hi0=None):
    lanes = s_ref.shape[1]
    inf = jnp.float32(jnp.inf)
    chunk = 8 * FOLD_CHAINS
    kidx0 = lax.broadcasted_iota(jnp.int32, (chunk, lanes), 0).astype(F32)

    def key_reduce(fn, op, finish):
        part = None
        for r in range(0, n_keys, chunk):
            v = fn(s_ref[r:r + chunk, :], r)
            part = v if part is None else op(part, v)
        return finish(_fold_rows(part, op), axis=0, keepdims=True)

    count_ge = lambda thr: key_reduce(lambda t, r: jnp.where(t >= thr, 1.0, 0.0), jnp.add, jnp.sum)
    tri = (lax.broadcasted_iota(jnp.int32, (chunk, chunk), 0)
           >= lax.broadcasted_iota(jnp.int32, (chunk, chunk), 1)).astype(BF16)

    def kth_tie_index(thr, need):
        seen = jnp.zeros((1, lanes), F32)
        best = jnp.full((1, lanes), -1.0, F32)
        for r in range(0, n_keys, chunk):
            tie = jnp.where(s_ref[r:r + chunk, :] == thr, 1.0, 0.0)
            upto = jnp.dot(tri, tie.astype(BF16), preferred_element_type=F32) + seen
            hit = jnp.where(tie > 0.5, jnp.where(upto == need, kidx0 + r, -1.0), -1.0)
            best = jnp.maximum(best, jnp.max(_fold_rows(hit, jnp.maximum), axis=0, keepdims=True))
            seen = upto[chunk - 1:chunk, :]
        return best
    min_ge = lambda thr: key_reduce(lambda t, r: jnp.where(t >= thr, t, inf), jnp.minimum, jnp.min)
    max_lt = lambda thr: key_reduce(lambda t, r: jnp.where(t < thr, t, -inf), jnp.maximum, jnp.max)
    if lo0 is None:
        lo0 = key_reduce(lambda t, r: jnp.where(t == -inf, inf, t), jnp.minimum, jnp.min)
        mx1 = key_reduce(lambda t, r: t, jnp.maximum, jnp.max)
        hi0 = mx1 + (jnp.abs(mx1) * (2.0 ** -20) + 1e-30)
    lo, jmax = _select_threshold(count_ge, kth_tie_index, min_ge, max_lt, lo0, hi0, n_allowed, n_keys)
    for r in range(0, n_keys, chunk):
        t = s_ref[r:r + chunk, :]
        tie = jnp.where(t == lo, jnp.where(kidx0 <= jmax - r, 0.0, NEG), NEG)
        s_ref[r:r + chunk, :] = jnp.where(t > lo, 0.0, tie)


def _attn_prompt_kernel(qt_ref, qit_ref, wit_ref, kbf_ref, kiw_ref, vt_ref, o_ref,
                        s_ref, sc_ref, ri_ref, rq_ref):
    i = pl.program_id(1)
    w = wit_ref[...]
    zeros_half = jnp.zeros((HEAD_DIM, LANES), BF16)

    for hp in range(4):
        a = qit_ref[(2 * hp) * 64:(2 * hp + 1) * 64, :]
        b = qit_ref[(2 * hp + 1) * 64:(2 * hp + 2) * 64, :]
        ri_ref[hp] = jnp.concatenate(
            [jnp.concatenate([a, zeros_half], axis=0), jnp.concatenate([b, zeros_half], axis=0)], axis=1)
        g = hp // 2
        a = qt_ref[(2 * hp) * 64:(2 * hp + 1) * 64, :]
        b = qt_ref[(2 * hp + 1) * 64:(2 * hp + 2) * 64, :]
        if g == 0:
            ab = jnp.concatenate([jnp.concatenate([a, zeros_half], axis=0),
                                  jnp.concatenate([b, zeros_half], axis=0)], axis=1)
        else:
            ab = jnp.concatenate([jnp.concatenate([zeros_half, a], axis=0),
                                  jnp.concatenate([zeros_half, b], axis=0)], axis=1)
        rq_ref[hp] = ab

    inf = jnp.float32(jnp.inf)
    n_steps = (i * LANES + LANES + KEY_STEP - 1) // KEY_STEP

    def process(ns):
        n_keys = ns * KEY_STEP
        keys = slice(0, n_keys)
        row = lax.broadcasted_iota(jnp.int32, (n_keys, LANES), 0)
        col = lax.broadcasted_iota(jnp.int32, (n_keys, LANES), 1)
        qpos = col + i * LANES
        allowed = row <= qpos

        kt = kiw_ref[keys, :]
        acc = jnp.zeros((n_keys, LANES), F32)
        for hp in range(4):
            s2 = jnp.dot(kt, ri_ref[hp], preferred_element_type=F32)
            acc = acc + jnp.maximum(s2[:, :LANES], 0.0) * w[2 * hp:2 * hp + 1, :]
            acc = acc + jnp.maximum(s2[:, LANES:], 0.0) * w[2 * hp + 1:2 * hp + 2, :]
        s_ref[keys, :] = jnp.where(allowed, acc, -inf)
        lo0 = jnp.min(_fold_rows(jnp.where(allowed, acc, inf), jnp.minimum), axis=0, keepdims=True)
        mx1 = jnp.max(_fold_rows(jnp.where(allowed, acc, -inf), jnp.maximum), axis=0, keepdims=True)
        hi0 = mx1 + (jnp.abs(mx1) * (2.0 ** -20) + 1e-30)
        n_allowed = (qpos[0:1, :] + 1).astype(F32)

        _topk_bias_in_place(s_ref, n_keys, n_allowed, lo0, hi0)

        kt = kbf_ref[keys, :]
        b = s_ref[keys, :]
        b2 = jnp.concatenate([b, b], axis=1)
        ms = []
        for hp in range(4):
            s2 = jnp.dot(kt, rq_ref[hp], preferred_element_type=F32) + b2
            sc_ref[hp, keys, :] = s2
            ms.append(jnp.max(_fold_rows(s2, jnp.maximum), axis=0, keepdims=True))
        vt = jnp.concatenate([vt_ref[j] for j in range(ns)], axis=1)
        ones_rows = jnp.ones((ONES_ROWS, n_keys), BF16)
        lhs = [jnp.concatenate([vt[g * 64:(g + 1) * 64, :], ones_rows], axis=0) for g in range(N_KV)]
        pieces = []
        for hp in range(4):
            p = jnp.exp2((sc_ref[hp, keys, :] - ms[hp]).astype(BF16))
            o = jnp.dot(lhs[hp // 2], p, preferred_element_type=F32)
            o2 = o[0:HEAD_DIM, :] / o[HEAD_DIM:HEAD_DIM + 1, :]
            pieces.append(o2[:, :LANES])
            pieces.append(o2[:, LANES:])
        o_ref[...] = jnp.concatenate(pieces, axis=0)

    for ns in range(1, SEQ // KEY_STEP + 1):
        pl.when(n_steps == ns)(functools.partial(process, ns))


def _attn_prompt(qt, qit, wit, kbf, kiw, vt3):
    n_blocks = SEQ // LANES
    grid = (BATCH, n_blocks)
    qmap = lambda n, i: (0, n * n_blocks + i)
    in_specs = [
        pl.BlockSpec((512, LANES), qmap),
        pl.BlockSpec((512, LANES), qmap),
        pl.BlockSpec((8, LANES), qmap),
        pl.BlockSpec((SEQ, 128), lambda n, i: (n, 0)),
        pl.BlockSpec((SEQ, 128), lambda n, i: (n, 0)),
        pl.BlockSpec((SEQ // KEY_STEP, 128, KEY_STEP), lambda n, i: (n, 0, 0)),
    ]
    return pl.pallas_call(
        _attn_prompt_kernel, grid=grid, in_specs=in_specs,
        out_specs=pl.BlockSpec((W_A, LANES), qmap),
        out_shape=jax.ShapeDtypeStruct((W_A, BATCH * SEQ), F32),
        scratch_shapes=[pltpu.VMEM((SEQ, LANES), F32), pltpu.VMEM((4, SEQ, 2 * LANES), F32),
                        pltpu.VMEM((4, 128, 2 * LANES), BF16), pltpu.VMEM((4, 128, 2 * LANES), BF16)],
        compiler_params=pltpu.CompilerParams(dimension_semantics=("arbitrary", "arbitrary"),
                                             vmem_limit_bytes=VMEM_LIMIT),
        name="attn_prompt",
    )(qt, qit, wit, kbf, kiw, vt3)


def _page_copies(pt_ref, group, slot, gsz, streams):
    out = []
    for g in range(gsz):
        seq = group * gsz + g
        for p in range(N_PAGES):
            page = pt_ref[seq, p]
            keys = pl.ds(p * PAGE_SIZE, PAGE_SIZE)
            for hbm, buf, sem in streams:
                out.append(pltpu.make_async_copy(hbm.at[page], buf.at[slot, g, :, keys], sem.at[slot]))
    return out


def _paged_prefetch(pt_ref, gsz, streams):
    step = pl.program_id(0)
    n_steps = pl.num_programs(0)
    ahead = PAGE_BUFS - 1
    slot = step % PAGE_BUFS

    @pl.when(step == 0)
    def _():
        tail = pl.ds(PAST_LEN, KEY_TILE)
        for _, buf, _ in streams:
            for s_ in range(PAGE_BUFS):
                for g in range(gsz):
                    buf[s_, g, :, tail] = jnp.zeros((buf.shape[2], KEY_TILE), F32)
        for s_ in range(ahead):
            for c in _page_copies(pt_ref, s_, s_, gsz, streams):
                c.start()

    @pl.when(step + ahead < n_steps)
    def _():
        for c in _page_copies(pt_ref, step + ahead, (step + ahead) % PAGE_BUFS, gsz, streams):
            c.start()

    for c in _page_copies(pt_ref, step, slot, gsz, streams):
        c.wait()
    return slot


def _sample_index_kernel(pt_ref, qi_ref, w_ref, kin_ref, cki_hbm, s_out_ref, kibuf, sem):
    gsz = SAMPLE_IDX_GROUP
    slot = _paged_prefetch(pt_ref, gsz, [(cki_hbm, kibuf, sem)])
    rows8 = lax.broadcasted_iota(jnp.int32, (SAMPLE_Q, SAMPLE_KEYS), 0)
    keyi = lax.broadcasted_iota(jnp.int32, (SAMPLE_Q, SAMPLE_KEYS), 1)
    allowed = keyi <= (PAST_LEN + rows8)
    new_keys = pl.ds(PAST_LEN, DEC_SEQ)
    scores = []
    for g in range(gsz):
        kibuf[slot, g, :, new_keys] = kin_ref[g]
        kib = kibuf[slot, g].astype(BF16)
        s = jnp.dot(qi_ref[g], kib, preferred_element_type=F32)
        sw = jnp.maximum(s, 0.0) * w_ref[g][:, 0:1]
        acc = jnp.sum(sw.reshape(N_IDX_HEADS, SAMPLE_Q, SAMPLE_KEYS), axis=0)
        scores.append(jnp.where(allowed, acc, -jnp.inf))
    for a in range(gsz // 2):
        s_out_ref[a] = jnp.where(rows8 < DEC_SEQ, scores[2 * a],
                                 pltpu.roll(scores[2 * a + 1], DEC_SEQ, 0))


def _sample_select_kernel(s_ref, b_ref, st_ref):
    rows = s_ref.shape[0]
    st_ref[...] = s_ref[...].T
    t4 = lax.broadcasted_iota(jnp.int32, (1, rows), 1) % DEC_SEQ
    _topk_bias_in_place(st_ref, SAMPLE_KEYS, (PAST_LEN + 1 + t4).astype(F32))
    b_ref[...] = st_ref[...].T


def _sample_attend_kernel(pt_ref, q_ref, b_ref, kn_ref, vn_ref, ck_hbm, cv_hbm, o_ref,
                          kbuf, vbuf, ksem, vsem):
    gsz = SAMPLE_GROUP
    slot = _paged_prefetch(pt_ref, gsz, [(ck_hbm, kbuf, ksem), (cv_hbm, vbuf, vsem)])
    new_keys = pl.ds(PAST_LEN, DEC_SEQ)
    lane = lax.broadcasted_iota(jnp.int32, (SAMPLE_Q, LANES), 1)
    scores = []
    for g in range(gsz):
        kbuf[slot, g, :, new_keys] = kn_ref[g]
        vbuf[slot, g, :, new_keys] = vn_ref[g]
        kb = kbuf[slot, g].astype(BF16)
        b = b_ref[g // 2]
        if g % 2:
            b = pltpu.roll(b, DEC_SEQ, 0)
        s = jnp.dot(q_ref[g], kb, preferred_element_type=F32)
        scores.append(s + jnp.concatenate([b] * N_HEADS, axis=0))
    probs = []
    for s in scores:
        p = jnp.exp2(s - jnp.max(s, axis=1, keepdims=True))
        probs.append((p.astype(BF16), jnp.sum(p, axis=1, keepdims=True)))
    for g in range(gsz):
        p, l = probs[g]
        vb = vbuf[slot, g].astype(BF16)
        o = lax.dot_general(p, vb, (((1,), (1,)), ((), ())),
                            preferred_element_type=F32) / l
        o_r = pltpu.roll(o, HEAD_DIM, 1)
        tiles = []
        for c in range(4):
            ha, hb = 2 * c, 2 * c + 1
            src_a = o if ha // 4 == 0 else o_r
            src_b = o if hb // 4 == 1 else o_r
            tiles.append(jnp.where(lane < HEAD_DIM, src_a[ha * SAMPLE_Q:(ha + 1) * SAMPLE_Q, :],
                                   src_b[hb * SAMPLE_Q:(hb + 1) * SAMPLE_Q, :]))
        o_ref[g] = jnp.concatenate(tiles, axis=1)


def _attn_sample(page_table, qi_s, q_s, w_s, ki_new, k_new, v_new, ck, cv, cki):
    params = pltpu.CompilerParams(dimension_semantics=("arbitrary",), vmem_limit_bytes=VMEM_LIMIT)
    blk = lambda n, shape: pl.BlockSpec((n,) + shape, lambda s, pt: (s, 0, 0))
    hbm = pl.BlockSpec(memory_space=pl.ANY)
    n_pairs = DEC_BATCH // 2

    gi = SAMPLE_IDX_GROUP
    scores = pl.pallas_call(
        _sample_index_kernel,
        grid_spec=pltpu.PrefetchScalarGridSpec(
            num_scalar_prefetch=1, grid=(DEC_BATCH // gi,),
            in_specs=[blk(gi, (64, IDX_DIM)), blk(gi, (64, LANES)), blk(gi, (IDX_DIM, DEC_SEQ)), hbm],
            out_specs=blk(gi // 2, (SAMPLE_Q, SAMPLE_KEYS)),
            scratch_shapes=[pltpu.VMEM((PAGE_BUFS, gi, IDX_DIM, SAMPLE_KEYS), F32),
                            pltpu.SemaphoreType.DMA((PAGE_BUFS,))]),
        out_shape=jax.ShapeDtypeStruct((n_pairs, SAMPLE_Q, SAMPLE_KEYS), F32),
        compiler_params=params, name="sample_index",
    )(page_table, qi_s, w_s, ki_new, cki)

    rows = n_pairs * SAMPLE_Q
    bias = pl.pallas_call(
        _sample_select_kernel, grid=(1,),
        in_specs=[pl.BlockSpec((rows, SAMPLE_KEYS), lambda s: (0, 0))],
        out_specs=pl.BlockSpec((rows, SAMPLE_KEYS), lambda s: (0, 0)),
        out_shape=jax.ShapeDtypeStruct((rows, SAMPLE_KEYS), F32),
        scratch_shapes=[pltpu.VMEM((SAMPLE_KEYS, rows), F32)],
        compiler_params=params, name="sample_select",
    )(scores.reshape(rows, SAMPLE_KEYS)).reshape(n_pairs, SAMPLE_Q, SAMPLE_KEYS)

    ga = SAMPLE_GROUP
    return pl.pallas_call(
        _sample_attend_kernel,
        grid_spec=pltpu.PrefetchScalarGridSpec(
            num_scalar_prefetch=1, grid=(DEC_BATCH // ga,),
            in_specs=[blk(ga, (64, 128)), blk(ga // 2, (SAMPLE_Q, SAMPLE_KEYS)),
                      blk(ga, (128, DEC_SEQ)), blk(ga, (128, DEC_SEQ)), hbm, hbm],
            out_specs=blk(ga, (SAMPLE_Q, W_A)),
            scratch_shapes=[pltpu.VMEM((PAGE_BUFS, ga, 128, SAMPLE_KEYS), F32),
                            pltpu.VMEM((PAGE_BUFS, ga, 128, SAMPLE_KEYS), F32),
                            pltpu.SemaphoreType.DMA((PAGE_BUFS,)),
                            pltpu.SemaphoreType.DMA((PAGE_BUFS,))]),
        out_shape=jax.ShapeDtypeStruct((DEC_BATCH, SAMPLE_Q, W_A), F32),
        compiler_params=params, name="sample_attend",
    )(page_table, q_s, bias, k_new, v_new, ck, cv)


def _layer_norm(x, g, b):
    mu = jnp.mean(x, axis=-1, keepdims=True)
    xc = x - mu
    var = jnp.mean(xc * xc, axis=-1, keepdims=True)
    return xc * lax.rsqrt(var + LN_EPS) * g + b


def _silu(x):
    return x / (1.0 + jnp.exp(-x))


def _mixer_kernel(x_ref, a_ref, wb_ref, wmix_ref, bsb_ref, lnvg_ref, lnvb_ref, wout_ref,
                  lng_ref, lnb_ref, y_ref, vn_ref, *, chunk, vn_last_only, tiles_per_seq):
    tm = x_ref.shape[0]
    x = x_ref[...]
    xb = x.astype(BF16)
    last = (((1,), (1,)), ((), ()))
    ga_t = lax.dot_general(wb_ref[0:W_A, :], xb, last, preferred_element_type=F32)
    zb = lax.dot_general(xb, wb_ref[W_A:, :], last, preferred_element_type=F32)
    u = zb[:, 0:512]
    vb = zb[:, 512:1024]
    gb = zb[:, 1024:1536]
    vn = _layer_norm(vb, lnvg_ref[...], lnvb_ref[...])
    if vn_last_only:
        @pl.when(pl.program_id(0) % tiles_per_seq == tiles_per_seq - 1)
        def _():
            vn_ref[0] = vn[tm - CHUNK:, :]
    else:
        vn_ref[...] = vn
    vnb = vn.astype(BF16)
    rows = []
    for c in range(tm // chunk):
        cols = []
        for g in range(N_GROUPS_B):
            blk = vnb[c * chunk:(c + 1) * chunk, g * GROUP_W_B:(g + 1) * GROUP_W_B]
            cols.append(jnp.dot(wmix_ref[g], blk, preferred_element_type=F32) + bsb_ref[g])
        rows.append(jnp.concatenate(cols, axis=1))
    mixed = jnp.concatenate(rows, axis=0) if len(rows) > 1 else rows[0]
    a_out_t = (a_ref[...] * _silu(ga_t)).astype(BF16)
    b_out = (u * mixed * _silu(gb)).astype(BF16)
    out = (lax.dot_general(a_out_t, wout_ref[0:W_A, :], (((0,), (0,)), ((), ())),
                           preferred_element_type=F32)
           + jnp.dot(b_out, wout_ref[W_A:, :], preferred_element_type=F32))
    y_ref[...] = _layer_norm(ALPHA * x + out, lng_ref[...], lnb_ref[...])


def _mixer(x2d, attn, wb, wmix, bsb, lnvg, lnvb, wout, lng, lnb, *, chunk, vn_last_only):
    rows = x2d.shape[0]
    tm = ROW_TILE
    tiles_per_seq = SEQ // tm
    grid = (rows // tm,)
    row_map = lambda r: (r, 0)
    const2 = lambda r: (0, 0)
    const3 = lambda r: (0, 0, 0)
    if vn_last_only:
        vn_shape = jax.ShapeDtypeStruct((rows // SEQ, CHUNK, W_B), F32)
        vn_spec = pl.BlockSpec((1, CHUNK, W_B), lambda r: (r // tiles_per_seq, 0, 0))
    else:
        vn_shape = jax.ShapeDtypeStruct((rows, W_B), F32)
        vn_spec = pl.BlockSpec((tm, W_B), row_map)
    kern = functools.partial(_mixer_kernel, chunk=chunk, vn_last_only=vn_last_only,
                             tiles_per_seq=tiles_per_seq)
    return pl.pallas_call(
        kern, grid=grid,
        in_specs=[pl.BlockSpec((tm, D_MODEL), row_map), pl.BlockSpec((W_A, tm), lambda r: (0, r)),
                  pl.BlockSpec(wb.shape, const2), pl.BlockSpec(wmix.shape, const3),
                  pl.BlockSpec(bsb.shape, const3), pl.BlockSpec((1, W_B), const2),
                  pl.BlockSpec((1, W_B), const2), pl.BlockSpec(wout.shape, const2),
                  pl.BlockSpec((1, D_MODEL), const2), pl.BlockSpec((1, D_MODEL), const2)],
        out_specs=(pl.BlockSpec((tm, D_MODEL), row_map), vn_spec),
        out_shape=(jax.ShapeDtypeStruct((rows, D_MODEL), F32), vn_shape),
        compiler_params=pltpu.CompilerParams(dimension_semantics=("arbitrary",),
                                             vmem_limit_bytes=VMEM_LIMIT),
        name="mixer_chunk%d" % chunk,
    )(x2d, attn, wb, wmix, bsb, lnvg, lnvb, wout, lng, lnb)


def kernel(x_prompt, x_sample, cache_k, cache_v, cache_k_idx, page_table, w_in, w_s, b_s,
           ln_v_g, ln_v_b, w_out, ln_g, ln_b):
    w = w_in[0]
    wq, wk, wv = w[:, 0:512], w[:, 512:640], w[:, 640:768]
    wga, wqi, wki, wwi = w[:, 768:1280], w[:, 1280:1792], w[:, 1792:1856], w[:, 1856:1864]
    wu, wvb, wgb = w[:, 1864:2376], w[:, 2376:2888], w[:, 2888:3400]
    wt = jnp.concatenate([wq.T * Q_SCALE, wqi.T, wv.T, wk.T, wki.T, wwi.T,
                          jnp.zeros((WT_ROWS - WT_WI - N_IDX_HEADS, D_MODEL), F32)], axis=0).astype(BF16)
    wb = jnp.concatenate([wga.T, wu.T, wvb.T, wgb.T], axis=0).astype(BF16)
    wout = w_out[0].astype(BF16)
    lnvg, lnvb = ln_v_g[0][None, :], ln_v_b[0][None, :]
    lng, lnb = ln_g[0][None, :], ln_b[0][None, :]

    xp = x_prompt.reshape(BATCH * SEQ, D_MODEL)
    tabs_p = _rope_tables(np.arange(SEQ))
    kt_p, vt_p, kit_p, kbf, kiw, qt, qit, vt3, wit = _project(xp, wt, *tabs_p, SEQ)
    attn_p = _attn_prompt(qt, qit, wit, kbf, kiw, vt3)
    tril = np.tril(np.ones((CHUNK, CHUNK), np.float32))
    wmix_p = (w_s[0] * tril[None]).astype(BF16)
    bsb_p = jnp.broadcast_to(b_s[0][:, :, None], (N_GROUPS_B, CHUNK, GROUP_W_B))
    y_p, vn_p = _mixer(xp, attn_p, wb, wmix_p, bsb_p, lnvg, lnvb, wout, lng, lnb,
                       chunk=CHUNK, vn_last_only=True)

    rows_s = DEC_BATCH * DEC_SEQ
    xs = x_sample.reshape(rows_s, D_MODEL)
    tabs_s = _rope_tables(PAST_LEN + (np.arange(rows_s) % DEC_SEQ))
    kt_s, vt_s, kit_s, _, _, qt_s, qit_s, _, wit_s = _project(xs, wt, *tabs_s, rows_s)

    def to_rows(a_t):
        a = a_t.reshape(N_HEADS, HEAD_DIM, DEC_BATCH, DEC_SEQ).transpose(2, 0, 3, 1)
        return jnp.pad(a, ((0, 0), (0, 0), (0, SAMPLE_Q - DEC_SEQ), (0, 0)))

    qi_rows = to_rows(qit_s).reshape(DEC_BATCH, N_IDX_HEADS * SAMPLE_Q, IDX_DIM)
    q5 = to_rows(qt_s).reshape(DEC_BATCH, N_KV, N_HEADS // N_KV, SAMPLE_Q, HEAD_DIM)
    zq = jnp.zeros_like(q5[:, 0])
    q_rows = jnp.stack([jnp.concatenate([q5[:, 0], zq], axis=-1),
                        jnp.concatenate([zq, q5[:, 1]], axis=-1)], axis=1)
    q_rows = q_rows.reshape(DEC_BATCH, N_HEADS * SAMPLE_Q, 2 * HEAD_DIM)
    w_rows = jnp.pad(wit_s.reshape(N_IDX_HEADS, DEC_BATCH, DEC_SEQ).transpose(1, 0, 2),
                     ((0, 0), (0, 0), (0, SAMPLE_Q - DEC_SEQ)))
    w_rows = jnp.broadcast_to(w_rows.reshape(DEC_BATCH, N_IDX_HEADS * SAMPLE_Q, 1),
                              (DEC_BATCH, N_IDX_HEADS * SAMPLE_Q, LANES))
    new_cols = lambda a: a.reshape(a.shape[0], DEC_BATCH, DEC_SEQ).transpose(1, 0, 2)
    n_pool = cache_k.shape[1]
    ck_t = cache_k[0].transpose(0, 2, 3, 1).reshape(n_pool, N_KV * HEAD_DIM, PAGE_SIZE)
    cv_t = cache_v[0].transpose(0, 2, 3, 1).reshape(n_pool, N_KV * HEAD_DIM, PAGE_SIZE)
    cki_t = cache_k_idx[0].transpose(0, 2, 1)
    attn_s8 = _attn_sample(page_table, qi_rows, q_rows, w_rows, new_cols(kit_s[0]),
                           new_cols(kt_s[0]), new_cols(vt_s[0]), ck_t, cv_t, cki_t)
    attn_s = attn_s8[:, :DEC_SEQ, :].reshape(rows_s, W_A).T
    ri = np.arange(rows_s)
    same_seq = (ri[:, None] // DEC_SEQ) == (ri[None, :] // DEC_SEQ)
    wmix_s = jnp.zeros((N_GROUPS_B, rows_s, rows_s), F32)
    for t in range(DEC_SEQ):
        for s in range(t + 1):
            hit = same_seq & (ri[:, None] % DEC_SEQ == t) & (ri[None, :] % DEC_SEQ == s)
            wmix_s = wmix_s + jnp.where(hit[None], w_s[0, :, t, s][:, None, None], 0.0)
    wmix_s = wmix_s.astype(BF16)
    bsb_s = jnp.broadcast_to(jnp.tile(b_s[0][:, :DEC_SEQ], (1, DEC_BATCH))[:, :, None],
                             (N_GROUPS_B, rows_s, GROUP_W_B))
    y_s, vn_s = _mixer(xs, attn_s, wb, wmix_s, bsb_s, lnvg, lnvb, wout, lng, lnb,
                       chunk=rows_s, vn_last_only=False)

    def heads_last(a, n, t):
        a = a.reshape(a.shape[0], N_KV, HEAD_DIM, a.shape[2]).transpose(0, 3, 1, 2)
        return a.reshape(1, n, t, N_KV, HEAD_DIM)

    return (y_p.reshape(BATCH, SEQ, D_MODEL),
            y_s.reshape(DEC_BATCH, DEC_SEQ, D_MODEL),
            heads_last(kt_p, BATCH, SEQ), heads_last(vt_p, BATCH, SEQ),
            kit_p.transpose(0, 2, 1)[None],
            vn_p.reshape(1, BATCH, CHUNK, W_B),
            heads_last(kt_s, DEC_BATCH, DEC_SEQ), heads_last(vt_s, DEC_BATCH, DEC_SEQ),
            kit_s[0].T.reshape(1, DEC_BATCH, DEC_SEQ, IDX_DIM),
            vn_s.reshape(1, DEC_BATCH, DEC_SEQ, W_B))
```

```python
import functools

import jax
import jax.numpy as jnp
import numpy as np
from jax import lax
from jax.experimental import pallas as pl
from jax.experimental.pallas import tpu as pltpu

F32 = jnp.float32
BF16 = jnp.bfloat16

D_MODEL = 1024
SEQ = 2048
BATCH = 8
DEC_BATCH = 128
DEC_SEQ = 4
PAST_LEN = 2048
PAGE_SIZE = 128
N_PAGES = PAST_LEN // PAGE_SIZE
W_A = 512
W_B = 512
HEAD_DIM = 64
N_HEADS = 8
N_KV = 2
N_IDX_HEADS = 8
IDX_DIM = 64
TOPK = 256
CHUNK = 128
N_GROUPS_B = 4
GROUP_W_B = 128
ROPE_THETA = 10000.0
LN_EPS = 1e-5
ALPHA = 2.0 ** 0.25
NEG = -1e30
IDX_SCALE = float((N_IDX_HEADS * IDX_DIM) ** -0.5)

LANES = 128
KEY_TILE = 128
KEY_STEP = 256
ROW_TILE = 512
PROJ_TILE = 1024
WT_Q, WT_QI, WT_V, WT_K, WT_KI, WT_WI, WT_ROWS = 0, 512, 1024, 1152, 1280, 1344, 1360
BISECT_ITERS = 20
FOLD_CHAINS = 8
ONES_ROWS = 16
Q_SCALE = float(HEAD_DIM ** -0.5 * 1.4426950408889634)
VMEM_LIMIT = 48 * 1024 * 1024

SAMPLE_GROUP = 4
PAGE_BUFS = 3
SAMPLE_IDX_GROUP = 8
SAMPLE_KEYS = PAST_LEN + KEY_TILE
SAMPLE_Q = 8


def _proj_kernel(x_ref, wt_ref, cost_ref, sint_ref,
                 kt_ref, vt32_ref, kit_ref, kbf_ref, kiw_ref, qt_ref, qit_ref, vt_ref, wit_ref):
    tm = x_ref.shape[0]
    xb = x_ref[...].astype(BF16)
    zt = lax.dot_general(wt_ref[...], xb, (((1,), (1,)), ((), ())),
                         preferred_element_type=F32)
    cost = cost_ref[...]
    sint = sint_ref[...]
    half = HEAD_DIM // 2

    def rope_rows(base, n_heads):
        pieces = []
        for h in range(n_heads):
            x1 = zt[base + h * HEAD_DIM: base + h * HEAD_DIM + half]
            x2 = zt[base + h * HEAD_DIM + half: base + (h + 1) * HEAD_DIM]
            pieces.append(x1 * cost - x2 * sint)
            pieces.append(x2 * cost + x1 * sint)
        return jnp.concatenate(pieces, axis=0)

    qt_ref[...] = rope_rows(WT_Q, N_HEADS).astype(BF16)
    qit_ref[...] = rope_rows(WT_QI, N_IDX_HEADS).astype(BF16)
    v_t = zt[WT_V:WT_V + 128]
    vt32_ref[0] = v_t
    vtb = v_t.astype(BF16)
    for c in range(tm // KEY_STEP):
        vt_ref[c] = vtb[:, c * KEY_STEP:(c + 1) * KEY_STEP]
    k_t = rope_rows(WT_K, N_KV)
    ki_t = rope_rows(WT_KI, 1)
    kt_ref[0] = k_t
    kit_ref[0] = ki_t
    kbf_ref[...] = k_t.T.astype(BF16)
    kiw_ref[...] = jnp.concatenate([ki_t, jnp.zeros_like(ki_t)], axis=0).T.astype(BF16)
    wit_ref[...] = zt[WT_WI:WT_WI + N_IDX_HEADS] * IDX_SCALE


def _project(x2d, wt, cost, sint, seq_len):
    rows = x2d.shape[0]
    tm = min(rows, PROJ_TILE)
    grid = (rows // tm,)
    n_pos_tiles = seq_len // tm
    row_map = lambda r: (r, 0)
    col_map = lambda r: (0, r)
    const = lambda r: (0, 0)
    seq_map = lambda r: (r // n_pos_tiles, 0, r % n_pos_tiles)
    out_shape = (
        jax.ShapeDtypeStruct((rows // seq_len, 128, seq_len), F32),
        jax.ShapeDtypeStruct((rows // seq_len, 128, seq_len), F32),
        jax.ShapeDtypeStruct((rows // seq_len, IDX_DIM, seq_len), F32),
        jax.ShapeDtypeStruct((rows, 128), BF16),
        jax.ShapeDtypeStruct((rows, 128), BF16),
        jax.ShapeDtypeStruct((512, rows), BF16),
        jax.ShapeDtypeStruct((512, rows), BF16),
        jax.ShapeDtypeStruct((rows // KEY_STEP, 128, KEY_STEP), BF16),
        jax.ShapeDtypeStruct((8, rows), F32),
    )
    out_specs = (
        pl.BlockSpec((1, 128, tm), seq_map),
        pl.BlockSpec((1, 128, tm), seq_map),
        pl.BlockSpec((1, IDX_DIM, tm), seq_map),
        pl.BlockSpec((tm, 128), row_map),
        pl.BlockSpec((tm, 128), row_map),
        pl.BlockSpec((512, tm), col_map),
        pl.BlockSpec((512, tm), col_map),
        pl.BlockSpec((tm // KEY_STEP, 128, KEY_STEP), lambda r: (r, 0, 0)),
        pl.BlockSpec((8, tm), col_map),
    )
    in_specs = [
        pl.BlockSpec((tm, D_MODEL), row_map),
        pl.BlockSpec(wt.shape, const),
        pl.BlockSpec((HEAD_DIM // 2, tm), lambda r: (0, r % n_pos_tiles)),
        pl.BlockSpec((HEAD_DIM // 2, tm), lambda r: (0, r % n_pos_tiles)),
    ]
    return pl.pallas_call(
        _proj_kernel, grid=grid, in_specs=in_specs, out_specs=out_specs, out_shape=out_shape,
        compiler_params=pltpu.CompilerParams(dimension_semantics=("arbitrary",),
                                             vmem_limit_bytes=VMEM_LIMIT),
        name="proj_attn_side",
    )(x2d, wt, cost, sint)


def _rope_tables(pos):
    half = HEAD_DIM // 2
    inv = ROPE_THETA ** (-np.arange(half, dtype=np.float64) / half)
    ang = inv[:, None] * pos.astype(np.float64)[None, :]
    return jnp.asarray(np.cos(ang), dtype=F32), jnp.asarray(np.sin(ang), dtype=F32)


def _select_threshold(count_ge, kth_tie_index, min_ge, max_lt, lo0, hi0, n_allowed, n_keys):
    k = float(TOPK)

    def split(mid, lo, hi, clo, chi):
        c = count_ge(mid)
        take = c >= k
        return (jnp.where(take, mid, lo), jnp.where(take, hi, mid),
                jnp.where(take, c, clo), jnp.where(take, chi, c))

    def step(_, carry):
        lo, hi, clo, chi = carry
        return split(0.5 * lo + 0.5 * hi, lo, hi, clo, chi)

    lo, hi, clo, chi = lax.fori_loop(0, BISECT_ITERS, step,
                                     (lo0, hi0, n_allowed, jnp.zeros_like(lo0)))
    big = jnp.full_like(lo, float(4 * n_keys))

    def any_true(mask):
        return jnp.max(jnp.where(mask, 1.0, 0.0)) > 0.5

    def exact(args):
        lo, hi, clo, chi = args

        def cond(c):
            _, _, clo, _, vlo, vhi, it = c
            open_ = jnp.logical_and(clo > k, vlo < vhi)
            return jnp.logical_and(any_true(open_), it < 2 * n_keys)

        def body(c):
            lo, hi, clo, chi, vlo, vhi, it = c
            mid = 0.5 * vlo + 0.5 * vhi
            lo, hi, clo, chi = split(jnp.where(mid > vlo, mid, vhi), lo, hi, clo, chi)
            return lo, hi, clo, chi, min_ge(lo), max_lt(hi), it + 1

        lo, hi, clo, chi, vlo, _, _ = lax.while_loop(
            cond, body, (lo, hi, clo, chi, min_ge(lo), max_lt(hi), jnp.int32(0)))
        return vlo, jnp.where(clo > k, kth_tie_index(vlo, k - chi), big)

    def fast(args):
        return args[0], big

    return lax.cond(any_true(clo > k), exact, fast, (lo, hi, clo, chi))


def _fold_rows(x, op):
    groups = [x[r:r + 8] for r in range(0, x.shape[0], 8)]
    parts = groups[:FOLD_CHAINS]
    for n, grp in enumerate(groups[FOLD_CHAINS:]):
        parts[n % FOLD_CHAINS] = op(parts[n % FOLD_CHAINS], grp)
    while len(parts) > 1:
        nxt = [op(parts[a], parts[a + 1]) for a in range(0, len(parts) - 1, 2)]
        if len(parts) % 2:
            nxt.append(parts[-1])
        parts = nxt
    return parts[0]


def _topk_bias_in_place(s_ref, n_keys, n_allowed, lo0=None, hi0=None):
    lanes = s_ref.shape[1]
    inf = jnp.float32(jnp.inf)
    chunk = 8 * FOLD_CHAINS
    kidx0 = lax.broadcasted_iota(jnp.int32, (chunk, lanes), 0).astype(F32)

    def key_reduce(fn, op, finish):
        part = None
        for r in range(0, n_keys, chunk):
            v = fn(s_ref[r:r + chunk, :], r)
            part = v if part is None else op(part, v)
        return finish(_fold_rows(part, op), axis=0, keepdims=True)

    count_ge = lambda thr: key_reduce(lambda t, r: jnp.where(t >= thr, 1.0, 0.0), jnp.add, jnp.sum)
    tri = (lax.broadcasted_iota(jnp.int32, (chunk, chunk), 0)
           >= lax.broadcasted_iota(jnp.int32, (chunk, chunk), 1)).astype(BF16)

    def kth_tie_index(thr, need):
        seen = jnp.zeros((1, lanes), F32)
        best = jnp.full((1, lanes), -1.0, F32)
        for r in range(0, n_keys, chunk):
            tie = jnp.where(s_ref[r:r + chunk, :] == thr, 1.0, 0.0)
            upto = jnp.dot(tri, tie.astype(BF16), preferred_element_type=F32) + seen
            hit = jnp.where(tie > 0.5, jnp.where(upto == need, kidx0 + r, -1.0), -1.0)
            best = jnp.maximum(best, jnp.max(_fold_rows(hit, jnp.maximum), axis=0, keepdims=True))
            seen = upto[chunk - 1:chunk, :]
        return best
    min_ge = lambda thr: key_reduce(lambda t, r: jnp.where(t >= thr, t, inf), jnp.minimum, jnp.min)
    max_lt = lambda thr: key_reduce(lambda t, r: jnp.where(t < thr, t, -inf), jnp.maximum, jnp.max)
    if lo0 is None:
        lo0 = key_reduce(lambda t, r: jnp.where(t == -inf, inf, t), jnp.minimum, jnp.min)
        mx1 = key_reduce(lambda t, r: t, jnp.maximum, jnp.max)
        hi0 = mx1 + (jnp.abs(mx1) * (2.0 ** -20) + 1e-30)
    lo, jmax = _select_threshold(count_ge, kth_tie_index, min_ge, max_lt, lo0, hi0, n_allowed, n_keys)
    for r in range(0, n_keys, chunk):
        t = s_ref[r:r + chunk, :]
        tie = jnp.where(t == lo, jnp.where(kidx0 <= jmax - r, 0.0, NEG), NEG)
        s_ref[r:r + chunk, :] = jnp.where(t > lo, 0.0, tie)


def _attn_prompt_kernel(qt_ref, qit_ref, wit_ref, kbf_ref, kiw_ref, vt_ref, o_ref,
                        s_ref, sc_ref, ri_ref, rq_ref):
    i = pl.program_id(1)
    w = wit_ref[...]
    zeros_half = jnp.zeros((HEAD_DIM, LANES), BF16)

    for hp in range(4):
        a = qit_ref[(2 * hp) * 64:(2 * hp + 1) * 64, :]
        b = qit_ref[(2 * hp + 1) * 64:(2 * hp + 2) * 64, :]
        ri_ref[hp] = jnp.concatenate(
            [jnp.concatenate([a, zeros_half], axis=0), jnp.concatenate([b, zeros_half], axis=0)], axis=1)
        g = hp // 2
        a = qt_ref[(2 * hp) * 64:(2 * hp + 1) * 64, :]
        b = qt_ref[(2 * hp + 1) * 64:(2 * hp + 2) * 64, :]
        if g == 0:
            ab = jnp.concatenate([jnp.concatenate([a, zeros_half], axis=0),
                                  jnp.concatenate([b, zeros_half], axis=0)], axis=1)
        else:
            ab = jnp.concatenate([jnp.concatenate([zeros_half, a], axis=0),
                                  jnp.concatenate([zeros_half, b], axis=0)], axis=1)
        rq_ref[hp] = ab

    inf = jnp.float32(jnp.inf)
    n_steps = (i * LANES + LANES + KEY_STEP - 1) // KEY_STEP

    def process(ns):
        n_keys = ns * KEY_STEP
        keys = slice(0, n_keys)
        row = lax.broadcasted_iota(jnp.int32, (n_keys, LANES), 0)
        col = lax.broadcasted_iota(jnp.int32, (n_keys, LANES), 1)
        qpos = col + i * LANES
        allowed = row <= qpos

        kt = kiw_ref[keys, :]
        acc = jnp.zeros((n_keys, LANES), F32)
        for hp in range(4):
            s2 = jnp.dot(kt, ri_ref[hp], preferred_element_type=F32)
            acc = acc + jnp.maximum(s2[:, :LANES], 0.0) * w[2 * hp:2 * hp + 1, :]
            acc = acc + jnp.maximum(s2[:, LANES:], 0.0) * w[2 * hp + 1:2 * hp + 2, :]
        s_ref[keys, :] = jnp.where(allowed, acc, -inf)
        lo0 = jnp.min(_fold_rows(jnp.where(allowed, acc, inf), jnp.minimum), axis=0, keepdims=True)
        mx1 = jnp.max(_fold_rows(jnp.where(allowed, acc, -inf), jnp.maximum), axis=0, keepdims=True)
        hi0 = mx1 + (jnp.abs(mx1) * (2.0 ** -20) + 1e-30)
        n_allowed = (qpos[0:1, :] + 1).astype(F32)

        _topk_bias_in_place(s_ref, n_keys, n_allowed, lo0, hi0)

        kt = kbf_ref[keys, :]
        b = s_ref[keys, :]
        b2 = jnp.concatenate([b, b], axis=1)
        ms = []
        for hp in range(4):
            s2 = jnp.dot(kt, rq_ref[hp], preferred_element_type=F32) + b2
            sc_ref[hp, keys, :] = s2
            ms.append(jnp.max(_fold_rows(s2, jnp.maximum), axis=0, keepdims=True))
        vt = jnp.concatenate([vt_ref[j] for j in range(ns)], axis=1)
        ones_rows = jnp.ones((ONES_ROWS, n_keys), BF16)
        lhs = [jnp.concatenate([vt[g * 64:(g + 1) * 64, :], ones_rows], axis=0) for g in range(N_KV)]
        pieces = []
        for hp in range(4):
            p = jnp.exp2((sc_ref[hp, keys, :] - ms[hp]).astype(BF16))
            o = jnp.dot(lhs[hp // 2], p, preferred_element_type=F32)
            o2 = o[0:HEAD_DIM, :] / o[HEAD_DIM:HEAD_DIM + 1, :]
            pieces.append(o2[:, :LANES])
            pieces.append(o2[:, LANES:])
        o_ref[...] = jnp.concatenate(pieces, axis=0)

    for ns in range(1, SEQ // KEY_STEP + 1):
        pl.when(n_steps == ns)(functools.partial(process, ns))


def _attn_prompt(qt, qit, wit, kbf, kiw, vt3):
    n_blocks = SEQ // LANES
    grid = (BATCH, n_blocks)
    qmap = lambda n, i: (0, n * n_blocks + i)
    in_specs = [
        pl.BlockSpec((512, LANES), qmap),
        pl.BlockSpec((512, LANES), qmap),
        pl.BlockSpec((8, LANES), qmap),
        pl.BlockSpec((SEQ, 128), lambda n, i: (n, 0)),
        pl.BlockSpec((SEQ, 128), lambda n, i: (n, 0)),
        pl.BlockSpec((SEQ // KEY_STEP, 128, KEY_STEP), lambda n, i: (n, 0, 0)),
    ]
    return pl.pallas_call(
        _attn_prompt_kernel, grid=grid, in_specs=in_specs,
        out_specs=pl.BlockSpec((W_A, LANES), qmap),
        out_shape=jax.ShapeDtypeStruct((W_A, BATCH * SEQ), F32),
        scratch_shapes=[pltpu.VMEM((SEQ, LANES), F32), pltpu.VMEM((4, SEQ, 2 * LANES), F32),
                        pltpu.VMEM((4, 128, 2 * LANES), BF16), pltpu.VMEM((4, 128, 2 * LANES), BF16)],
        compiler_params=pltpu.CompilerParams(dimension_semantics=("arbitrary", "arbitrary"),
                                             vmem_limit_bytes=VMEM_LIMIT),
        name="attn_prompt",
    )(qt, qit, wit, kbf, kiw, vt3)


def _page_copies(pt_ref, group, slot, gsz, streams):
    out = []
    for g in range(gsz):
        seq = group * gsz + g
        for p in range(N_PAGES):
            page = pt_ref[seq, p]
            keys = pl.ds(p * PAGE_SIZE, PAGE_SIZE)
            for hbm, buf, sem in streams:
                out.append(pltpu.make_async_copy(hbm.at[page], buf.at[slot, g, :, keys], sem.at[slot]))
    return out


def _paged_prefetch(pt_ref, gsz, streams):
    step = pl.program_id(0)
    n_steps = pl.num_programs(0)
    ahead = PAGE_BUFS - 1
    slot = step % PAGE_BUFS

    @pl.when(step == 0)
    def _():
        tail = pl.ds(PAST_LEN, KEY_TILE)
        for _, buf, _ in streams:
            for s_ in range(PAGE_BUFS):
                for g in range(gsz):
                    buf[s_, g, :, tail] = jnp.zeros((buf.shape[2], KEY_TILE), F32)
        for s_ in range(ahead):
            for n, c in enumerate(_page_copies(pt_ref, s_, s_, gsz, streams)):
                c.start(priority=n % 2)

    @pl.when(step + ahead < n_steps)
    def _():
        copies = _page_copies(pt_ref, step + ahead, (step + ahead) % PAGE_BUFS, gsz, streams)
        for n, c in enumerate(copies):
            c.start(priority=n % 2)

    for c in _page_copies(pt_ref, step, slot, gsz, streams):
        c.wait()
    return slot


def _sample_index_kernel(pt_ref, qi_ref, w_ref, kin_ref, cki_hbm, s_out_ref, kibuf, sem):
    gsz = SAMPLE_IDX_GROUP
    slot = _paged_prefetch(pt_ref, gsz, [(cki_hbm, kibuf, sem)])
    rows8 = lax.broadcasted_iota(jnp.int32, (SAMPLE_Q, SAMPLE_KEYS), 0)
    keyi = lax.broadcasted_iota(jnp.int32, (SAMPLE_Q, SAMPLE_KEYS), 1)
    allowed = keyi <= (PAST_LEN + rows8)
    new_keys = pl.ds(PAST_LEN, DEC_SEQ)
    scores = []
    for g in range(gsz):
        kibuf[slot, g, :, new_keys] = kin_ref[g]
        kib = kibuf[slot, g].astype(BF16)
        s = jnp.dot(qi_ref[g], kib, preferred_element_type=F32)
        sw = jnp.maximum(s, 0.0) * w_ref[g][:, 0:1]
        acc = jnp.sum(sw.reshape(N_IDX_HEADS, SAMPLE_Q, SAMPLE_KEYS), axis=0)
        scores.append(jnp.where(allowed, acc, -jnp.inf))
    for a in range(gsz // 2):
        s_out_ref[a] = jnp.where(rows8 < DEC_SEQ, scores[2 * a],
                                 pltpu.roll(scores[2 * a + 1], DEC_SEQ, 0))


def _sample_select_kernel(s_ref, b_ref, st_ref):
    rows = s_ref.shape[0]
    st_ref[...] = s_ref[...].T
    t4 = lax.broadcasted_iota(jnp.int32, (1, rows), 1) % DEC_SEQ
    _topk_bias_in_place(st_ref, SAMPLE_KEYS, (PAST_LEN + 1 + t4).astype(F32))
    b_ref[...] = st_ref[...].T


def _sample_attend_kernel(pt_ref, q_ref, b_ref, kn_ref, vn_ref, ck_hbm, cv_hbm, o_ref,
                          kbuf, vbuf, ksem, vsem):
    gsz = SAMPLE_GROUP
    slot = _paged_prefetch(pt_ref, gsz, [(ck_hbm, kbuf, ksem), (cv_hbm, vbuf, vsem)])
    new_keys = pl.ds(PAST_LEN, DEC_SEQ)
    lane = lax.broadcasted_iota(jnp.int32, (SAMPLE_Q, LANES), 1)
    scores = []
    for g in range(gsz):
        kbuf[slot, g, :, new_keys] = kn_ref[g]
        vbuf[slot, g, :, new_keys] = vn_ref[g]
        kb = kbuf[slot, g].astype(BF16)
        b = b_ref[g // 2]
        if g % 2:
            b = pltpu.roll(b, DEC_SEQ, 0)
        s = jnp.dot(q_ref[g], kb, preferred_element_type=F32)
        scores.append(s + jnp.concatenate([b] * N_HEADS, axis=0))
    probs = []
    for s in scores:
        p = jnp.exp2(s - jnp.max(s, axis=1, keepdims=True))
        probs.append((p.astype(BF16), jnp.sum(p, axis=1, keepdims=True)))
    for g in range(gsz):
        p, l = probs[g]
        vb = vbuf[slot, g].astype(BF16)
        o = lax.dot_general(p, vb, (((1,), (1,)), ((), ())),
                            preferred_element_type=F32) / l
        o_r = pltpu.roll(o, HEAD_DIM, 1)
        tiles = []
        for c in range(4):
            ha, hb = 2 * c, 2 * c + 1
            src_a = o if ha // 4 == 0 else o_r
            src_b = o if hb // 4 == 1 else o_r
            tiles.append(jnp.where(lane < HEAD_DIM, src_a[ha * SAMPLE_Q:(ha + 1) * SAMPLE_Q, :],
                                   src_b[hb * SAMPLE_Q:(hb + 1) * SAMPLE_Q, :]))
        o_ref[g] = jnp.concatenate(tiles, axis=1)


def _attn_sample(page_table, qi_s, q_s, w_s, ki_new, k_new, v_new, ck, cv, cki):
    params = pltpu.CompilerParams(dimension_semantics=("arbitrary",), vmem_limit_bytes=VMEM_LIMIT)
    blk = lambda n, shape: pl.BlockSpec((n,) + shape, lambda s, pt: (s, 0, 0))
    hbm = pl.BlockSpec(memory_space=pl.ANY)
    n_pairs = DEC_BATCH // 2

    gi = SAMPLE_IDX_GROUP
    scores = pl.pallas_call(
        _sample_index_kernel,
        grid_spec=pltpu.PrefetchScalarGridSpec(
            num_scalar_prefetch=1, grid=(DEC_BATCH // gi,),
            in_specs=[blk(gi, (64, IDX_DIM)), blk(gi, (64, LANES)), blk(gi, (IDX_DIM, DEC_SEQ)), hbm],
            out_specs=blk(gi // 2, (SAMPLE_Q, SAMPLE_KEYS)),
            scratch_shapes=[pltpu.VMEM((PAGE_BUFS, gi, IDX_DIM, SAMPLE_KEYS), F32),
                            pltpu.SemaphoreType.DMA((PAGE_BUFS,))]),
        out_shape=jax.ShapeDtypeStruct((n_pairs, SAMPLE_Q, SAMPLE_KEYS), F32),
        compiler_params=params, name="sample_index",
    )(page_table, qi_s, w_s, ki_new, cki)

    rows = n_pairs * SAMPLE_Q
    bias = pl.pallas_call(
        _sample_select_kernel, grid=(1,),
        in_specs=[pl.BlockSpec((rows, SAMPLE_KEYS), lambda s: (0, 0))],
        out_specs=pl.BlockSpec((rows, SAMPLE_KEYS), lambda s: (0, 0)),
        out_shape=jax.ShapeDtypeStruct((rows, SAMPLE_KEYS), F32),
        scratch_shapes=[pltpu.VMEM((SAMPLE_KEYS, rows), F32)],
        compiler_params=params, name="sample_select",
    )(scores.reshape(rows, SAMPLE_KEYS)).reshape(n_pairs, SAMPLE_Q, SAMPLE_KEYS)

    ga = SAMPLE_GROUP
    return pl.pallas_call(
        _sample_attend_kernel,
        grid_spec=pltpu.PrefetchScalarGridSpec(
            num_scalar_prefetch=1, grid=(DEC_BATCH // ga,),
            in_specs=[blk(ga, (64, 128)), blk(ga // 2, (SAMPLE_Q, SAMPLE_KEYS)),
                      blk(ga, (128, DEC_SEQ)), blk(ga, (128, DEC_SEQ)), hbm, hbm],
            out_specs=blk(ga, (SAMPLE_Q, W_A)),
            scratch_shapes=[pltpu.VMEM((PAGE_BUFS, ga, 128, SAMPLE_KEYS), F32),
                            pltpu.VMEM((PAGE_BUFS, ga, 128, SAMPLE_KEYS), F32),
                            pltpu.SemaphoreType.DMA((PAGE_BUFS,)),
                            pltpu.SemaphoreType.DMA((PAGE_BUFS,))]),
        out_shape=jax.ShapeDtypeStruct((DEC_BATCH, SAMPLE_Q, W_A), F32),
        compiler_params=params, name="sample_attend",
    )(page_table, q_s, bias, k_new, v_new, ck, cv)


def _layer_norm(x, g, b):
    mu = jnp.mean(x, axis=-1, keepdims=True)
    xc = x - mu
    var = jnp.mean(xc * xc, axis=-1, keepdims=True)
    return xc * lax.rsqrt(var + LN_EPS) * g + b


def _silu(x):
    return x / (1.0 + jnp.exp(-x))


def _mixer_kernel(x_ref, a_ref, wb_ref, wmix_ref, bsb_ref, lnvg_ref, lnvb_ref, wout_ref,
                  lng_ref, lnb_ref, y_ref, vn_ref, *, chunk, vn_last_only, tiles_per_seq):
    tm = x_ref.shape[0]
    x = x_ref[...]
    xb = x.astype(BF16)
    last = (((1,), (1,)), ((), ()))
    ga_t = lax.dot_general(wb_ref[0:W_A, :], xb, last, preferred_element_type=F32)
    zb = lax.dot_general(xb, wb_ref[W_A:, :], last, preferred_element_type=F32)
    u = zb[:, 0:512]
    vb = zb[:, 512:1024]
    gb = zb[:, 1024:1536]
    vn = _layer_norm(vb, lnvg_ref[...], lnvb_ref[...])
    if vn_last_only:
        @pl.when(pl.program_id(0) % tiles_per_seq == tiles_per_seq - 1)
        def _():
            vn_ref[0] = vn[tm - CHUNK:, :]
    else:
        vn_ref[...] = vn
    vnb = vn.astype(BF16)
    rows = []
    for c in range(tm // chunk):
        cols = []
        for g in range(N_GROUPS_B):
            blk = vnb[c * chunk:(c + 1) * chunk, g * GROUP_W_B:(g + 1) * GROUP_W_B]
            cols.append(jnp.dot(wmix_ref[g], blk, preferred_element_type=F32) + bsb_ref[g])
        rows.append(jnp.concatenate(cols, axis=1))
    mixed = jnp.concatenate(rows, axis=0) if len(rows) > 1 else rows[0]
    a_out_t = (a_ref[...] * _silu(ga_t)).astype(BF16)
    b_out = (u * mixed * _silu(gb)).astype(BF16)
    out = (lax.dot_general(a_out_t, wout_ref[0:W_A, :], (((0,), (0,)), ((), ())),
                           preferred_element_type=F32)
           + jnp.dot(b_out, wout_ref[W_A:, :], preferred_element_type=F32))
    y_ref[...] = _layer_norm(ALPHA * x + out, lng_ref[...], lnb_ref[...])


def _mixer(x2d, attn, wb, wmix, bsb, lnvg, lnvb, wout, lng, lnb, *, chunk, vn_last_only):
    rows = x2d.shape[0]
    tm = ROW_TILE
    tiles_per_seq = SEQ // tm
    grid = (rows // tm,)
    row_map = lambda r: (r, 0)
    const2 = lambda r: (0, 0)
    const3 = lambda r: (0, 0, 0)
    if vn_last_only:
        vn_shape = jax.ShapeDtypeStruct((rows // SEQ, CHUNK, W_B), F32)
        vn_spec = pl.BlockSpec((1, CHUNK, W_B), lambda r: (r // tiles_per_seq, 0, 0))
    else:
        vn_shape = jax.ShapeDtypeStruct((rows, W_B), F32)
        vn_spec = pl.BlockSpec((tm, W_B), row_map)
    kern = functools.partial(_mixer_kernel, chunk=chunk, vn_last_only=vn_last_only,
                             tiles_per_seq=tiles_per_seq)
    return pl.pallas_call(
        kern, grid=grid,
        in_specs=[pl.BlockSpec((tm, D_MODEL), row_map), pl.BlockSpec((W_A, tm), lambda r: (0, r)),
                  pl.BlockSpec(wb.shape, const2), pl.BlockSpec(wmix.shape, const3),
                  pl.BlockSpec(bsb.shape, const3), pl.BlockSpec((1, W_B), const2),
                  pl.BlockSpec((1, W_B), const2), pl.BlockSpec(wout.shape, const2),
                  pl.BlockSpec((1, D_MODEL), const2), pl.BlockSpec((1, D_MODEL), const2)],
        out_specs=(pl.BlockSpec((tm, D_MODEL), row_map), vn_spec),
        out_shape=(jax.ShapeDtypeStruct((rows, D_MODEL), F32), vn_shape),
        compiler_params=pltpu.CompilerParams(dimension_semantics=("arbitrary",),
                                             vmem_limit_bytes=VMEM_LIMIT),
        name="mixer_chunk%d" % chunk,
    )(x2d, attn, wb, wmix, bsb, lnvg, lnvb, wout, lng, lnb)


def kernel(x_prompt, x_sample, cache_k, cache_v, cache_k_idx, page_table, w_in, w_s, b_s,
           ln_v_g, ln_v_b, w_out, ln_g, ln_b):
    w = w_in[0]
    wq, wk, wv = w[:, 0:512], w[:, 512:640], w[:, 640:768]
    wga, wqi, wki, wwi = w[:, 768:1280], w[:, 1280:1792], w[:, 1792:1856], w[:, 1856:1864]
    wu, wvb, wgb = w[:, 1864:2376], w[:, 2376:2888], w[:, 2888:3400]
    wt = jnp.concatenate([wq.T * Q_SCALE, wqi.T, wv.T, wk.T, wki.T, wwi.T,
                          jnp.zeros((WT_ROWS - WT_WI - N_IDX_HEADS, D_MODEL), F32)], axis=0).astype(BF16)
    wb = jnp.concatenate([wga.T, wu.T, wvb.T, wgb.T], axis=0).astype(BF16)
    wout = w_out[0].astype(BF16)
    lnvg, lnvb = ln_v_g[0][None, :], ln_v_b[0][None, :]
    lng, lnb = ln_g[0][None, :], ln_b[0][None, :]

    xp = x_prompt.reshape(BATCH * SEQ, D_MODEL)
    tabs_p = _rope_tables(np.arange(SEQ))
    kt_p, vt_p, kit_p, kbf, kiw, qt, qit, vt3, wit = _project(xp, wt, *tabs_p, SEQ)
    attn_p = _attn_prompt(qt, qit, wit, kbf, kiw, vt3)
    tril = np.tril(np.ones((CHUNK, CHUNK), np.float32))
    wmix_p = (w_s[0] * tril[None]).astype(BF16)
    bsb_p = jnp.broadcast_to(b_s[0][:, :, None], (N_GROUPS_B, CHUNK, GROUP_W_B))
    y_p, vn_p = _mixer(xp, attn_p, wb, wmix_p, bsb_p, lnvg, lnvb, wout, lng, lnb,
                       chunk=CHUNK, vn_last_only=True)

    rows_s = DEC_BATCH * DEC_SEQ
    xs = x_sample.reshape(rows_s, D_MODEL)
    tabs_s = _rope_tables(PAST_LEN + (np.arange(rows_s) % DEC_SEQ))
    kt_s, vt_s, kit_s, _, _, qt_s, qit_s, _, wit_s = _project(xs, wt, *tabs_s, rows_s)

    def to_rows(a_t):
        a = a_t.reshape(N_HEADS, HEAD_DIM, DEC_BATCH, DEC_SEQ).transpose(2, 0, 3, 1)
        return jnp.pad(a, ((0, 0), (0, 0), (0, SAMPLE_Q - DEC_SEQ), (0, 0)))

    qi_rows = to_rows(qit_s).reshape(DEC_BATCH, N_IDX_HEADS * SAMPLE_Q, IDX_DIM)
    q5 = to_rows(qt_s).reshape(DEC_BATCH, N_KV, N_HEADS // N_KV, SAMPLE_Q, HEAD_DIM)
    zq = jnp.zeros_like(q5[:, 0])
    q_rows = jnp.stack([jnp.concatenate([q5[:, 0], zq], axis=-1),
                        jnp.concatenate([zq, q5[:, 1]], axis=-1)], axis=1)
    q_rows = q_rows.reshape(DEC_BATCH, N_HEADS * SAMPLE_Q, 2 * HEAD_DIM)
    w_rows = jnp.pad(wit_s.reshape(N_IDX_HEADS, DEC_BATCH, DEC_SEQ).transpose(1, 0, 2),
                     ((0, 0), (0, 0), (0, SAMPLE_Q - DEC_SEQ)))
    w_rows = jnp.broadcast_to(w_rows.reshape(DEC_BATCH, N_IDX_HEADS * SAMPLE_Q, 1),
                              (DEC_BATCH, N_IDX_HEADS * SAMPLE_Q, LANES))
    new_cols = lambda a: a.reshape(a.shape[0], DEC_BATCH, DEC_SEQ).transpose(1, 0, 2)
    n_pool = cache_k.shape[1]
    ck_t = cache_k[0].transpose(0, 2, 3, 1).reshape(n_pool, N_KV * HEAD_DIM, PAGE_SIZE)
    cv_t = cache_v[0].transpose(0, 2, 3, 1).reshape(n_pool, N_KV * HEAD_DIM, PAGE_SIZE)
    cki_t = cache_k_idx[0].transpose(0, 2, 1)
    attn_s8 = _attn_sample(page_table, qi_rows, q_rows, w_rows, new_cols(kit_s[0]),
                           new_cols(kt_s[0]), new_cols(vt_s[0]), ck_t, cv_t, cki_t)
    attn_s = attn_s8[:, :DEC_SEQ, :].reshape(rows_s, W_A).T
    ri = np.arange(rows_s)
    same_seq = (ri[:, None] // DEC_SEQ) == (ri[None, :] // DEC_SEQ)
    wmix_s = jnp.zeros((N_GROUPS_B, rows_s, rows_s), F32)
    for t in range(DEC_SEQ):
        for s in range(t + 1):
            hit = same_seq & (ri[:, None] % DEC_SEQ == t) & (ri[None, :] % DEC_SEQ == s)
            wmix_s = wmix_s + jnp.where(hit[None], w_s[0, :, t, s][:, None, None], 0.0)
    wmix_s = wmix_s.astype(BF16)
    bsb_s = jnp.broadcast_to(jnp.tile(b_s[0][:, :DEC_SEQ], (1, DEC_BATCH))[:, :, None],
                             (N_GROUPS_B, rows_s, GROUP_W_B))
    y_s, vn_s = _mixer(xs, attn_s, wb, wmix_s, bsb_s, lnvg, lnvb, wout, lng, lnb,
                       chunk=rows_s, vn_last_only=False)

    def heads_last(a, n, t):
        a = a.reshape(a.shape[0], N_KV, HEAD_DIM, a.shape[2]).transpose(0, 3, 1, 2)
        return a.reshape(1, n, t, N_KV, HEAD_DIM)

    return (y_p.reshape(BATCH, SEQ, D_MODEL),
            y_s.reshape(DEC_BATCH, DEC_SEQ, D_MODEL),
            heads_last(kt_p, BATCH, SEQ), heads_last(vt_p, BATCH, SEQ),
            kit_p.transpose(0, 2, 1)[None],
            vn_p.reshape(1, BATCH, CHUNK, W_B),
            heads_last(kt_s, DEC_BATCH, DEC_SEQ), heads_last(vt_s, DEC_BATCH, DEC_SEQ),
            kit_s[0].T.reshape(1, DEC_BATCH, DEC_SEQ, IDX_DIM),
            vn_s.reshape(1, DEC_BATCH, DEC_SEQ, W_B))
```
